```python
import functools
import jax, jax.numpy as jnp
from jax import lax
import numpy as np

D_MODEL = 4096
BATCH = 4
SEQ = 2048
DEPTH = 2
DEC_BATCH = 8
DEC_SEQ = 8
PAST_LEN = 16384
PAGE_SIZE = 128

H_A = 16
DK_A = 128
DV_A = 128
WIDTH_A = H_A * DV_A
CONV_A = 4
GDN_CHUNK = 64
H_B = 8
N_KV_B = 2
GROUP_B = H_B // N_KV_B
HEAD_DIM_B = 128
WIDTH_B = H_B * HEAD_DIM_B
KV_WIDTH_B = N_KV_B * HEAD_DIM_B
H_I = 32
D_I = 64
TOPK_MAX = 256
Q_BLOCK = 128
WIDTH_C = 1024
CONV_C = 3
D_FF = -(-8 * D_MODEL // (3 * 256)) * 256
EPS = 1e-6

IN_SIZES = (WIDTH_A, WIDTH_A, WIDTH_A, WIDTH_A, H_A, H_A,
            WIDTH_B, KV_WIDTH_B, KV_WIDTH_B, H_I * D_I, D_I, H_I,
            WIDTH_C, WIDTH_C, WIDTH_C,
            D_MODEL, D_MODEL, D_MODEL)
N_IN = sum(IN_SIZES)
SPLIT_POINTS = tuple(int(s) for s in np.cumsum(IN_SIZES)[:-1])

kernel_name = "hybrid_gdn_dsa_shortconv_decoder_step"


def rmsnorm(x, g):
    xf = x.astype(jnp.float32)
    y = xf * lax.rsqrt(jnp.mean(xf * xf, axis=-1, keepdims=True) + EPS)
    return (y * g.astype(jnp.float32)).astype(x.dtype)


def l2norm(x):
    return x * lax.rsqrt(jnp.sum(x * x, axis=-1, keepdims=True) + EPS)


def causal_conv(x, w, buf):
    width = w.shape[0]
    t = x.shape[1]
    xp = jnp.concatenate([buf.astype(x.dtype), x], axis=1)
    y = sum(xp[:, j:j + t] * w[j] for j in range(width))
    return y, xp[:, xp.shape[1] - (width - 1):]


def gated_delta_chunked(q, k, v, g, beta, s0, chunk):
    b, t, h, dk = q.shape
    dv = v.shape[-1]
    n = t // chunk

    def to_chunks(a):
        return a.reshape(b, n, chunk, h, *a.shape[3:]).swapaxes(2, 3)

    qc, kc, vc, gc, bc = map(to_chunks, (q, k, v, g, beta))
    gcum = jnp.cumsum(gc, axis=-1)
    tri_incl = jnp.tril(jnp.ones((chunk, chunk), bool))
    tri_strict = jnp.tril(jnp.ones((chunk, chunk), bool), -1)
    decay = jnp.exp(jnp.where(tri_incl, gcum[..., :, None] - gcum[..., None, :], -jnp.inf))
    kb = kc * bc[..., None]
    m = jnp.where(tri_strict, jnp.einsum('bnhcd,bnhed->bnhce', kb, kc) * decay, 0.0)
    rhs = jnp.concatenate([vc * bc[..., None], kb * jnp.exp(gcum)[..., None]], axis=-1)
    sol = lax.linalg.triangular_solve(m + jnp.eye(chunk, dtype=jnp.float32), rhs,
                                      left_side=True, lower=True)
    u, w = sol[..., :dv], sol[..., dv:]
    attn = jnp.einsum('bnhcd,bnhed->bnhce', qc, kc) * decay
    qg = qc * jnp.exp(gcum)[..., None]
    kd = kc * jnp.exp(gcum[..., -1:] - gcum)[..., None]
    glast = jnp.exp(gcum[..., -1])

    def step(s, inp):
        u_n, w_n, attn_n, qg_n, kd_n, gl_n = inp
        v_new = u_n - jnp.einsum('bhcd,bhde->bhce', w_n, s)
        o = jnp.einsum('bhcd,bhde->bhce', qg_n, s) + jnp.einsum('bhce,bhef->bhcf', attn_n, v_new)
        s = s * gl_n[..., None, None] + jnp.einsum('bhcd,bhce->bhde', kd_n, v_new)
        return s, o

    xs = tuple(a.swapaxes(0, 1) for a in (u, w, attn, qg, kd, glast))
    s_fin, o = lax.scan(step, s0, xs)
    o = o.swapaxes(0, 1).swapaxes(2, 3).reshape(b, t, h, dv)
    return o, s_fin


def gdn_branch(qa, ka, va, za, aa, ba, conv_w, a_log, dt_bias, norm_g, conv_buf, s0, chunk):
    b, t, _ = qa.shape
    f32 = jnp.float32
    qkv, new_buf = causal_conv(jnp.concatenate([qa, ka, va], axis=-1), conv_w, conv_buf)
    qkv = jax.nn.silu(qkv.astype(f32))
    q, k, v = jnp.split(qkv, [WIDTH_A, 2 * WIDTH_A], axis=-1)
    q = l2norm(q.reshape(b, t, H_A, DK_A)) * (DK_A ** -0.5)
    k = l2norm(k.reshape(b, t, H_A, DK_A))
    v = v.reshape(b, t, H_A, DV_A)
    g = -jnp.exp(a_log.astype(f32)) * jax.nn.softplus(aa.astype(f32) + dt_bias.astype(f32))
    beta = jax.nn.sigmoid(ba.astype(f32))
    o, s = gated_delta_chunked(q, k, v, g, beta, s0.astype(f32), chunk)
    o = o * lax.rsqrt(jnp.mean(o * o, axis=-1, keepdims=True) + EPS) * norm_g.astype(f32)
    o = o * jax.nn.silu(za.astype(f32).reshape(b, t, H_A, DV_A))
    return o.reshape(b, t, WIDTH_A).astype(qa.dtype), new_buf, s.astype(s0.dtype)


def indexer_scores(qi, wi, ki):
    dots = jnp.einsum('bthd,bsd->bths', qi, ki, preferred_element_type=jnp.float32)
    return jnp.einsum('bths,bth->bts', jax.nn.relu(dots) * (D_I ** -0.5),
                      wi.astype(jnp.float32) * (H_I ** -0.5))


def sparse_attend(q, k_sel, v_sel, valid):
    s = jnp.einsum('btngd,btknd->btngk', q, k_sel, preferred_element_type=jnp.float32) * (HEAD_DIM_B ** -0.5)
    s = jnp.where(valid[:, :, None, None, :], s, -jnp.inf)
    p = jax.nn.softmax(s, axis=-1)
    return jnp.einsum('btngk,btknd->btngd', p.astype(v_sel.dtype), v_sel)


def take_rows(a, idx):
    return jax.vmap(lambda aa, ii: aa[ii])(a, idx)


def sparse_attn_prompt(q, k, v, qi, wi, ki):
    b, s = q.shape[:2]
    topk = min(TOPK_MAX, s // 4)
    kpos = jnp.arange(s)

    def block(i):
        t0 = i * Q_BLOCK
        sl = lambda a: lax.dynamic_slice_in_dim(a, t0, Q_BLOCK, axis=1)
        tpos = t0 + jnp.arange(Q_BLOCK)
        sc = indexer_scores(sl(qi), sl(wi), ki)
        sc = jnp.where(kpos[None, None, :] <= tpos[None, :, None], sc, -jnp.inf)
        idx = lax.top_k(sc, topk)[1]
        valid = idx <= tpos[None, :, None]
        return sparse_attend(sl(q), take_rows(k, idx), take_rows(v, idx), valid)

    out = lax.map(block, jnp.arange(s // Q_BLOCK))
    return jnp.moveaxis(out, 0, 1).reshape(b, s, WIDTH_B)


def sparse_attn_sample(q, k, v, qi, wi, ki, pool_k, pool_v, pool_ki, page_table):
    b, t = q.shape[:2]
    past = page_table.shape[1] * PAGE_SIZE
    topk = min(TOPK_MAX, (past + t) // 4)
    ki_past = pool_ki[page_table].reshape(b, past, D_I)
    sc = indexer_scores(qi, wi, jnp.concatenate([ki_past, ki.astype(ki_past.dtype)], axis=1))
    kpos = jnp.arange(past + t)
    tpos = past + jnp.arange(t)
    sc = jnp.where(kpos[None, None, :] <= tpos[None, :, None], sc, -jnp.inf)
    idx = lax.top_k(sc, topk)[1]
    valid = idx <= tpos[None, :, None]
    in_past = (idx < past)[..., None, None]
    pidx = jnp.minimum(idx, past - 1)
    phys = jax.vmap(lambda pt, ii: pt[ii // PAGE_SIZE] * PAGE_SIZE + ii % PAGE_SIZE)(page_table, pidx)
    nidx = jnp.clip(idx - past, 0, t - 1)
    flat_k = pool_k.reshape(-1, N_KV_B, HEAD_DIM_B)
    flat_v = pool_v.reshape(-1, N_KV_B, HEAD_DIM_B)
    k_sel = jnp.where(in_past, flat_k[phys].astype(k.dtype), take_rows(k, nidx))
    v_sel = jnp.where(in_past, flat_v[phys].astype(v.dtype), take_rows(v, nidx))
    return sparse_attend(q, k_sel, v_sel, valid).reshape(b, t, WIDTH_B)


def trunk_layer(x, lw, gdn_buf, gdn_s, sc_buf, attn_b, chunk):
    b, t, _ = x.shape
    xn = rmsnorm(x, lw['norm1'])
    (qa, ka, va, za, aa, ba, qb, kb, vb, qi, ki, wi,
     gate_b, gate_c, hc, ga, gb, gc) = jnp.split(xn @ lw['w_in'], SPLIT_POINTS, axis=-1)
    oa, gdn_buf_new, gdn_s_new = gdn_branch(qa, ka, va, za, aa, ba, lw['conv_a'], lw['a_log'],
                                            lw['dt_bias'], lw['gdn_norm'], gdn_buf, gdn_s, chunk)
    kb = kb.reshape(b, t, N_KV_B, HEAD_DIM_B)
    vb = vb.reshape(b, t, N_KV_B, HEAD_DIM_B)
    ob = attn_b(qb.reshape(b, t, N_KV_B, GROUP_B, HEAD_DIM_B), kb, vb,
                qi.reshape(b, t, H_I, D_I), wi, ki)
    uc, sc_buf_new = causal_conv(gate_c * hc, lw['conv_c'], sc_buf)
    oc = gate_b * uc
    merged = (jax.nn.sigmoid(ga) * (oa @ lw['w_branch_a'])
              + jax.nn.sigmoid(gb) * (ob @ lw['w_branch_b'])
              + jax.nn.sigmoid(gc) * (oc @ lw['w_branch_c']))
    x = x + merged @ lw['w_o']
    hn = rmsnorm(x, lw['norm2'])
    x = x + (jax.nn.silu(hn @ lw['w_gate']) * (hn @ lw['w_up'])) @ lw['w_down']
    return x, kb, vb, ki, gdn_s_new, gdn_buf_new, sc_buf_new


def setup_inputs(seed: int = 0) -> dict:
    key = jax.random.key(seed)
    ks = jax.random.split(key, 32)
    f = jnp.float32
    nrm = lambda k, shape, scale: jax.random.normal(k, shape, f) * scale
    n_pages = PAST_LEN // PAGE_SIZE
    used = DEC_BATCH * n_pages
    n_pool = used + max(1, used // 4)
    page_table = jax.random.permutation(ks[0], n_pool)[:used].reshape(DEC_BATCH, n_pages).astype(jnp.int32)
    dt = jnp.exp(jax.random.uniform(ks[1], (DEPTH, H_A), f, np.log(1e-3), np.log(1e-1)))
    dt_bias = dt + jnp.log(-jnp.expm1(-dt))
    return {
        'x_prompt': nrm(ks[2], (BATCH, SEQ, D_MODEL), 1.0),
        'x_sample': nrm(ks[3], (DEC_BATCH, DEC_SEQ, D_MODEL), 1.0),
        'cache_k': nrm(ks[4], (DEPTH, n_pool, PAGE_SIZE, N_KV_B, HEAD_DIM_B), 1.0),
        'cache_v': nrm(ks[5], (DEPTH, n_pool, PAGE_SIZE, N_KV_B, HEAD_DIM_B), 1.0),
        'cache_kidx': nrm(ks[6], (DEPTH, n_pool, PAGE_SIZE, D_I), 1.0),
        'page_table': page_table,
        'state_gdn': nrm(ks[7], (DEPTH, DEC_BATCH, H_A, DK_A, DV_A), 0.05),
        'state_gdn_conv': nrm(ks[8], (DEPTH, DEC_BATCH, CONV_A - 1, 3 * WIDTH_A), 1.0),
        'state_sconv': nrm(ks[9], (DEPTH, DEC_BATCH, CONV_C - 1, WIDTH_C), 1.0),
        'final_norm': 1.0 + nrm(ks[10], (D_MODEL,), 0.02),
        'norm1': 1.0 + nrm(ks[11], (DEPTH, D_MODEL), 0.02),
        'norm2': 1.0 + nrm(ks[12], (DEPTH, D_MODEL), 0.02),
        'w_in': nrm(ks[13], (DEPTH, D_MODEL, N_IN), D_MODEL ** -0.5),
        'conv_a': nrm(ks[14], (DEPTH, CONV_A, 3 * WIDTH_A), CONV_A ** -0.5),
        'a_log': jnp.log(jax.random.uniform(ks[15], (DEPTH, H_A), f, 1.0, 16.0)),
        'dt_bias': dt_bias,
        'gdn_norm': 1.0 + nrm(ks[16], (DEPTH, DV_A), 0.02),
        'conv_c': nrm(ks[17], (DEPTH, CONV_C, WIDTH_C), CONV_C ** -0.5),
        'w_branch_a': nrm(ks[18], (DEPTH, WIDTH_A, D_MODEL), WIDTH_A ** -0.5),
        'w_branch_b': nrm(ks[19], (DEPTH, WIDTH_B, D_MODEL), WIDTH_B ** -0.5),
        'w_branch_c': nrm(ks[20], (DEPTH, WIDTH_C, D_MODEL), WIDTH_C ** -0.5),
        'w_o': nrm(ks[21], (DEPTH, D_MODEL, D_MODEL), D_MODEL ** -0.5),
        'w_gate': nrm(ks[22], (DEPTH, D_MODEL, D_FF), D_MODEL ** -0.5),
        'w_up': nrm(ks[23], (DEPTH, D_MODEL, D_FF), D_MODEL ** -0.5),
        'w_down': nrm(ks[24], (DEPTH, D_FF, D_MODEL), D_FF ** -0.5),
    }


def reference(x_prompt, x_sample, cache_k, cache_v, cache_kidx, page_table, state_gdn, state_gdn_conv,
              state_sconv, final_norm, norm1, norm2, w_in, conv_a, a_log, dt_bias, gdn_norm, conv_c,
              w_branch_a, w_branch_b, w_branch_c, w_o, w_gate, w_up, w_down):
    xp, xs = x_prompt, x_sample
    bp, tp = xp.shape[:2]
    ts = xs.shape[1]
    new_p = [[] for _ in range(6)]
    new_s = [[] for _ in range(6)]
    for l in range(DEPTH):
        lw = {'norm1': norm1[l], 'norm2': norm2[l], 'w_in': w_in[l], 'conv_a': conv_a[l],
              'a_log': a_log[l], 'dt_bias': dt_bias[l], 'gdn_norm': gdn_norm[l], 'conv_c': conv_c[l],
              'w_branch_a': w_branch_a[l], 'w_branch_b': w_branch_b[l], 'w_branch_c': w_branch_c[l],
              'w_o': w_o[l], 'w_gate': w_gate[l], 'w_up': w_up[l], 'w_down': w_down[l]}
        xp, *st_p = trunk_layer(
            xp, lw,
            jnp.zeros((bp, CONV_A - 1, 3 * WIDTH_A), xp.dtype),
            jnp.zeros((bp, H_A, DK_A, DV_A), xp.dtype),
            jnp.zeros((bp, CONV_C - 1, WIDTH_C), xp.dtype),
            sparse_attn_prompt, min(GDN_CHUNK, tp))
        attn_s = functools.partial(sparse_attn_sample, pool_k=cache_k[l], pool_v=cache_v[l],
                                   pool_ki=cache_kidx[l], page_table=page_table)
        xs, *st_s = trunk_layer(xs, lw, state_gdn_conv[l], state_gdn[l], state_sconv[l], attn_s, ts)
        for lst, val in zip(new_p, st_p):
            lst.append(val)
        for lst, val in zip(new_s, st_s):
            lst.append(val)
    y_prompt = rmsnorm(xp, final_norm)
    y_sample = rmsnorm(xs, final_norm)
    k_p, v_p, ki_p, gdn_p, gconv_p, sconv_p = [jnp.stack(a) for a in new_p]
    k_s, v_s, ki_s, gdn_s, gconv_s, sconv_s = [jnp.stack(a) for a in new_s]
    return (y_prompt, y_sample, k_p, v_p, ki_p, gdn_p, gconv_p, sconv_p,
            k_s, v_s, ki_s, gdn_s, gconv_s, sconv_s)
```

```python
import functools
import math

import jax
import jax.numpy as jnp
from jax import lax
from jax.experimental import pallas as pl
from jax.experimental.pallas import tpu as pltpu

F32 = jnp.float32
BF16 = jnp.bfloat16

D_MODEL = 4096
DEPTH = 2
PAGE_SIZE = 128
H_A = 16
DK_A = 128
DV_A = 128
WIDTH_A = H_A * DV_A
CONV_A = 4
GDN_CHUNK = 64
H_B = 8
N_KV_B = 2
GROUP_B = H_B // N_KV_B
HEAD_DIM_B = 128
WIDTH_B = H_B * HEAD_DIM_B
KV_WIDTH_B = N_KV_B * HEAD_DIM_B
H_I = 32
D_I = 64
TOPK_MAX = 256
Q_BLOCK = 128
WIDTH_C = 1024
CONV_C = 3
D_FF = -(-8 * D_MODEL // (3 * 256)) * 256
EPS = 1e-6

IN_NAMES = ("qa", "ka", "va", "za", "aa", "ba", "qb", "kb", "vb", "qi", "ki", "wi",
            "gate_b", "gate_c", "hc", "ga", "gb", "gc")
IN_SIZES = (WIDTH_A, WIDTH_A, WIDTH_A, WIDTH_A, H_A, H_A,
            WIDTH_B, KV_WIDTH_B, KV_WIDTH_B, H_I * D_I, D_I, H_I,
            WIDTH_C, WIDTH_C, WIDTH_C, D_MODEL, D_MODEL, D_MODEL)

P_ORDER = ("qa", "ka", "va", "za", "qi", "qb", "kb", "vb", "gate_b", "gate_c", "hc",
           "ga", "gb", "gc", "ki", "aa", "ba", "wi")
LANES = 128
N_BLOCK = 512
P_WIDTH = -(-sum(IN_SIZES) // N_BLOCK) * N_BLOCK
D_FF_PAD = -(-D_FF // 1024) * 1024
VMEM_LIMIT = 56 * 1024 * 1024
NEG_BIG = -1e30


def _p_offsets():
    sizes = dict(zip(IN_NAMES, IN_SIZES))
    off, out = 0, {}
    for name in P_ORDER:
        out[name] = off
        off += sizes[name]
    return out


P_OFF = _p_offsets()
TAIL_OFF = P_OFF["ki"]
TAIL_AA = P_OFF["aa"] - TAIL_OFF
TAIL_BA = P_OFF["ba"] - TAIL_OFF
TAIL_WI = P_OFF["wi"] - TAIL_OFF


def _cparams(*sem):
    return pltpu.CompilerParams(dimension_semantics=sem, vmem_limit_bytes=VMEM_LIMIT)


def _dot(a, b):
    return jnp.dot(a, b, preferred_element_type=F32)


def _dot_nt(a, b):
    return lax.dot_general(a, b, (((1,), (1,)), ((), ())), preferred_element_type=F32)


def _dot_tn(a, b):
    return lax.dot_general(a, b, (((0,), (0,)), ((), ())), preferred_element_type=F32)


def _dot_hi(a, b):
    return jnp.dot(a, b, preferred_element_type=F32, precision=lax.Precision.HIGHEST)


def _sigmoid(x):
    return jax.nn.sigmoid(x)


def _silu(x):
    return x * jax.nn.sigmoid(x)


def _rmsnorm_body(x_ref, g_ref, o_ref):
    x = x_ref[...]
    ms = jnp.mean(x * x, axis=-1, keepdims=True)
    o_ref[...] = (x * lax.rsqrt(ms + EPS) * g_ref[...]).astype(o_ref.dtype)


def _row_block(m):
    for bm in (192, 128, 64, 32, 16, 8):
        if m % bm == 0:
            return bm
    raise ValueError(f"unsupported row count {m}")


def _rmsnorm(x, g, out_dtype):
    m, d = x.shape
    bm = _row_block(m)
    return pl.pallas_call(
        _rmsnorm_body,
        grid=(m // bm,),
        in_specs=[pl.BlockSpec((bm, d), lambda i: (i, 0)), pl.BlockSpec((1, d), lambda i: (0, 0))],
        out_specs=pl.BlockSpec((bm, d), lambda i: (i, 0)),
        out_shape=jax.ShapeDtypeStruct((m, d), out_dtype),
        compiler_params=_cparams("parallel"),
        name="rmsnorm",
    )(x, g.reshape(1, d).astype(F32))


def _gdn_gates_body(t_ref, alog_ref, dtb_ref, g_ref, b_ref):
    t = t_ref[...]
    aa = t[:, TAIL_AA:TAIL_AA + H_A]
    ba = t[:, TAIL_BA:TAIL_BA + H_A]
    x = aa + dtb_ref[...]
    softplus = jnp.maximum(x, 0.0) + jnp.log1p(jnp.exp(-jnp.abs(x)))
    g_ref[...] = -jnp.exp(alog_ref[...]) * softplus
    b_ref[...] = _sigmoid(ba)


def _gdn_gates(p, a_log, dt_bias):
    m = p.shape[0]
    bm = _row_block(m)
    tail_blk = TAIL_OFF // LANES
    return pl.pallas_call(
        _gdn_gates_body,
        grid=(m // bm,),
        in_specs=[pl.BlockSpec((bm, LANES), lambda i: (i, tail_blk)),
                  pl.BlockSpec((1, H_A), lambda i: (0, 0)),
                  pl.BlockSpec((1, H_A), lambda i: (0, 0))],
        out_specs=[pl.BlockSpec((bm, H_A), lambda i: (i, 0)), pl.BlockSpec((bm, H_A), lambda i: (i, 0))],
        out_shape=[jax.ShapeDtypeStruct((m, H_A), F32)] * 2,
        compiler_params=_cparams("parallel"),
        name="gdn_gates",
    )(p, a_log.reshape(1, H_A).astype(F32), dt_bias.reshape(1, H_A).astype(F32))


def _mm_body(a_ref, w_ref, o_ref):
    o_ref[...] = _dot(a_ref[...], w_ref[...]).astype(o_ref.dtype)


def _mm_res_body(a_ref, w_ref, r_ref, o_ref):
    o_ref[...] = (r_ref[...] + _dot(a_ref[...], w_ref[...])).astype(o_ref.dtype)


def _big_row_block(m):
    for bm in (1376, 1024, 688, 512, 256, 128, 64):
        if m % bm == 0:
            return bm
    raise ValueError(f"unsupported row count {m}")


def _matmul(a, w, out_dtype, residual=None, name="matmul"):
    m, k = a.shape
    n = w.shape[1]
    bm, bn = _big_row_block(m), N_BLOCK
    in_specs = [pl.BlockSpec((bm, k), lambda i, j: (i, 0)), pl.BlockSpec((k, bn), lambda i, j: (0, j))]
    args = [a, w]
    body = _mm_body
    if residual is not None:
        in_specs.append(pl.BlockSpec((bm, bn), lambda i, j: (i, j)))
        args.append(residual)
        body = _mm_res_body
    return pl.pallas_call(
        body,
        grid=(m // bm, n // bn),
        in_specs=in_specs,
        out_specs=pl.BlockSpec((bm, bn), lambda i, j: (i, j)),
        out_shape=jax.ShapeDtypeStruct((m, n), out_dtype),
        compiler_params=_cparams("parallel", "arbitrary"),
        name=name,
    )(*args)


def _mm_k_res_body(a_ref, w_ref, r_ref, o_ref, acc_ref):
    kk = pl.program_id(2)

    @pl.when(kk == 0)
    def _():
        acc_ref[...] = r_ref[...]

    acc_ref[...] += _dot(a_ref[...], w_ref[...])

    @pl.when(kk == pl.num_programs(2) - 1)
    def _():
        o_ref[...] = acc_ref[...]


def _matmul_k_res(a, w, residual, bk, name="matmul_k"):
    m, k = a.shape
    n = w.shape[1]
    bm, bn = _big_row_block(m), N_BLOCK
    return pl.pallas_call(
        _mm_k_res_body,
        grid=(m // bm, n // bn, k // bk),
        in_specs=[pl.BlockSpec((bm, bk), lambda i, j, q: (i, q)),
                  pl.BlockSpec((bk, bn), lambda i, j, q: (q, j)),
                  pl.BlockSpec((bm, bn), lambda i, j, q: (i, j))],
        out_specs=pl.BlockSpec((bm, bn), lambda i, j, q: (i, j)),
        out_shape=jax.ShapeDtypeStruct((m, n), F32),
        scratch_shapes=[pltpu.VMEM((bm, bn), F32)],
        compiler_params=_cparams("parallel", "arbitrary", "arbitrary"),
        name=name,
    )(a, w, residual)


def _swiglu_body(a_ref, wg_ref, wu_ref, o_ref):
    a = a_ref[...]
    g = _dot(a, wg_ref[...])
    u = _dot(a, wu_ref[...])
    o_ref[...] = (_silu(g) * u).astype(o_ref.dtype)


def _swiglu(a, wg, wu):
    m, k = a.shape
    n = wg.shape[1]
    bm, bn = _big_row_block(m), N_BLOCK
    return pl.pallas_call(
        _swiglu_body,
        grid=(m // bm, n // bn),
        in_specs=[pl.BlockSpec((bm, k), lambda i, j: (i, 0)),
                  pl.BlockSpec((k, bn), lambda i, j: (0, j)),
                  pl.BlockSpec((k, bn), lambda i, j: (0, j))],
        out_specs=pl.BlockSpec((bm, bn), lambda i, j: (i, j)),
        out_shape=jax.ShapeDtypeStruct((m, n), BF16),
        compiler_params=_cparams("parallel", "arbitrary"),
        name="swiglu",
    )(a, wg, wu)


def _merge_body(oa_ref, ob_ref, oc_ref, wa_ref, wb_ref, wc_ref, ga_ref, gb_ref, gc_ref, o_ref):
    acc = _sigmoid(ga_ref[...]) * _dot(oa_ref[...], wa_ref[...])
    acc += _sigmoid(gb_ref[...]) * _dot(ob_ref[...], wb_ref[...])
    acc += _sigmoid(gc_ref[...]) * _dot(oc_ref[...], wc_ref[...])
    o_ref[...] = acc.astype(o_ref.dtype)


def _merge(oa, ob, oc, wa, wb, wc, p):
    m = oa.shape[0]
    n = wa.shape[1]
    bm, bn = 688 if m % 688 == 0 else _big_row_block(m), N_BLOCK
    ga0, gb0, gc0 = (P_OFF[s] // bn for s in ("ga", "gb", "gc"))
    row = lambda i, j: (i, 0)
    col = lambda i, j: (0, j)
    return pl.pallas_call(
        _merge_body,
        grid=(m // bm, n // bn),
        in_specs=[pl.BlockSpec((bm, oa.shape[1]), row), pl.BlockSpec((bm, ob.shape[1]), row),
                  pl.BlockSpec((bm, oc.shape[1]), row),
                  pl.BlockSpec((wa.shape[0], bn), col), pl.BlockSpec((wb.shape[0], bn), col),
                  pl.BlockSpec((wc.shape[0], bn), col),
                  pl.BlockSpec((bm, bn), lambda i, j: (i, ga0 + j)),
                  pl.BlockSpec((bm, bn), lambda i, j: (i, gb0 + j)),
                  pl.BlockSpec((bm, bn), lambda i, j: (i, gc0 + j))],
        out_specs=pl.BlockSpec((bm, bn), lambda i, j: (i, j)),
        out_shape=jax.ShapeDtypeStruct((m, n), BF16),
        compiler_params=_cparams("parallel", "arbitrary"),
        name="merge",
    )(oa, ob, oc, wa, wb, wc, p, p, p)


GDN_HEADS_PER_STEP = 4


def _gdn_body(q_ref, k_ref, v_ref, z_ref, wq_ref, wk_ref, wv_ref, tq_ref, tk_ref, tv_ref,
              grow_ref, gcol_ref, bcol_ref, s0_ref, ng_ref, o_ref, sout_ref, s_scr, tail_scr,
              *, chunk, hg):
    c = chunk
    w = hg * DK_A
    hgi = pl.program_id(1)
    n = pl.program_id(2)

    @pl.when(n == 0)
    def _():
        s_scr[...] = s0_ref[...].astype(F32)
        tail_scr[:, 0:w] = tq_ref[...]
        tail_scr[:, w:2 * w] = tk_ref[...]
        tail_scr[:, 2 * w:3 * w] = tv_ref[...]

    act = []
    for idx, (x_ref, w_ref) in enumerate(((q_ref, wq_ref), (k_ref, wk_ref), (v_ref, wv_ref))):
        x = x_ref[...]
        cw = w_ref[...]
        xfull = jnp.concatenate([tail_scr[:, idx * w:(idx + 1) * w], x], axis=0)
        y = x * cw[CONV_A - 1:CONV_A, :]
        for j in range(1, CONV_A):
            y = y + pltpu.roll(xfull, j, axis=0)[8:, :] * cw[CONV_A - 1 - j:CONV_A - j, :]
        tail_scr[:, idx * w:(idx + 1) * w] = xfull[c:c + 8, :]
        act.append(_silu(y))
    qs, ks, vs = act
    z = z_ref[...]

    ii = lax.broadcasted_iota(jnp.int32, (c, c), 0)
    jj = lax.broadcasted_iota(jnp.int32, (c, c), 1)
    incl = jj <= ii
    strict = jj < ii
    eye = (ii == jj).astype(F32)
    n_double = int(math.log2(c)) - 1
    ng = ng_ref[...]

    for hh in range(hg):
        sl = slice(hh * DK_A, (hh + 1) * DK_A)
        q = qs[:, sl]
        k = ks[:, sl]
        v = vs[:, sl]
        q = q * lax.rsqrt(jnp.sum(q * q, axis=-1, keepdims=True) + EPS) * (DK_A ** -0.5)
        k = k * lax.rsqrt(jnp.sum(k * k, axis=-1, keepdims=True) + EPS)
        gr = grow_ref[pl.ds(hgi * hg + hh, 1), :]
        gc = gcol_ref[hh]
        bc = bcol_ref[hh]
        gcum_c = jnp.sum(jnp.where(incl, gr, 0.0), axis=1, keepdims=True)
        gcum_r = jnp.sum(jnp.where(ii <= jj, gc, 0.0), axis=0, keepdims=True)
        decay = jnp.exp(jnp.where(incl, gcum_c - gcum_r, -jnp.inf))
        kb = k * bc
        mm = jnp.where(strict, _dot_nt(kb, k) * decay, 0.0)
        egc = jnp.exp(gcum_c)
        rhs = jnp.concatenate([v * bc, kb * egc], axis=1)
        pw = -mm
        tinv = eye + pw
        for _ in range(n_double):
            pw = _dot_hi(pw, pw)
            tinv = tinv + _dot_hi(tinv, pw)
        sol = _dot_hi(tinv, rhs)
        u = sol[:, :DV_A]
        wmat = sol[:, DV_A:]
        attn = _dot_nt(q, k) * decay
        qg = q * egc
        glast = gcum_c[c - 1:c, :]
        kd = k * jnp.exp(glast - gcum_c)
        s = s_scr[hh]
        v_new = u - _dot(wmat, s)
        o = _dot(qg, s) + _dot(attn, v_new)
        s_scr[hh] = s * jnp.exp(glast) + _dot_tn(kd, v_new)
        o = o * lax.rsqrt(jnp.mean(o * o, axis=-1, keepdims=True) + EPS) * ng
        o = o * _silu(z[:, sl])
        o_ref[:, sl] = o.astype(o_ref.dtype)

    @pl.when(n == pl.num_programs(2) - 1)
    def _():
        sout_ref[...] = s_scr[...].astype(sout_ref.dtype)


def _gdn(p, row0, nseq, t, chunk, conv_w, g, beta, conv_buf, s0, norm_g):
    hg = GDN_HEADS_PER_STEP
    w = hg * DK_A
    nch = t // chunk
    nhg = H_A // hg
    rb0 = row0 // chunk
    gs = g.reshape(nseq, nch, chunk, H_A)
    grow = gs.transpose(0, 1, 3, 2)
    gcol = grow[..., None]
    bcol = beta.reshape(nseq, nch, chunk, H_A).transpose(0, 1, 3, 2)[..., None]
    tail = jnp.pad(conv_buf.astype(F32), ((0, 0), (8 - (CONV_A - 1), 0), (0, 0)))
    qb0, kb0, vb0, zb0 = (P_OFF[s] // w for s in ("qa", "ka", "va", "za"))
    pspec = lambda b0: pl.BlockSpec((chunk, w), lambda b, h, n: (rb0 + b * nch + n, b0 + h))
    wspec = lambda b0: pl.BlockSpec((CONV_A, w), lambda b, h, n: (0, b0 + h))
    tspec = lambda b0: pl.BlockSpec((None, 8, w), lambda b, h, n: (b, 0, b0 + h))
    o, s_out = pl.pallas_call(
        functools.partial(_gdn_body, chunk=chunk, hg=hg),
        grid=(nseq, nhg, nch),
        in_specs=[pspec(qb0), pspec(kb0), pspec(vb0), pspec(zb0),
                  wspec(0), wspec(nhg), wspec(2 * nhg),
                  tspec(0), tspec(nhg), tspec(2 * nhg),
                  pl.BlockSpec((None, None, H_A, chunk), lambda b, h, n: (b, n, 0, 0)),
                  pl.BlockSpec((None, None, hg, chunk, 1), lambda b, h, n: (b, n, h, 0, 0)),
                  pl.BlockSpec((None, None, hg, chunk, 1), lambda b, h, n: (b, n, h, 0, 0)),
                  pl.BlockSpec((None, hg, DK_A, DV_A), lambda b, h, n: (b, h, 0, 0)),
                  pl.BlockSpec((1, DV_A), lambda b, h, n: (0, 0))],
        out_specs=[pl.BlockSpec((chunk, w), lambda b, h, n: (b * nch + n, h)),
                   pl.BlockSpec((None, hg, DK_A, DV_A), lambda b, h, n: (b, h, 0, 0))],
        out_shape=[jax.ShapeDtypeStruct((nseq * t, WIDTH_A), BF16),
                   jax.ShapeDtypeStruct((nseq, H_A, DK_A, DV_A), s0.dtype)],
        scratch_shapes=[pltpu.VMEM((hg, DK_A, DV_A), F32), pltpu.VMEM((8, 3 * w), F32)],
        compiler_params=_cparams("parallel", "parallel", "arbitrary"),
        name=f"gdn_c{chunk}",
    )(p, p, p, p, conv_w, conv_w, conv_w, tail, tail, tail, grow, gcol, bcol, s0,
      norm_g.reshape(1, DV_A).astype(F32))
    return o, s_out


SCONV_COLS = 256


def _sconv_body(gb_ref, gc_ref, hc_ref, w_ref, buf_ref, o_ref, st_ref, *, t):
    pr = gc_ref[...] * hc_ref[...]
    cw = w_ref[...]
    xfull = jnp.concatenate([buf_ref[...], pr], axis=0)
    y = pr * cw[CONV_C - 1:CONV_C, :]
    for j in range(1, CONV_C):
        y = y + pltpu.roll(xfull, j, axis=0)[8:, :] * cw[CONV_C - 1 - j:CONV_C - j, :]
    o_ref[...] = (gb_ref[...] * y).astype(o_ref.dtype)
    st_ref[...] = xfull[t:t + 8, :]


def _sconv(p, row0, nseq, t, conv_w, buf):
    cols = SCONV_COLS
    ncb = WIDTH_C // cols
    rb0 = row0 // t
    b0, c0, h0 = (P_OFF[s] // cols for s in ("gate_b", "gate_c", "hc"))
    buf8 = jnp.pad(buf.astype(F32), ((0, 0), (8 - (CONV_C - 1), 0), (0, 0)))
    pspec = lambda o: pl.BlockSpec((t, cols), lambda b, j: (rb0 + b, o + j))
    o, st = pl.pallas_call(
        functools.partial(_sconv_body, t=t),
        grid=(nseq, ncb),
        in_specs=[pspec(b0), pspec(c0), pspec(h0),
                  pl.BlockSpec((CONV_C, cols), lambda b, j: (0, j)),
                  pl.BlockSpec((None, 8, cols), lambda b, j: (b, 0, j))],
        out_specs=[pl.BlockSpec((t, cols), lambda b, j: (b, j)),
                   pl.BlockSpec((None, 8, cols), lambda b, j: (b, 0, j))],
        out_shape=[jax.ShapeDtypeStruct((nseq * t, WIDTH_C), BF16),
                   jax.ShapeDtypeStruct((nseq, 8, WIDTH_C), F32)],
        compiler_params=_cparams("parallel", "parallel"),
        name=f"sconv_t{t}",
    )(p, p, p, conv_w, buf8)
    return o, st[:, 8 - (CONV_C - 1):, :]


MAX_BISECT = 320


def _select_threshold(count_fn, lo0, hi0, need, k, shape):
    kf = jnp.float32(k)

    def cond(st):
        it, _, _, _, done = st
        return jnp.logical_and(it < MAX_BISECT, jnp.min(done) < 0.5)

    def body(st):
        it, lo, hi, t, done = st
        mid = jnp.where(it == 0, hi, jnp.where(it == 1, lo, 0.5 * lo + 0.5 * hi))
        cgt, cge = count_fn(mid)
        ok = jnp.logical_and(cgt <= kf, cge >= kf)
        newly = jnp.logical_and(ok, done < 0.5)
        t = jnp.where(newly, mid, t)
        lo = jnp.where(cgt > kf, mid, lo)
        hi = jnp.where(cge < kf, mid, hi)
        done = jnp.where(ok, 1.0, done)
        return it + 1, lo, hi, t, done

    done0 = jnp.where(need, 0.0, 1.0).astype(F32)
    st = lax.while_loop(cond, body, (jnp.int32(0), lo0, hi0, lo0, done0))
    return st[3]


KEY_CHUNK = 512


def _dsa_prompt_body(qi_ref, tq_ref, qb_ref, kb_ref, vb_ref, tk_ref, o_ref,
                     sc_scr, m_scr, l_scr, acc_scr, *, tq, sc, topk):
    i = pl.program_id(1)
    nck = (i * tq + tq - 1) // sc + 1
    wt = tq_ref[...].T[TAIL_WI:TAIL_WI + H_I, :] * ((D_I ** -0.5) * (H_I ** -0.5))
    qi = qi_ref[...]
    pairs = [jnp.concatenate([qi[:, (2 * p) * D_I:(2 * p + 1) * D_I],
                              qi[:, (2 * p + 1) * D_I:(2 * p + 2) * D_I]], axis=0).astype(BF16)
             for p in range(H_I // 2)]
    tpos = i * tq + lax.broadcasted_iota(jnp.int32, (1, tq), 1)

    def score_chunk(c, carry):
        r0 = pl.multiple_of(c * sc, sc)
        kic = tk_ref[pl.ds(r0, sc), 0:D_I].astype(BF16)
        acc = jnp.zeros((sc, tq), F32)
        for p in range(H_I // 2):
            d = _dot_nt(kic, pairs[p])
            acc = acc + jnp.maximum(d[:, :tq], 0.0) * wt[2 * p:2 * p + 1, :]
            acc = acc + jnp.maximum(d[:, tq:], 0.0) * wt[2 * p + 1:2 * p + 2, :]
        kpos = r0 + lax.broadcasted_iota(jnp.int32, (sc, tq), 0)
        sc_scr[pl.ds(r0, sc), :] = jnp.where(kpos <= tpos, acc, -jnp.inf)
        return carry

    lax.fori_loop(0, nck, score_chunk, 0)

    def minmax_chunk(c, carry):
        mn, mx = carry
        x = sc_scr[pl.ds(pl.multiple_of(c * sc, sc), sc), :]
        mx = jnp.maximum(mx, jnp.max(x, axis=0, keepdims=True))
        mn = jnp.minimum(mn, jnp.min(jnp.where(x == -jnp.inf, jnp.inf, x), axis=0, keepdims=True))
        return mn, mx

    mn, mx = lax.fori_loop(0, nck, minmax_chunk,
                           (jnp.full((1, tq), jnp.inf, F32), jnp.full((1, tq), -jnp.inf, F32)))

    def count_fn(mid):
        def body(c, carry):
            cgt, cge = carry
            x = sc_scr[pl.ds(pl.multiple_of(c * sc, sc), sc), :]
            cgt = cgt + jnp.sum(jnp.where(x > mid, 1.0, 0.0), axis=0, keepdims=True)
            cge = cge + jnp.sum(jnp.where(x >= mid, 1.0, 0.0), axis=0, keepdims=True)
            return cgt, cge
        z = jnp.zeros((1, tq), F32)
        return lax.fori_loop(0, nck, body, (z, z))

    thr = _select_threshold(count_fn, mn, mx, tpos + 1 > topk, topk, (1, tq))

    m_scr[...] = jnp.full(m_scr.shape, NEG_BIG, F32)
    l_scr[...] = jnp.zeros(l_scr.shape, F32)
    acc_scr[...] = jnp.zeros(acc_scr.shape, F32)
    q = qb_ref[...].astype(BF16)
    scale = HEAD_DIM_B ** -0.5

    def attn_chunk(c, carry):
        r0 = pl.multiple_of(c * sc, sc)
        sel = sc_scr[pl.ds(r0, sc), :] >= thr
        for n in range(N_KV_B):
            kc = kb_ref[pl.ds(r0, sc), n * HEAD_DIM_B:(n + 1) * HEAD_DIM_B].astype(BF16)
            vc = vb_ref[pl.ds(r0, sc), n * HEAD_DIM_B:(n + 1) * HEAD_DIM_B].astype(BF16)
            for g in range(GROUP_B):
                h = n * GROUP_B + g
                hs = slice(h * HEAD_DIM_B, (h + 1) * HEAD_DIM_B)
                s = jnp.where(sel, _dot_nt(kc, q[:, hs]) * scale, NEG_BIG)
                m_old = m_scr[h:h + 1, :]
                m_new = jnp.maximum(m_old, jnp.max(s, axis=0, keepdims=True))
                pexp = jnp.exp(s - m_new)
                alpha = jnp.exp(m_old - m_new)
                l_scr[h:h + 1, :] = alpha * l_scr[h:h + 1, :] + jnp.sum(pexp, axis=0, keepdims=True)
                acc_scr[hs, :] = alpha * acc_scr[hs, :] + _dot_tn(vc, pexp.astype(BF16))
                m_scr[h:h + 1, :] = m_new
        return carry

    lax.fori_loop(0, nck, attn_chunk, 0)

    for h in range(H_B):
        hs = slice(h * HEAD_DIM_B, (h + 1) * HEAD_DIM_B)
        ot = acc_scr[hs, :] / l_scr[h:h + 1, :]
        o_ref[:, hs] = ot.T.astype(o_ref.dtype)


def _dsa_prompt(p, nseq, s):
    tq, sc = Q_BLOCK, min(KEY_CHUNK, s)
    topk = min(TOPK_MAX, s // 4)
    nqb = s // tq
    qi0 = P_OFF["qi"] // (H_I * D_I)
    qb0 = P_OFF["qb"] // WIDTH_B
    kb0 = P_OFF["kb"] // KV_WIDTH_B
    vb0 = P_OFF["vb"] // KV_WIDTH_B
    tl0 = TAIL_OFF // LANES
    return pl.pallas_call(
        functools.partial(_dsa_prompt_body, tq=tq, sc=sc, topk=topk),
        grid=(nseq, nqb),
        in_specs=[pl.BlockSpec((tq, H_I * D_I), lambda b, i: (b * nqb + i, qi0)),
                  pl.BlockSpec((tq, LANES), lambda b, i: (b * nqb + i, tl0)),
                  pl.BlockSpec((tq, WIDTH_B), lambda b, i: (b * nqb + i, qb0)),
                  pl.BlockSpec((s, KV_WIDTH_B), lambda b, i: (b, kb0)),
                  pl.BlockSpec((s, KV_WIDTH_B), lambda b, i: (b, vb0)),
                  pl.BlockSpec((s, LANES), lambda b, i: (b, tl0))],
        out_specs=pl.BlockSpec((tq, WIDTH_B), lambda b, i: (b * nqb + i, 0)),
        out_shape=jax.ShapeDtypeStruct((nseq * s, WIDTH_B), BF16),
        scratch_shapes=[pltpu.VMEM((s, tq), F32), pltpu.VMEM((H_B, tq), F32),
                        pltpu.VMEM((H_B, tq), F32), pltpu.VMEM((WIDTH_B, tq), F32)],
        compiler_params=_cparams("parallel", "arbitrary"),
        name="dsa_prompt",
    )(p, p, p, p, p, p)


def _dsa_sample_scores_body(pt_ref, q_ref, w_ref, page_ref, new_ref, o_ref, *, npages, t, past):
    j = pl.program_id(1)
    kp = jnp.where(j == npages, new_ref[...], page_ref[...]).astype(BF16)
    d = _dot_nt(q_ref[...].astype(BF16), kp)
    r = jnp.maximum(d, 0.0) * (w_ref[...] * ((D_I ** -0.5) * (H_I ** -0.5)))
    sc = jnp.sum(r.reshape(t, H_I, PAGE_SIZE), axis=1)
    kpos = j * PAGE_SIZE + lax.broadcasted_iota(jnp.int32, (t, PAGE_SIZE), 1)
    tpos = past + lax.broadcasted_iota(jnp.int32, (t, PAGE_SIZE), 0)
    o_ref[...] = jnp.where(kpos <= tpos, sc, -jnp.inf)


def _dsa_sample_scores(page_table, q, wv, cache_kidx, layer, ki_new):
    nseq, npages = page_table.shape
    t = q.shape[1] // H_I
    past = npages * PAGE_SIZE
    return pl.pallas_call(
        functools.partial(_dsa_sample_scores_body, npages=npages, t=t, past=past),
        grid_spec=pltpu.PrefetchScalarGridSpec(
            num_scalar_prefetch=1,
            grid=(nseq, npages + 1),
            in_specs=[pl.BlockSpec((None, t * H_I, D_I), lambda b, j, pt: (b, 0, 0)),
                      pl.BlockSpec((None, t * H_I, PAGE_SIZE), lambda b, j, pt: (b, 0, 0)),
                      pl.BlockSpec((None, None, PAGE_SIZE, D_I),
                                   lambda b, j, pt: (layer, pt[b, jnp.minimum(j, npages - 1)], 0, 0)),
                      pl.BlockSpec((None, PAGE_SIZE, D_I), lambda b, j, pt: (b, 0, 0))],
            out_specs=pl.BlockSpec((None, t, PAGE_SIZE), lambda b, j, pt: (b, 0, j)),
        ),
        out_shape=jax.ShapeDtypeStruct((nseq, t, (npages + 1) * PAGE_SIZE), F32),
        compiler_params=_cparams("parallel", "arbitrary"),
        name="dsa_sample_scores",
    )(page_table, q, wv, cache_kidx, ki_new)


def _dsa_sample_attn_body(pt_ref, sc_ref, q_ref, kpage_ref, vpage_ref, knew_ref, vnew_ref, o_ref,
                          thr_scr, m_scr, l_scr, acc_scr, *, npages, t, past, topk):
    j = pl.program_id(1)

    @pl.when(j == 0)
    def _():
        x = sc_ref[...]
        mx = jnp.max(x, axis=1, keepdims=True)
        mn = jnp.min(jnp.where(x == -jnp.inf, jnp.inf, x), axis=1, keepdims=True)

        def count_fn(mid):
            xx = sc_ref[...]
            return (jnp.sum(jnp.where(xx > mid, 1.0, 0.0), axis=1, keepdims=True),
                    jnp.sum(jnp.where(xx >= mid, 1.0, 0.0), axis=1, keepdims=True))

        tpos = past + lax.broadcasted_iota(jnp.int32, (t, 1), 0)
        thr = _select_threshold(count_fn, mn, mx, tpos + 1 > topk, topk, (t, 1))
        thr_scr[...] = jnp.broadcast_to(thr, thr_scr.shape)
        m_scr[...] = jnp.full(m_scr.shape, NEG_BIG, F32)
        l_scr[...] = jnp.zeros(l_scr.shape, F32)
        acc_scr[...] = jnp.zeros(acc_scr.shape, F32)

    x = sc_ref[:, pl.ds(pl.multiple_of(j * PAGE_SIZE, PAGE_SIZE), PAGE_SIZE)]
    sel = (jnp.concatenate([x] * GROUP_B, axis=0)
           >= jnp.concatenate([thr_scr[...]] * GROUP_B, axis=0))
    is_new = j == npages
    scale = HEAD_DIM_B ** -0.5
    for n in range(N_KV_B):
        hs = slice(n * HEAD_DIM_B, (n + 1) * HEAD_DIM_B)
        kp = jnp.where(is_new, knew_ref[:, hs], kpage_ref[:, hs]).astype(BF16)
        vp = jnp.where(is_new, vnew_ref[:, hs], vpage_ref[:, hs]).astype(BF16)
        s = jnp.where(sel, _dot_nt(q_ref[n].astype(BF16), kp) * scale, NEG_BIG)
        m_old = m_scr[n]
        m_new = jnp.maximum(m_old, jnp.max(s, axis=1, keepdims=True))
        pexp = jnp.exp(s - m_new)
        alpha = jnp.exp(m_old - m_new)
        l_scr[n] = alpha * l_scr[n] + jnp.sum(pexp, axis=1, keepdims=True)
        acc_scr[n] = alpha * acc_scr[n] + _dot(pexp.astype(BF16), vp)
        m_scr[n] = m_new

    @pl.when(j == npages)
    def _():
        for n in range(N_KV_B):
            o_ref[n] = (acc_scr[n] / l_scr[n]).astype(o_ref.dtype)


def _dsa_sample_attn(page_table, scores, q, cache_k, cache_v, layer, k_new, v_new):
    nseq, npages = page_table.shape
    t = scores.shape[1]
    past = npages * PAGE_SIZE
    topk = min(TOPK_MAX, (past + t) // 4)
    rows = GROUP_B * t
    pidx = lambda b, j, pt: (layer, pt[b, jnp.minimum(j, npages - 1)], 0, 0)
    return pl.pallas_call(
        functools.partial(_dsa_sample_attn_body, npages=npages, t=t, past=past, topk=topk),
        grid_spec=pltpu.PrefetchScalarGridSpec(
            num_scalar_prefetch=1,
            grid=(nseq, npages + 1),
            in_specs=[pl.BlockSpec((None, t, scores.shape[2]), lambda b, j, pt: (b, 0, 0)),
                      pl.BlockSpec((None, N_KV_B, rows, HEAD_DIM_B), lambda b, j, pt: (b, 0, 0, 0)),
                      pl.BlockSpec((None, None, PAGE_SIZE, KV_WIDTH_B), pidx),
                      pl.BlockSpec((None, None, PAGE_SIZE, KV_WIDTH_B), pidx),
                      pl.BlockSpec((None, PAGE_SIZE, KV_WIDTH_B), lambda b, j, pt: (b, 0, 0)),
                      pl.BlockSpec((None, PAGE_SIZE, KV_WIDTH_B), lambda b, j, pt: (b, 0, 0))],
            out_specs=pl.BlockSpec((None, N_KV_B, rows, HEAD_DIM_B), lambda b, j, pt: (b, 0, 0, 0)),
            scratch_shapes=[pltpu.VMEM((t, PAGE_SIZE), F32),
                            pltpu.VMEM((N_KV_B, rows, 1), F32),
                            pltpu.VMEM((N_KV_B, rows, 1), F32),
                            pltpu.VMEM((N_KV_B, rows, HEAD_DIM_B), F32)],
        ),
        out_shape=jax.ShapeDtypeStruct((nseq, N_KV_B, rows, HEAD_DIM_B), BF16),
        compiler_params=_cparams("parallel", "arbitrary"),
        name="dsa_sample_attn",
    )(page_table, scores, q, cache_k, cache_v, k_new, v_new)


def _dsa_sample(ps, page_table, cache_k, cache_v, cache_kidx, layer):
    nseq = page_table.shape[0]
    t = ps.shape[0] // nseq
    seg = lambda name, width: ps[:, P_OFF[name]:P_OFF[name] + width]
    qi = seg("qi", H_I * D_I).reshape(nseq, t * H_I, D_I)
    wi = seg("wi", H_I).reshape(nseq, t * H_I, 1)
    wv = jnp.broadcast_to(wi, (nseq, t * H_I, PAGE_SIZE))
    pad_rows = lambda a: jnp.pad(a.reshape(nseq, t, -1), ((0, 0), (0, PAGE_SIZE - t), (0, 0)))
    ki_new = pad_rows(seg("ki", D_I))
    k_new = pad_rows(seg("kb", KV_WIDTH_B))
    v_new = pad_rows(seg("vb", KV_WIDTH_B))
    scores = _dsa_sample_scores(page_table, qi, wv, cache_kidx, layer, ki_new)
    q = seg("qb", WIDTH_B).reshape(nseq, t, N_KV_B, GROUP_B, HEAD_DIM_B)
    q = q.transpose(0, 2, 3, 1, 4).reshape(nseq, N_KV_B, GROUP_B * t, HEAD_DIM_B)
    npool = cache_k.shape[1]
    ck = cache_k.reshape(DEPTH, npool, PAGE_SIZE, KV_WIDTH_B)
    cv = cache_v.reshape(DEPTH, npool, PAGE_SIZE, KV_WIDTH_B)
    o = _dsa_sample_attn(page_table, scores, q, ck, cv, layer, k_new, v_new)
    o = o.reshape(nseq, N_KV_B, GROUP_B, t, HEAD_DIM_B).transpose(0, 3, 1, 2, 4)
    return o.reshape(nseq * t, WIDTH_B)


def _prep_w_in(w):
    sizes = dict(zip(IN_NAMES, IN_SIZES))
    src, off = {}, 0
    for name in IN_NAMES:
        src[name] = off
        off += sizes[name]
    cols = [w[:, src[name]:src[name] + sizes[name]].astype(BF16) for name in P_ORDER]
    cols.append(jnp.zeros((w.shape[0], P_WIDTH - off), BF16))
    return jnp.concatenate(cols, axis=1)


def kernel(x_prompt, x_sample, cache_k, cache_v, cache_kidx, page_table, state_gdn, state_gdn_conv,
           state_sconv, final_norm, norm1, norm2, w_in, conv_a, a_log, dt_bias, gdn_norm, conv_c,
           w_branch_a, w_branch_b, w_branch_c, w_o, w_gate, w_up, w_down):
    bp, tp, d = x_prompt.shape
    bs, ts = x_sample.shape[:2]
    mp, ms = bp * tp, bs * ts
    x = jnp.concatenate([x_prompt.reshape(mp, d), x_sample.reshape(ms, d)], axis=0)
    new_p = [[] for _ in range(6)]
    new_s = [[] for _ in range(6)]
    ff_pad = D_FF_PAD - D_FF
    for l in range(DEPTH):
        xn = _rmsnorm(x, norm1[l], BF16)
        p = _matmul(xn, _prep_w_in(w_in[l]), F32, name="in_proj")
        g, beta = _gdn_gates(p, a_log[l], dt_bias[l])
        conv_w = conv_a[l].astype(F32)

        oa_p, gdn_p = _gdn(p, 0, bp, tp, min(GDN_CHUNK, tp), conv_w, g[:mp], beta[:mp],
                           jnp.zeros((bp, CONV_A - 1, 3 * WIDTH_A), F32),
                           jnp.zeros((bp, H_A, DK_A, DV_A), F32), gdn_norm[l])
        oa_s, gdn_s = _gdn(p, mp, bs, ts, ts, conv_w, g[mp:], beta[mp:],
                           state_gdn_conv[l], state_gdn[l], gdn_norm[l])
        ob_p = _dsa_prompt(p, bp, tp)
        ob_s = _dsa_sample(p[mp:], page_table, cache_k, cache_v, cache_kidx, l)
        cw = conv_c[l].astype(F32)
        oc_p, sconv_p = _sconv(p, 0, bp, tp, cw, jnp.zeros((bp, CONV_C - 1, WIDTH_C), F32))
        oc_s, sconv_s = _sconv(p, mp, bs, ts, cw, state_sconv[l])

        oa = jnp.concatenate([oa_p, oa_s], axis=0)
        ob = jnp.concatenate([ob_p, ob_s], axis=0)
        oc = jnp.concatenate([oc_p, oc_s], axis=0)
        merged = _merge(oa, ob, oc, w_branch_a[l].astype(BF16), w_branch_b[l].astype(BF16),
                        w_branch_c[l].astype(BF16), p)
        x = _matmul(merged, w_o[l].astype(BF16), F32, residual=x, name="out_proj")

        hn = _rmsnorm(x, norm2[l], BF16)
        wg = jnp.pad(w_gate[l].astype(BF16), ((0, 0), (0, ff_pad)))
        wu = jnp.pad(w_up[l].astype(BF16), ((0, 0), (0, ff_pad)))
        wd = jnp.pad(w_down[l].astype(BF16), ((0, ff_pad), (0, 0)))
        h = _swiglu(hn, wg, wu)
        x = _matmul_k_res(h, wd, x, bk=D_FF_PAD // 4, name="ffn_down")

        kv = lambda name, rows, b, t: p[rows, P_OFF[name]:P_OFF[name] + KV_WIDTH_B].reshape(
            b, t, N_KV_B, HEAD_DIM_B)
        ki = lambda rows, b, t: p[rows, P_OFF["ki"]:P_OFF["ki"] + D_I].reshape(b, t, D_I)
        qkv_raw = p[:, :3 * WIDTH_A]
        rp, rs = slice(0, mp), slice(mp, mp + ms)
        gconv_p = qkv_raw[rp].reshape(bp, tp, -1)[:, tp - (CONV_A - 1):]
        gconv_s = jnp.concatenate([state_gdn_conv[l].astype(F32), qkv_raw[rs].reshape(bs, ts, -1)],
                                  axis=1)[:, ts:]
        for lst, val in zip(new_p, (kv("kb", rp, bp, tp), kv("vb", rp, bp, tp), ki(rp, bp, tp),
                                    gdn_p, gconv_p, sconv_p)):
            lst.append(val)
        for lst, val in zip(new_s, (kv("kb", rs, bs, ts), kv("vb", rs, bs, ts), ki(rs, bs, ts),
                                    gdn_s, gconv_s, sconv_s)):
            lst.append(val)

    y = _rmsnorm(x, final_norm, F32)
    y_prompt = y[:mp].reshape(bp, tp, d)
    y_sample = y[mp:].reshape(bs, ts, d)
    outs_p = [jnp.stack(a) for a in new_p]
    outs_s = [jnp.stack(a) for a in new_s]
    return (y_prompt, y_sample, *outs_p, *outs_s)
```

```python
import functools
import math

import jax
import jax.numpy as jnp
from jax import lax
from jax.experimental import pallas as pl
from jax.experimental.pallas import tpu as pltpu

F32 = jnp.float32
BF16 = jnp.bfloat16

D_MODEL = 4096
DEPTH = 2
PAGE_SIZE = 128
H_A = 16
DK_A = 128
DV_A = 128
WIDTH_A = H_A * DV_A
CONV_A = 4
GDN_CHUNK = 64
H_B = 8
N_KV_B = 2
GROUP_B = H_B // N_KV_B
HEAD_DIM_B = 128
WIDTH_B = H_B * HEAD_DIM_B
KV_WIDTH_B = N_KV_B * HEAD_DIM_B
H_I = 32
D_I = 64
TOPK_MAX = 256
Q_BLOCK = 128
WIDTH_C = 1024
CONV_C = 3
D_FF = -(-8 * D_MODEL // (3 * 256)) * 256
EPS = 1e-6

IN_NAMES = ("qa", "ka", "va", "za", "aa", "ba", "qb", "kb", "vb", "qi", "ki", "wi",
            "gate_b", "gate_c", "hc", "ga", "gb", "gc")
IN_SIZES = (WIDTH_A, WIDTH_A, WIDTH_A, WIDTH_A, H_A, H_A,
            WIDTH_B, KV_WIDTH_B, KV_WIDTH_B, H_I * D_I, D_I, H_I,
            WIDTH_C, WIDTH_C, WIDTH_C, D_MODEL, D_MODEL, D_MODEL)

P_ORDER = ("qa", "ka", "va", "za", "qi", "qb", "kb", "vb", "gate_b", "gate_c", "hc",
           "ga", "gb", "gc", "ki", "aa", "ba", "wi")
LANES = 128
N_BLOCK = 512
P_WIDTH = -(-sum(IN_SIZES) // N_BLOCK) * N_BLOCK
D_FF_PAD = -(-D_FF // 1024) * 1024
VMEM_LIMIT = 56 * 1024 * 1024
NEG_BIG = -1e30


def _p_offsets():
    sizes = dict(zip(IN_NAMES, IN_SIZES))
    off, out = 0, {}
    for name in P_ORDER:
        out[name] = off
        off += sizes[name]
    return out


P_OFF = _p_offsets()
TAIL_OFF = P_OFF["ki"]
TAIL_AA = P_OFF["aa"] - TAIL_OFF
TAIL_BA = P_OFF["ba"] - TAIL_OFF
TAIL_WI = P_OFF["wi"] - TAIL_OFF


def _cparams(*sem):
    return pltpu.CompilerParams(dimension_semantics=sem, vmem_limit_bytes=VMEM_LIMIT)


def _dot(a, b):
    return jnp.dot(a, b, preferred_element_type=F32)


def _dot_nt(a, b):
    return lax.dot_general(a, b, (((1,), (1,)), ((), ())), preferred_element_type=F32)


def _dot_tn(a, b):
    return lax.dot_general(a, b, (((0,), (0,)), ((), ())), preferred_element_type=F32)


def _dot_hi(a, b):
    return jnp.dot(a, b, preferred_element_type=F32, precision=lax.Precision.HIGHEST)


def _sigmoid(x):
    return jax.nn.sigmoid(x)


def _silu(x):
    return x * jax.nn.sigmoid(x)


def _rmsnorm_body(x_ref, g_ref, o_ref):
    x = x_ref[...]
    ms = jnp.mean(x * x, axis=-1, keepdims=True)
    o_ref[...] = (x * lax.rsqrt(ms + EPS) * g_ref[...]).astype(o_ref.dtype)


def _row_block(m):
    for bm in (192, 128, 64, 32, 16, 8):
        if m % bm == 0:
            return bm
    raise ValueError(f"unsupported row count {m}")


def _rmsnorm(x, g, out_dtype):
    m, d = x.shape
    bm = _row_block(m)
    return pl.pallas_call(
        _rmsnorm_body,
        grid=(m // bm,),
        in_specs=[pl.BlockSpec((bm, d), lambda i: (i, 0)), pl.BlockSpec((1, d), lambda i: (0, 0))],
        out_specs=pl.BlockSpec((bm, d), lambda i: (i, 0)),
        out_shape=jax.ShapeDtypeStruct((m, d), out_dtype),
        compiler_params=_cparams("parallel"),
        name="rmsnorm",
    )(x, g.reshape(1, d).astype(F32))


def _gdn_gates_body(t_ref, alog_ref, dtb_ref, g_ref, b_ref):
    t = t_ref[...]
    aa = t[:, TAIL_AA:TAIL_AA + H_A]
    ba = t[:, TAIL_BA:TAIL_BA + H_A]
    x = aa + dtb_ref[...]
    softplus = jnp.maximum(x, 0.0) + jnp.log1p(jnp.exp(-jnp.abs(x)))
    g_ref[...] = -jnp.exp(alog_ref[...]) * softplus
    b_ref[...] = _sigmoid(ba)


def _gdn_gates(p, a_log, dt_bias):
    m = p.shape[0]
    bm = _row_block(m)
    tail_blk = TAIL_OFF // LANES
    return pl.pallas_call(
        _gdn_gates_body,
        grid=(m // bm,),
        in_specs=[pl.BlockSpec((bm, LANES), lambda i: (i, tail_blk)),
                  pl.BlockSpec((1, H_A), lambda i: (0, 0)),
                  pl.BlockSpec((1, H_A), lambda i: (0, 0))],
        out_specs=[pl.BlockSpec((bm, H_A), lambda i: (i, 0)), pl.BlockSpec((bm, H_A), lambda i: (i, 0))],
        out_shape=[jax.ShapeDtypeStruct((m, H_A), F32)] * 2,
        compiler_params=_cparams("parallel"),
        name="gdn_gates",
    )(p, a_log.reshape(1, H_A).astype(F32), dt_bias.reshape(1, H_A).astype(F32))


def _mm_body(a_ref, w_ref, o_ref):
    o_ref[...] = _dot(a_ref[...], w_ref[...]).astype(o_ref.dtype)


def _mm_res_body(a_ref, w_ref, r_ref, o_ref):
    o_ref[...] = (r_ref[...] + _dot(a_ref[...], w_ref[...])).astype(o_ref.dtype)


def _big_row_block(m):
    for bm in (1376, 1024, 688, 512, 256, 128, 64):
        if m % bm == 0:
            return bm
    raise ValueError(f"unsupported row count {m}")


def _matmul(a, w, out_dtype, residual=None, name="matmul"):
    m, k = a.shape
    n = w.shape[1]
    bm, bn = _big_row_block(m), N_BLOCK
    in_specs = [pl.BlockSpec((bm, k), lambda i, j: (i, 0)), pl.BlockSpec((k, bn), lambda i, j: (0, j))]
    args = [a, w]
    body = _mm_body
    if residual is not None:
        in_specs.append(pl.BlockSpec((bm, bn), lambda i, j: (i, j)))
        args.append(residual)
        body = _mm_res_body
    return pl.pallas_call(
        body,
        grid=(m // bm, n // bn),
        in_specs=in_specs,
        out_specs=pl.BlockSpec((bm, bn), lambda i, j: (i, j)),
        out_shape=jax.ShapeDtypeStruct((m, n), out_dtype),
        compiler_params=_cparams("parallel", "arbitrary"),
        name=name,
    )(*args)


def _in_proj_body(a_ref, w_ref, p_ref, k_ref, v_ref, ki_ref, *, jkv, jtail):
    acc = _dot(a_ref[...], w_ref[...])
    p_ref[...] = acc
    j = pl.program_id(1)

    @pl.when(j == jkv)
    def _():
        k_ref[...] = acc[:, :KV_WIDTH_B]
        v_ref[...] = acc[:, KV_WIDTH_B:2 * KV_WIDTH_B]

    @pl.when(j == jtail)
    def _():
        ki_ref[...] = acc[:, :D_I]


def _in_proj(a, w):
    m, k = a.shape
    n = w.shape[1]
    bm, bn = _big_row_block(m), N_BLOCK
    assert P_OFF["kb"] % bn == 0 and P_OFF["vb"] == P_OFF["kb"] + KV_WIDTH_B and TAIL_OFF % bn == 0
    row = lambda i, j: (i, 0)
    return pl.pallas_call(
        functools.partial(_in_proj_body, jkv=P_OFF["kb"] // bn, jtail=TAIL_OFF // bn),
        grid=(m // bm, n // bn),
        in_specs=[pl.BlockSpec((bm, k), row), pl.BlockSpec((k, bn), lambda i, j: (0, j))],
        out_specs=[pl.BlockSpec((bm, bn), lambda i, j: (i, j)),
                   pl.BlockSpec((bm, KV_WIDTH_B), row), pl.BlockSpec((bm, KV_WIDTH_B), row),
                   pl.BlockSpec((bm, D_I), row)],
        out_shape=[jax.ShapeDtypeStruct((m, n), F32), jax.ShapeDtypeStruct((m, KV_WIDTH_B), F32),
                   jax.ShapeDtypeStruct((m, KV_WIDTH_B), F32), jax.ShapeDtypeStruct((m, D_I), F32)],
        compiler_params=_cparams("parallel", "arbitrary"),
        name="in_proj",
    )(a, w)


def _mm_k_res_body(a_ref, w_ref, r_ref, o_ref, acc_ref):
    kk = pl.program_id(2)

    @pl.when(kk == 0)
    def _():
        acc_ref[...] = r_ref[...]

    acc_ref[...] += _dot(a_ref[...], w_ref[...])

    @pl.when(kk == pl.num_programs(2) - 1)
    def _():
        o_ref[...] = acc_ref[...]


def _matmul_k_res(a, w, residual, bk, name="matmul_k"):
    m, k = a.shape
    n = w.shape[1]
    bm, bn = _big_row_block(m), N_BLOCK
    return pl.pallas_call(
        _mm_k_res_body,
        grid=(m // bm, n // bn, k // bk),
        in_specs=[pl.BlockSpec((bm, bk), lambda i, j, q: (i, q)),
                  pl.BlockSpec((bk, bn), lambda i, j, q: (q, j)),
                  pl.BlockSpec((bm, bn), lambda i, j, q: (i, j))],
        out_specs=pl.BlockSpec((bm, bn), lambda i, j, q: (i, j)),
        out_shape=jax.ShapeDtypeStruct((m, n), F32),
        scratch_shapes=[pltpu.VMEM((bm, bn), F32)],
        compiler_params=_cparams("parallel", "arbitrary", "arbitrary"),
        name=name,
    )(a, w, residual)


def _swiglu_body(a_ref, wg_ref, wu_ref, o_ref):
    a = a_ref[...]
    g = _dot(a, wg_ref[...])
    u = _dot(a, wu_ref[...])
    o_ref[...] = (_silu(g) * u).astype(o_ref.dtype)


def _swiglu(a, wg, wu):
    m, k = a.shape
    n = wg.shape[1]
    bm, bn = _big_row_block(m), N_BLOCK
    return pl.pallas_call(
        _swiglu_body,
        grid=(m // bm, n // bn),
        in_specs=[pl.BlockSpec((bm, k), lambda i, j: (i, 0)),
                  pl.BlockSpec((k, bn), lambda i, j: (0, j)),
                  pl.BlockSpec((k, bn), lambda i, j: (0, j))],
        out_specs=pl.BlockSpec((bm, bn), lambda i, j: (i, j)),
        out_shape=jax.ShapeDtypeStruct((m, n), BF16),
        compiler_params=_cparams("parallel", "arbitrary"),
        name="swiglu",
    )(a, wg, wu)


def _merge_body(oa_ref, ob_ref, oc_ref, wa_ref, wb_ref, wc_ref, ga_ref, gb_ref, gc_ref, o_ref):
    acc = _sigmoid(ga_ref[...]) * _dot(oa_ref[...], wa_ref[...])
    acc += _sigmoid(gb_ref[...]) * _dot(ob_ref[...], wb_ref[...])
    acc += _sigmoid(gc_ref[...]) * _dot(oc_ref[...], wc_ref[...])
    o_ref[...] = acc.astype(o_ref.dtype)


def _merge(oa, ob, oc, wa, wb, wc, p):
    m = oa.shape[0]
    n = wa.shape[1]
    bm, bn = 688 if m % 688 == 0 else _big_row_block(m), N_BLOCK
    ga0, gb0, gc0 = (P_OFF[s] // bn for s in ("ga", "gb", "gc"))
    row = lambda i, j: (i, 0)
    col = lambda i, j: (0, j)
    return pl.pallas_call(
        _merge_body,
        grid=(m // bm, n // bn),
        in_specs=[pl.BlockSpec((bm, oa.shape[1]), row), pl.BlockSpec((bm, ob.shape[1]), row),
                  pl.BlockSpec((bm, oc.shape[1]), row),
                  pl.BlockSpec((wa.shape[0], bn), col), pl.BlockSpec((wb.shape[0], bn), col),
                  pl.BlockSpec((wc.shape[0], bn), col),
                  pl.BlockSpec((bm, bn), lambda i, j: (i, ga0 + j)),
                  pl.BlockSpec((bm, bn), lambda i, j: (i, gb0 + j)),
                  pl.BlockSpec((bm, bn), lambda i, j: (i, gc0 + j))],
        out_specs=pl.BlockSpec((bm, bn), lambda i, j: (i, j)),
        out_shape=jax.ShapeDtypeStruct((m, n), BF16),
        compiler_params=_cparams("parallel", "arbitrary"),
        name="merge",
    )(oa, ob, oc, wa, wb, wc, p, p, p)


GDN_HEADS_PER_STEP = 8


def _gdn_body(q_ref, k_ref, v_ref, z_ref, wq_ref, wk_ref, wv_ref, tq_ref, tk_ref, tv_ref,
              grow_ref, gcol_ref, bcol_ref, s0_ref, ng_ref, o_ref, sout_ref, s_scr, tail_scr,
              *, chunk, hg):
    c = chunk
    w = hg * DK_A
    hgi = pl.program_id(1)
    n = pl.program_id(2)

    @pl.when(n == 0)
    def _():
        s_scr[...] = s0_ref[...].astype(F32)
        tail_scr[:, 0:w] = tq_ref[...]
        tail_scr[:, w:2 * w] = tk_ref[...]
        tail_scr[:, 2 * w:3 * w] = tv_ref[...]

    act = []
    for idx, (x_ref, w_ref) in enumerate(((q_ref, wq_ref), (k_ref, wk_ref), (v_ref, wv_ref))):
        x = x_ref[...]
        cw = w_ref[...]
        xfull = jnp.concatenate([tail_scr[:, idx * w:(idx + 1) * w], x], axis=0)
        y = x * cw[CONV_A - 1:CONV_A, :]
        for j in range(1, CONV_A):
            y = y + pltpu.roll(xfull, j, axis=0)[8:, :] * cw[CONV_A - 1 - j:CONV_A - j, :]
        tail_scr[:, idx * w:(idx + 1) * w] = xfull[c:c + 8, :]
        act.append(_silu(y))
    qs, ks, vs = act
    z = z_ref[...]

    ii = lax.broadcasted_iota(jnp.int32, (c, c), 0)
    jj = lax.broadcasted_iota(jnp.int32, (c, c), 1)
    incl = jj <= ii
    strict = jj < ii
    eye = (ii == jj).astype(F32)
    n_double = int(math.log2(c)) - 1
    ng = ng_ref[...]

    heads = range(hg)
    hsl = [slice(hh * DK_A, (hh + 1) * DK_A) for hh in heads]
    q_l, k_l, k16_l, decay_l, egc_l, kdec_l, glast_l, pw_l, tinv_l, rhs_l = ([] for _ in range(10))
    for hh in heads:
        q = qs[:, hsl[hh]]
        k = ks[:, hsl[hh]]
        q = q * lax.rsqrt(jnp.sum(q * q, axis=-1, keepdims=True) + EPS) * (DK_A ** -0.5)
        k = k * lax.rsqrt(jnp.sum(k * k, axis=-1, keepdims=True) + EPS)
        gr = grow_ref[pl.ds(hgi * hg + hh, 1), :]
        gc = gcol_ref[hh]
        bc = bcol_ref[hh]
        gcum_c = jnp.sum(jnp.where(incl, gr, 0.0), axis=1, keepdims=True)
        gcum_r = jnp.sum(jnp.where(ii <= jj, gc, 0.0), axis=0, keepdims=True)
        decay = jnp.exp(jnp.where(incl, gcum_c - gcum_r, -jnp.inf))
        kb = k * bc
        k16 = k.astype(BF16)
        egc = jnp.exp(gcum_c)
        glast = gcum_c[c - 1:c, :]
        mm = jnp.where(strict, _dot_nt(kb.astype(BF16), k16) * decay, 0.0)
        q_l.append(q)
        k_l.append(k)
        k16_l.append(k16)
        decay_l.append(decay)
        egc_l.append(egc)
        glast_l.append(glast)
        kdec_l.append(jnp.exp(glast - gcum_c))
        rhs_l.append(jnp.concatenate([vs[:, hsl[hh]] * bc, kb * egc], axis=1).astype(BF16))
        pw_l.append(-mm)
        tinv_l.append(eye - mm)
    for _ in range(n_double):
        for hh in heads:
            pw16 = pw_l[hh].astype(BF16)
            pw_l[hh] = _dot(pw16, pw16)
        for hh in heads:
            tinv_l[hh] = tinv_l[hh] + _dot(tinv_l[hh].astype(BF16), pw_l[hh].astype(BF16))
    sol_l = [_dot(tinv_l[hh].astype(BF16), rhs_l[hh]) for hh in heads]
    attn_l = [(_dot_nt(q_l[hh].astype(BF16), k16_l[hh]) * decay_l[hh]).astype(BF16) for hh in heads]
    s_l = [s_scr[hh] for hh in heads]
    s16_l = [s_l[hh].astype(BF16) for hh in heads]
    v16_l = [(sol_l[hh][:, :DV_A] - _dot(sol_l[hh][:, DV_A:].astype(BF16), s16_l[hh])).astype(BF16)
             for hh in heads]
    o_l = [_dot((q_l[hh] * egc_l[hh]).astype(BF16), s16_l[hh]) + _dot(attn_l[hh], v16_l[hh]) for hh in heads]
    snew_l = [s_l[hh] * jnp.exp(glast_l[hh]) + _dot_tn((k_l[hh] * kdec_l[hh]).astype(BF16), v16_l[hh])
              for hh in heads]
    s_scr[...] = jnp.stack(snew_l, axis=0)
    o_l = [o * lax.rsqrt(jnp.mean(o * o, axis=-1, keepdims=True) + EPS) * ng for o in o_l]
    o_ref[...] = (jnp.concatenate(o_l, axis=1) * _silu(z)).astype(o_ref.dtype)

    @pl.when(n == pl.num_programs(2) - 1)
    def _():
        sout_ref[...] = s_scr[...].astype(sout_ref.dtype)


def _gdn(p, row0, nseq, t, chunk, conv_w, g, beta, conv_buf, s0, norm_g):
    hg = GDN_HEADS_PER_STEP
    w = hg * DK_A
    nch = t // chunk
    nhg = H_A // hg
    rb0 = row0 // chunk
    gs = g.reshape(nseq, nch, chunk, H_A)
    grow = gs.transpose(0, 1, 3, 2)
    gcol = grow[..., None]
    bcol = beta.reshape(nseq, nch, chunk, H_A).transpose(0, 1, 3, 2)[..., None]
    tail = jnp.pad(conv_buf.astype(F32), ((0, 0), (8 - (CONV_A - 1), 0), (0, 0)))
    qb0, kb0, vb0, zb0 = (P_OFF[s] // w for s in ("qa", "ka", "va", "za"))
    pspec = lambda b0: pl.BlockSpec((chunk, w), lambda b, h, n: (rb0 + b * nch + n, b0 + h))
    wspec = lambda b0: pl.BlockSpec((CONV_A, w), lambda b, h, n: (0, b0 + h))
    tspec = lambda b0: pl.BlockSpec((None, 8, w), lambda b, h, n: (b, 0, b0 + h))
    o, s_out = pl.pallas_call(
        functools.partial(_gdn_body, chunk=chunk, hg=hg),
        grid=(nseq, nhg, nch),
        in_specs=[pspec(qb0), pspec(kb0), pspec(vb0), pspec(zb0),
                  wspec(0), wspec(nhg), wspec(2 * nhg),
                  tspec(0), tspec(nhg), tspec(2 * nhg),
                  pl.BlockSpec((None, None, H_A, chunk), lambda b, h, n: (b, n, 0, 0)),
                  pl.BlockSpec((None, None, hg, chunk, 1), lambda b, h, n: (b, n, h, 0, 0)),
                  pl.BlockSpec((None, None, hg, chunk, 1), lambda b, h, n: (b, n, h, 0, 0)),
                  pl.BlockSpec((None, hg, DK_A, DV_A), lambda b, h, n: (b, h, 0, 0)),
                  pl.BlockSpec((1, DV_A), lambda b, h, n: (0, 0))],
        out_specs=[pl.BlockSpec((chunk, w), lambda b, h, n: (b * nch + n, h)),
                   pl.BlockSpec((None, hg, DK_A, DV_A), lambda b, h, n: (b, h, 0, 0))],
        out_shape=[jax.ShapeDtypeStruct((nseq * t, WIDTH_A), BF16),
                   jax.ShapeDtypeStruct((nseq, H_A, DK_A, DV_A), s0.dtype)],
        scratch_shapes=[pltpu.VMEM((hg, DK_A, DV_A), F32), pltpu.VMEM((8, 3 * w), F32)],
        compiler_params=_cparams("parallel", "parallel", "arbitrary"),
        name=f"gdn_c{chunk}",
    )(p, p, p, p, conv_w, conv_w, conv_w, tail, tail, tail, grow, gcol, bcol, s0,
      norm_g.reshape(1, DV_A).astype(F32))
    return o, s_out


SCONV_COLS = 256


def _sconv_body(gb_ref, gc_ref, hc_ref, w_ref, buf_ref, o_ref, st_ref, *, t):
    pr = gc_ref[...] * hc_ref[...]
    cw = w_ref[...]
    xfull = jnp.concatenate([buf_ref[...], pr], axis=0)
    y = pr * cw[CONV_C - 1:CONV_C, :]
    for j in range(1, CONV_C):
        y = y + pltpu.roll(xfull, j, axis=0)[8:, :] * cw[CONV_C - 1 - j:CONV_C - j, :]
    o_ref[...] = (gb_ref[...] * y).astype(o_ref.dtype)
    st_ref[...] = xfull[t:t + 8, :]


def _sconv(p, row0, nseq, t, conv_w, buf):
    cols = SCONV_COLS
    ncb = WIDTH_C // cols
    rb0 = row0 // t
    b0, c0, h0 = (P_OFF[s] // cols for s in ("gate_b", "gate_c", "hc"))
    buf8 = jnp.pad(buf.astype(F32), ((0, 0), (8 - (CONV_C - 1), 0), (0, 0)))
    pspec = lambda o: pl.BlockSpec((t, cols), lambda b, j: (rb0 + b, o + j))
    o, st = pl.pallas_call(
        functools.partial(_sconv_body, t=t),
        grid=(nseq, ncb),
        in_specs=[pspec(b0), pspec(c0), pspec(h0),
                  pl.BlockSpec((CONV_C, cols), lambda b, j: (0, j)),
                  pl.BlockSpec((None, 8, cols), lambda b, j: (b, 0, j))],
        out_specs=[pl.BlockSpec((t, cols), lambda b, j: (b, j)),
                   pl.BlockSpec((None, 8, cols), lambda b, j: (b, 0, j))],
        out_shape=[jax.ShapeDtypeStruct((nseq * t, WIDTH_C), BF16),
                   jax.ShapeDtypeStruct((nseq, 8, WIDTH_C), F32)],
        compiler_params=_cparams("parallel", "parallel"),
        name=f"sconv_t{t}",
    )(p, p, p, conv_w, buf8)
    return o, st[:, 8 - (CONV_C - 1):, :]


MAX_BISECT = 320


def _select_threshold(count_fn, lo0, hi0, need, k, shape):
    kf = jnp.float32(k)

    def cond(st):
        it, _, _, _, done = st
        return jnp.logical_and(it < MAX_BISECT, jnp.min(done) < 0.5)

    def body(st):
        it, lo, hi, t, done = st
        mid = jnp.where(it == 0, hi, jnp.where(it == 1, lo, 0.5 * lo + 0.5 * hi))
        cgt, cge = count_fn(mid)
        ok = jnp.logical_and(cgt <= kf, cge >= kf)
        newly = jnp.logical_and(ok, done < 0.5)
        t = jnp.where(newly, mid, t)
        lo = jnp.where(cgt > kf, mid, lo)
        hi = jnp.where(cge < kf, mid, hi)
        done = jnp.where(ok, 1.0, done)
        return it + 1, lo, hi, t, done

    done0 = jnp.where(need, 0.0, 1.0).astype(F32)
    st = lax.while_loop(cond, body, (jnp.int32(0), lo0, hi0, lo0, done0))
    return st[3]


KEY_CHUNK = 512


def _dsa_prompt_body(qi_ref, tq_ref, qb_ref, kb_ref, vb_ref, tk_ref, o_ref,
                     sc_scr, m_scr, l_scr, acc_scr, *, tq, sc, topk):
    i = pl.program_id(1)
    nck = (i * tq + tq - 1) // sc + 1
    wt = tq_ref[...].T[TAIL_WI:TAIL_WI + H_I, :] * ((D_I ** -0.5) * (H_I ** -0.5))
    qi = qi_ref[...]
    pairs = [jnp.concatenate([qi[:, (2 * p) * D_I:(2 * p + 1) * D_I],
                              qi[:, (2 * p + 1) * D_I:(2 * p + 2) * D_I]], axis=0).astype(BF16)
             for p in range(H_I // 2)]
    tpos = i * tq + lax.broadcasted_iota(jnp.int32, (1, tq), 1)

    def score_chunk(c, carry):
        r0 = pl.multiple_of(c * sc, sc)
        kic = tk_ref[pl.ds(r0, sc), :].astype(BF16)
        acc = jnp.zeros((sc, tq), F32)
        for p in range(H_I // 2):
            d = _dot_nt(kic, pairs[p])
            acc = acc + jnp.maximum(d[:, :tq], 0.0) * wt[2 * p:2 * p + 1, :]
            acc = acc + jnp.maximum(d[:, tq:], 0.0) * wt[2 * p + 1:2 * p + 2, :]
        kpos = r0 + lax.broadcasted_iota(jnp.int32, (sc, tq), 0)
        sc_scr[pl.ds(r0, sc), :] = jnp.where(kpos <= tpos, acc, -jnp.inf)
        return carry

    lax.fori_loop(0, nck, score_chunk, 0)

    def minmax_chunk(c, carry):
        mn, mx = carry
        x = sc_scr[pl.ds(pl.multiple_of(c * sc, sc), sc), :]
        mx = jnp.maximum(mx, jnp.max(x, axis=0, keepdims=True))
        mn = jnp.minimum(mn, jnp.min(jnp.where(x == -jnp.inf, jnp.inf, x), axis=0, keepdims=True))
        return mn, mx

    mn, mx = lax.fori_loop(0, nck, minmax_chunk,
                           (jnp.full((1, tq), jnp.inf, F32), jnp.full((1, tq), -jnp.inf, F32)))

    def count_fn(mid):
        def body(c, carry):
            cgt, cge = carry
            x = sc_scr[pl.ds(pl.multiple_of(c * sc, sc), sc), :]
            cgt = cgt + jnp.sum(jnp.where(x > mid, 1.0, 0.0), axis=0, keepdims=True)
            cge = cge + jnp.sum(jnp.where(x >= mid, 1.0, 0.0), axis=0, keepdims=True)
            return cgt, cge
        z = jnp.zeros((1, tq), F32)
        return lax.fori_loop(0, nck, body, (z, z))

    thr = _select_threshold(count_fn, mn, mx, tpos + 1 > topk, topk, (1, tq))

    m_scr[...] = jnp.full(m_scr.shape, NEG_BIG, F32)
    l_scr[...] = jnp.zeros(l_scr.shape, F32)
    acc_scr[...] = jnp.zeros(acc_scr.shape, F32)
    q = qb_ref[...].astype(BF16)
    scale = HEAD_DIM_B ** -0.5

    def attn_chunk(c, carry):
        r0 = pl.multiple_of(c * sc, sc)
        sel = sc_scr[pl.ds(r0, sc), :] >= thr
        for n in range(N_KV_B):
            kc = kb_ref[pl.ds(r0, sc), n * HEAD_DIM_B:(n + 1) * HEAD_DIM_B].astype(BF16)
            vc = vb_ref[pl.ds(r0, sc), n * HEAD_DIM_B:(n + 1) * HEAD_DIM_B].astype(BF16)
            for g in range(GROUP_B):
                h = n * GROUP_B + g
                hs = slice(h * HEAD_DIM_B, (h + 1) * HEAD_DIM_B)
                s = jnp.where(sel, _dot_nt(kc, q[:, hs]) * scale, NEG_BIG)
                m_old = m_scr[h:h + 1, :]
                m_new = jnp.maximum(m_old, jnp.max(s, axis=0, keepdims=True))
                pexp = jnp.exp(s - m_new)
                alpha = jnp.exp(m_old - m_new)
                l_scr[h:h + 1, :] = alpha * l_scr[h:h + 1, :] + jnp.sum(pexp, axis=0, keepdims=True)
                acc_scr[hs, :] = alpha * acc_scr[hs, :] + _dot_tn(vc, pexp.astype(BF16))
                m_scr[h:h + 1, :] = m_new
        return carry

    lax.fori_loop(0, nck, attn_chunk, 0)

    for h in range(H_B):
        hs = slice(h * HEAD_DIM_B, (h + 1) * HEAD_DIM_B)
        ot = acc_scr[hs, :] / l_scr[h:h + 1, :]
        o_ref[:, hs] = ot.T.astype(o_ref.dtype)


def _dsa_prompt(p, k_new, v_new, ki_new, nseq, s):
    tq, sc = Q_BLOCK, min(KEY_CHUNK, s)
    topk = min(TOPK_MAX, s // 4)
    nqb = s // tq
    qi0 = P_OFF["qi"] // (H_I * D_I)
    qb0 = P_OFF["qb"] // WIDTH_B
    tl0 = TAIL_OFF // LANES
    return pl.pallas_call(
        functools.partial(_dsa_prompt_body, tq=tq, sc=sc, topk=topk),
        grid=(nseq, nqb),
        in_specs=[pl.BlockSpec((tq, H_I * D_I), lambda b, i: (b * nqb + i, qi0)),
                  pl.BlockSpec((tq, LANES), lambda b, i: (b * nqb + i, tl0)),
                  pl.BlockSpec((tq, WIDTH_B), lambda b, i: (b * nqb + i, qb0)),
                  pl.BlockSpec((s, KV_WIDTH_B), lambda b, i: (b, 0)),
                  pl.BlockSpec((s, KV_WIDTH_B), lambda b, i: (b, 0)),
                  pl.BlockSpec((s, D_I), lambda b, i: (b, 0))],
        out_specs=pl.BlockSpec((tq, WIDTH_B), lambda b, i: (b * nqb + i, 0)),
        out_shape=jax.ShapeDtypeStruct((nseq * s, WIDTH_B), BF16),
        scratch_shapes=[pltpu.VMEM((s, tq), F32), pltpu.VMEM((H_B, tq), F32),
                        pltpu.VMEM((H_B, tq), F32), pltpu.VMEM((WIDTH_B, tq), F32)],
        compiler_params=_cparams("parallel", "arbitrary"),
        name="dsa_prompt",
    )(p, p, p, k_new, v_new, ki_new)


PAGES_PER_STEP = 8


def _dsa_sample_scores_body(pt_ref, q_ref, w_ref, *rest, nsteps, pps, t):
    page_refs, new_ref, o_ref, onew_ref = rest[:pps], rest[pps], rest[pps + 1], rest[pps + 2]
    j = pl.program_id(1)
    q = q_ref[...].astype(BF16)
    wv = w_ref[...] * ((D_I ** -0.5) * (H_I ** -0.5))

    def scores(keys):
        d = _dot_nt(q, keys.astype(BF16))
        return jnp.sum((jnp.maximum(d, 0.0) * wv).reshape(t, H_I, PAGE_SIZE), axis=1)

    @pl.when(j < nsteps)
    def _():
        o_ref[...] = jnp.concatenate([scores(r[...]) for r in page_refs], axis=1)

    @pl.when(j == nsteps)
    def _():
        knew = lax.broadcasted_iota(jnp.int32, (t, PAGE_SIZE), 1)
        tnew = lax.broadcasted_iota(jnp.int32, (t, PAGE_SIZE), 0)
        onew_ref[...] = jnp.where(knew <= tnew, scores(new_ref[...]), -jnp.inf)


def _page_specs(block, layer, nsteps, pps):
    def spec(p):
        def index(b, j, pt):
            return (layer, pt[b, jnp.minimum(j, nsteps - 1) * pps + p]) + (0,) * (len(block) - 2)
        return pl.BlockSpec(block, index)
    return [spec(p) for p in range(pps)]


def _dsa_sample_scores(page_table, q, wv, cache_kidx, layer, ki_new):
    nseq, npages = page_table.shape
    t = q.shape[1] // H_I
    pps = math.gcd(PAGES_PER_STEP, npages)
    nsteps = npages // pps
    return pl.pallas_call(
        functools.partial(_dsa_sample_scores_body, nsteps=nsteps, pps=pps, t=t),
        grid_spec=pltpu.PrefetchScalarGridSpec(
            num_scalar_prefetch=1,
            grid=(nseq, nsteps + 1),
            in_specs=[pl.BlockSpec((None, t * H_I, D_I), lambda b, j, pt: (b, 0, 0)),
                      pl.BlockSpec((None, t * H_I, PAGE_SIZE), lambda b, j, pt: (b, 0, 0)),
                      *_page_specs((None, None, PAGE_SIZE, D_I), layer, nsteps, pps),
                      pl.BlockSpec((None, PAGE_SIZE, D_I), lambda b, j, pt: (b, 0, 0))],
            out_specs=[pl.BlockSpec((None, t, pps * PAGE_SIZE),
                                    lambda b, j, pt: (b, 0, jnp.minimum(j, nsteps - 1))),
                       pl.BlockSpec((None, t, PAGE_SIZE), lambda b, j, pt: (b, 0, 0))],
        ),
        out_shape=[jax.ShapeDtypeStruct((nseq, t, npages * PAGE_SIZE), F32),
                   jax.ShapeDtypeStruct((nseq, t, PAGE_SIZE), F32)],
        compiler_params=_cparams("parallel", "arbitrary"),
        name="dsa_sample_scores",
    )(page_table, q, wv, *([cache_kidx] * pps), ki_new)


def _dsa_sample_attn_body(pt_ref, sc_ref, scn_ref, q_ref, *rest, nsteps, pps, t, past, topk):
    kpage_refs, vpage_refs = rest[:pps], rest[pps:2 * pps]
    knew_ref, vnew_ref, o_ref, thr_scr, m_scr, l_scr, acc_scr = rest[2 * pps:]
    j = pl.program_id(1)

    @pl.when(j == 0)
    def _():
        xp, xn = sc_ref[...], scn_ref[...]
        mx = jnp.maximum(jnp.max(xp, axis=1, keepdims=True), jnp.max(xn, axis=1, keepdims=True))
        mn = jnp.minimum(jnp.min(xp, axis=1, keepdims=True),
                         jnp.min(jnp.where(xn == -jnp.inf, jnp.inf, xn), axis=1, keepdims=True))

        def count_fn(mid):
            cnt = lambda hit: jnp.sum(jnp.where(hit, 1.0, 0.0), axis=1, keepdims=True)
            a, b = sc_ref[...], scn_ref[...]
            return cnt(a > mid) + cnt(b > mid), cnt(a >= mid) + cnt(b >= mid)

        tpos = past + lax.broadcasted_iota(jnp.int32, (t, 1), 0)
        thr = _select_threshold(count_fn, mn, mx, tpos + 1 > topk, topk, (t, 1))
        thr_scr[...] = jnp.broadcast_to(thr, thr_scr.shape)
        m_scr[...] = jnp.full(m_scr.shape, NEG_BIG, F32)
        l_scr[...] = jnp.zeros(l_scr.shape, F32)
        acc_scr[...] = jnp.zeros(acc_scr.shape, F32)

    scale = HEAD_DIM_B ** -0.5

    def attend(x, keys, vals):
        sel_t = jnp.where(x >= thr_scr[:, 0:1], 1.0, 0.0)
        sel = jnp.concatenate([sel_t] * GROUP_B, axis=0) > 0.5
        for n in range(N_KV_B):
            s = jnp.where(sel, _dot_nt(q_ref[n].astype(BF16), keys(n).astype(BF16)) * scale, NEG_BIG)
            m_old = m_scr[n]
            m_new = jnp.maximum(m_old, jnp.max(s, axis=1, keepdims=True))
            pexp = jnp.exp(s - m_new)
            alpha = jnp.exp(m_old - m_new)
            l_scr[n] = alpha * l_scr[n] + jnp.sum(pexp, axis=1, keepdims=True)
            acc_scr[n] = alpha * acc_scr[n] + _dot(pexp.astype(BF16), vals(n).astype(BF16))
            m_scr[n] = m_new

    @pl.when(j < nsteps)
    def _():
        width = pps * PAGE_SIZE
        x = sc_ref[:, pl.ds(pl.multiple_of(j * width, width), width)]
        attend(x,
               lambda n: jnp.concatenate([r[:, n, :] for r in kpage_refs], axis=0),
               lambda n: jnp.concatenate([r[:, n, :] for r in vpage_refs], axis=0))

    @pl.when(j == nsteps)
    def _():
        hs = lambda n: slice(n * HEAD_DIM_B, (n + 1) * HEAD_DIM_B)
        attend(scn_ref[...], lambda n: knew_ref[:, hs(n)], lambda n: vnew_ref[:, hs(n)])
        for n in range(N_KV_B):
            o_ref[n] = (acc_scr[n] / l_scr[n]).astype(o_ref.dtype)


def _dsa_sample_attn(page_table, scores, scores_new, q, cache_k, cache_v, layer, k_new, v_new):
    nseq, npages = page_table.shape
    t = scores.shape[1]
    past = npages * PAGE_SIZE
    topk = min(TOPK_MAX, (past + t) // 4)
    rows = GROUP_B * t
    pps = math.gcd(PAGES_PER_STEP, npages)
    nsteps = npages // pps
    page_block = (None, None, PAGE_SIZE, N_KV_B, HEAD_DIM_B)
    const3 = lambda b, j, pt: (b, 0, 0)
    return pl.pallas_call(
        functools.partial(_dsa_sample_attn_body, nsteps=nsteps, pps=pps, t=t, past=past, topk=topk),
        grid_spec=pltpu.PrefetchScalarGridSpec(
            num_scalar_prefetch=1,
            grid=(nseq, nsteps + 1),
            in_specs=[pl.BlockSpec((None, t, past), const3),
                      pl.BlockSpec((None, t, PAGE_SIZE), const3),
                      pl.BlockSpec((None, N_KV_B, rows, HEAD_DIM_B), lambda b, j, pt: (b, 0, 0, 0)),
                      *_page_specs(page_block, layer, nsteps, pps),
                      *_page_specs(page_block, layer, nsteps, pps),
                      pl.BlockSpec((None, PAGE_SIZE, KV_WIDTH_B), const3),
                      pl.BlockSpec((None, PAGE_SIZE, KV_WIDTH_B), const3)],
            out_specs=pl.BlockSpec((None, N_KV_B, rows, HEAD_DIM_B), lambda b, j, pt: (b, 0, 0, 0)),
            scratch_shapes=[pltpu.VMEM((t, PAGE_SIZE), F32),
                            pltpu.VMEM((N_KV_B, rows, 1), F32),
                            pltpu.VMEM((N_KV_B, rows, 1), F32),
                            pltpu.VMEM((N_KV_B, rows, HEAD_DIM_B), F32)],
        ),
        out_shape=jax.ShapeDtypeStruct((nseq, N_KV_B, rows, HEAD_DIM_B), BF16),
        compiler_params=_cparams("parallel", "arbitrary"),
        name="dsa_sample_attn",
    )(page_table, scores, scores_new, q, *([cache_k] * pps), *([cache_v] * pps), k_new, v_new)


def _dsa_sample(ps, k_s, v_s, ki_s, page_table, cache_k, cache_v, cache_kidx, layer):
    nseq = page_table.shape[0]
    t = ps.shape[0] // nseq
    seg = lambda name, width: ps[:, P_OFF[name]:P_OFF[name] + width]
    qi = seg("qi", H_I * D_I).reshape(nseq, t * H_I, D_I)
    wi = seg("wi", H_I).reshape(nseq, t * H_I, 1)
    wv = jnp.broadcast_to(wi, (nseq, t * H_I, PAGE_SIZE))
    pad_rows = lambda a: jnp.pad(a.reshape(nseq, t, -1), ((0, 0), (0, PAGE_SIZE - t), (0, 0)))
    scores, scores_new = _dsa_sample_scores(page_table, qi, wv, cache_kidx, layer, pad_rows(ki_s))
    q = seg("qb", WIDTH_B).reshape(nseq, t, N_KV_B, GROUP_B, HEAD_DIM_B)
    q = q.transpose(0, 2, 3, 1, 4).reshape(nseq, N_KV_B, GROUP_B * t, HEAD_DIM_B)
    o = _dsa_sample_attn(page_table, scores, scores_new, q, cache_k, cache_v, layer,
                         pad_rows(k_s), pad_rows(v_s))
    o = o.reshape(nseq, N_KV_B, GROUP_B, t, HEAD_DIM_B).transpose(0, 3, 1, 2, 4)
    return o.reshape(nseq * t, WIDTH_B)


def _prep_w_in(w):
    sizes = dict(zip(IN_NAMES, IN_SIZES))
    src, off = {}, 0
    for name in IN_NAMES:
        src[name] = off
        off += sizes[name]
    cols = [w[:, src[name]:src[name] + sizes[name]].astype(BF16) for name in P_ORDER]
    cols.append(jnp.zeros((w.shape[0], P_WIDTH - off), BF16))
    return jnp.concatenate(cols, axis=1)


def kernel(x_prompt, x_sample, cache_k, cache_v, cache_kidx, page_table, state_gdn, state_gdn_conv,
           state_sconv, final_norm, norm1, norm2, w_in, conv_a, a_log, dt_bias, gdn_norm, conv_c,
           w_branch_a, w_branch_b, w_branch_c, w_o, w_gate, w_up, w_down):
    bp, tp, d = x_prompt.shape
    bs, ts = x_sample.shape[:2]
    mp, ms = bp * tp, bs * ts
    x = jnp.concatenate([x_prompt.reshape(mp, d), x_sample.reshape(ms, d)], axis=0)
    new_p = [[] for _ in range(6)]
    new_s = [[] for _ in range(6)]
    ff_pad = D_FF_PAD - D_FF
    for l in range(DEPTH):
        xn = _rmsnorm(x, norm1[l], BF16)
        p, k_new, v_new, ki_new = _in_proj(xn, _prep_w_in(w_in[l]))
        g, beta = _gdn_gates(p, a_log[l], dt_bias[l])
        conv_w = conv_a[l].astype(F32)

        oa_p, gdn_p = _gdn(p, 0, bp, tp, min(GDN_CHUNK, tp), conv_w, g[:mp], beta[:mp],
                           jnp.zeros((bp, CONV_A - 1, 3 * WIDTH_A), F32),
                           jnp.zeros((bp, H_A, DK_A, DV_A), F32), gdn_norm[l])
        oa_s, gdn_s = _gdn(p, mp, bs, ts, ts, conv_w, g[mp:], beta[mp:],
                           state_gdn_conv[l], state_gdn[l], gdn_norm[l])
        ob_p = _dsa_prompt(p, k_new, v_new, ki_new, bp, tp)
        ob_s = _dsa_sample(p[mp:], k_new[mp:], v_new[mp:], ki_new[mp:], page_table,
                           cache_k, cache_v, cache_kidx, l)
        cw = conv_c[l].astype(F32)
        oc_p, sconv_p = _sconv(p, 0, bp, tp, cw, jnp.zeros((bp, CONV_C - 1, WIDTH_C), F32))
        oc_s, sconv_s = _sconv(p, mp, bs, ts, cw, state_sconv[l])

        oa = jnp.concatenate([oa_p, oa_s], axis=0)
        ob = jnp.concatenate([ob_p, ob_s], axis=0)
        oc = jnp.concatenate([oc_p, oc_s], axis=0)
        merged = _merge(oa, ob, oc, w_branch_a[l].astype(BF16), w_branch_b[l].astype(BF16),
                        w_branch_c[l].astype(BF16), p)
        x = _matmul(merged, w_o[l].astype(BF16), F32, residual=x, name="out_proj")

        hn = _rmsnorm(x, norm2[l], BF16)
        wg = jnp.pad(w_gate[l].astype(BF16), ((0, 0), (0, ff_pad)))
        wu = jnp.pad(w_up[l].astype(BF16), ((0, 0), (0, ff_pad)))
        wd = jnp.pad(w_down[l].astype(BF16), ((0, ff_pad), (0, 0)))
        h = _swiglu(hn, wg, wu)
        x = _matmul_k_res(h, wd, x, bk=D_FF_PAD // 4, name="ffn_down")

        nb = CONV_A - 1
        tail_rows = lambda r0, t: lax.slice(p, (r0 + max(t - nb, 0), 0), (r0 + t, 3 * WIDTH_A))
        gconv_p = jnp.stack([tail_rows(b * tp, tp) for b in range(bp)])
        gconv_s = jnp.stack([tail_rows(mp + b * ts, ts) for b in range(bs)])
        if tp < nb:
            gconv_p = jnp.concatenate([jnp.zeros((bp, nb - tp, 3 * WIDTH_A), F32), gconv_p], axis=1)
        if ts < nb:
            gconv_s = jnp.concatenate([state_gdn_conv[l].astype(F32)[:, ts:], gconv_s], axis=1)
        kv = lambda a, rows, b, t: a[rows].reshape(b, t, N_KV_B, HEAD_DIM_B)
        rp, rs = slice(0, mp), slice(mp, mp + ms)
        for lst, val in zip(new_p, (kv(k_new, rp, bp, tp), kv(v_new, rp, bp, tp),
                                    ki_new[rp].reshape(bp, tp, D_I), gdn_p, gconv_p, sconv_p)):
            lst.append(val)
        for lst, val in zip(new_s, (kv(k_new, rs, bs, ts), kv(v_new, rs, bs, ts),
                                    ki_new[rs].reshape(bs, ts, D_I), gdn_s, gconv_s, sconv_s)):
            lst.append(val)

    y = _rmsnorm(x, final_norm, F32)
    y_prompt = y[:mp].reshape(bp, tp, d)
    y_sample = y[mp:].reshape(bs, ts, d)
    outs_p = [jnp.stack(a) for a in new_p]
    outs_s = [jnp.stack(a) for a in new_s]
    return (y_prompt, y_sample, *outs_p, *outs_s)
```

```python
import functools
import math

import jax
import jax.numpy as jnp
from jax import lax
from jax.experimental import pallas as pl
from jax.experimental.pallas import tpu as pltpu

F32 = jnp.float32
BF16 = jnp.bfloat16

D_MODEL = 4096
DEPTH = 2
PAGE_SIZE = 128
H_A = 16
DK_A = 128
DV_A = 128
WIDTH_A = H_A * DV_A
CONV_A = 4
GDN_CHUNK = 64
H_B = 8
N_KV_B = 2
GROUP_B = H_B // N_KV_B
HEAD_DIM_B = 128
WIDTH_B = H_B * HEAD_DIM_B
KV_WIDTH_B = N_KV_B * HEAD_DIM_B
H_I = 32
D_I = 64
TOPK_MAX = 256
Q_BLOCK = 128
WIDTH_C = 1024
CONV_C = 3
D_FF = -(-8 * D_MODEL // (3 * 256)) * 256
EPS = 1e-6

IN_NAMES = ("qa", "ka", "va", "za", "aa", "ba", "qb", "kb", "vb", "qi", "ki", "wi",
            "gate_b", "gate_c", "hc", "ga", "gb", "gc")
IN_SIZES = (WIDTH_A, WIDTH_A, WIDTH_A, WIDTH_A, H_A, H_A,
            WIDTH_B, KV_WIDTH_B, KV_WIDTH_B, H_I * D_I, D_I, H_I,
            WIDTH_C, WIDTH_C, WIDTH_C, D_MODEL, D_MODEL, D_MODEL)

LANES = 128
N_BLOCK = 512
VMEM_LIMIT = 56 * 1024 * 1024
NEG_BIG = -1e30

MAIN_ORDER = ("qa", "ka", "va", "za", "gate_b", "gate_c", "hc", "ga", "gb", "gc")
MID_ORDER = ("qi", "qb", "kb", "vb", "ki", "aa", "ba", "wi")


def _offsets(order):
    sizes = dict(zip(IN_NAMES, IN_SIZES))
    off, out = 0, {}
    for name in order:
        out[name] = off
        off += sizes[name]
    return out, off


SRC_OFF, _ = _offsets(IN_NAMES)
PM_OFF, PM_WIDTH = _offsets(MAIN_ORDER)
PD_OFF, _mid_cols = _offsets(MID_ORDER)
PD_WIDTH = -(-_mid_cols // N_BLOCK) * N_BLOCK
MAIN_RUN0 = PM_OFF["gate_b"]
MAIN_SHIFT = SRC_OFF["gate_b"] - MAIN_RUN0
assert PM_WIDTH % N_BLOCK == 0 and MAIN_RUN0 % N_BLOCK == 0 and MAIN_SHIFT % LANES == 0
assert all(SRC_OFF[n] == PM_OFF[n] for n in MAIN_ORDER[:4])
assert all(SRC_OFF[n] == PM_OFF[n] + MAIN_SHIFT for n in MAIN_ORDER[4:])
TAIL_OFF = PD_OFF["ki"]
TAIL_AA = PD_OFF["aa"] - TAIL_OFF
TAIL_BA = PD_OFF["ba"] - TAIL_OFF
TAIL_WI = PD_OFF["wi"] - TAIL_OFF
assert TAIL_OFF % N_BLOCK == 0 and PD_OFF["wi"] + H_I - TAIL_OFF == LANES


def _cparams(*sem):
    return pltpu.CompilerParams(dimension_semantics=sem, vmem_limit_bytes=VMEM_LIMIT)


def _dot(a, b):
    return jnp.dot(a, b, preferred_element_type=F32)


def _dot_nt(a, b):
    return lax.dot_general(a, b, (((1,), (1,)), ((), ())), preferred_element_type=F32)


def _dot_tn(a, b):
    return lax.dot_general(a, b, (((0,), (0,)), ((), ())), preferred_element_type=F32)


def _dot_hi(a, b):
    return jnp.dot(a, b, preferred_element_type=F32, precision=lax.Precision.HIGHEST)


def _sigmoid(x):
    return jax.nn.sigmoid(x)


def _silu(x):
    return x * jax.nn.sigmoid(x)


def _rmsnorm_body(x_ref, g_ref, o_ref):
    x = x_ref[...]
    ms = jnp.mean(x * x, axis=-1, keepdims=True)
    o_ref[...] = (x * lax.rsqrt(ms + EPS) * g_ref[...]).astype(o_ref.dtype)


def _row_block(m):
    for bm in (192, 128, 64, 32, 16, 8):
        if m % bm == 0:
            return bm
    raise ValueError(f"unsupported row count {m}")


def _rmsnorm(x, g, out_dtype, row0=0, nrows=None):
    d = x.shape[1]
    m = x.shape[0] - row0 if nrows is None else nrows
    bm = _row_block(math.gcd(m, row0) if row0 else m)
    rb0 = row0 // bm
    return pl.pallas_call(
        _rmsnorm_body,
        grid=(m // bm,),
        in_specs=[pl.BlockSpec((bm, d), lambda i: (rb0 + i, 0)), pl.BlockSpec((1, d), lambda i: (0, 0))],
        out_specs=pl.BlockSpec((bm, d), lambda i: (i, 0)),
        out_shape=jax.ShapeDtypeStruct((m, d), out_dtype),
        compiler_params=_cparams("parallel"),
        name="rmsnorm",
    )(x, g.reshape(1, d).astype(F32))


def _gdn_gates_body(t_ref, alog_ref, dtb_ref, g_ref, b_ref):
    t = t_ref[...]
    aa = t[:, TAIL_AA:TAIL_AA + H_A]
    ba = t[:, TAIL_BA:TAIL_BA + H_A]
    x = aa + dtb_ref[...]
    softplus = jnp.maximum(x, 0.0) + jnp.log1p(jnp.exp(-jnp.abs(x)))
    g_ref[...] = -jnp.exp(alog_ref[...]) * softplus
    b_ref[...] = _sigmoid(ba)


def _gdn_gates(p, a_log, dt_bias):
    m = p.shape[0]
    bm = _row_block(m)
    tail_blk = TAIL_OFF // LANES
    return pl.pallas_call(
        _gdn_gates_body,
        grid=(m // bm,),
        in_specs=[pl.BlockSpec((bm, LANES), lambda i: (i, tail_blk)),
                  pl.BlockSpec((1, H_A), lambda i: (0, 0)),
                  pl.BlockSpec((1, H_A), lambda i: (0, 0))],
        out_specs=[pl.BlockSpec((bm, H_A), lambda i: (i, 0)), pl.BlockSpec((bm, H_A), lambda i: (i, 0))],
        out_shape=[jax.ShapeDtypeStruct((m, H_A), F32)] * 2,
        compiler_params=_cparams("parallel"),
        name="gdn_gates",
    )(p, a_log.reshape(1, H_A).astype(F32), dt_bias.reshape(1, H_A).astype(F32))


def _big_row_block(m):
    for bm in (1376, 1024, 688, 512, 256, 128, 64):
        if m % bm == 0:
            return bm
    raise ValueError(f"unsupported row count {m}")


def _out_proj_body(a_ref, w_ref, r_ref, o_ref):
    o_ref[...] = r_ref[...] + _dot(a_ref[...], w_ref[...].astype(BF16))


def _out_proj(a, w, layer, residual):
    m, k = a.shape
    n = w.shape[2]
    bm, bn = _big_row_block(m), N_BLOCK // 2
    return pl.pallas_call(
        _out_proj_body,
        grid=(m // bm, n // bn),
        in_specs=[pl.BlockSpec((bm, k), lambda i, j: (i, 0)),
                  pl.BlockSpec((None, k, bn), lambda i, j: (layer, 0, j)),
                  pl.BlockSpec((bm, bn), lambda i, j: (i, j))],
        out_specs=pl.BlockSpec((bm, bn), lambda i, j: (i, j)),
        out_shape=jax.ShapeDtypeStruct((m, n), F32),
        compiler_params=_cparams("parallel", "arbitrary"),
        name="out_proj",
    )(a, w, residual)


IN_PROJ_PIECES = N_BLOCK // LANES


def _in_proj_main_body(a_ref, *rest):
    w_refs, p_ref = rest[:IN_PROJ_PIECES], rest[IN_PROJ_PIECES]
    w = jnp.concatenate([r[...].astype(BF16) for r in w_refs], axis=1)
    p_ref[...] = _dot(a_ref[...], w)


def _in_proj_main(a, w_in, layer):
    m, k = a.shape
    bm, bn = _big_row_block(m), N_BLOCK
    run0_blk, shift_blk = MAIN_RUN0 // bn, MAIN_SHIFT // LANES

    def piece(q):
        def index(i, j):
            return (layer, 0, IN_PROJ_PIECES * j + q + jnp.where(j >= run0_blk, shift_blk, 0))
        return pl.BlockSpec((None, k, LANES), index)

    return pl.pallas_call(
        _in_proj_main_body,
        grid=(m // bm, PM_WIDTH // bn),
        in_specs=[pl.BlockSpec((bm, k), lambda i, j: (i, 0))] + [piece(q) for q in range(IN_PROJ_PIECES)],
        out_specs=pl.BlockSpec((bm, bn), lambda i, j: (i, j)),
        out_shape=jax.ShapeDtypeStruct((m, PM_WIDTH), F32),
        compiler_params=_cparams("parallel", "arbitrary"),
        name="in_proj_main",
    )(a, *([w_in] * IN_PROJ_PIECES))


def _in_proj_mid_body(a_ref, w_ref, p_ref, k_ref, v_ref, ki_ref, *, jkv, jtail):
    acc = _dot(a_ref[...], w_ref[...])
    p_ref[...] = acc
    j = pl.program_id(1)

    @pl.when(j == jkv)
    def _():
        k_ref[...] = acc[:, :KV_WIDTH_B]
        v_ref[...] = acc[:, KV_WIDTH_B:2 * KV_WIDTH_B]

    @pl.when(j == jtail)
    def _():
        ki_ref[...] = acc[:, :D_I]


def _in_proj_mid(a, w_mid, layer):
    m, k = a.shape
    bm, bn = _big_row_block(m), N_BLOCK
    assert PD_OFF["kb"] % bn == 0 and PD_OFF["vb"] == PD_OFF["kb"] + KV_WIDTH_B
    row = lambda i, j: (i, 0)
    return pl.pallas_call(
        functools.partial(_in_proj_mid_body, jkv=PD_OFF["kb"] // bn, jtail=TAIL_OFF // bn),
        grid=(m // bm, PD_WIDTH // bn),
        in_specs=[pl.BlockSpec((bm, k), row), pl.BlockSpec((None, k, bn), lambda i, j: (layer, 0, j))],
        out_specs=[pl.BlockSpec((bm, bn), lambda i, j: (i, j)),
                   pl.BlockSpec((bm, KV_WIDTH_B), row), pl.BlockSpec((bm, KV_WIDTH_B), row),
                   pl.BlockSpec((bm, D_I), row)],
        out_shape=[jax.ShapeDtypeStruct((m, PD_WIDTH), F32), jax.ShapeDtypeStruct((m, KV_WIDTH_B), F32),
                   jax.ShapeDtypeStruct((m, KV_WIDTH_B), F32), jax.ShapeDtypeStruct((m, D_I), F32)],
        compiler_params=_cparams("parallel", "arbitrary"),
        name="in_proj_mid",
    )(a, w_mid)


def _mm_k_res_body(a_ref, w_ref, r_ref, o_ref, acc_ref):
    kk = pl.program_id(2)

    @pl.when(kk == 0)
    def _():
        acc_ref[...] = r_ref[...]

    acc_ref[...] += _dot(a_ref[...], w_ref[...])

    @pl.when(kk == pl.num_programs(2) - 1)
    def _():
        o_ref[...] = acc_ref[...]


def _half_row_block(m):
    return 688 if m % 688 == 0 else _big_row_block(m)


def _ffn_down(a, w16, layer, residual):
    m, k = a.shape
    n = w16.shape[2]
    bm, bn, bk = _half_row_block(m), N_BLOCK, k // 2
    assert bk % LANES == 0
    return pl.pallas_call(
        _mm_k_res_body,
        grid=(m // bm, n // bn, k // bk),
        in_specs=[pl.BlockSpec((bm, bk), lambda i, j, q: (i, q)),
                  pl.BlockSpec((None, bk, bn), lambda i, j, q: (layer, q, j)),
                  pl.BlockSpec((bm, bn), lambda i, j, q: (i, j))],
        out_specs=pl.BlockSpec((bm, bn), lambda i, j, q: (i, j)),
        out_shape=jax.ShapeDtypeStruct((m, n), F32),
        scratch_shapes=[pltpu.VMEM((bm, bn), F32)],
        compiler_params=_cparams("parallel", "arbitrary", "arbitrary"),
        name="ffn_down",
    )(a, w16, residual)


def _swiglu_body(a_ref, wg_ref, wu_ref, o_ref):
    a = a_ref[...]
    g = _dot(a, wg_ref[...].astype(BF16))
    u = _dot(a, wu_ref[...].astype(BF16))
    o_ref[...] = (_silu(g) * u).astype(o_ref.dtype)


def _swiglu(a, w_gate, w_up, layer):
    m, k = a.shape
    n = w_gate.shape[2]
    bm, bn = _big_row_block(m), N_BLOCK // 2
    assert n % bn == 0
    wspec = pl.BlockSpec((None, k, bn), lambda i, j: (layer, 0, j))
    return pl.pallas_call(
        _swiglu_body,
        grid=(m // bm, n // bn),
        in_specs=[pl.BlockSpec((bm, k), lambda i, j: (i, 0)), wspec, wspec],
        out_specs=pl.BlockSpec((bm, bn), lambda i, j: (i, j)),
        out_shape=jax.ShapeDtypeStruct((m, n), BF16),
        compiler_params=_cparams("parallel", "arbitrary"),
        name="swiglu",
    )(a, w_gate, w_up)


def _merge_body(oa_ref, ob_ref, oc_ref, wa_ref, wb_ref, wc_ref, ga_ref, gb_ref, gc_ref, o_ref):
    acc = _sigmoid(ga_ref[...]) * _dot(oa_ref[...], wa_ref[...].astype(BF16))
    acc += _sigmoid(gb_ref[...]) * _dot(ob_ref[...], wb_ref[...].astype(BF16))
    acc += _sigmoid(gc_ref[...]) * _dot(oc_ref[...], wc_ref[...].astype(BF16))
    o_ref[...] = acc.astype(o_ref.dtype)


def _merge(oa, ob, oc, wa, wb, wc, layer, p):
    m = oa.shape[0]
    n = wa.shape[2]
    bm, bn = _half_row_block(m), N_BLOCK
    ga0, gb0, gc0 = (PM_OFF[s] // bn for s in ("ga", "gb", "gc"))
    row = lambda i, j: (i, 0)
    wspec = lambda w: pl.BlockSpec((None, w.shape[1], bn), lambda i, j: (layer, 0, j))
    return pl.pallas_call(
        _merge_body,
        grid=(m // bm, n // bn),
        in_specs=[pl.BlockSpec((bm, oa.shape[1]), row), pl.BlockSpec((bm, ob.shape[1]), row),
                  pl.BlockSpec((bm, oc.shape[1]), row),
                  wspec(wa), wspec(wb), wspec(wc),
                  pl.BlockSpec((bm, bn), lambda i, j: (i, ga0 + j)),
                  pl.BlockSpec((bm, bn), lambda i, j: (i, gb0 + j)),
                  pl.BlockSpec((bm, bn), lambda i, j: (i, gc0 + j))],
        out_specs=pl.BlockSpec((bm, bn), lambda i, j: (i, j)),
        out_shape=jax.ShapeDtypeStruct((m, n), BF16),
        compiler_params=_cparams("parallel", "arbitrary"),
        name="merge",
    )(oa, ob, oc, wa, wb, wc, p, p, p)


GDN_HEADS_PER_STEP = 8


def _gdn_body(q_ref, k_ref, v_ref, z_ref, wq_ref, wk_ref, wv_ref, tq_ref, tk_ref, tv_ref,
              grow_ref, gcol_ref, bcol_ref, s0_ref, ng_ref, o_ref, sout_ref, s_scr, tail_scr,
              *, chunk, hg):
    c = chunk
    w = hg * DK_A
    hgi = pl.program_id(1)
    n = pl.program_id(2)

    @pl.when(n == 0)
    def _():
        s_scr[...] = s0_ref[...].astype(F32)
        tail_scr[:, 0:w] = tq_ref[...]
        tail_scr[:, w:2 * w] = tk_ref[...]
        tail_scr[:, 2 * w:3 * w] = tv_ref[...]

    act = []
    for idx, (x_ref, w_ref) in enumerate(((q_ref, wq_ref), (k_ref, wk_ref), (v_ref, wv_ref))):
        x = x_ref[...]
        cw = w_ref[...]
        xfull = jnp.concatenate([tail_scr[:, idx * w:(idx + 1) * w], x], axis=0)
        y = x * cw[CONV_A - 1:CONV_A, :]
        for j in range(1, CONV_A):
            y = y + pltpu.roll(xfull, j, axis=0)[8:, :] * cw[CONV_A - 1 - j:CONV_A - j, :]
        tail_scr[:, idx * w:(idx + 1) * w] = xfull[c:c + 8, :]
        act.append(_silu(y))
    qs, ks, vs = act
    z = z_ref[...]

    ii = lax.broadcasted_iota(jnp.int32, (c, c), 0)
    jj = lax.broadcasted_iota(jnp.int32, (c, c), 1)
    incl = jj <= ii
    strict = jj < ii
    eye = (ii == jj).astype(F32)
    n_double = int(math.log2(c)) - 1
    ng = ng_ref[...]

    heads = range(hg)
    hsl = [slice(hh * DK_A, (hh + 1) * DK_A) for hh in heads]
    q_l, k_l, k16_l, decay_l, egc_l, kdec_l, glast_l, pw_l, tinv_l, rhs_l = ([] for _ in range(10))
    for hh in heads:
        q = qs[:, hsl[hh]]
        k = ks[:, hsl[hh]]
        q = q * lax.rsqrt(jnp.sum(q * q, axis=-1, keepdims=True) + EPS) * (DK_A ** -0.5)
        k = k * lax.rsqrt(jnp.sum(k * k, axis=-1, keepdims=True) + EPS)
        gr = grow_ref[pl.ds(hgi * hg + hh, 1), :]
        gc = gcol_ref[hh]
        bc = bcol_ref[hh]
        gcum_c = jnp.sum(jnp.where(incl, gr, 0.0), axis=1, keepdims=True)
        gcum_r = jnp.sum(jnp.where(ii <= jj, gc, 0.0), axis=0, keepdims=True)
        decay = jnp.exp(jnp.where(incl, gcum_c - gcum_r, -jnp.inf))
        kb = k * bc
        k16 = k.astype(BF16)
        egc = jnp.exp(gcum_c)
        glast = gcum_c[c - 1:c, :]
        mm = jnp.where(strict, _dot_nt(kb.astype(BF16), k16) * decay, 0.0)
        q_l.append(q)
        k_l.append(k)
        k16_l.append(k16)
        decay_l.append(decay)
        egc_l.append(egc)
        glast_l.append(glast)
        kdec_l.append(jnp.exp(glast - gcum_c))
        rhs_l.append(jnp.concatenate([vs[:, hsl[hh]] * bc, kb * egc], axis=1).astype(BF16))
        pw_l.append(-mm)
        tinv_l.append(eye - mm)
    for _ in range(n_double):
        for hh in heads:
            pw16 = pw_l[hh].astype(BF16)
            pw_l[hh] = _dot(pw16, pw16)
        for hh in heads:
            tinv_l[hh] = tinv_l[hh] + _dot(tinv_l[hh].astype(BF16), pw_l[hh].astype(BF16))
    sol_l = [_dot(tinv_l[hh].astype(BF16), rhs_l[hh]) for hh in heads]
    attn_l = [(_dot_nt(q_l[hh].astype(BF16), k16_l[hh]) * decay_l[hh]).astype(BF16) for hh in heads]
    s_l = [s_scr[hh] for hh in heads]
    s16_l = [s_l[hh].astype(BF16) for hh in heads]
    v16_l = [(sol_l[hh][:, :DV_A] - _dot(sol_l[hh][:, DV_A:].astype(BF16), s16_l[hh])).astype(BF16)
             for hh in heads]
    o_l = [_dot((q_l[hh] * egc_l[hh]).astype(BF16), s16_l[hh]) + _dot(attn_l[hh], v16_l[hh]) for hh in heads]
    snew_l = [s_l[hh] * jnp.exp(glast_l[hh]) + _dot_tn((k_l[hh] * kdec_l[hh]).astype(BF16), v16_l[hh])
              for hh in heads]
    s_scr[...] = jnp.stack(snew_l, axis=0)
    o_l = [o * lax.rsqrt(jnp.mean(o * o, axis=-1, keepdims=True) + EPS) * ng for o in o_l]
    o_ref[...] = (jnp.concatenate(o_l, axis=1) * _silu(z)).astype(o_ref.dtype)

    @pl.when(n == pl.num_programs(2) - 1)
    def _():
        sout_ref[...] = s_scr[...].astype(sout_ref.dtype)


def _gdn(p, row0, nseq, t, chunk, conv_w, g, beta, conv_buf, s0, norm_g):
    hg = GDN_HEADS_PER_STEP
    w = hg * DK_A
    nch = t // chunk
    nhg = H_A // hg
    rb0 = row0 // chunk
    gs = g.reshape(nseq, nch, chunk, H_A)
    grow = gs.transpose(0, 1, 3, 2)
    gcol = grow[..., None]
    bcol = beta.reshape(nseq, nch, chunk, H_A).transpose(0, 1, 3, 2)[..., None]
    tail = jnp.pad(conv_buf.astype(F32), ((0, 0), (8 - (CONV_A - 1), 0), (0, 0)))
    qb0, kb0, vb0, zb0 = (PM_OFF[s] // w for s in ("qa", "ka", "va", "za"))
    pspec = lambda b0: pl.BlockSpec((chunk, w), lambda b, h, n: (rb0 + b * nch + n, b0 + h))
    wspec = lambda b0: pl.BlockSpec((CONV_A, w), lambda b, h, n: (0, b0 + h))
    tspec = lambda b0: pl.BlockSpec((None, 8, w), lambda b, h, n: (b, 0, b0 + h))
    o, s_out = pl.pallas_call(
        functools.partial(_gdn_body, chunk=chunk, hg=hg),
        grid=(nseq, nhg, nch),
        in_specs=[pspec(qb0), pspec(kb0), pspec(vb0), pspec(zb0),
                  wspec(0), wspec(nhg), wspec(2 * nhg),
                  tspec(0), tspec(nhg), tspec(2 * nhg),
                  pl.BlockSpec((None, None, H_A, chunk), lambda b, h, n: (b, n, 0, 0)),
                  pl.BlockSpec((None, None, hg, chunk, 1), lambda b, h, n: (b, n, h, 0, 0)),
                  pl.BlockSpec((None, None, hg, chunk, 1), lambda b, h, n: (b, n, h, 0, 0)),
                  pl.BlockSpec((None, hg, DK_A, DV_A), lambda b, h, n: (b, h, 0, 0)),
                  pl.BlockSpec((1, DV_A), lambda b, h, n: (0, 0))],
        out_specs=[pl.BlockSpec((chunk, w), lambda b, h, n: (b * nch + n, h)),
                   pl.BlockSpec((None, hg, DK_A, DV_A), lambda b, h, n: (b, h, 0, 0))],
        out_shape=[jax.ShapeDtypeStruct((nseq * t, WIDTH_A), BF16),
                   jax.ShapeDtypeStruct((nseq, H_A, DK_A, DV_A), s0.dtype)],
        scratch_shapes=[pltpu.VMEM((hg, DK_A, DV_A), F32), pltpu.VMEM((8, 3 * w), F32)],
        compiler_params=_cparams("parallel", "parallel", "arbitrary"),
        name=f"gdn_c{chunk}",
    )(p, p, p, p, conv_w, conv_w, conv_w, tail, tail, tail, grow, gcol, bcol, s0,
      norm_g.reshape(1, DV_A).astype(F32))
    return o, s_out


SCONV_COLS = 256


def _sconv_body(gb_ref, gc_ref, hc_ref, w_ref, buf_ref, o_ref, st_ref, *, t):
    pr = gc_ref[...] * hc_ref[...]
    cw = w_ref[...]
    xfull = jnp.concatenate([buf_ref[...], pr], axis=0)
    y = pr * cw[CONV_C - 1:CONV_C, :]
    for j in range(1, CONV_C):
        y = y + pltpu.roll(xfull, j, axis=0)[8:, :] * cw[CONV_C - 1 - j:CONV_C - j, :]
    o_ref[...] = (gb_ref[...] * y).astype(o_ref.dtype)
    st_ref[...] = xfull[t:t + 8, :]


def _sconv(p, row0, nseq, t, conv_w, buf):
    cols = SCONV_COLS
    ncb = WIDTH_C // cols
    rb0 = row0 // t
    b0, c0, h0 = (PM_OFF[s] // cols for s in ("gate_b", "gate_c", "hc"))
    buf8 = jnp.pad(buf.astype(F32), ((0, 0), (8 - (CONV_C - 1), 0), (0, 0)))
    pspec = lambda o: pl.BlockSpec((t, cols), lambda b, j: (rb0 + b, o + j))
    o, st = pl.pallas_call(
        functools.partial(_sconv_body, t=t),
        grid=(nseq, ncb),
        in_specs=[pspec(b0), pspec(c0), pspec(h0),
                  pl.BlockSpec((CONV_C, cols), lambda b, j: (0, j)),
                  pl.BlockSpec((None, 8, cols), lambda b, j: (b, 0, j))],
        out_specs=[pl.BlockSpec((t, cols), lambda b, j: (b, j)),
                   pl.BlockSpec((None, 8, cols), lambda b, j: (b, 0, j))],
        out_shape=[jax.ShapeDtypeStruct((nseq * t, WIDTH_C), BF16),
                   jax.ShapeDtypeStruct((nseq, 8, WIDTH_C), F32)],
        compiler_params=_cparams("parallel", "parallel"),
        name=f"sconv_t{t}",
    )(p, p, p, conv_w, buf8)
    return o, st[:, 8 - (CONV_C - 1):, :]


MAX_BISECT = 320


def _select_threshold(count_fn, lo0, hi0, need, k, shape):
    kf = jnp.float32(k)

    def cond(st):
        it, _, _, _, done = st
        return jnp.logical_and(it < MAX_BISECT, jnp.min(done) < 0.5)

    def body(st):
        it, lo, hi, t, done = st
        mid = jnp.where(it == 0, hi, jnp.where(it == 1, lo, 0.5 * lo + 0.5 * hi))
        cgt, cge = count_fn(mid)
        ok = jnp.logical_and(cgt <= kf, cge >= kf)
        newly = jnp.logical_and(ok, done < 0.5)
        t = jnp.where(newly, mid, t)
        lo = jnp.where(cgt > kf, mid, lo)
        hi = jnp.where(cge < kf, mid, hi)
        done = jnp.where(ok, 1.0, done)
        return it + 1, lo, hi, t, done

    done0 = jnp.where(need, 0.0, 1.0).astype(F32)
    st = lax.while_loop(cond, body, (jnp.int32(0), lo0, hi0, lo0, done0))
    return st[3]


KEY_CHUNK = 512


def _dsa_prompt_body(qi_ref, tq_ref, qb_ref, kb_ref, vb_ref, tk_ref, o_ref,
                     sc_scr, m_scr, l_scr, acc_scr, *, tq, sc, topk):
    i = pl.program_id(1)
    nck = (i * tq + tq - 1) // sc + 1
    wt = tq_ref[...].T[TAIL_WI:TAIL_WI + H_I, :] * ((D_I ** -0.5) * (H_I ** -0.5))
    qi = qi_ref[...]
    pairs = [jnp.concatenate([qi[:, (2 * p) * D_I:(2 * p + 1) * D_I],
                              qi[:, (2 * p + 1) * D_I:(2 * p + 2) * D_I]], axis=0).astype(BF16)
             for p in range(H_I // 2)]
    tpos = i * tq + lax.broadcasted_iota(jnp.int32, (1, tq), 1)

    def score_chunk(c, carry):
        r0 = pl.multiple_of(c * sc, sc)
        kic = tk_ref[pl.ds(r0, sc), :].astype(BF16)
        acc = jnp.zeros((sc, tq), F32)
        for p in range(H_I // 2):
            d = _dot_nt(kic, pairs[p])
            acc = acc + jnp.maximum(d[:, :tq], 0.0) * wt[2 * p:2 * p + 1, :]
            acc = acc + jnp.maximum(d[:, tq:], 0.0) * wt[2 * p + 1:2 * p + 2, :]
        kpos = r0 + lax.broadcasted_iota(jnp.int32, (sc, tq), 0)
        sc_scr[pl.ds(r0, sc), :] = jnp.where(kpos <= tpos, acc, -jnp.inf)
        return carry

    lax.fori_loop(0, nck, score_chunk, 0)

    def minmax_chunk(c, carry):
        mn, mx = carry
        x = sc_scr[pl.ds(pl.multiple_of(c * sc, sc), sc), :]
        mx = jnp.maximum(mx, jnp.max(x, axis=0, keepdims=True))
        mn = jnp.minimum(mn, jnp.min(jnp.where(x == -jnp.inf, jnp.inf, x), axis=0, keepdims=True))
        return mn, mx

    mn, mx = lax.fori_loop(0, nck, minmax_chunk,
                           (jnp.full((1, tq), jnp.inf, F32), jnp.full((1, tq), -jnp.inf, F32)))

    def count_fn(mid):
        def body(c, carry):
            cgt, cge = carry
            x = sc_scr[pl.ds(pl.multiple_of(c * sc, sc), sc), :]
            cgt = cgt + jnp.sum(jnp.where(x > mid, 1.0, 0.0), axis=0, keepdims=True)
            cge = cge + jnp.sum(jnp.where(x >= mid, 1.0, 0.0), axis=0, keepdims=True)
            return cgt, cge
        z = jnp.zeros((1, tq), F32)
        return lax.fori_loop(0, nck, body, (z, z))

    thr = _select_threshold(count_fn, mn, mx, tpos + 1 > topk, topk, (1, tq))

    m_scr[...] = jnp.full(m_scr.shape, NEG_BIG, F32)
    l_scr[...] = jnp.zeros(l_scr.shape, F32)
    acc_scr[...] = jnp.zeros(acc_scr.shape, F32)
    q = qb_ref[...].astype(BF16)
    scale = HEAD_DIM_B ** -0.5

    def attn_chunk(c, carry):
        r0 = pl.multiple_of(c * sc, sc)
        sel = sc_scr[pl.ds(r0, sc), :] >= thr
        for n in range(N_KV_B):
            kc = kb_ref[pl.ds(r0, sc), n * HEAD_DIM_B:(n + 1) * HEAD_DIM_B].astype(BF16)
            vc = vb_ref[pl.ds(r0, sc), n * HEAD_DIM_B:(n + 1) * HEAD_DIM_B].astype(BF16)
            for g in range(GROUP_B):
                h = n * GROUP_B + g
                hs = slice(h * HEAD_DIM_B, (h + 1) * HEAD_DIM_B)
                s = jnp.where(sel, _dot_nt(kc, q[:, hs]) * scale, NEG_BIG)
                m_old = m_scr[h:h + 1, :]
                m_new = jnp.maximum(m_old, jnp.max(s, axis=0, keepdims=True))
                pexp = jnp.exp(s - m_new)
                alpha = jnp.exp(m_old - m_new)
                l_scr[h:h + 1, :] = alpha * l_scr[h:h + 1, :] + jnp.sum(pexp, axis=0, keepdims=True)
                acc_scr[hs, :] = alpha * acc_scr[hs, :] + _dot_tn(vc, pexp.astype(BF16))
                m_scr[h:h + 1, :] = m_new
        return carry

    lax.fori_loop(0, nck, attn_chunk, 0)

    for h in range(H_B):
        hs = slice(h * HEAD_DIM_B, (h + 1) * HEAD_DIM_B)
        ot = acc_scr[hs, :] / l_scr[h:h + 1, :]
        o_ref[:, hs] = ot.T.astype(o_ref.dtype)


def _dsa_prompt(p, k_new, v_new, ki_new, nseq, s):
    tq, sc = Q_BLOCK, min(KEY_CHUNK, s)
    topk = min(TOPK_MAX, s // 4)
    nqb = s // tq
    qi0 = PD_OFF["qi"] // (H_I * D_I)
    qb0 = PD_OFF["qb"] // WIDTH_B
    tl0 = TAIL_OFF // LANES
    return pl.pallas_call(
        functools.partial(_dsa_prompt_body, tq=tq, sc=sc, topk=topk),
        grid=(nseq, nqb),
        in_specs=[pl.BlockSpec((tq, H_I * D_I), lambda b, i: (b * nqb + i, qi0)),
                  pl.BlockSpec((tq, LANES), lambda b, i: (b * nqb + i, tl0)),
                  pl.BlockSpec((tq, WIDTH_B), lambda b, i: (b * nqb + i, qb0)),
                  pl.BlockSpec((s, KV_WIDTH_B), lambda b, i: (b, 0)),
                  pl.BlockSpec((s, KV_WIDTH_B), lambda b, i: (b, 0)),
                  pl.BlockSpec((s, D_I), lambda b, i: (b, 0))],
        out_specs=pl.BlockSpec((tq, WIDTH_B), lambda b, i: (b * nqb + i, 0)),
        out_shape=jax.ShapeDtypeStruct((nseq * s, WIDTH_B), BF16),
        scratch_shapes=[pltpu.VMEM((s, tq), F32), pltpu.VMEM((H_B, tq), F32),
                        pltpu.VMEM((H_B, tq), F32), pltpu.VMEM((WIDTH_B, tq), F32)],
        compiler_params=_cparams("parallel", "arbitrary"),
        name="dsa_prompt",
    )(p, p, p, k_new, v_new, ki_new)


PAGES_PER_STEP = 8


def _dsa_sample_scores_body(pt_ref, q_ref, w_ref, *rest, nsteps, pps, t):
    page_refs, new_ref, o_ref, onew_ref = rest[:pps], rest[pps], rest[pps + 1], rest[pps + 2]
    j = pl.program_id(1)
    q = q_ref[...].astype(BF16)
    wv = w_ref[...] * ((D_I ** -0.5) * (H_I ** -0.5))

    def scores(keys):
        d = _dot_nt(q, keys.astype(BF16))
        return jnp.sum((jnp.maximum(d, 0.0) * wv).reshape(t, H_I, PAGE_SIZE), axis=1)

    @pl.when(j < nsteps)
    def _():
        o_ref[...] = jnp.concatenate([scores(r[...]) for r in page_refs], axis=1)

    @pl.when(j == nsteps)
    def _():
        knew = lax.broadcasted_iota(jnp.int32, (t, PAGE_SIZE), 1)
        tnew = lax.broadcasted_iota(jnp.int32, (t, PAGE_SIZE), 0)
        onew_ref[...] = jnp.where(knew <= tnew, scores(new_ref[...]), -jnp.inf)


def _page_specs(block, layer, nsteps, pps):
    def spec(p):
        def index(b, j, pt):
            return (layer, pt[b, jnp.minimum(j, nsteps - 1) * pps + p]) + (0,) * (len(block) - 2)
        return pl.BlockSpec(block, index)
    return [spec(p) for p in range(pps)]


def _dsa_sample_scores(page_table, q, wv, cache_kidx, layer, ki_new):
    nseq, npages = page_table.shape
    t = q.shape[1] // H_I
    pps = math.gcd(PAGES_PER_STEP, npages)
    nsteps = npages // pps
    return pl.pallas_call(
        functools.partial(_dsa_sample_scores_body, nsteps=nsteps, pps=pps, t=t),
        grid_spec=pltpu.PrefetchScalarGridSpec(
            num_scalar_prefetch=1,
            grid=(nseq, nsteps + 1),
            in_specs=[pl.BlockSpec((None, t * H_I, D_I), lambda b, j, pt: (b, 0, 0)),
                      pl.BlockSpec((None, t * H_I, PAGE_SIZE), lambda b, j, pt: (b, 0, 0)),
                      *_page_specs((None, None, PAGE_SIZE, D_I), layer, nsteps, pps),
                      pl.BlockSpec((None, PAGE_SIZE, D_I), lambda b, j, pt: (b, 0, 0))],
            out_specs=[pl.BlockSpec((None, t, pps * PAGE_SIZE),
                                    lambda b, j, pt: (b, 0, jnp.minimum(j, nsteps - 1))),
                       pl.BlockSpec((None, t, PAGE_SIZE), lambda b, j, pt: (b, 0, 0))],
        ),
        out_shape=[jax.ShapeDtypeStruct((nseq, t, npages * PAGE_SIZE), F32),
                   jax.ShapeDtypeStruct((nseq, t, PAGE_SIZE), F32)],
        compiler_params=_cparams("parallel", "arbitrary"),
        name="dsa_sample_scores",
    )(page_table, q, wv, *([cache_kidx] * pps), ki_new)


def _dsa_sample_attn_body(pt_ref, sc_ref, scn_ref, q_ref, *rest, nsteps, pps, t, past, topk):
    kpage_refs, vpage_refs = rest[:pps], rest[pps:2 * pps]
    knew_ref, vnew_ref, o_ref, thr_scr, m_scr, l_scr, acc_scr = rest[2 * pps:]
    j = pl.program_id(1)

    @pl.when(j == 0)
    def _():
        xp, xn = sc_ref[...], scn_ref[...]
        mx = jnp.maximum(jnp.max(xp, axis=1, keepdims=True), jnp.max(xn, axis=1, keepdims=True))
        mn = jnp.minimum(jnp.min(xp, axis=1, keepdims=True),
                         jnp.min(jnp.where(xn == -jnp.inf, jnp.inf, xn), axis=1, keepdims=True))

        def count_fn(mid):
            cnt = lambda hit: jnp.sum(jnp.where(hit, 1.0, 0.0), axis=1, keepdims=True)
            a, b = sc_ref[...], scn_ref[...]
            return cnt(a > mid) + cnt(b > mid), cnt(a >= mid) + cnt(b >= mid)

        tpos = past + lax.broadcasted_iota(jnp.int32, (t, 1), 0)
        thr = _select_threshold(count_fn, mn, mx, tpos + 1 > topk, topk, (t, 1))
        thr_scr[...] = jnp.broadcast_to(thr, thr_scr.shape)
        m_scr[...] = jnp.full(m_scr.shape, NEG_BIG, F32)
        l_scr[...] = jnp.zeros(l_scr.shape, F32)
        acc_scr[...] = jnp.zeros(acc_scr.shape, F32)

    scale = HEAD_DIM_B ** -0.5

    def attend(x, keys, vals):
        sel_t = jnp.where(x >= thr_scr[:, 0:1], 1.0, 0.0)
        sel = jnp.concatenate([sel_t] * GROUP_B, axis=0) > 0.5
        for n in range(N_KV_B):
            s = jnp.where(sel, _dot_nt(q_ref[n].astype(BF16), keys(n).astype(BF16)) * scale, NEG_BIG)
            m_old = m_scr[n]
            m_new = jnp.maximum(m_old, jnp.max(s, axis=1, keepdims=True))
            pexp = jnp.exp(s - m_new)
            alpha = jnp.exp(m_old - m_new)
            l_scr[n] = alpha * l_scr[n] + jnp.sum(pexp, axis=1, keepdims=True)
            acc_scr[n] = alpha * acc_scr[n] + _dot(pexp.astype(BF16), vals(n).astype(BF16))
            m_scr[n] = m_new

    @pl.when(j < nsteps)
    def _():
        width = pps * PAGE_SIZE
        x = sc_ref[:, pl.ds(pl.multiple_of(j * width, width), width)]
        attend(x,
               lambda n: jnp.concatenate([r[:, n, :] for r in kpage_refs], axis=0),
               lambda n: jnp.concatenate([r[:, n, :] for r in vpage_refs], axis=0))

    @pl.when(j == nsteps)
    def _():
        hs = lambda n: slice(n * HEAD_DIM_B, (n + 1) * HEAD_DIM_B)
        attend(scn_ref[...], lambda n: knew_ref[:, hs(n)], lambda n: vnew_ref[:, hs(n)])
        for n in range(N_KV_B):
            o_ref[n] = (acc_scr[n] / l_scr[n]).astype(o_ref.dtype)


def _dsa_sample_attn(page_table, scores, scores_new, q, cache_k, cache_v, layer, k_new, v_new):
    nseq, npages = page_table.shape
    t = scores.shape[1]
    past = npages * PAGE_SIZE
    topk = min(TOPK_MAX, (past + t) // 4)
    rows = GROUP_B * t
    pps = math.gcd(PAGES_PER_STEP, npages)
    nsteps = npages // pps
    page_block = (None, None, PAGE_SIZE, N_KV_B, HEAD_DIM_B)
    const3 = lambda b, j, pt: (b, 0, 0)
    return pl.pallas_call(
        functools.partial(_dsa_sample_attn_body, nsteps=nsteps, pps=pps, t=t, past=past, topk=topk),
        grid_spec=pltpu.PrefetchScalarGridSpec(
            num_scalar_prefetch=1,
            grid=(nseq, nsteps + 1),
            in_specs=[pl.BlockSpec((None, t, past), const3),
                      pl.BlockSpec((None, t, PAGE_SIZE), const3),
                      pl.BlockSpec((None, N_KV_B, rows, HEAD_DIM_B), lambda b, j, pt: (b, 0, 0, 0)),
                      *_page_specs(page_block, layer, nsteps, pps),
                      *_page_specs(page_block, layer, nsteps, pps),
                      pl.BlockSpec((None, PAGE_SIZE, KV_WIDTH_B), const3),
                      pl.BlockSpec((None, PAGE_SIZE, KV_WIDTH_B), const3)],
            out_specs=pl.BlockSpec((None, N_KV_B, rows, HEAD_DIM_B), lambda b, j, pt: (b, 0, 0, 0)),
            scratch_shapes=[pltpu.VMEM((t, PAGE_SIZE), F32),
                            pltpu.VMEM((N_KV_B, rows, 1), F32),
                            pltpu.VMEM((N_KV_B, rows, 1), F32),
                            pltpu.VMEM((N_KV_B, rows, HEAD_DIM_B), F32)],
        ),
        out_shape=jax.ShapeDtypeStruct((nseq, N_KV_B, rows, HEAD_DIM_B), BF16),
        compiler_params=_cparams("parallel", "arbitrary"),
        name="dsa_sample_attn",
    )(page_table, scores, scores_new, q, *([cache_k] * pps), *([cache_v] * pps), k_new, v_new)


def _dsa_sample(ps, k_s, v_s, ki_s, page_table, cache_k, cache_v, cache_kidx, layer):
    nseq = page_table.shape[0]
    t = ps.shape[0] // nseq
    seg = lambda name, width: ps[:, PD_OFF[name]:PD_OFF[name] + width]
    qi = seg("qi", H_I * D_I).reshape(nseq, t * H_I, D_I)
    wi = seg("wi", H_I).reshape(nseq, t * H_I, 1)
    wv = jnp.broadcast_to(wi, (nseq, t * H_I, PAGE_SIZE))
    pad_rows = lambda a: jnp.pad(a.reshape(nseq, t, -1), ((0, 0), (0, PAGE_SIZE - t), (0, 0)))
    scores, scores_new = _dsa_sample_scores(page_table, qi, wv, cache_kidx, layer, pad_rows(ki_s))
    q = seg("qb", WIDTH_B).reshape(nseq, t, N_KV_B, GROUP_B, HEAD_DIM_B)
    q = q.transpose(0, 2, 3, 1, 4).reshape(nseq, N_KV_B, GROUP_B * t, HEAD_DIM_B)
    o = _dsa_sample_attn(page_table, scores, scores_new, q, cache_k, cache_v, layer,
                         pad_rows(k_s), pad_rows(v_s))
    o = o.reshape(nseq, N_KV_B, GROUP_B, t, HEAD_DIM_B).transpose(0, 3, 1, 2, 4)
    return o.reshape(nseq * t, WIDTH_B)


def _prep_w_mid(w_in):
    sizes = dict(zip(IN_NAMES, IN_SIZES))
    cols = [w_in[:, :, SRC_OFF[name]:SRC_OFF[name] + sizes[name]].astype(BF16) for name in MID_ORDER]
    cols.append(jnp.zeros(w_in.shape[:2] + (PD_WIDTH - sum(sizes[n] for n in MID_ORDER),), BF16))
    return jnp.concatenate(cols, axis=2)


def kernel(x_prompt, x_sample, cache_k, cache_v, cache_kidx, page_table, state_gdn, state_gdn_conv,
           state_sconv, final_norm, norm1, norm2, w_in, conv_a, a_log, dt_bias, gdn_norm, conv_c,
           w_branch_a, w_branch_b, w_branch_c, w_o, w_gate, w_up, w_down):
    bp, tp, d = x_prompt.shape
    bs, ts = x_sample.shape[:2]
    mp, ms = bp * tp, bs * ts
    x = jnp.concatenate([x_prompt.reshape(mp, d), x_sample.reshape(ms, d)], axis=0)
    new_p = [[] for _ in range(6)]
    new_s = [[] for _ in range(6)]
    w_mid = _prep_w_mid(w_in)
    w_down16 = w_down.astype(BF16)
    for l in range(DEPTH):
        xn = _rmsnorm(x, norm1[l], BF16)
        p = _in_proj_main(xn, w_in, l)
        pd, k_new, v_new, ki_new = _in_proj_mid(xn, w_mid, l)
        g, beta = _gdn_gates(pd, a_log[l], dt_bias[l])
        conv_w = conv_a[l].astype(F32)

        oa_p, gdn_p = _gdn(p, 0, bp, tp, min(GDN_CHUNK, tp), conv_w, g[:mp], beta[:mp],
                           jnp.zeros((bp, CONV_A - 1, 3 * WIDTH_A), F32),
                           jnp.zeros((bp, H_A, DK_A, DV_A), F32), gdn_norm[l])
        oa_s, gdn_s = _gdn(p, mp, bs, ts, ts, conv_w, g[mp:], beta[mp:],
                           state_gdn_conv[l], state_gdn[l], gdn_norm[l])
        ob_p = _dsa_prompt(pd, k_new, v_new, ki_new, bp, tp)
        ob_s = _dsa_sample(pd[mp:], k_new[mp:], v_new[mp:], ki_new[mp:], page_table,
                           cache_k, cache_v, cache_kidx, l)
        cw = conv_c[l].astype(F32)
        oc_p, sconv_p = _sconv(p, 0, bp, tp, cw, jnp.zeros((bp, CONV_C - 1, WIDTH_C), F32))
        oc_s, sconv_s = _sconv(p, mp, bs, ts, cw, state_sconv[l])

        oa = jnp.concatenate([oa_p, oa_s], axis=0)
        ob = jnp.concatenate([ob_p, ob_s], axis=0)
        oc = jnp.concatenate([oc_p, oc_s], axis=0)
        merged = _merge(oa, ob, oc, w_branch_a, w_branch_b, w_branch_c, l, p)
        x = _out_proj(merged, w_o, l, x)

        hn = _rmsnorm(x, norm2[l], BF16)
        h = _swiglu(hn, w_gate, w_up, l)
        x = _ffn_down(h, w_down16, l, x)

        nb = CONV_A - 1
        tail_rows = lambda r0, t: lax.slice(p, (r0 + max(t - nb, 0), 0), (r0 + t, 3 * WIDTH_A))
        gconv_p = jnp.stack([tail_rows(b * tp, tp) for b in range(bp)])
        gconv_s = jnp.stack([tail_rows(mp + b * ts, ts) for b in range(bs)])
        if tp < nb:
            gconv_p = jnp.concatenate([jnp.zeros((bp, nb - tp, 3 * WIDTH_A), F32), gconv_p], axis=1)
        if ts < nb:
            gconv_s = jnp.concatenate([state_gdn_conv[l].astype(F32)[:, ts:], gconv_s], axis=1)
        kv = lambda a, rows, b, t: a[rows].reshape(b, t, N_KV_B, HEAD_DIM_B)
        rp, rs = slice(0, mp), slice(mp, mp + ms)
        for lst, val in zip(new_p, (kv(k_new, rp, bp, tp), kv(v_new, rp, bp, tp),
                                    ki_new[rp].reshape(bp, tp, D_I), gdn_p, gconv_p, sconv_p)):
            lst.append(val)
        for lst, val in zip(new_s, (kv(k_new, rs, bs, ts), kv(v_new, rs, bs, ts),
                                    ki_new[rs].reshape(bs, ts, D_I), gdn_s, gconv_s, sconv_s)):
            lst.append(val)

    y_prompt = _rmsnorm(x, final_norm, F32, 0, mp).reshape(bp, tp, d)
    y_sample = _rmsnorm(x, final_norm, F32, mp, ms).reshape(bs, ts, d)
    outs_p = [jnp.stack(a) for a in new_p]
    outs_s = [jnp.stack(a) for a in new_s]
    return (y_prompt, y_sample, *outs_p, *outs_s)
```

```python
import functools
import math

import jax
import jax.numpy as jnp
from jax import lax
from jax.experimental import pallas as pl
from jax.experimental.pallas import tpu as pltpu

F32 = jnp.float32
BF16 = jnp.bfloat16

D_MODEL = 4096
DEPTH = 2
PAGE_SIZE = 128
H_A = 16
DK_A = 128
DV_A = 128
WIDTH_A = H_A * DV_A
CONV_A = 4
GDN_CHUNK = 64
H_B = 8
N_KV_B = 2
GROUP_B = H_B // N_KV_B
HEAD_DIM_B = 128
WIDTH_B = H_B * HEAD_DIM_B
KV_WIDTH_B = N_KV_B * HEAD_DIM_B
H_I = 32
D_I = 64
TOPK_MAX = 256
Q_BLOCK = 128
WIDTH_C = 1024
CONV_C = 3
D_FF = -(-8 * D_MODEL // (3 * 256)) * 256
EPS = 1e-6

IN_NAMES = ("qa", "ka", "va", "za", "aa", "ba", "qb", "kb", "vb", "qi", "ki", "wi",
            "gate_b", "gate_c", "hc", "ga", "gb", "gc")
IN_SIZES = (WIDTH_A, WIDTH_A, WIDTH_A, WIDTH_A, H_A, H_A,
            WIDTH_B, KV_WIDTH_B, KV_WIDTH_B, H_I * D_I, D_I, H_I,
            WIDTH_C, WIDTH_C, WIDTH_C, D_MODEL, D_MODEL, D_MODEL)

LANES = 128
N_BLOCK = 512
VMEM_LIMIT = 56 * 1024 * 1024
NEG_BIG = -1e30

MAIN_ORDER = ("qa", "ka", "va", "za", "gate_b", "gate_c", "hc", "ga", "gb", "gc")
MID_ORDER = ("qi", "qb", "kb", "vb", "ki", "aa", "ba", "wi")


def _offsets(order):
    sizes = dict(zip(IN_NAMES, IN_SIZES))
    off, out = 0, {}
    for name in order:
        out[name] = off
        off += sizes[name]
    return out, off


SRC_OFF, _ = _offsets(IN_NAMES)
PM_OFF, PM_WIDTH = _offsets(MAIN_ORDER)
PD_OFF, _mid_cols = _offsets(MID_ORDER)
PD_WIDTH = -(-_mid_cols // N_BLOCK) * N_BLOCK
MAIN_RUN0 = PM_OFF["gate_b"]
MAIN_SHIFT = SRC_OFF["gate_b"] - MAIN_RUN0
assert PM_WIDTH % N_BLOCK == 0 and MAIN_RUN0 % N_BLOCK == 0 and MAIN_SHIFT % LANES == 0
assert all(SRC_OFF[n] == PM_OFF[n] for n in MAIN_ORDER[:4])
assert all(SRC_OFF[n] == PM_OFF[n] + MAIN_SHIFT for n in MAIN_ORDER[4:])
TAIL_OFF = PD_OFF["ki"]
TAIL_AA = PD_OFF["aa"] - TAIL_OFF
TAIL_BA = PD_OFF["ba"] - TAIL_OFF
TAIL_WI = PD_OFF["wi"] - TAIL_OFF
assert TAIL_OFF % N_BLOCK == 0 and PD_OFF["wi"] + H_I - TAIL_OFF == LANES


def _cparams(*sem):
    return pltpu.CompilerParams(dimension_semantics=sem, vmem_limit_bytes=VMEM_LIMIT)


def _dot(a, b):
    return jnp.dot(a, b, preferred_element_type=F32)


def _dot_nt(a, b):
    return lax.dot_general(a, b, (((1,), (1,)), ((), ())), preferred_element_type=F32)


def _dot_tn(a, b):
    return lax.dot_general(a, b, (((0,), (0,)), ((), ())), preferred_element_type=F32)


def _dot_hi(a, b):
    return jnp.dot(a, b, preferred_element_type=F32, precision=lax.Precision.HIGHEST)


def _sigmoid(x):
    return jax.nn.sigmoid(x)


def _silu(x):
    return x * jax.nn.sigmoid(x)


def _rmsnorm_body(x_ref, g_ref, o_ref):
    x = x_ref[...]
    ms = jnp.mean(x * x, axis=-1, keepdims=True)
    o_ref[...] = (x * lax.rsqrt(ms + EPS) * g_ref[...]).astype(o_ref.dtype)


def _row_block(m):
    for bm in (192, 128, 64, 32, 16, 8):
        if m % bm == 0:
            return bm
    raise ValueError(f"unsupported row count {m}")


def _rmsnorm(x, g, out_dtype, row0=0, nrows=None):
    d = x.shape[1]
    m = x.shape[0] - row0 if nrows is None else nrows
    bm = _row_block(math.gcd(m, row0) if row0 else m)
    rb0 = row0 // bm
    return pl.pallas_call(
        _rmsnorm_body,
        grid=(m // bm,),
        in_specs=[pl.BlockSpec((bm, d), lambda i: (rb0 + i, 0)), pl.BlockSpec((1, d), lambda i: (0, 0))],
        out_specs=pl.BlockSpec((bm, d), lambda i: (i, 0)),
        out_shape=jax.ShapeDtypeStruct((m, d), out_dtype),
        compiler_params=_cparams("parallel"),
        name="rmsnorm",
    )(x, g.reshape(1, d).astype(F32))


def _gdn_gates_body(t_ref, alog_ref, dtb_ref, g_ref, b_ref):
    t = t_ref[...]
    aa = t[:, TAIL_AA:TAIL_AA + H_A]
    ba = t[:, TAIL_BA:TAIL_BA + H_A]
    x = aa + dtb_ref[...]
    softplus = jnp.maximum(x, 0.0) + jnp.log1p(jnp.exp(-jnp.abs(x)))
    g_ref[...] = -jnp.exp(alog_ref[...]) * softplus
    b_ref[...] = _sigmoid(ba)


def _gdn_gates(p, a_log, dt_bias):
    m = p.shape[0]
    bm = _row_block(m)
    tail_blk = TAIL_OFF // LANES
    return pl.pallas_call(
        _gdn_gates_body,
        grid=(m // bm,),
        in_specs=[pl.BlockSpec((bm, LANES), lambda i: (i, tail_blk)),
                  pl.BlockSpec((1, H_A), lambda i: (0, 0)),
                  pl.BlockSpec((1, H_A), lambda i: (0, 0))],
        out_specs=[pl.BlockSpec((bm, H_A), lambda i: (i, 0)), pl.BlockSpec((bm, H_A), lambda i: (i, 0))],
        out_shape=[jax.ShapeDtypeStruct((m, H_A), F32)] * 2,
        compiler_params=_cparams("parallel"),
        name="gdn_gates",
    )(p, a_log.reshape(1, H_A).astype(F32), dt_bias.reshape(1, H_A).astype(F32))


def _big_row_block(m):
    for bm in (1376, 1024, 688, 512, 256, 128, 64):
        if m % bm == 0:
            return bm
    raise ValueError(f"unsupported row count {m}")


def _out_proj_body(a_ref, w_ref, r_ref, o_ref):
    o_ref[...] = r_ref[...] + _dot(a_ref[...], w_ref[...].astype(BF16))


def _out_proj(a, w, layer, residual):
    m, k = a.shape
    n = w.shape[2]
    bm, bn = _big_row_block(m), N_BLOCK // 2
    return pl.pallas_call(
        _out_proj_body,
        grid=(m // bm, n // bn),
        in_specs=[pl.BlockSpec((bm, k), lambda i, j: (i, 0)),
                  pl.BlockSpec((None, k, bn), lambda i, j: (layer, 0, j)),
                  pl.BlockSpec((bm, bn), lambda i, j: (i, j))],
        out_specs=pl.BlockSpec((bm, bn), lambda i, j: (i, j)),
        out_shape=jax.ShapeDtypeStruct((m, n), F32),
        compiler_params=_cparams("parallel", "arbitrary"),
        name="out_proj",
    )(a, w, residual)


IN_PROJ_PIECES = N_BLOCK // LANES


def _in_proj_main_body(a_ref, *rest):
    w_refs, p_ref = rest[:IN_PROJ_PIECES], rest[IN_PROJ_PIECES]
    w = jnp.concatenate([r[...].astype(BF16) for r in w_refs], axis=1)
    p_ref[...] = _dot(a_ref[...], w)


def _in_proj_main(a, w_in, layer):
    m, k = a.shape
    bm, bn = _big_row_block(m), N_BLOCK
    run0_blk, shift_blk = MAIN_RUN0 // bn, MAIN_SHIFT // LANES

    def piece(q):
        def index(i, j):
            return (layer, 0, IN_PROJ_PIECES * j + q + jnp.where(j >= run0_blk, shift_blk, 0))
        return pl.BlockSpec((None, k, LANES), index)

    return pl.pallas_call(
        _in_proj_main_body,
        grid=(m // bm, PM_WIDTH // bn),
        in_specs=[pl.BlockSpec((bm, k), lambda i, j: (i, 0))] + [piece(q) for q in range(IN_PROJ_PIECES)],
        out_specs=pl.BlockSpec((bm, bn), lambda i, j: (i, j)),
        out_shape=jax.ShapeDtypeStruct((m, PM_WIDTH), F32),
        compiler_params=_cparams("parallel", "arbitrary"),
        name="in_proj_main",
    )(a, *([w_in] * IN_PROJ_PIECES))


def _in_proj_mid_body(a_ref, w_ref, p_ref, k_ref, v_ref, ki_ref, *, jkv, jtail):
    acc = _dot(a_ref[...], w_ref[...])
    p_ref[...] = acc
    j = pl.program_id(1)

    @pl.when(j == jkv)
    def _():
        k_ref[...] = acc[:, :KV_WIDTH_B]
        v_ref[...] = acc[:, KV_WIDTH_B:2 * KV_WIDTH_B]

    @pl.when(j == jtail)
    def _():
        ki_ref[...] = acc[:, :D_I]


def _in_proj_mid(a, w_mid, layer):
    m, k = a.shape
    bm, bn = _big_row_block(m), N_BLOCK
    assert PD_OFF["kb"] % bn == 0 and PD_OFF["vb"] == PD_OFF["kb"] + KV_WIDTH_B
    row = lambda i, j: (i, 0)
    return pl.pallas_call(
        functools.partial(_in_proj_mid_body, jkv=PD_OFF["kb"] // bn, jtail=TAIL_OFF // bn),
        grid=(m // bm, PD_WIDTH // bn),
        in_specs=[pl.BlockSpec((bm, k), row), pl.BlockSpec((None, k, bn), lambda i, j: (layer, 0, j))],
        out_specs=[pl.BlockSpec((bm, bn), lambda i, j: (i, j)),
                   pl.BlockSpec((bm, KV_WIDTH_B), row), pl.BlockSpec((bm, KV_WIDTH_B), row),
                   pl.BlockSpec((bm, D_I), row)],
        out_shape=[jax.ShapeDtypeStruct((m, PD_WIDTH), F32), jax.ShapeDtypeStruct((m, KV_WIDTH_B), F32),
                   jax.ShapeDtypeStruct((m, KV_WIDTH_B), F32), jax.ShapeDtypeStruct((m, D_I), F32)],
        compiler_params=_cparams("parallel", "arbitrary"),
        name="in_proj_mid",
    )(a, w_mid)


def _mm_k_res_body(a_ref, w_ref, r_ref, o_ref, acc_ref):
    kk = pl.program_id(2)

    @pl.when(kk == 0)
    def _():
        acc_ref[...] = r_ref[...]

    acc_ref[...] += _dot(a_ref[...], w_ref[...].astype(BF16))

    @pl.when(kk == pl.num_programs(2) - 1)
    def _():
        o_ref[...] = acc_ref[...]


def _ffn_down(a, w16, layer, residual):
    m, k = a.shape
    n = w16.shape[2]
    bm, bn, bk = _big_row_block(m), N_BLOCK // 2, k // 2
    assert bk % LANES == 0
    return pl.pallas_call(
        _mm_k_res_body,
        grid=(m // bm, n // bn, k // bk),
        in_specs=[pl.BlockSpec((bm, bk), lambda i, j, q: (i, q)),
                  pl.BlockSpec((None, bk, bn), lambda i, j, q: (layer, q, j)),
                  pl.BlockSpec((bm, bn), lambda i, j, q: (i, j))],
        out_specs=pl.BlockSpec((bm, bn), lambda i, j, q: (i, j)),
        out_shape=jax.ShapeDtypeStruct((m, n), F32),
        scratch_shapes=[pltpu.VMEM((bm, bn), F32)],
        compiler_params=_cparams("parallel", "arbitrary", "arbitrary"),
        name="ffn_down",
    )(a, w16, residual)


def _swiglu_body(a_ref, wg_ref, wu_ref, o_ref):
    a = a_ref[...]
    g = _dot(a, wg_ref[...].astype(BF16))
    u = _dot(a, wu_ref[...].astype(BF16))
    o_ref[...] = (_silu(g) * u).astype(o_ref.dtype)


def _swiglu(a, w_gate, w_up, layer):
    m, k = a.shape
    n = w_gate.shape[2]
    bm, bn = _big_row_block(m), N_BLOCK // 2
    assert n % bn == 0
    wspec = pl.BlockSpec((None, k, bn), lambda i, j: (layer, 0, j))
    return pl.pallas_call(
        _swiglu_body,
        grid=(m // bm, n // bn),
        in_specs=[pl.BlockSpec((bm, k), lambda i, j: (i, 0)), wspec, wspec],
        out_specs=pl.BlockSpec((bm, bn), lambda i, j: (i, j)),
        out_shape=jax.ShapeDtypeStruct((m, n), BF16),
        compiler_params=_cparams("parallel", "arbitrary"),
        name="swiglu",
    )(a, w_gate, w_up)


def _merge_body(oa_ref, ob_ref, oc_ref, wa_ref, wb_ref, wc_ref, ga_ref, gb_ref, gc_ref, o_ref):
    acc = _sigmoid(ga_ref[...]) * _dot(oa_ref[...], wa_ref[...].astype(BF16))
    acc += _sigmoid(gb_ref[...]) * _dot(ob_ref[...], wb_ref[...].astype(BF16))
    acc += _sigmoid(gc_ref[...]) * _dot(oc_ref[...], wc_ref[...].astype(BF16))
    o_ref[...] = acc.astype(o_ref.dtype)


def _merge(oa, ob, oc, wa, wb, wc, layer, p):
    m = oa.shape[0]
    n = wa.shape[2]
    bm, bn = _big_row_block(m), N_BLOCK // 2
    ga0, gb0, gc0 = (PM_OFF[s] // bn for s in ("ga", "gb", "gc"))
    row = lambda i, j: (i, 0)
    wspec = lambda w: pl.BlockSpec((None, w.shape[1], bn), lambda i, j: (layer, 0, j))
    return pl.pallas_call(
        _merge_body,
        grid=(m // bm, n // bn),
        in_specs=[pl.BlockSpec((bm, oa.shape[1]), row), pl.BlockSpec((bm, ob.shape[1]), row),
                  pl.BlockSpec((bm, oc.shape[1]), row),
                  wspec(wa), wspec(wb), wspec(wc),
                  pl.BlockSpec((bm, bn), lambda i, j: (i, ga0 + j)),
                  pl.BlockSpec((bm, bn), lambda i, j: (i, gb0 + j)),
                  pl.BlockSpec((bm, bn), lambda i, j: (i, gc0 + j))],
        out_specs=pl.BlockSpec((bm, bn), lambda i, j: (i, j)),
        out_shape=jax.ShapeDtypeStruct((m, n), BF16),
        compiler_params=_cparams("parallel", "arbitrary"),
        name="merge",
    )(oa, ob, oc, wa, wb, wc, p, p, p)


GDN_HEADS_PER_STEP = 8


def _gdn_body(q_ref, k_ref, v_ref, z_ref, wq_ref, wk_ref, wv_ref, tq_ref, tk_ref, tv_ref,
              grow_ref, gcol_ref, bcol_ref, s0_ref, ng_ref, o_ref, sout_ref, s_scr, tail_scr,
              *, chunk, hg):
    c = chunk
    w = hg * DK_A
    hgi = pl.program_id(1)
    n = pl.program_id(2)

    @pl.when(n == 0)
    def _():
        s_scr[...] = s0_ref[...].astype(F32)
        tail_scr[:, 0:w] = tq_ref[...]
        tail_scr[:, w:2 * w] = tk_ref[...]
        tail_scr[:, 2 * w:3 * w] = tv_ref[...]

    act = []
    for idx, (x_ref, w_ref) in enumerate(((q_ref, wq_ref), (k_ref, wk_ref), (v_ref, wv_ref))):
        x = x_ref[...]
        cw = w_ref[...]
        xfull = jnp.concatenate([tail_scr[:, idx * w:(idx + 1) * w], x], axis=0)
        y = x * cw[CONV_A - 1:CONV_A, :]
        for j in range(1, CONV_A):
            y = y + pltpu.roll(xfull, j, axis=0)[8:, :] * cw[CONV_A - 1 - j:CONV_A - j, :]
        tail_scr[:, idx * w:(idx + 1) * w] = xfull[c:c + 8, :]
        act.append(_silu(y))
    qs, ks, vs = act
    z = z_ref[...]

    ii = lax.broadcasted_iota(jnp.int32, (c, c), 0)
    jj = lax.broadcasted_iota(jnp.int32, (c, c), 1)
    incl = jj <= ii
    strict = jj < ii
    eye = (ii == jj).astype(F32)
    n_double = int(math.log2(c)) - 1
    ng = ng_ref[...]

    heads = range(hg)
    hsl = [slice(hh * DK_A, (hh + 1) * DK_A) for hh in heads]
    q_l, k_l, k16_l, decay_l, egc_l, kdec_l, glast_l, pw_l, tinv_l, rhs_l = ([] for _ in range(10))
    for hh in heads:
        q = qs[:, hsl[hh]]
        k = ks[:, hsl[hh]]
        q = q * lax.rsqrt(jnp.sum(q * q, axis=-1, keepdims=True) + EPS) * (DK_A ** -0.5)
        k = k * lax.rsqrt(jnp.sum(k * k, axis=-1, keepdims=True) + EPS)
        gr = grow_ref[pl.ds(hgi * hg + hh, 1), :]
        gc = gcol_ref[hh]
        bc = bcol_ref[hh]
        gcum_c = jnp.sum(jnp.where(incl, gr, 0.0), axis=1, keepdims=True)
        gcum_r = jnp.sum(jnp.where(ii <= jj, gc, 0.0), axis=0, keepdims=True)
        decay = jnp.exp(jnp.where(incl, gcum_c - gcum_r, -jnp.inf))
        kb = k * bc
        k16 = k.astype(BF16)
        egc = jnp.exp(gcum_c)
        glast = gcum_c[c - 1:c, :]
        mm = jnp.where(strict, _dot_nt(kb.astype(BF16), k16) * decay, 0.0)
        q_l.append(q)
        k_l.append(k)
        k16_l.append(k16)
        decay_l.append(decay)
        egc_l.append(egc)
        glast_l.append(glast)
        kdec_l.append(jnp.exp(glast - gcum_c))
        rhs_l.append(jnp.concatenate([vs[:, hsl[hh]] * bc, kb * egc], axis=1).astype(BF16))
        pw_l.append(-mm)
        tinv_l.append(eye - mm)
    for _ in range(n_double):
        for hh in heads:
            pw16 = pw_l[hh].astype(BF16)
            pw_l[hh] = _dot(pw16, pw16)
        for hh in heads:
            tinv_l[hh] = tinv_l[hh] + _dot(tinv_l[hh].astype(BF16), pw_l[hh].astype(BF16))
    sol_l = [_dot(tinv_l[hh].astype(BF16), rhs_l[hh]) for hh in heads]
    attn_l = [(_dot_nt(q_l[hh].astype(BF16), k16_l[hh]) * decay_l[hh]).astype(BF16) for hh in heads]
    s_l = [s_scr[hh] for hh in heads]
    s16_l = [s_l[hh].astype(BF16) for hh in heads]
    v16_l = [(sol_l[hh][:, :DV_A] - _dot(sol_l[hh][:, DV_A:].astype(BF16), s16_l[hh])).astype(BF16)
             for hh in heads]
    o_l = [_dot((q_l[hh] * egc_l[hh]).astype(BF16), s16_l[hh]) + _dot(attn_l[hh], v16_l[hh]) for hh in heads]
    snew_l = [s_l[hh] * jnp.exp(glast_l[hh]) + _dot_tn((k_l[hh] * kdec_l[hh]).astype(BF16), v16_l[hh])
              for hh in heads]
    s_scr[...] = jnp.stack(snew_l, axis=0)
    o_l = [o * lax.rsqrt(jnp.mean(o * o, axis=-1, keepdims=True) + EPS) * ng for o in o_l]
    o_ref[...] = (jnp.concatenate(o_l, axis=1) * _silu(z)).astype(o_ref.dtype)

    @pl.when(n == pl.num_programs(2) - 1)
    def _():
        sout_ref[...] = s_scr[...].astype(sout_ref.dtype)


def _gdn(p, row0, nseq, t, chunk, conv_w, g, beta, conv_buf, s0, norm_g):
    hg = GDN_HEADS_PER_STEP
    w = hg * DK_A
    nch = t // chunk
    nhg = H_A // hg
    rb0 = row0 // chunk
    gs = g.reshape(nseq, nch, chunk, H_A)
    grow = gs.transpose(0, 1, 3, 2)
    gcol = grow[..., None]
    bcol = beta.reshape(nseq, nch, chunk, H_A).transpose(0, 1, 3, 2)[..., None]
    tail = jnp.pad(conv_buf.astype(F32), ((0, 0), (8 - (CONV_A - 1), 0), (0, 0)))
    qb0, kb0, vb0, zb0 = (PM_OFF[s] // w for s in ("qa", "ka", "va", "za"))
    pspec = lambda b0: pl.BlockSpec((chunk, w), lambda b, h, n: (rb0 + b * nch + n, b0 + h))
    wspec = lambda b0: pl.BlockSpec((CONV_A, w), lambda b, h, n: (0, b0 + h))
    tspec = lambda b0: pl.BlockSpec((None, 8, w), lambda b, h, n: (b, 0, b0 + h))
    o, s_out = pl.pallas_call(
        functools.partial(_gdn_body, chunk=chunk, hg=hg),
        grid=(nseq, nhg, nch),
        in_specs=[pspec(qb0), pspec(kb0), pspec(vb0), pspec(zb0),
                  wspec(0), wspec(nhg), wspec(2 * nhg),
                  tspec(0), tspec(nhg), tspec(2 * nhg),
                  pl.BlockSpec((None, None, H_A, chunk), lambda b, h, n: (b, n, 0, 0)),
                  pl.BlockSpec((None, None, hg, chunk, 1), lambda b, h, n: (b, n, h, 0, 0)),
                  pl.BlockSpec((None, None, hg, chunk, 1), lambda b, h, n: (b, n, h, 0, 0)),
                  pl.BlockSpec((None, hg, DK_A, DV_A), lambda b, h, n: (b, h, 0, 0)),
                  pl.BlockSpec((1, DV_A), lambda b, h, n: (0, 0))],
        out_specs=[pl.BlockSpec((chunk, w), lambda b, h, n: (b * nch + n, h)),
                   pl.BlockSpec((None, hg, DK_A, DV_A), lambda b, h, n: (b, h, 0, 0))],
        out_shape=[jax.ShapeDtypeStruct((nseq * t, WIDTH_A), BF16),
                   jax.ShapeDtypeStruct((nseq, H_A, DK_A, DV_A), s0.dtype)],
        scratch_shapes=[pltpu.VMEM((hg, DK_A, DV_A), F32), pltpu.VMEM((8, 3 * w), F32)],
        compiler_params=_cparams("parallel", "parallel", "arbitrary"),
        name=f"gdn_c{chunk}",
    )(p, p, p, p, conv_w, conv_w, conv_w, tail, tail, tail, grow, gcol, bcol, s0,
      norm_g.reshape(1, DV_A).astype(F32))
    return o, s_out


SCONV_COLS = 256


def _sconv_body(gb_ref, gc_ref, hc_ref, w_ref, buf_ref, o_ref, st_ref, *, t):
    pr = gc_ref[...] * hc_ref[...]
    cw = w_ref[...]
    xfull = jnp.concatenate([buf_ref[...], pr], axis=0)
    y = pr * cw[CONV_C - 1:CONV_C, :]
    for j in range(1, CONV_C):
        y = y + pltpu.roll(xfull, j, axis=0)[8:, :] * cw[CONV_C - 1 - j:CONV_C - j, :]
    o_ref[...] = (gb_ref[...] * y).astype(o_ref.dtype)
    st_ref[...] = xfull[t:t + 8, :]


def _sconv(p, row0, nseq, t, conv_w, buf):
    cols = SCONV_COLS
    ncb = WIDTH_C // cols
    rb0 = row0 // t
    b0, c0, h0 = (PM_OFF[s] // cols for s in ("gate_b", "gate_c", "hc"))
    buf8 = jnp.pad(buf.astype(F32), ((0, 0), (8 - (CONV_C - 1), 0), (0, 0)))
    pspec = lambda o: pl.BlockSpec((t, cols), lambda b, j: (rb0 + b, o + j))
    o, st = pl.pallas_call(
        functools.partial(_sconv_body, t=t),
        grid=(nseq, ncb),
        in_specs=[pspec(b0), pspec(c0), pspec(h0),
                  pl.BlockSpec((CONV_C, cols), lambda b, j: (0, j)),
                  pl.BlockSpec((None, 8, cols), lambda b, j: (b, 0, j))],
        out_specs=[pl.BlockSpec((t, cols), lambda b, j: (b, j)),
                   pl.BlockSpec((None, 8, cols), lambda b, j: (b, 0, j))],
        out_shape=[jax.ShapeDtypeStruct((nseq * t, WIDTH_C), BF16),
                   jax.ShapeDtypeStruct((nseq, 8, WIDTH_C), F32)],
        compiler_params=_cparams("parallel", "parallel"),
        name=f"sconv_t{t}",
    )(p, p, p, conv_w, buf8)
    return o, st[:, 8 - (CONV_C - 1):, :]


BISECT_EVERY = 8
MAX_SEARCH = 2400


def _select_threshold(count_ge, lo0, hi0, n_valid, k):
    kf = jnp.float32(k)

    def cond(st):
        return jnp.logical_and(st[0] < MAX_SEARCH, jnp.min(st[-1]) < 0.5)

    def body(st):
        it, lo, hi, glo, ghi, t, side, done = st
        half = 0.5 * lo + 0.5 * hi
        cand = lo + (hi - lo) * (glo / jnp.maximum(glo - ghi, 1e-9))
        mid = jnp.where(it % BISECT_EVERY == BISECT_EVERY - 1, half, cand)
        mid = jnp.where(jnp.logical_and(mid > lo, mid < hi), mid, half)
        adjacent = jnp.logical_not(jnp.logical_and(mid > lo, mid < hi))
        first = it == 0
        mid = jnp.where(first, hi, mid)
        adjacent = jnp.logical_and(adjacent, jnp.logical_not(first))
        g = count_ge(mid) - kf
        ge = g >= 0.0
        finish = jnp.logical_or(jnp.logical_or(g == 0.0, adjacent), jnp.logical_and(first, ge))
        newly = jnp.logical_and(finish, done < 0.5)
        t = jnp.where(newly, jnp.where(adjacent, lo, mid), t)
        new_side = jnp.where(ge, 1.0, -1.0)
        damp = jnp.where(jnp.logical_and(new_side == side, jnp.logical_not(first)), 0.5, 1.0)
        glo = jnp.where(ge, g, glo * damp)
        ghi = jnp.where(ge, ghi * damp, g)
        lo = jnp.where(ge, mid, lo)
        hi = jnp.where(ge, hi, mid)
        done = jnp.where(finish, 1.0, done)
        return it + 1, lo, hi, glo, ghi, t, new_side, done

    done0 = jnp.where(n_valid > kf, 0.0, 1.0)
    st = lax.while_loop(cond, body, (jnp.int32(0), lo0, hi0, n_valid - kf, jnp.full_like(lo0, -kf),
                                     lo0, jnp.zeros_like(lo0), done0))
    return st[5]


KEY_CHUNK = 512


def _dsa_prompt_body(qi_ref, tq_ref, qb_ref, kb_ref, vb_ref, tk_ref, o_ref,
                     sc_scr, m_scr, l_scr, acc_scr, *, tq, sc, topk):
    i = pl.program_id(1)
    nck = (i * tq + tq - 1) // sc + 1
    wt = tq_ref[...].T[TAIL_WI:TAIL_WI + H_I, :] * ((D_I ** -0.5) * (H_I ** -0.5))
    qi = qi_ref[...]
    pairs = [jnp.concatenate([qi[:, (2 * p) * D_I:(2 * p + 1) * D_I],
                              qi[:, (2 * p + 1) * D_I:(2 * p + 2) * D_I]], axis=0).astype(BF16)
             for p in range(H_I // 2)]
    tpos = i * tq + lax.broadcasted_iota(jnp.int32, (1, tq), 1)

    def score_chunk(c, carry):
        r0 = pl.multiple_of(c * sc, sc)
        kic = tk_ref[pl.ds(r0, sc), :].astype(BF16)
        acc = jnp.zeros((sc, tq), F32)
        for p in range(H_I // 2):
            d = _dot_nt(kic, pairs[p])
            acc = acc + jnp.maximum(d[:, :tq], 0.0) * wt[2 * p:2 * p + 1, :]
            acc = acc + jnp.maximum(d[:, tq:], 0.0) * wt[2 * p + 1:2 * p + 2, :]
        kpos = r0 + lax.broadcasted_iota(jnp.int32, (sc, tq), 0)
        sc_scr[pl.ds(r0, sc), :] = jnp.where(kpos <= tpos, acc, -jnp.inf)
        return carry

    lax.fori_loop(0, nck, score_chunk, 0)

    def minmax_chunk(c, carry):
        mn, mx = carry
        x = sc_scr[pl.ds(pl.multiple_of(c * sc, sc), sc), :]
        mx = jnp.maximum(mx, jnp.max(x, axis=0, keepdims=True))
        mn = jnp.minimum(mn, jnp.min(jnp.where(x == -jnp.inf, jnp.inf, x), axis=0, keepdims=True))
        return mn, mx

    mn, mx = lax.fori_loop(0, nck, minmax_chunk,
                           (jnp.full((1, tq), jnp.inf, F32), jnp.full((1, tq), -jnp.inf, F32)))

    def count_ge(mid):
        def body(c, acc):
            x = sc_scr[pl.ds(pl.multiple_of(c * sc, sc), sc), :]
            hit = jnp.where(x >= mid, 1.0, 0.0).reshape(8, sc // 64, 8, tq)
            return acc + jnp.sum(jnp.sum(hit, axis=1), axis=0)
        return jnp.sum(lax.fori_loop(0, nck, body, jnp.zeros((8, tq), F32)), axis=0, keepdims=True)

    thr = _select_threshold(count_ge, mn, mx, (tpos + 1).astype(F32), topk)

    m_scr[...] = jnp.full(m_scr.shape, NEG_BIG, F32)
    l_scr[...] = jnp.zeros(l_scr.shape, F32)
    acc_scr[...] = jnp.zeros(acc_scr.shape, F32)
    q = qb_ref[...].astype(BF16)
    scale = HEAD_DIM_B ** -0.5

    def attn_chunk(c, carry):
        r0 = pl.multiple_of(c * sc, sc)
        sel = sc_scr[pl.ds(r0, sc), :] >= thr
        for n in range(N_KV_B):
            kc = kb_ref[pl.ds(r0, sc), n * HEAD_DIM_B:(n + 1) * HEAD_DIM_B].astype(BF16)
            vc = vb_ref[pl.ds(r0, sc), n * HEAD_DIM_B:(n + 1) * HEAD_DIM_B].astype(BF16)
            for g in range(GROUP_B):
                h = n * GROUP_B + g
                hs = slice(h * HEAD_DIM_B, (h + 1) * HEAD_DIM_B)
                s = jnp.where(sel, _dot_nt(kc, q[:, hs]) * scale, NEG_BIG)
                m_old = m_scr[h:h + 1, :]
                m_new = jnp.maximum(m_old, jnp.max(s, axis=0, keepdims=True))
                pexp = jnp.exp(s - m_new)
                alpha = jnp.exp(m_old - m_new)
                l_scr[h:h + 1, :] = alpha * l_scr[h:h + 1, :] + jnp.sum(pexp, axis=0, keepdims=True)
                acc_scr[hs, :] = alpha * acc_scr[hs, :] + _dot_tn(vc, pexp.astype(BF16))
                m_scr[h:h + 1, :] = m_new
        return carry

    lax.fori_loop(0, nck, attn_chunk, 0)

    for h in range(H_B):
        hs = slice(h * HEAD_DIM_B, (h + 1) * HEAD_DIM_B)
        ot = acc_scr[hs, :] / l_scr[h:h + 1, :]
        o_ref[:, hs] = ot.T.astype(o_ref.dtype)


def _dsa_prompt(p, k_new, v_new, ki_new, nseq, s):
    tq, sc = Q_BLOCK, min(KEY_CHUNK, s)
    topk = min(TOPK_MAX, s // 4)
    nqb = s // tq
    qi0 = PD_OFF["qi"] // (H_I * D_I)
    qb0 = PD_OFF["qb"] // WIDTH_B
    tl0 = TAIL_OFF // LANES
    return pl.pallas_call(
        functools.partial(_dsa_prompt_body, tq=tq, sc=sc, topk=topk),
        grid=(nseq, nqb),
        in_specs=[pl.BlockSpec((tq, H_I * D_I), lambda b, i: (b * nqb + i, qi0)),
                  pl.BlockSpec((tq, LANES), lambda b, i: (b * nqb + i, tl0)),
                  pl.BlockSpec((tq, WIDTH_B), lambda b, i: (b * nqb + i, qb0)),
                  pl.BlockSpec((s, KV_WIDTH_B), lambda b, i: (b, 0)),
                  pl.BlockSpec((s, KV_WIDTH_B), lambda b, i: (b, 0)),
                  pl.BlockSpec((s, D_I), lambda b, i: (b, 0))],
        out_specs=pl.BlockSpec((tq, WIDTH_B), lambda b, i: (b * nqb + i, 0)),
        out_shape=jax.ShapeDtypeStruct((nseq * s, WIDTH_B), BF16),
        scratch_shapes=[pltpu.VMEM((s, tq), F32), pltpu.VMEM((H_B, tq), F32),
                        pltpu.VMEM((H_B, tq), F32), pltpu.VMEM((WIDTH_B, tq), F32)],
        compiler_params=_cparams("parallel", "arbitrary"),
        name="dsa_prompt",
    )(p, p, p, k_new, v_new, ki_new)


PAGES_PER_STEP = 8


def _dsa_sample_scores_body(pt_ref, q_ref, w_ref, *rest, nsteps, pps, t):
    page_refs, new_ref, o_ref, onew_ref = rest[:pps], rest[pps], rest[pps + 1], rest[pps + 2]
    j = pl.program_id(1)
    q = q_ref[...].astype(BF16)
    wv = w_ref[...] * ((D_I ** -0.5) * (H_I ** -0.5))

    def scores(keys_t):
        d = _dot(q, keys_t.astype(BF16))
        return jnp.sum((jnp.maximum(d, 0.0) * wv).reshape(t, H_I, PAGE_SIZE), axis=1)

    @pl.when(j < nsteps)
    def _():
        o_ref[...] = jnp.concatenate([scores(r[...]) for r in page_refs], axis=1)

    @pl.when(j == nsteps)
    def _():
        knew = lax.broadcasted_iota(jnp.int32, (t, PAGE_SIZE), 1)
        tnew = lax.broadcasted_iota(jnp.int32, (t, PAGE_SIZE), 0)
        onew_ref[...] = jnp.where(knew <= tnew, scores(new_ref[...]), -jnp.inf)


def _page_specs(block, layer, nsteps, pps):
    def spec(p):
        def index(b, j, pt):
            return (layer, pt[b, jnp.minimum(j, nsteps - 1) * pps + p]) + (0,) * (len(block) - 2)
        return pl.BlockSpec(block, index)
    return [spec(p) for p in range(pps)]


def _dsa_sample_scores(page_table, q, wv, cache_kidx, layer, ki_new):
    nseq, npages = page_table.shape
    t = q.shape[1] // H_I
    pps = math.gcd(PAGES_PER_STEP, npages)
    nsteps = npages // pps
    return pl.pallas_call(
        functools.partial(_dsa_sample_scores_body, nsteps=nsteps, pps=pps, t=t),
        grid_spec=pltpu.PrefetchScalarGridSpec(
            num_scalar_prefetch=1,
            grid=(nseq, nsteps + 1),
            in_specs=[pl.BlockSpec((None, t * H_I, D_I), lambda b, j, pt: (b, 0, 0)),
                      pl.BlockSpec((None, t * H_I, PAGE_SIZE), lambda b, j, pt: (b, 0, 0)),
                      *_page_specs((None, None, D_I, PAGE_SIZE), layer, nsteps, pps),
                      pl.BlockSpec((None, D_I, PAGE_SIZE), lambda b, j, pt: (b, 0, 0))],
            out_specs=[pl.BlockSpec((None, t, pps * PAGE_SIZE),
                                    lambda b, j, pt: (b, 0, jnp.minimum(j, nsteps - 1))),
                       pl.BlockSpec((None, t, PAGE_SIZE), lambda b, j, pt: (b, 0, 0))],
        ),
        out_shape=[jax.ShapeDtypeStruct((nseq, t, npages * PAGE_SIZE), F32),
                   jax.ShapeDtypeStruct((nseq, t, PAGE_SIZE), F32)],
        compiler_params=_cparams("parallel", "arbitrary"),
        name="dsa_sample_scores",
    )(page_table, q, wv, *([cache_kidx] * pps), ki_new)


def _dsa_sample_attn_body(pt_ref, sc_ref, scn_ref, q_ref, *rest, nsteps, pps, t, past, topk):
    kpage_refs, vpage_refs = rest[:pps], rest[pps:2 * pps]
    knew_ref, vnew_ref, o_ref, thr_scr, m_scr, l_scr, acc_scr = rest[2 * pps:]
    j = pl.program_id(1)

    @pl.when(j == 0)
    def _():
        xp, xn = sc_ref[...], scn_ref[...]
        mx = jnp.maximum(jnp.max(xp, axis=1, keepdims=True), jnp.max(xn, axis=1, keepdims=True))
        mn = jnp.minimum(jnp.min(xp, axis=1, keepdims=True),
                         jnp.min(jnp.where(xn == -jnp.inf, jnp.inf, xn), axis=1, keepdims=True))

        def count_ge(mid):
            cnt = lambda hit: jnp.sum(jnp.where(hit, 1.0, 0.0), axis=1, keepdims=True)
            return cnt(sc_ref[...] >= mid) + cnt(scn_ref[...] >= mid)

        n_valid = (past + 1 + lax.broadcasted_iota(jnp.int32, (t, 1), 0)).astype(F32)
        thr = _select_threshold(count_ge, mn, mx, n_valid, topk)
        thr_scr[...] = jnp.broadcast_to(thr, thr_scr.shape)
        m_scr[...] = jnp.full(m_scr.shape, NEG_BIG, F32)
        l_scr[...] = jnp.zeros(l_scr.shape, F32)
        acc_scr[...] = jnp.zeros(acc_scr.shape, F32)

    scale = HEAD_DIM_B ** -0.5

    def attend(x, keys, vals):
        sel_t = jnp.where(x >= thr_scr[:, 0:1], 1.0, 0.0)
        sel = jnp.concatenate([sel_t] * GROUP_B, axis=0) > 0.5
        for n in range(N_KV_B):
            s = jnp.where(sel, _dot_nt(q_ref[n].astype(BF16), keys(n).astype(BF16)) * scale, NEG_BIG)
            m_old = m_scr[n]
            m_new = jnp.maximum(m_old, jnp.max(s, axis=1, keepdims=True))
            pexp = jnp.exp(s - m_new)
            alpha = jnp.exp(m_old - m_new)
            l_scr[n] = alpha * l_scr[n] + jnp.sum(pexp, axis=1, keepdims=True)
            acc_scr[n] = alpha * acc_scr[n] + _dot(pexp.astype(BF16), vals(n).astype(BF16))
            m_scr[n] = m_new

    def page_head(r, n):
        return r[pl.ds(n, PAGE_SIZE, stride=N_KV_B), :]

    @pl.when(j < nsteps)
    def _():
        width = pps * PAGE_SIZE
        x = sc_ref[:, pl.ds(pl.multiple_of(j * width, width), width)]
        attend(x,
               lambda n: jnp.concatenate([page_head(r, n) for r in kpage_refs], axis=0),
               lambda n: jnp.concatenate([page_head(r, n) for r in vpage_refs], axis=0))

    @pl.when(j == nsteps)
    def _():
        hs = lambda n: slice(n * HEAD_DIM_B, (n + 1) * HEAD_DIM_B)
        attend(scn_ref[...], lambda n: knew_ref[:, hs(n)], lambda n: vnew_ref[:, hs(n)])
        for n in range(N_KV_B):
            o_ref[n] = (acc_scr[n] / l_scr[n]).astype(o_ref.dtype)


def _dsa_sample_attn(page_table, scores, scores_new, q, cache_k, cache_v, layer, k_new, v_new):
    nseq, npages = page_table.shape
    t = scores.shape[1]
    past = npages * PAGE_SIZE
    topk = min(TOPK_MAX, (past + t) // 4)
    rows = GROUP_B * t
    pps = math.gcd(PAGES_PER_STEP, npages)
    nsteps = npages // pps
    page_block = (None, None, PAGE_SIZE * N_KV_B, HEAD_DIM_B)
    const3 = lambda b, j, pt: (b, 0, 0)
    return pl.pallas_call(
        functools.partial(_dsa_sample_attn_body, nsteps=nsteps, pps=pps, t=t, past=past, topk=topk),
        grid_spec=pltpu.PrefetchScalarGridSpec(
            num_scalar_prefetch=1,
            grid=(nseq, nsteps + 1),
            in_specs=[pl.BlockSpec((None, t, past), const3),
                      pl.BlockSpec((None, t, PAGE_SIZE), const3),
                      pl.BlockSpec((None, N_KV_B, rows, HEAD_DIM_B), lambda b, j, pt: (b, 0, 0, 0)),
                      *_page_specs(page_block, layer, nsteps, pps),
                      *_page_specs(page_block, layer, nsteps, pps),
                      pl.BlockSpec((None, PAGE_SIZE, KV_WIDTH_B), const3),
                      pl.BlockSpec((None, PAGE_SIZE, KV_WIDTH_B), const3)],
            out_specs=pl.BlockSpec((None, N_KV_B, rows, HEAD_DIM_B), lambda b, j, pt: (b, 0, 0, 0)),
            scratch_shapes=[pltpu.VMEM((t, PAGE_SIZE), F32),
                            pltpu.VMEM((N_KV_B, rows, 1), F32),
                            pltpu.VMEM((N_KV_B, rows, 1), F32),
                            pltpu.VMEM((N_KV_B, rows, HEAD_DIM_B), F32)],
        ),
        out_shape=jax.ShapeDtypeStruct((nseq, N_KV_B, rows, HEAD_DIM_B), BF16),
        compiler_params=_cparams("parallel", "arbitrary"),
        name="dsa_sample_attn",
    )(page_table, scores, scores_new, q, *([cache_k] * pps), *([cache_v] * pps), k_new, v_new)


def _dsa_sample(ps, k_s, v_s, ki_s, page_table, cache_k, cache_v, cache_kidx, layer):
    nseq = page_table.shape[0]
    t = ps.shape[0] // nseq
    seg = lambda name, width: ps[:, PD_OFF[name]:PD_OFF[name] + width]
    qi = seg("qi", H_I * D_I).reshape(nseq, t * H_I, D_I)
    wi = seg("wi", H_I).reshape(nseq, t * H_I, 1)
    wv = jnp.broadcast_to(wi, (nseq, t * H_I, PAGE_SIZE))
    pad_rows = lambda a: jnp.pad(a.reshape(nseq, t, -1), ((0, 0), (0, PAGE_SIZE - t), (0, 0)))
    kidx_t = cache_kidx.transpose(0, 1, 3, 2)
    scores, scores_new = _dsa_sample_scores(page_table, qi, wv, kidx_t, layer,
                                            pad_rows(ki_s).transpose(0, 2, 1))
    q = seg("qb", WIDTH_B).reshape(nseq, t, N_KV_B, GROUP_B, HEAD_DIM_B)
    q = q.transpose(0, 2, 3, 1, 4).reshape(nseq, N_KV_B, GROUP_B * t, HEAD_DIM_B)
    pool_rows = lambda c: c.reshape(c.shape[0], c.shape[1], PAGE_SIZE * N_KV_B, HEAD_DIM_B)
    o = _dsa_sample_attn(page_table, scores, scores_new, q, pool_rows(cache_k), pool_rows(cache_v),
                         layer, pad_rows(k_s), pad_rows(v_s))
    o = o.reshape(nseq, N_KV_B, GROUP_B, t, HEAD_DIM_B).transpose(0, 3, 1, 2, 4)
    return o.reshape(nseq * t, WIDTH_B)


def _prep_w_mid(w_in):
    sizes = dict(zip(IN_NAMES, IN_SIZES))
    z0 = min(SRC_OFF[n] for n in MID_ORDER + ("aa", "ba"))
    z1 = max(SRC_OFF[n] + sizes[n] for n in MID_ORDER)
    zone = lax.optimization_barrier(lax.slice_in_dim(w_in, z0, z1, axis=2))
    cols = [zone[:, :, SRC_OFF[n] - z0:SRC_OFF[n] - z0 + sizes[n]].astype(BF16) for n in MID_ORDER]
    cols.append(jnp.zeros(w_in.shape[:2] + (PD_WIDTH - sum(sizes[n] for n in MID_ORDER),), BF16))
    return jnp.concatenate(cols, axis=2)


def kernel(x_prompt, x_sample, cache_k, cache_v, cache_kidx, page_table, state_gdn, state_gdn_conv,
           state_sconv, final_norm, norm1, norm2, w_in, conv_a, a_log, dt_bias, gdn_norm, conv_c,
           w_branch_a, w_branch_b, w_branch_c, w_o, w_gate, w_up, w_down):
    bp, tp, d = x_prompt.shape
    bs, ts = x_sample.shape[:2]
    mp, ms = bp * tp, bs * ts
    x = jnp.concatenate([x_prompt.reshape(mp, d), x_sample.reshape(ms, d)], axis=0)
    new_p = [[] for _ in range(6)]
    new_s = [[] for _ in range(6)]
    w_mid = _prep_w_mid(w_in)
    for l in range(DEPTH):
        xn = _rmsnorm(x, norm1[l], BF16)
        p = _in_proj_main(xn, w_in, l)
        pd, k_new, v_new, ki_new = _in_proj_mid(xn, w_mid, l)
        g, beta = _gdn_gates(pd, a_log[l], dt_bias[l])
        conv_w = conv_a[l].astype(F32)

        oa_p, gdn_p = _gdn(p, 0, bp, tp, min(GDN_CHUNK, tp), conv_w, g[:mp], beta[:mp],
                           jnp.zeros((bp, CONV_A - 1, 3 * WIDTH_A), F32),
                           jnp.zeros((bp, H_A, DK_A, DV_A), F32), gdn_norm[l])
        oa_s, gdn_s = _gdn(p, mp, bs, ts, ts, conv_w, g[mp:], beta[mp:],
                           state_gdn_conv[l], state_gdn[l], gdn_norm[l])
        ob_p = _dsa_prompt(pd, k_new, v_new, ki_new, bp, tp)
        ob_s = _dsa_sample(pd[mp:], k_new[mp:], v_new[mp:], ki_new[mp:], page_table,
                           cache_k, cache_v, cache_kidx, l)
        cw = conv_c[l].astype(F32)
        oc_p, sconv_p = _sconv(p, 0, bp, tp, cw, jnp.zeros((bp, CONV_C - 1, WIDTH_C), F32))
        oc_s, sconv_s = _sconv(p, mp, bs, ts, cw, state_sconv[l])

        oa = jnp.concatenate([oa_p, oa_s], axis=0)
        ob = jnp.concatenate([ob_p, ob_s], axis=0)
        oc = jnp.concatenate([oc_p, oc_s], axis=0)
        merged = _merge(oa, ob, oc, w_branch_a, w_branch_b, w_branch_c, l, p)
        x = _out_proj(merged, w_o, l, x)

        hn = _rmsnorm(x, norm2[l], BF16)
        h = _swiglu(hn, w_gate, w_up, l)
        x = _ffn_down(h, w_down, l, x)

        nb = CONV_A - 1
        tail_rows = lambda r0, t: lax.slice(p, (r0 + max(t - nb, 0), 0), (r0 + t, 3 * WIDTH_A))
        gconv_p = jnp.stack([tail_rows(b * tp, tp) for b in range(bp)])
        gconv_s = jnp.stack([tail_rows(mp + b * ts, ts) for b in range(bs)])
        if tp < nb:
            gconv_p = jnp.concatenate([jnp.zeros((bp, nb - tp, 3 * WIDTH_A), F32), gconv_p], axis=1)
        if ts < nb:
            gconv_s = jnp.concatenate([state_gdn_conv[l].astype(F32)[:, ts:], gconv_s], axis=1)
        kv = lambda a, rows, b, t: a[rows].reshape(b, t, N_KV_B, HEAD_DIM_B)
        rp, rs = slice(0, mp), slice(mp, mp + ms)
        for lst, val in zip(new_p, (kv(k_new, rp, bp, tp), kv(v_new, rp, bp, tp),
                                    ki_new[rp].reshape(bp, tp, D_I), gdn_p, gconv_p, sconv_p)):
            lst.append(val)
        for lst, val in zip(new_s, (kv(k_new, rs, bs, ts), kv(v_new, rs, bs, ts),
                                    ki_new[rs].reshape(bs, ts, D_I), gdn_s, gconv_s, sconv_s)):
            lst.append(val)

    y_prompt = _rmsnorm(x, final_norm, F32, 0, mp).reshape(bp, tp, d)
    y_sample = _rmsnorm(x, final_norm, F32, mp, ms).reshape(bs, ts, d)
    outs_p = [jnp.stack(a) for a in new_p]
    outs_s = [jnp.stack(a) for a in new_s]
    return (y_prompt, y_sample, *outs_p, *outs_s)
```

```python
import functools
import math

import jax
import jax.numpy as jnp
from jax import lax
from jax.experimental import pallas as pl
from jax.experimental.pallas import tpu as pltpu

F32 = jnp.float32
BF16 = jnp.bfloat16

D_MODEL = 4096
DEPTH = 2
PAGE_SIZE = 128
H_A = 16
DK_A = 128
DV_A = 128
WIDTH_A = H_A * DV_A
CONV_A = 4
GDN_CHUNK = 64
H_B = 8
N_KV_B = 2
GROUP_B = H_B // N_KV_B
HEAD_DIM_B = 128
WIDTH_B = H_B * HEAD_DIM_B
KV_WIDTH_B = N_KV_B * HEAD_DIM_B
H_I = 32
D_I = 64
TOPK_MAX = 256
Q_BLOCK = 128
WIDTH_C = 1024
CONV_C = 3
D_FF = -(-8 * D_MODEL // (3 * 256)) * 256
EPS = 1e-6

IN_NAMES = ("qa", "ka", "va", "za", "aa", "ba", "qb", "kb", "vb", "qi", "ki", "wi",
            "gate_b", "gate_c", "hc", "ga", "gb", "gc")
IN_SIZES = (WIDTH_A, WIDTH_A, WIDTH_A, WIDTH_A, H_A, H_A,
            WIDTH_B, KV_WIDTH_B, KV_WIDTH_B, H_I * D_I, D_I, H_I,
            WIDTH_C, WIDTH_C, WIDTH_C, D_MODEL, D_MODEL, D_MODEL)

LANES = 128
N_BLOCK = 512
VMEM_LIMIT = 56 * 1024 * 1024
ACT_VMEM_BYTES = 32 * 1024 * 1024
NEG_BIG = -1e30

MAIN_ORDER = ("qa", "ka", "va", "za", "gate_b", "gate_c", "hc", "ga", "gb", "gc")
MID_ORDER = ("qi", "qb", "kb", "vb", "ki", "aa", "ba", "wi")


def _offsets(order):
    sizes = dict(zip(IN_NAMES, IN_SIZES))
    off, out = 0, {}
    for name in order:
        out[name] = off
        off += sizes[name]
    return out, off


SRC_OFF, _ = _offsets(IN_NAMES)
PM_OFF, PM_WIDTH = _offsets(MAIN_ORDER)
PD_OFF, _mid_cols = _offsets(MID_ORDER)
PD_WIDTH = -(-_mid_cols // N_BLOCK) * N_BLOCK
MAIN_RUN0 = PM_OFF["gate_b"]
MAIN_SHIFT = SRC_OFF["gate_b"] - MAIN_RUN0
assert PM_WIDTH % N_BLOCK == 0 and MAIN_RUN0 % N_BLOCK == 0 and MAIN_SHIFT % LANES == 0
assert all(SRC_OFF[n] == PM_OFF[n] for n in MAIN_ORDER[:4])
assert all(SRC_OFF[n] == PM_OFF[n] + MAIN_SHIFT for n in MAIN_ORDER[4:])
TAIL_OFF = PD_OFF["ki"]
TAIL_AA = PD_OFF["aa"] - TAIL_OFF
TAIL_BA = PD_OFF["ba"] - TAIL_OFF
TAIL_WI = PD_OFF["wi"] - TAIL_OFF
assert TAIL_OFF % N_BLOCK == 0 and PD_OFF["wi"] + H_I - TAIL_OFF == LANES


def _cparams(*sem):
    return pltpu.CompilerParams(dimension_semantics=sem, vmem_limit_bytes=VMEM_LIMIT)


def _dot(a, b):
    return jnp.dot(a, b, preferred_element_type=F32)


def _dot_nt(a, b):
    return lax.dot_general(a, b, (((1,), (1,)), ((), ())), preferred_element_type=F32)


def _dot_tn(a, b):
    return lax.dot_general(a, b, (((0,), (0,)), ((), ())), preferred_element_type=F32)


def _dot_hi(a, b):
    return jnp.dot(a, b, preferred_element_type=F32, precision=lax.Precision.HIGHEST)


def _sigmoid(x):
    return jax.nn.sigmoid(x)


def _silu(x):
    return x * jax.nn.sigmoid(x)


def _rmsnorm_body(x_ref, g_ref, o_ref):
    x = x_ref[...]
    ms = jnp.mean(x * x, axis=-1, keepdims=True)
    o_ref[...] = (x * lax.rsqrt(ms + EPS) * g_ref[...]).astype(o_ref.dtype)


def _row_block(m):
    for bm in (192, 128, 64, 32, 16, 8):
        if m % bm == 0:
            return bm
    raise ValueError(f"unsupported row count {m}")


def _rmsnorm(x, g, out_dtype, row0=0, nrows=None):
    d = x.shape[1]
    m = x.shape[0] - row0 if nrows is None else nrows
    bm = _row_block(math.gcd(m, row0) if row0 else m)
    rb0 = row0 // bm
    return pl.pallas_call(
        _rmsnorm_body,
        grid=(m // bm,),
        in_specs=[pl.BlockSpec((bm, d), lambda i: (rb0 + i, 0)), pl.BlockSpec((1, d), lambda i: (0, 0))],
        out_specs=pl.BlockSpec((bm, d), lambda i: (i, 0)),
        out_shape=jax.ShapeDtypeStruct((m, d), out_dtype),
        compiler_params=_cparams("parallel"),
        name="rmsnorm",
    )(x, g.reshape(1, d).astype(F32))


def _gdn_gates_body(t_ref, alog_ref, dtb_ref, g_ref, b_ref):
    t = t_ref[...]
    aa = t[:, TAIL_AA:TAIL_AA + H_A]
    ba = t[:, TAIL_BA:TAIL_BA + H_A]
    x = aa + dtb_ref[...]
    softplus = jnp.maximum(x, 0.0) + jnp.log1p(jnp.exp(-jnp.abs(x)))
    g_ref[...] = -jnp.exp(alog_ref[...]) * softplus
    b_ref[...] = _sigmoid(ba)


def _gdn_gates(p, a_log, dt_bias):
    m = p.shape[0]
    bm = _row_block(m)
    tail_blk = TAIL_OFF // LANES
    return pl.pallas_call(
        _gdn_gates_body,
        grid=(m // bm,),
        in_specs=[pl.BlockSpec((bm, LANES), lambda i: (i, tail_blk)),
                  pl.BlockSpec((1, H_A), lambda i: (0, 0)),
                  pl.BlockSpec((1, H_A), lambda i: (0, 0))],
        out_specs=[pl.BlockSpec((bm, H_A), lambda i: (i, 0)), pl.BlockSpec((bm, H_A), lambda i: (i, 0))],
        out_shape=[jax.ShapeDtypeStruct((m, H_A), F32)] * 2,
        compiler_params=_cparams("parallel"),
        name="gdn_gates",
    )(p, a_log.reshape(1, H_A).astype(F32), dt_bias.reshape(1, H_A).astype(F32))


def _big_row_block(m):
    for bm in (1376, 1024, 688, 512, 256, 128, 64):
        if m % bm == 0:
            return bm
    raise ValueError(f"unsupported row count {m}")


def _out_proj_body(a_ref, w_ref, r_ref, o_ref):
    o_ref[...] = r_ref[...] + _dot(a_ref[...], w_ref[...].astype(BF16))


def _out_proj(a, w, layer, residual, name="out_proj"):
    m, k = a.shape
    n = w.shape[2]
    bn = N_BLOCK // 2
    bm = next(b for b in (1376, 1024, 688, 512, 256, 128, 64)
              if m % b == 0 and 2 * b * k * 2 <= ACT_VMEM_BYTES)
    return pl.pallas_call(
        _out_proj_body,
        grid=(m // bm, n // bn),
        in_specs=[pl.BlockSpec((bm, k), lambda i, j: (i, 0)),
                  pl.BlockSpec((None, k, bn), lambda i, j: (layer, 0, j)),
                  pl.BlockSpec((bm, bn), lambda i, j: (i, j))],
        out_specs=pl.BlockSpec((bm, bn), lambda i, j: (i, j)),
        out_shape=jax.ShapeDtypeStruct((m, n), F32),
        compiler_params=_cparams("parallel", "arbitrary"),
        name=name,
    )(a, w, residual)


IN_PROJ_PIECES = N_BLOCK // LANES


def _in_proj_main_body(a_ref, *rest):
    w_refs, p_ref = rest[:IN_PROJ_PIECES], rest[IN_PROJ_PIECES]
    w = jnp.concatenate([r[...].astype(BF16) for r in w_refs], axis=1)
    p_ref[...] = _dot(a_ref[...], w)


def _in_proj_main(a, w_in, layer):
    m, k = a.shape
    bm, bn = _big_row_block(m), N_BLOCK
    run0_blk, shift_blk = MAIN_RUN0 // bn, MAIN_SHIFT // LANES

    def piece(q):
        def index(i, j):
            return (layer, 0, IN_PROJ_PIECES * j + q + jnp.where(j >= run0_blk, shift_blk, 0))
        return pl.BlockSpec((None, k, LANES), index)

    return pl.pallas_call(
        _in_proj_main_body,
        grid=(m // bm, PM_WIDTH // bn),
        in_specs=[pl.BlockSpec((bm, k), lambda i, j: (i, 0))] + [piece(q) for q in range(IN_PROJ_PIECES)],
        out_specs=pl.BlockSpec((bm, bn), lambda i, j: (i, j)),
        out_shape=jax.ShapeDtypeStruct((m, PM_WIDTH), F32),
        compiler_params=_cparams("parallel", "arbitrary"),
        name="in_proj_main",
    )(a, *([w_in] * IN_PROJ_PIECES))


def _in_proj_mid_body(a_ref, w_ref, p_ref, k_ref, v_ref, ki_ref, *, jkv, jtail):
    acc = _dot(a_ref[...], w_ref[...])
    p_ref[...] = acc
    j = pl.program_id(1)

    @pl.when(j == jkv)
    def _():
        k_ref[...] = acc[:, :KV_WIDTH_B]
        v_ref[...] = acc[:, KV_WIDTH_B:2 * KV_WIDTH_B]

    @pl.when(j == jtail)
    def _():
        ki_ref[...] = acc[:, :D_I]


def _in_proj_mid(a, w_mid, layer):
    m, k = a.shape
    bm, bn = _big_row_block(m), N_BLOCK
    assert PD_OFF["kb"] % bn == 0 and PD_OFF["vb"] == PD_OFF["kb"] + KV_WIDTH_B
    row = lambda i, j: (i, 0)
    return pl.pallas_call(
        functools.partial(_in_proj_mid_body, jkv=PD_OFF["kb"] // bn, jtail=TAIL_OFF // bn),
        grid=(m // bm, PD_WIDTH // bn),
        in_specs=[pl.BlockSpec((bm, k), row), pl.BlockSpec((None, k, bn), lambda i, j: (layer, 0, j))],
        out_specs=[pl.BlockSpec((bm, bn), lambda i, j: (i, j)),
                   pl.BlockSpec((bm, KV_WIDTH_B), row), pl.BlockSpec((bm, KV_WIDTH_B), row),
                   pl.BlockSpec((bm, D_I), row)],
        out_shape=[jax.ShapeDtypeStruct((m, PD_WIDTH), F32), jax.ShapeDtypeStruct((m, KV_WIDTH_B), F32),
                   jax.ShapeDtypeStruct((m, KV_WIDTH_B), F32), jax.ShapeDtypeStruct((m, D_I), F32)],
        compiler_params=_cparams("parallel", "arbitrary"),
        name="in_proj_mid",
    )(a, w_mid)


def _swiglu_body(a_ref, wg_ref, wu_ref, o_ref):
    a = a_ref[...]
    g = _dot(a, wg_ref[...].astype(BF16))
    u = _dot(a, wu_ref[...].astype(BF16))
    o_ref[...] = (_silu(g) * u).astype(o_ref.dtype)


def _swiglu(a, w_gate, w_up, layer):
    m, k = a.shape
    n = w_gate.shape[2]
    bm, bn = _big_row_block(m), N_BLOCK // 2
    assert n % bn == 0
    wspec = pl.BlockSpec((None, k, bn), lambda i, j: (layer, 0, j))
    return pl.pallas_call(
        _swiglu_body,
        grid=(m // bm, n // bn),
        in_specs=[pl.BlockSpec((bm, k), lambda i, j: (i, 0)), wspec, wspec],
        out_specs=pl.BlockSpec((bm, bn), lambda i, j: (i, j)),
        out_shape=jax.ShapeDtypeStruct((m, n), BF16),
        compiler_params=_cparams("parallel", "arbitrary"),
        name="swiglu",
    )(a, w_gate, w_up)


def _merge_body(oa_ref, ob_ref, oc_ref, wa_ref, wb_ref, wc_ref, ga_ref, gb_ref, gc_ref, o_ref):
    acc = _sigmoid(ga_ref[...]) * _dot(oa_ref[...], wa_ref[...].astype(BF16))
    acc += _sigmoid(gb_ref[...]) * _dot(ob_ref[...], wb_ref[...].astype(BF16))
    acc += _sigmoid(gc_ref[...]) * _dot(oc_ref[...], wc_ref[...].astype(BF16))
    o_ref[...] = acc.astype(o_ref.dtype)


def _merge(oa, ob, oc, wa, wb, wc, layer, p):
    m = oa.shape[0]
    n = wa.shape[2]
    bm, bn = _big_row_block(m), N_BLOCK // 2
    ga0, gb0, gc0 = (PM_OFF[s] // bn for s in ("ga", "gb", "gc"))
    row = lambda i, j: (i, 0)
    wspec = lambda w: pl.BlockSpec((None, w.shape[1], bn), lambda i, j: (layer, 0, j))
    return pl.pallas_call(
        _merge_body,
        grid=(m // bm, n // bn),
        in_specs=[pl.BlockSpec((bm, oa.shape[1]), row), pl.BlockSpec((bm, ob.shape[1]), row),
                  pl.BlockSpec((bm, oc.shape[1]), row),
                  wspec(wa), wspec(wb), wspec(wc),
                  pl.BlockSpec((bm, bn), lambda i, j: (i, ga0 + j)),
                  pl.BlockSpec((bm, bn), lambda i, j: (i, gb0 + j)),
                  pl.BlockSpec((bm, bn), lambda i, j: (i, gc0 + j))],
        out_specs=pl.BlockSpec((bm, bn), lambda i, j: (i, j)),
        out_shape=jax.ShapeDtypeStruct((m, n), BF16),
        compiler_params=_cparams("parallel", "arbitrary"),
        name="merge",
    )(oa, ob, oc, wa, wb, wc, p, p, p)


GDN_HEADS_PER_STEP = 16


def _gdn_body(q_ref, k_ref, v_ref, z_ref, wq_ref, wk_ref, wv_ref, tq_ref, tk_ref, tv_ref,
              g_ref, b_ref, s0_ref, ng_ref, o_ref, sout_ref, s_scr, tail_scr,
              *, chunk, hg):
    c = chunk
    w = hg * DK_A
    n = pl.program_id(2)

    @pl.when(n == 0)
    def _():
        s_scr[...] = s0_ref[...].astype(F32)
        tail_scr[:, 0:w] = tq_ref[...]
        tail_scr[:, w:2 * w] = tk_ref[...]
        tail_scr[:, 2 * w:3 * w] = tv_ref[...]

    act = []
    for idx, (x_ref, w_ref) in enumerate(((q_ref, wq_ref), (k_ref, wk_ref), (v_ref, wv_ref))):
        x = x_ref[...]
        cw = w_ref[...]
        xfull = jnp.concatenate([tail_scr[:, idx * w:(idx + 1) * w], x], axis=0)
        y = x * cw[CONV_A - 1:CONV_A, :]
        for j in range(1, CONV_A):
            y = y + pltpu.roll(xfull, j, axis=0)[8:, :] * cw[CONV_A - 1 - j:CONV_A - j, :]
        tail_scr[:, idx * w:(idx + 1) * w] = xfull[c:c + 8, :]
        act.append(_silu(y))
    qs, ks, vs = act
    z = z_ref[...]

    ii = lax.broadcasted_iota(jnp.int32, (c, c), 0)
    jj = lax.broadcasted_iota(jnp.int32, (c, c), 1)
    incl = jj <= ii
    strict = jj < ii
    eye = (ii == jj).astype(F32)
    n_double = int(math.log2(c)) - 1
    ng = ng_ref[...]
    b_blk = b_ref[...]
    gcum_all = _dot_hi(incl.astype(F32), g_ref[...])
    gcum_t = gcum_all.T

    heads = range(hg)
    hsl = [slice(hh * DK_A, (hh + 1) * DK_A) for hh in heads]
    q_l, k_l, k16_l, decay_l, egc_l, kdec_l, glast_l, pw_l, tinv_l, rhs_l = ([] for _ in range(10))
    for hh in heads:
        q = qs[:, hsl[hh]]
        k = ks[:, hsl[hh]]
        q = q * lax.rsqrt(jnp.sum(q * q, axis=-1, keepdims=True) + EPS) * (DK_A ** -0.5)
        k = k * lax.rsqrt(jnp.sum(k * k, axis=-1, keepdims=True) + EPS)
        bc = b_blk[:, hh:hh + 1]
        gcum_c = gcum_all[:, hh:hh + 1]
        gcum_r = gcum_t[hh:hh + 1, :]
        decay = jnp.exp(jnp.where(incl, gcum_c - gcum_r, -jnp.inf))
        kb = k * bc
        k16 = k.astype(BF16)
        egc = jnp.exp(gcum_c)
        glast = gcum_c[c - 1:c, :]
        mm = jnp.where(strict, _dot_nt(kb.astype(BF16), k16) * decay, 0.0)
        q_l.append(q)
        k_l.append(k)
        k16_l.append(k16)
        decay_l.append(decay)
        egc_l.append(egc)
        glast_l.append(glast)
        kdec_l.append(jnp.exp(glast - gcum_c))
        rhs_l.append(jnp.concatenate([vs[:, hsl[hh]] * bc, kb * egc], axis=1).astype(BF16))
        pw_l.append(-mm)
        tinv_l.append(eye - mm)
    for _ in range(n_double):
        for hh in heads:
            pw16 = pw_l[hh].astype(BF16)
            pw_l[hh] = _dot(pw16, pw16)
        for hh in heads:
            tinv_l[hh] = tinv_l[hh] + _dot(tinv_l[hh].astype(BF16), pw_l[hh].astype(BF16))
    sol_l = [_dot(tinv_l[hh].astype(BF16), rhs_l[hh]) for hh in heads]
    attn_l = [(_dot_nt(q_l[hh].astype(BF16), k16_l[hh]) * decay_l[hh]).astype(BF16) for hh in heads]
    s_l = [s_scr[hh] for hh in heads]
    s16_l = [s_l[hh].astype(BF16) for hh in heads]
    v16_l = [(sol_l[hh][:, :DV_A] - _dot(sol_l[hh][:, DV_A:].astype(BF16), s16_l[hh])).astype(BF16)
             for hh in heads]
    o_l = [_dot((q_l[hh] * egc_l[hh]).astype(BF16), s16_l[hh]) + _dot(attn_l[hh], v16_l[hh]) for hh in heads]
    snew_l = [s_l[hh] * jnp.exp(glast_l[hh]) + _dot_tn((k_l[hh] * kdec_l[hh]).astype(BF16), v16_l[hh])
              for hh in heads]
    s_scr[...] = jnp.stack(snew_l, axis=0)
    o_l = [o * lax.rsqrt(jnp.mean(o * o, axis=-1, keepdims=True) + EPS) * ng for o in o_l]
    o_ref[...] = (jnp.concatenate(o_l, axis=1) * _silu(z)).astype(o_ref.dtype)

    @pl.when(n == pl.num_programs(2) - 1)
    def _():
        sout_ref[...] = s_scr[...].astype(sout_ref.dtype)


def _gdn(p, row0, nseq, t, chunk, conv_w, g, beta, conv_buf, s0, norm_g):
    hg = GDN_HEADS_PER_STEP
    assert hg == H_A
    w = hg * DK_A
    nch = t // chunk
    nhg = H_A // hg
    rb0 = row0 // chunk
    gspec = pl.BlockSpec((chunk, H_A), lambda b, h, n: (rb0 + b * nch + n, 0))
    tail = jnp.pad(conv_buf.astype(F32), ((0, 0), (8 - (CONV_A - 1), 0), (0, 0)))
    qb0, kb0, vb0, zb0 = (PM_OFF[s] // w for s in ("qa", "ka", "va", "za"))
    pspec = lambda b0: pl.BlockSpec((chunk, w), lambda b, h, n: (rb0 + b * nch + n, b0 + h))
    wspec = lambda b0: pl.BlockSpec((CONV_A, w), lambda b, h, n: (0, b0 + h))
    tspec = lambda b0: pl.BlockSpec((None, 8, w), lambda b, h, n: (b, 0, b0 + h))
    o, s_out = pl.pallas_call(
        functools.partial(_gdn_body, chunk=chunk, hg=hg),
        grid=(nseq, nhg, nch),
        in_specs=[pspec(qb0), pspec(kb0), pspec(vb0), pspec(zb0),
                  wspec(0), wspec(nhg), wspec(2 * nhg),
                  tspec(0), tspec(nhg), tspec(2 * nhg),
                  gspec, gspec,
                  pl.BlockSpec((None, hg, DK_A, DV_A), lambda b, h, n: (b, h, 0, 0)),
                  pl.BlockSpec((1, DV_A), lambda b, h, n: (0, 0))],
        out_specs=[pl.BlockSpec((chunk, w), lambda b, h, n: (b * nch + n, h)),
                   pl.BlockSpec((None, hg, DK_A, DV_A), lambda b, h, n: (b, h, 0, 0))],
        out_shape=[jax.ShapeDtypeStruct((nseq * t, WIDTH_A), BF16),
                   jax.ShapeDtypeStruct((nseq, H_A, DK_A, DV_A), s0.dtype)],
        scratch_shapes=[pltpu.VMEM((hg, DK_A, DV_A), F32), pltpu.VMEM((8, 3 * w), F32)],
        compiler_params=_cparams("parallel", "parallel", "arbitrary"),
        name=f"gdn_c{chunk}",
    )(p, p, p, p, conv_w, conv_w, conv_w, tail, tail, tail, g, beta, s0,
      norm_g.reshape(1, DV_A).astype(F32))
    return o, s_out


SCONV_COLS = 256


def _sconv_body(gb_ref, gc_ref, hc_ref, w_ref, buf_ref, o_ref, st_ref, *, t):
    pr = gc_ref[...] * hc_ref[...]
    cw = w_ref[...]
    xfull = jnp.concatenate([buf_ref[...], pr], axis=0)
    y = pr * cw[CONV_C - 1:CONV_C, :]
    for j in range(1, CONV_C):
        y = y + pltpu.roll(xfull, j, axis=0)[8:, :] * cw[CONV_C - 1 - j:CONV_C - j, :]
    o_ref[...] = (gb_ref[...] * y).astype(o_ref.dtype)
    st_ref[...] = xfull[t:t + 8, :]


def _sconv(p, row0, nseq, t, conv_w, buf):
    cols = SCONV_COLS
    ncb = WIDTH_C // cols
    rb0 = row0 // t
    b0, c0, h0 = (PM_OFF[s] // cols for s in ("gate_b", "gate_c", "hc"))
    buf8 = jnp.pad(buf.astype(F32), ((0, 0), (8 - (CONV_C - 1), 0), (0, 0)))
    pspec = lambda o: pl.BlockSpec((t, cols), lambda b, j: (rb0 + b, o + j))
    o, st = pl.pallas_call(
        functools.partial(_sconv_body, t=t),
        grid=(nseq, ncb),
        in_specs=[pspec(b0), pspec(c0), pspec(h0),
                  pl.BlockSpec((CONV_C, cols), lambda b, j: (0, j)),
                  pl.BlockSpec((None, 8, cols), lambda b, j: (b, 0, j))],
        out_specs=[pl.BlockSpec((t, cols), lambda b, j: (b, j)),
                   pl.BlockSpec((None, 8, cols), lambda b, j: (b, 0, j))],
        out_shape=[jax.ShapeDtypeStruct((nseq * t, WIDTH_C), BF16),
                   jax.ShapeDtypeStruct((nseq, 8, WIDTH_C), F32)],
        compiler_params=_cparams("parallel", "parallel"),
        name=f"sconv_t{t}",
    )(p, p, p, conv_w, buf8)
    return o, st[:, 8 - (CONV_C - 1):, :]


BISECT_EVERY = 8
MAX_SEARCH = 2400


def _select_threshold(count_ge, lo0, hi0, n_valid, k):
    kf = jnp.float32(k)

    def cond(st):
        return jnp.logical_and(st[0] < MAX_SEARCH, jnp.min(st[-1]) < 0.5)

    def body(st):
        it, lo, hi, glo, ghi, t, side, done = st
        half = 0.5 * lo + 0.5 * hi
        cand = lo + (hi - lo) * (glo / jnp.maximum(glo - ghi, 1e-9))
        mid = jnp.where(it % BISECT_EVERY == BISECT_EVERY - 1, half, cand)
        mid = jnp.where(jnp.logical_and(mid > lo, mid < hi), mid, half)
        adjacent = jnp.logical_not(jnp.logical_and(mid > lo, mid < hi))
        first = it == 0
        mid = jnp.where(first, hi, mid)
        adjacent = jnp.logical_and(adjacent, jnp.logical_not(first))
        g = count_ge(mid) - kf
        ge = g >= 0.0
        finish = jnp.logical_or(jnp.logical_or(g == 0.0, adjacent), jnp.logical_and(first, ge))
        newly = jnp.logical_and(finish, done < 0.5)
        t = jnp.where(newly, jnp.where(adjacent, lo, mid), t)
        new_side = jnp.where(ge, 1.0, -1.0)
        damp = jnp.where(jnp.logical_and(new_side == side, jnp.logical_not(first)), 0.5, 1.0)
        glo = jnp.where(ge, g, glo * damp)
        ghi = jnp.where(ge, ghi * damp, g)
        lo = jnp.where(ge, mid, lo)
        hi = jnp.where(ge, hi, mid)
        done = jnp.where(finish, 1.0, done)
        return it + 1, lo, hi, glo, ghi, t, new_side, done

    done0 = jnp.where(n_valid > kf, 0.0, 1.0)
    st = lax.while_loop(cond, body, (jnp.int32(0), lo0, hi0, n_valid - kf, jnp.full_like(lo0, -kf),
                                     lo0, jnp.zeros_like(lo0), done0))
    return st[5]


KEY_CHUNK = 512


def _dsa_prompt_body(qi_ref, tq_ref, qb_ref, kb_ref, vb_ref, tk_ref, o_ref,
                     sc_scr, m_scr, l_scr, acc_scr, *, tq, sc, topk):
    i = pl.program_id(1)
    nck = (i * tq + tq - 1) // sc + 1
    wt = tq_ref[...].T[TAIL_WI:TAIL_WI + H_I, :] * ((D_I ** -0.5) * (H_I ** -0.5))
    qi = qi_ref[...]
    pairs = [jnp.concatenate([qi[:, (2 * p) * D_I:(2 * p + 1) * D_I],
                              qi[:, (2 * p + 1) * D_I:(2 * p + 2) * D_I]], axis=0).astype(BF16)
             for p in range(H_I // 2)]
    tpos = i * tq + lax.broadcasted_iota(jnp.int32, (1, tq), 1)

    def score_chunk(c, carry):
        r0 = pl.multiple_of(c * sc, sc)
        kic = tk_ref[pl.ds(r0, sc), :].astype(BF16)
        acc = jnp.zeros((sc, tq), F32)
        for p in range(H_I // 2):
            d = _dot_nt(kic, pairs[p])
            acc = acc + jnp.maximum(d[:, :tq], 0.0) * wt[2 * p:2 * p + 1, :]
            acc = acc + jnp.maximum(d[:, tq:], 0.0) * wt[2 * p + 1:2 * p + 2, :]
        kpos = r0 + lax.broadcasted_iota(jnp.int32, (sc, tq), 0)
        sc_scr[pl.ds(r0, sc), :] = jnp.where(kpos <= tpos, acc, -jnp.inf)
        return carry

    lax.fori_loop(0, nck, score_chunk, 0)

    def minmax_chunk(c, carry):
        mn, mx = carry
        x = sc_scr[pl.ds(pl.multiple_of(c * sc, sc), sc), :]
        mx = jnp.maximum(mx, jnp.max(x, axis=0, keepdims=True))
        mn = jnp.minimum(mn, jnp.min(jnp.where(x == -jnp.inf, jnp.inf, x), axis=0, keepdims=True))
        return mn, mx

    mn, mx = lax.fori_loop(0, nck, minmax_chunk,
                           (jnp.full((1, tq), jnp.inf, F32), jnp.full((1, tq), -jnp.inf, F32)))

    def count_ge(mid):
        def body(c, acc):
            x = sc_scr[pl.ds(pl.multiple_of(c * sc, sc), sc), :]
            hit = jnp.where(x >= mid, 1.0, 0.0).reshape(8, sc // 64, 8, tq)
            return acc + jnp.sum(jnp.sum(hit, axis=1), axis=0)
        return jnp.sum(lax.fori_loop(0, nck, body, jnp.zeros((8, tq), F32)), axis=0, keepdims=True)

    thr = _select_threshold(count_ge, mn, mx, (tpos + 1).astype(F32), topk)

    m_scr[...] = jnp.full(m_scr.shape, NEG_BIG, F32)
    l_scr[...] = jnp.zeros(l_scr.shape, F32)
    acc_scr[...] = jnp.zeros(acc_scr.shape, F32)
    q = qb_ref[...].astype(BF16)
    scale = HEAD_DIM_B ** -0.5

    def attn_chunk(c, carry):
        r0 = pl.multiple_of(c * sc, sc)
        sel = sc_scr[pl.ds(r0, sc), :] >= thr
        for n in range(N_KV_B):
            kc = kb_ref[pl.ds(r0, sc), n * HEAD_DIM_B:(n + 1) * HEAD_DIM_B].astype(BF16)
            vc = vb_ref[pl.ds(r0, sc), n * HEAD_DIM_B:(n + 1) * HEAD_DIM_B].astype(BF16)
            for g in range(GROUP_B):
                h = n * GROUP_B + g
                hs = slice(h * HEAD_DIM_B, (h + 1) * HEAD_DIM_B)
                s = jnp.where(sel, _dot_nt(kc, q[:, hs]) * scale, NEG_BIG)
                m_old = m_scr[h:h + 1, :]
                m_new = jnp.maximum(m_old, jnp.max(s, axis=0, keepdims=True))
                pexp = jnp.exp(s - m_new)
                alpha = jnp.exp(m_old - m_new)
                l_scr[h:h + 1, :] = alpha * l_scr[h:h + 1, :] + jnp.sum(pexp, axis=0, keepdims=True)
                acc_scr[hs, :] = alpha * acc_scr[hs, :] + _dot_tn(vc, pexp.astype(BF16))
                m_scr[h:h + 1, :] = m_new
        return carry

    lax.fori_loop(0, nck, attn_chunk, 0)

    for h in range(H_B):
        hs = slice(h * HEAD_DIM_B, (h + 1) * HEAD_DIM_B)
        ot = acc_scr[hs, :] / l_scr[h:h + 1, :]
        o_ref[:, hs] = ot.T.astype(o_ref.dtype)


def _dsa_prompt(p, k_new, v_new, ki_new, nseq, s):
    tq, sc = Q_BLOCK, min(KEY_CHUNK, s)
    topk = min(TOPK_MAX, s // 4)
    nqb = s // tq
    qi0 = PD_OFF["qi"] // (H_I * D_I)
    qb0 = PD_OFF["qb"] // WIDTH_B
    tl0 = TAIL_OFF // LANES
    return pl.pallas_call(
        functools.partial(_dsa_prompt_body, tq=tq, sc=sc, topk=topk),
        grid=(nseq, nqb),
        in_specs=[pl.BlockSpec((tq, H_I * D_I), lambda b, i: (b * nqb + i, qi0)),
                  pl.BlockSpec((tq, LANES), lambda b, i: (b * nqb + i, tl0)),
                  pl.BlockSpec((tq, WIDTH_B), lambda b, i: (b * nqb + i, qb0)),
                  pl.BlockSpec((s, KV_WIDTH_B), lambda b, i: (b, 0)),
                  pl.BlockSpec((s, KV_WIDTH_B), lambda b, i: (b, 0)),
                  pl.BlockSpec((s, D_I), lambda b, i: (b, 0))],
        out_specs=pl.BlockSpec((tq, WIDTH_B), lambda b, i: (b * nqb + i, 0)),
        out_shape=jax.ShapeDtypeStruct((nseq * s, WIDTH_B), BF16),
        scratch_shapes=[pltpu.VMEM((s, tq), F32), pltpu.VMEM((H_B, tq), F32),
                        pltpu.VMEM((H_B, tq), F32), pltpu.VMEM((WIDTH_B, tq), F32)],
        compiler_params=_cparams("parallel", "arbitrary"),
        name="dsa_prompt",
    )(p, p, p, k_new, v_new, ki_new)


PAGES_PER_STEP = 8


def _dsa_sample_scores_body(pt_ref, q_ref, w_ref, *rest, nsteps, pps, t):
    page_refs, new_ref, o_ref, onew_ref = rest[:pps], rest[pps], rest[pps + 1], rest[pps + 2]
    j = pl.program_id(1)
    q = q_ref[...].astype(BF16)
    wv = w_ref[...] * ((D_I ** -0.5) * (H_I ** -0.5))

    def scores(keys_t):
        d = _dot(q, keys_t.astype(BF16))
        return jnp.sum((jnp.maximum(d, 0.0) * wv).reshape(t, H_I, PAGE_SIZE), axis=1)

    @pl.when(j < nsteps)
    def _():
        o_ref[...] = jnp.concatenate([scores(r[...]) for r in page_refs], axis=1)

    @pl.when(j == nsteps)
    def _():
        knew = lax.broadcasted_iota(jnp.int32, (t, PAGE_SIZE), 1)
        tnew = lax.broadcasted_iota(jnp.int32, (t, PAGE_SIZE), 0)
        onew_ref[...] = jnp.where(knew <= tnew, scores(new_ref[...]), -jnp.inf)


def _page_specs(block, layer, nsteps, pps):
    def spec(p):
        def index(b, j, pt):
            return (layer, pt[b, jnp.minimum(j, nsteps - 1) * pps + p]) + (0,) * (len(block) - 2)
        return pl.BlockSpec(block, index)
    return [spec(p) for p in range(pps)]


def _dsa_sample_scores(page_table, q, wv, cache_kidx, layer, ki_new):
    nseq, npages = page_table.shape
    t = q.shape[1] // H_I
    pps = math.gcd(PAGES_PER_STEP, npages)
    nsteps = npages // pps
    return pl.pallas_call(
        functools.partial(_dsa_sample_scores_body, nsteps=nsteps, pps=pps, t=t),
        grid_spec=pltpu.PrefetchScalarGridSpec(
            num_scalar_prefetch=1,
            grid=(nseq, nsteps + 1),
            in_specs=[pl.BlockSpec((None, t * H_I, D_I), lambda b, j, pt: (b, 0, 0)),
                      pl.BlockSpec((None, t * H_I, PAGE_SIZE), lambda b, j, pt: (b, 0, 0)),
                      *_page_specs((None, None, D_I, PAGE_SIZE), layer, nsteps, pps),
                      pl.BlockSpec((None, D_I, PAGE_SIZE), lambda b, j, pt: (b, 0, 0))],
            out_specs=[pl.BlockSpec((None, t, pps * PAGE_SIZE),
                                    lambda b, j, pt: (b, 0, jnp.minimum(j, nsteps - 1))),
                       pl.BlockSpec((None, t, PAGE_SIZE), lambda b, j, pt: (b, 0, 0))],
        ),
        out_shape=[jax.ShapeDtypeStruct((nseq, t, npages * PAGE_SIZE), F32),
                   jax.ShapeDtypeStruct((nseq, t, PAGE_SIZE), F32)],
        compiler_params=_cparams("parallel", "arbitrary"),
        name="dsa_sample_scores",
    )(page_table, q, wv, *([cache_kidx] * pps), ki_new)


def _dsa_sample_attn_body(pt_ref, sc_ref, scn_ref, q_ref, *rest, nsteps, pps, t, past, topk):
    kpage_refs, vpage_refs = rest[:pps], rest[pps:2 * pps]
    knew_ref, vnew_ref, o_ref, thr_scr, m_scr, l_scr, acc_scr = rest[2 * pps:]
    j = pl.program_id(1)

    @pl.when(j == 0)
    def _():
        xp, xn = sc_ref[...], scn_ref[...]
        mx = jnp.maximum(jnp.max(xp, axis=1, keepdims=True), jnp.max(xn, axis=1, keepdims=True))
        mn = jnp.minimum(jnp.min(xp, axis=1, keepdims=True),
                         jnp.min(jnp.where(xn == -jnp.inf, jnp.inf, xn), axis=1, keepdims=True))

        def count_ge(mid):
            cnt = lambda hit: jnp.sum(jnp.where(hit, 1.0, 0.0), axis=1, keepdims=True)
            return cnt(sc_ref[...] >= mid) + cnt(scn_ref[...] >= mid)

        n_valid = (past + 1 + lax.broadcasted_iota(jnp.int32, (t, 1), 0)).astype(F32)
        thr = _select_threshold(count_ge, mn, mx, n_valid, topk)
        thr_scr[...] = jnp.broadcast_to(thr, thr_scr.shape)
        m_scr[...] = jnp.full(m_scr.shape, NEG_BIG, F32)
        l_scr[...] = jnp.zeros(l_scr.shape, F32)
        acc_scr[...] = jnp.zeros(acc_scr.shape, F32)

    scale = HEAD_DIM_B ** -0.5

    def attend(x, keys, vals):
        sel_t = jnp.where(x >= thr_scr[:, 0:1], 1.0, 0.0)
        sel = jnp.concatenate([sel_t] * GROUP_B, axis=0) > 0.5
        for n in range(N_KV_B):
            s = jnp.where(sel, _dot_nt(q_ref[n].astype(BF16), keys(n).astype(BF16)) * scale, NEG_BIG)
            m_old = m_scr[n]
            m_new = jnp.maximum(m_old, jnp.max(s, axis=1, keepdims=True))
            pexp = jnp.exp(s - m_new)
            alpha = jnp.exp(m_old - m_new)
            l_scr[n] = alpha * l_scr[n] + jnp.sum(pexp, axis=1, keepdims=True)
            acc_scr[n] = alpha * acc_scr[n] + _dot(pexp.astype(BF16), vals(n).astype(BF16))
            m_scr[n] = m_new

    def page_head(r, n):
        return r[pl.ds(n, PAGE_SIZE, stride=N_KV_B), :]

    @pl.when(j < nsteps)
    def _():
        width = pps * PAGE_SIZE
        x = sc_ref[:, pl.ds(pl.multiple_of(j * width, width), width)]
        attend(x,
               lambda n: jnp.concatenate([page_head(r, n) for r in kpage_refs], axis=0),
               lambda n: jnp.concatenate([page_head(r, n) for r in vpage_refs], axis=0))

    @pl.when(j == nsteps)
    def _():
        hs = lambda n: slice(n * HEAD_DIM_B, (n + 1) * HEAD_DIM_B)
        attend(scn_ref[...], lambda n: knew_ref[:, hs(n)], lambda n: vnew_ref[:, hs(n)])
        for n in range(N_KV_B):
            o_ref[n] = (acc_scr[n] / l_scr[n]).astype(o_ref.dtype)


def _dsa_sample_attn(page_table, scores, scores_new, q, cache_k, cache_v, layer, k_new, v_new):
    nseq, npages = page_table.shape
    t = scores.shape[1]
    past = npages * PAGE_SIZE
    topk = min(TOPK_MAX, (past + t) // 4)
    rows = GROUP_B * t
    pps = math.gcd(PAGES_PER_STEP, npages)
    nsteps = npages // pps
    page_block = (None, None, PAGE_SIZE * N_KV_B, HEAD_DIM_B)
    const3 = lambda b, j, pt: (b, 0, 0)
    return pl.pallas_call(
        functools.partial(_dsa_sample_attn_body, nsteps=nsteps, pps=pps, t=t, past=past, topk=topk),
        grid_spec=pltpu.PrefetchScalarGridSpec(
            num_scalar_prefetch=1,
            grid=(nseq, nsteps + 1),
            in_specs=[pl.BlockSpec((None, t, past), const3),
                      pl.BlockSpec((None, t, PAGE_SIZE), const3),
                      pl.BlockSpec((None, N_KV_B, rows, HEAD_DIM_B), lambda b, j, pt: (b, 0, 0, 0)),
                      *_page_specs(page_block, layer, nsteps, pps),
                      *_page_specs(page_block, layer, nsteps, pps),
                      pl.BlockSpec((None, PAGE_SIZE, KV_WIDTH_B), const3),
                      pl.BlockSpec((None, PAGE_SIZE, KV_WIDTH_B), const3)],
            out_specs=pl.BlockSpec((None, N_KV_B, rows, HEAD_DIM_B), lambda b, j, pt: (b, 0, 0, 0)),
            scratch_shapes=[pltpu.VMEM((t, PAGE_SIZE), F32),
                            pltpu.VMEM((N_KV_B, rows, 1), F32),
                            pltpu.VMEM((N_KV_B, rows, 1), F32),
                            pltpu.VMEM((N_KV_B, rows, HEAD_DIM_B), F32)],
        ),
        out_shape=jax.ShapeDtypeStruct((nseq, N_KV_B, rows, HEAD_DIM_B), BF16),
        compiler_params=_cparams("parallel", "arbitrary"),
        name="dsa_sample_attn",
    )(page_table, scores, scores_new, q, *([cache_k] * pps), *([cache_v] * pps), k_new, v_new)


def _dsa_sample(ps, k_s, v_s, ki_s, page_table, cache_k, cache_v, cache_kidx, layer):
    nseq = page_table.shape[0]
    t = ps.shape[0] // nseq
    seg = lambda name, width: ps[:, PD_OFF[name]:PD_OFF[name] + width]
    qi = seg("qi", H_I * D_I).reshape(nseq, t * H_I, D_I)
    wi = seg("wi", H_I).reshape(nseq, t * H_I, 1)
    wv = jnp.broadcast_to(wi, (nseq, t * H_I, PAGE_SIZE))
    pad_rows = lambda a: jnp.pad(a.reshape(nseq, t, -1), ((0, 0), (0, PAGE_SIZE - t), (0, 0)))
    kidx_t = cache_kidx.transpose(0, 1, 3, 2)
    scores, scores_new = _dsa_sample_scores(page_table, qi, wv, kidx_t, layer,
                                            pad_rows(ki_s).transpose(0, 2, 1))
    q = seg("qb", WIDTH_B).reshape(nseq, t, N_KV_B, GROUP_B, HEAD_DIM_B)
    q = q.transpose(0, 2, 3, 1, 4).reshape(nseq, N_KV_B, GROUP_B * t, HEAD_DIM_B)
    pool_rows = lambda c: c.reshape(c.shape[0], c.shape[1], PAGE_SIZE * N_KV_B, HEAD_DIM_B)
    o = _dsa_sample_attn(page_table, scores, scores_new, q, pool_rows(cache_k), pool_rows(cache_v),
                         layer, pad_rows(k_s), pad_rows(v_s))
    o = o.reshape(nseq, N_KV_B, GROUP_B, t, HEAD_DIM_B).transpose(0, 3, 1, 2, 4)
    return o.reshape(nseq * t, WIDTH_B)


def _prep_w_mid(w_in):
    sizes = dict(zip(IN_NAMES, IN_SIZES))
    z0 = min(SRC_OFF[n] for n in MID_ORDER + ("aa", "ba"))
    z1 = max(SRC_OFF[n] + sizes[n] for n in MID_ORDER)
    zone = lax.optimization_barrier(lax.slice_in_dim(w_in, z0, z1, axis=2))
    cols = [zone[:, :, SRC_OFF[n] - z0:SRC_OFF[n] - z0 + sizes[n]].astype(BF16) for n in MID_ORDER]
    cols.append(jnp.zeros(w_in.shape[:2] + (PD_WIDTH - sum(sizes[n] for n in MID_ORDER),), BF16))
    return jnp.concatenate(cols, axis=2)


def kernel(x_prompt, x_sample, cache_k, cache_v, cache_kidx, page_table, state_gdn, state_gdn_conv,
           state_sconv, final_norm, norm1, norm2, w_in, conv_a, a_log, dt_bias, gdn_norm, conv_c,
           w_branch_a, w_branch_b, w_branch_c, w_o, w_gate, w_up, w_down):
    bp, tp, d = x_prompt.shape
    bs, ts = x_sample.shape[:2]
    mp, ms = bp * tp, bs * ts
    x = jnp.concatenate([x_prompt.reshape(mp, d), x_sample.reshape(ms, d)], axis=0)
    new_p = [[] for _ in range(6)]
    new_s = [[] for _ in range(6)]
    w_mid = _prep_w_mid(w_in)
    w_down16 = w_down.astype(BF16)
    for l in range(DEPTH):
        xn = _rmsnorm(x, norm1[l], BF16)
        p = _in_proj_main(xn, w_in, l)
        pd, k_new, v_new, ki_new = _in_proj_mid(xn, w_mid, l)
        g, beta = _gdn_gates(pd, a_log[l], dt_bias[l])
        conv_w = conv_a[l].astype(F32)

        oa_p, gdn_p = _gdn(p, 0, bp, tp, min(GDN_CHUNK, tp), conv_w, g, beta,
                           jnp.zeros((bp, CONV_A - 1, 3 * WIDTH_A), F32),
                           jnp.zeros((bp, H_A, DK_A, DV_A), F32), gdn_norm[l])
        oa_s, gdn_s = _gdn(p, mp, bs, ts, ts, conv_w, g, beta,
                           state_gdn_conv[l], state_gdn[l], gdn_norm[l])
        ob_p = _dsa_prompt(pd, k_new, v_new, ki_new, bp, tp)
        ob_s = _dsa_sample(pd[mp:], k_new[mp:], v_new[mp:], ki_new[mp:], page_table,
                           cache_k, cache_v, cache_kidx, l)
        cw = conv_c[l].astype(F32)
        oc_p, sconv_p = _sconv(p, 0, bp, tp, cw, jnp.zeros((bp, CONV_C - 1, WIDTH_C), F32))
        oc_s, sconv_s = _sconv(p, mp, bs, ts, cw, state_sconv[l])

        oa = jnp.concatenate([oa_p, oa_s], axis=0)
        ob = jnp.concatenate([ob_p, ob_s], axis=0)
        oc = jnp.concatenate([oc_p, oc_s], axis=0)
        merged = _merge(oa, ob, oc, w_branch_a, w_branch_b, w_branch_c, l, p)
        x = _out_proj(merged, w_o, l, x)

        hn = _rmsnorm(x, norm2[l], BF16)
        h = _swiglu(hn, w_gate, w_up, l)
        x = _out_proj(h, w_down16, l, x, name="ffn_down")

        nb = CONV_A - 1
        tail_rows = lambda r0, t: lax.slice(p, (r0 + max(t - nb, 0), 0), (r0 + t, 3 * WIDTH_A))
        gconv_p = jnp.stack([tail_rows(b * tp, tp) for b in range(bp)])
        gconv_s = jnp.stack([tail_rows(mp + b * ts, ts) for b in range(bs)])
        if tp < nb:
            gconv_p = jnp.concatenate([jnp.zeros((bp, nb - tp, 3 * WIDTH_A), F32), gconv_p], axis=1)
        if ts < nb:
            gconv_s = jnp.concatenate([state_gdn_conv[l].astype(F32)[:, ts:], gconv_s], axis=1)
        kv = lambda a, rows, b, t: a[rows].reshape(b, t, N_KV_B, HEAD_DIM_B)
        rp, rs = slice(0, mp), slice(mp, mp + ms)
        for lst, val in zip(new_p, (kv(k_new, rp, bp, tp), kv(v_new, rp, bp, tp),
                                    ki_new[rp].reshape(bp, tp, D_I), gdn_p, gconv_p, sconv_p)):
            lst.append(val)
        for lst, val in zip(new_s, (kv(k_new, rs, bs, ts), kv(v_new, rs, bs, ts),
                                    ki_new[rs].reshape(bs, ts, D_I), gdn_s, gconv_s, sconv_s)):
            lst.append(val)

    y_prompt = _rmsnorm(x, final_norm, F32, 0, mp).reshape(bp, tp, d)
    y_sample = _rmsnorm(x, final_norm, F32, mp, ms).reshape(bs, ts, d)
    outs_p = [jnp.stack(a) for a in new_p]
    outs_s = [jnp.stack(a) for a in new_s]
    return (y_prompt, y_sample, *outs_p, *outs_s)
```

```python
import functools
import math

import jax
import jax.numpy as jnp
from jax import lax
from jax.experimental import pallas as pl
from jax.experimental.pallas import tpu as pltpu

F32 = jnp.float32
BF16 = jnp.bfloat16

D_MODEL = 4096
DEPTH = 2
PAGE_SIZE = 128
H_A = 16
DK_A = 128
DV_A = 128
WIDTH_A = H_A * DV_A
CONV_A = 4
GDN_CHUNK = 64
H_B = 8
N_KV_B = 2
GROUP_B = H_B // N_KV_B
HEAD_DIM_B = 128
WIDTH_B = H_B * HEAD_DIM_B
KV_WIDTH_B = N_KV_B * HEAD_DIM_B
H_I = 32
D_I = 64
TOPK_MAX = 256
Q_BLOCK = 128
WIDTH_C = 1024
CONV_C = 3
D_FF = -(-8 * D_MODEL // (3 * 256)) * 256
EPS = 1e-6

IN_NAMES = ("qa", "ka", "va", "za", "aa", "ba", "qb", "kb", "vb", "qi", "ki", "wi",
            "gate_b", "gate_c", "hc", "ga", "gb", "gc")
IN_SIZES = (WIDTH_A, WIDTH_A, WIDTH_A, WIDTH_A, H_A, H_A,
            WIDTH_B, KV_WIDTH_B, KV_WIDTH_B, H_I * D_I, D_I, H_I,
            WIDTH_C, WIDTH_C, WIDTH_C, D_MODEL, D_MODEL, D_MODEL)

LANES = 128
N_BLOCK = 512
VMEM_LIMIT = 56 * 1024 * 1024
ACT_VMEM_BYTES = 32 * 1024 * 1024
NEG_BIG = -1e30

MAIN_ORDER = ("qa", "ka", "va", "za", "gate_b", "gate_c", "hc", "ga", "gb", "gc")
MID_ORDER = ("qi", "qb", "kb", "vb", "ki", "aa", "ba", "wi")


def _offsets(order):
    sizes = dict(zip(IN_NAMES, IN_SIZES))
    off, out = 0, {}
    for name in order:
        out[name] = off
        off += sizes[name]
    return out, off


SRC_OFF, _ = _offsets(IN_NAMES)
PM_OFF, PM_WIDTH = _offsets(MAIN_ORDER)
PD_OFF, _mid_cols = _offsets(MID_ORDER)
PD_WIDTH = -(-_mid_cols // N_BLOCK) * N_BLOCK
MAIN_RUN0 = PM_OFF["gate_b"]
MAIN_SHIFT = SRC_OFF["gate_b"] - MAIN_RUN0
assert PM_WIDTH % N_BLOCK == 0 and MAIN_RUN0 % N_BLOCK == 0 and MAIN_SHIFT % LANES == 0
assert all(SRC_OFF[n] == PM_OFF[n] for n in MAIN_ORDER[:4])
assert all(SRC_OFF[n] == PM_OFF[n] + MAIN_SHIFT for n in MAIN_ORDER[4:])
TAIL_OFF = PD_OFF["ki"]
TAIL_AA = PD_OFF["aa"] - TAIL_OFF
TAIL_BA = PD_OFF["ba"] - TAIL_OFF
TAIL_WI = PD_OFF["wi"] - TAIL_OFF
assert TAIL_OFF % N_BLOCK == 0 and PD_OFF["wi"] + H_I - TAIL_OFF == LANES


def _cparams(*sem):
    return pltpu.CompilerParams(dimension_semantics=sem, vmem_limit_bytes=VMEM_LIMIT)


def _dot(a, b):
    return jnp.dot(a, b, preferred_element_type=F32)


def _dot_nt(a, b):
    return lax.dot_general(a, b, (((1,), (1,)), ((), ())), preferred_element_type=F32)


def _dot_tn(a, b):
    return lax.dot_general(a, b, (((0,), (0,)), ((), ())), preferred_element_type=F32)


def _dot_hi(a, b):
    return jnp.dot(a, b, preferred_element_type=F32, precision=lax.Precision.HIGHEST)


def _sigmoid(x):
    return jax.nn.sigmoid(x)


def _silu(x):
    return x * jax.nn.sigmoid(x)


def _rmsnorm_body(x_ref, g_ref, o_ref):
    x = x_ref[...]
    ms = jnp.mean(x * x, axis=-1, keepdims=True)
    o_ref[...] = (x * lax.rsqrt(ms + EPS) * g_ref[...]).astype(o_ref.dtype)


def _row_block(m):
    for bm in (192, 128, 64, 32, 16, 8):
        if m % bm == 0:
            return bm
    raise ValueError(f"unsupported row count {m}")


def _rmsnorm(x, g, out_dtype, row0=0, nrows=None):
    d = x.shape[1]
    m = x.shape[0] - row0 if nrows is None else nrows
    bm = _row_block(math.gcd(m, row0) if row0 else m)
    rb0 = row0 // bm
    return pl.pallas_call(
        _rmsnorm_body,
        grid=(m // bm,),
        in_specs=[pl.BlockSpec((bm, d), lambda i: (rb0 + i, 0)), pl.BlockSpec((1, d), lambda i: (0, 0))],
        out_specs=pl.BlockSpec((bm, d), lambda i: (i, 0)),
        out_shape=jax.ShapeDtypeStruct((m, d), out_dtype),
        compiler_params=_cparams("parallel"),
        name="rmsnorm",
    )(x, g.reshape(1, d).astype(F32))


def _gdn_gates_body(t_ref, alog_ref, dtb_ref, g_ref, b_ref):
    t = t_ref[...]
    aa = t[:, TAIL_AA:TAIL_AA + H_A]
    ba = t[:, TAIL_BA:TAIL_BA + H_A]
    x = aa + dtb_ref[...]
    softplus = jnp.maximum(x, 0.0) + jnp.log1p(jnp.exp(-jnp.abs(x)))
    g_ref[...] = -jnp.exp(alog_ref[...]) * softplus
    b_ref[...] = _sigmoid(ba)


def _gdn_gates(p, a_log, dt_bias):
    m = p.shape[0]
    bm = _row_block(m)
    tail_blk = TAIL_OFF // LANES
    return pl.pallas_call(
        _gdn_gates_body,
        grid=(m // bm,),
        in_specs=[pl.BlockSpec((bm, LANES), lambda i: (i, tail_blk)),
                  pl.BlockSpec((1, H_A), lambda i: (0, 0)),
                  pl.BlockSpec((1, H_A), lambda i: (0, 0))],
        out_specs=[pl.BlockSpec((bm, H_A), lambda i: (i, 0)), pl.BlockSpec((bm, H_A), lambda i: (i, 0))],
        out_shape=[jax.ShapeDtypeStruct((m, H_A), F32)] * 2,
        compiler_params=_cparams("parallel"),
        name="gdn_gates",
    )(p, a_log.reshape(1, H_A).astype(F32), dt_bias.reshape(1, H_A).astype(F32))


def _big_row_block(m):
    for bm in (1376, 1024, 688, 512, 256, 128, 64):
        if m % bm == 0:
            return bm
    raise ValueError(f"unsupported row count {m}")


def _out_proj_body(a_ref, w_ref, r_ref, o_ref):
    o_ref[...] = r_ref[...] + _dot(a_ref[...], w_ref[...].astype(BF16))


def _out_proj(a, w, layer, residual, name="out_proj"):
    m, k = a.shape
    n = w.shape[2]
    bn = N_BLOCK // 2
    bm = next(b for b in (1376, 1024, 688, 512, 256, 128, 64)
              if m % b == 0 and 2 * b * k * 2 <= ACT_VMEM_BYTES)
    return pl.pallas_call(
        _out_proj_body,
        grid=(m // bm, n // bn),
        in_specs=[pl.BlockSpec((bm, k), lambda i, j: (i, 0)),
                  pl.BlockSpec((None, k, bn), lambda i, j: (layer, 0, j)),
                  pl.BlockSpec((bm, bn), lambda i, j: (i, j))],
        out_specs=pl.BlockSpec((bm, bn), lambda i, j: (i, j)),
        out_shape=jax.ShapeDtypeStruct((m, n), F32),
        compiler_params=_cparams("parallel", "arbitrary"),
        name=name,
    )(a, w, residual)


IN_PROJ_PIECES = N_BLOCK // LANES


def _in_proj_main_body(a_ref, *rest):
    w_refs, p_ref = rest[:IN_PROJ_PIECES], rest[IN_PROJ_PIECES]
    w = jnp.concatenate([r[...].astype(BF16) for r in w_refs], axis=1)
    p_ref[...] = _dot(a_ref[...], w)


def _in_proj_main(a, w_in, layer):
    m, k = a.shape
    bm, bn = _big_row_block(m), N_BLOCK
    run0_blk, shift_blk = MAIN_RUN0 // bn, MAIN_SHIFT // LANES

    def piece(q):
        def index(i, j):
            return (layer, 0, IN_PROJ_PIECES * j + q + jnp.where(j >= run0_blk, shift_blk, 0))
        return pl.BlockSpec((None, k, LANES), index)

    return pl.pallas_call(
        _in_proj_main_body,
        grid=(m // bm, PM_WIDTH // bn),
        in_specs=[pl.BlockSpec((bm, k), lambda i, j: (i, 0))] + [piece(q) for q in range(IN_PROJ_PIECES)],
        out_specs=pl.BlockSpec((bm, bn), lambda i, j: (i, j)),
        out_shape=jax.ShapeDtypeStruct((m, PM_WIDTH), F32),
        compiler_params=_cparams("parallel", "arbitrary"),
        name="in_proj_main",
    )(a, *([w_in] * IN_PROJ_PIECES))


def _in_proj_mid_body(a_ref, w_ref, p_ref, k_ref, v_ref, ki_ref, *, jkv, jtail):
    acc = _dot(a_ref[...], w_ref[...])
    p_ref[...] = acc
    j = pl.program_id(1)

    @pl.when(j == jkv)
    def _():
        k_ref[...] = acc[:, :KV_WIDTH_B]
        v_ref[...] = acc[:, KV_WIDTH_B:2 * KV_WIDTH_B]

    @pl.when(j == jtail)
    def _():
        ki_ref[...] = acc[:, :D_I]


def _in_proj_mid(a, w_mid, layer):
    m, k = a.shape
    bm, bn = _big_row_block(m), N_BLOCK
    assert PD_OFF["kb"] % bn == 0 and PD_OFF["vb"] == PD_OFF["kb"] + KV_WIDTH_B
    row = lambda i, j: (i, 0)
    return pl.pallas_call(
        functools.partial(_in_proj_mid_body, jkv=PD_OFF["kb"] // bn, jtail=TAIL_OFF // bn),
        grid=(m // bm, PD_WIDTH // bn),
        in_specs=[pl.BlockSpec((bm, k), row), pl.BlockSpec((None, k, bn), lambda i, j: (layer, 0, j))],
        out_specs=[pl.BlockSpec((bm, bn), lambda i, j: (i, j)),
                   pl.BlockSpec((bm, KV_WIDTH_B), row), pl.BlockSpec((bm, KV_WIDTH_B), row),
                   pl.BlockSpec((bm, D_I), row)],
        out_shape=[jax.ShapeDtypeStruct((m, PD_WIDTH), F32), jax.ShapeDtypeStruct((m, KV_WIDTH_B), F32),
                   jax.ShapeDtypeStruct((m, KV_WIDTH_B), F32), jax.ShapeDtypeStruct((m, D_I), F32)],
        compiler_params=_cparams("parallel", "arbitrary"),
        name="in_proj_mid",
    )(a, w_mid)


def _swiglu_body(a_ref, wg_ref, wu_ref, o_ref):
    a = a_ref[...]
    g = _dot(a, wg_ref[...].astype(BF16))
    u = _dot(a, wu_ref[...].astype(BF16))
    o_ref[...] = (_silu(g) * u).astype(o_ref.dtype)


def _swiglu(a, w_gate, w_up, layer):
    m, k = a.shape
    n = w_gate.shape[2]
    bm, bn = _big_row_block(m), N_BLOCK // 2
    assert n % bn == 0
    wspec = pl.BlockSpec((None, k, bn), lambda i, j: (layer, 0, j))
    return pl.pallas_call(
        _swiglu_body,
        grid=(m // bm, n // bn),
        in_specs=[pl.BlockSpec((bm, k), lambda i, j: (i, 0)), wspec, wspec],
        out_specs=pl.BlockSpec((bm, bn), lambda i, j: (i, j)),
        out_shape=jax.ShapeDtypeStruct((m, n), BF16),
        compiler_params=_cparams("parallel", "arbitrary"),
        name="swiglu",
    )(a, w_gate, w_up)


def _merge_body(oa_ref, ob_ref, oc_ref, wa_ref, wb_ref, wc_ref, ga_ref, gb_ref, gc_ref, o_ref):
    acc = _sigmoid(ga_ref[...]) * _dot(oa_ref[...], wa_ref[...].astype(BF16))
    acc += _sigmoid(gb_ref[...]) * _dot(ob_ref[...], wb_ref[...].astype(BF16))
    acc += _sigmoid(gc_ref[...]) * _dot(oc_ref[...], wc_ref[...].astype(BF16))
    o_ref[...] = acc.astype(o_ref.dtype)


def _merge(oa, ob, oc, wa, wb, wc, layer, p):
    m = oa.shape[0]
    n = wa.shape[2]
    bm, bn = _big_row_block(m), N_BLOCK // 2
    ga0, gb0, gc0 = (PM_OFF[s] // bn for s in ("ga", "gb", "gc"))
    row = lambda i, j: (i, 0)
    wspec = lambda w: pl.BlockSpec((None, w.shape[1], bn), lambda i, j: (layer, 0, j))
    return pl.pallas_call(
        _merge_body,
        grid=(m // bm, n // bn),
        in_specs=[pl.BlockSpec((bm, oa.shape[1]), row), pl.BlockSpec((bm, ob.shape[1]), row),
                  pl.BlockSpec((bm, oc.shape[1]), row),
                  wspec(wa), wspec(wb), wspec(wc),
                  pl.BlockSpec((bm, bn), lambda i, j: (i, ga0 + j)),
                  pl.BlockSpec((bm, bn), lambda i, j: (i, gb0 + j)),
                  pl.BlockSpec((bm, bn), lambda i, j: (i, gc0 + j))],
        out_specs=pl.BlockSpec((bm, bn), lambda i, j: (i, j)),
        out_shape=jax.ShapeDtypeStruct((m, n), BF16),
        compiler_params=_cparams("parallel", "arbitrary"),
        name="merge",
    )(oa, ob, oc, wa, wb, wc, p, p, p)


GDN_HEADS_PER_STEP = 16


def _gdn_body(q_ref, k_ref, v_ref, z_ref, wq_ref, wk_ref, wv_ref, tq_ref, tk_ref, tv_ref,
              g_ref, b_ref, s0_ref, ng_ref, o_ref, sout_ref, s_scr, tail_scr,
              *, chunk, hg):
    c = chunk
    w = hg * DK_A
    n = pl.program_id(2)

    @pl.when(n == 0)
    def _():
        s_scr[...] = s0_ref[...].astype(F32)
        tail_scr[:, 0:w] = tq_ref[...]
        tail_scr[:, w:2 * w] = tk_ref[...]
        tail_scr[:, 2 * w:3 * w] = tv_ref[...]

    act = []
    for idx, (x_ref, w_ref) in enumerate(((q_ref, wq_ref), (k_ref, wk_ref), (v_ref, wv_ref))):
        x = x_ref[...]
        cw = w_ref[...]
        xfull = jnp.concatenate([tail_scr[:, idx * w:(idx + 1) * w], x], axis=0)
        y = x * cw[CONV_A - 1:CONV_A, :]
        for j in range(1, CONV_A):
            y = y + pltpu.roll(xfull, j, axis=0)[8:, :] * cw[CONV_A - 1 - j:CONV_A - j, :]
        tail_scr[:, idx * w:(idx + 1) * w] = xfull[c:c + 8, :]
        act.append(_silu(y))
    qs, ks, vs = act
    z = z_ref[...]

    ii = lax.broadcasted_iota(jnp.int32, (c, c), 0)
    jj = lax.broadcasted_iota(jnp.int32, (c, c), 1)
    incl = jj <= ii
    strict = jj < ii
    eye = (ii == jj).astype(F32)
    n_double = int(math.log2(c)) - 1
    ng = ng_ref[...]
    b_blk = b_ref[...]
    gcum_all = _dot_hi(incl.astype(F32), g_ref[...])
    gcum_t = gcum_all.T

    heads = range(hg)
    hsl = [slice(hh * DK_A, (hh + 1) * DK_A) for hh in heads]
    q_l, k_l, k16_l, decay_l, egc_l, kdec_l, glast_l, pw_l, tinv_l, rhs_l = ([] for _ in range(10))
    for hh in heads:
        q = qs[:, hsl[hh]]
        k = ks[:, hsl[hh]]
        q = q * lax.rsqrt(jnp.sum(q * q, axis=-1, keepdims=True) + EPS) * (DK_A ** -0.5)
        k = k * lax.rsqrt(jnp.sum(k * k, axis=-1, keepdims=True) + EPS)
        bc = b_blk[:, hh:hh + 1]
        gcum_c = gcum_all[:, hh:hh + 1]
        gcum_r = gcum_t[hh:hh + 1, :]
        decay = jnp.exp(jnp.where(incl, gcum_c - gcum_r, -jnp.inf))
        kb = k * bc
        k16 = k.astype(BF16)
        egc = jnp.exp(gcum_c)
        glast = gcum_c[c - 1:c, :]
        mm = jnp.where(strict, _dot_nt(kb.astype(BF16), k16) * decay, 0.0)
        q_l.append(q)
        k_l.append(k)
        k16_l.append(k16)
        decay_l.append(decay)
        egc_l.append(egc)
        glast_l.append(glast)
        kdec_l.append(jnp.exp(glast - gcum_c))
        rhs_l.append(jnp.concatenate([vs[:, hsl[hh]] * bc, kb * egc], axis=1).astype(BF16))
        pw_l.append(-mm)
        tinv_l.append(eye - mm)
    for _ in range(n_double):
        for hh in heads:
            pw16 = pw_l[hh].astype(BF16)
            pw_l[hh] = _dot(pw16, pw16)
        for hh in heads:
            tinv_l[hh] = tinv_l[hh] + _dot(tinv_l[hh].astype(BF16), pw_l[hh].astype(BF16))
    sol_l = [_dot(tinv_l[hh].astype(BF16), rhs_l[hh]) for hh in heads]
    attn_l = [(_dot_nt(q_l[hh].astype(BF16), k16_l[hh]) * decay_l[hh]).astype(BF16) for hh in heads]
    s_l = [s_scr[hh] for hh in heads]
    s16_l = [s_l[hh].astype(BF16) for hh in heads]
    v16_l = [(sol_l[hh][:, :DV_A] - _dot(sol_l[hh][:, DV_A:].astype(BF16), s16_l[hh])).astype(BF16)
             for hh in heads]
    o_l = [_dot((q_l[hh] * egc_l[hh]).astype(BF16), s16_l[hh]) + _dot(attn_l[hh], v16_l[hh]) for hh in heads]
    snew_l = [s_l[hh] * jnp.exp(glast_l[hh]) + _dot_tn((k_l[hh] * kdec_l[hh]).astype(BF16), v16_l[hh])
              for hh in heads]
    s_scr[...] = jnp.stack(snew_l, axis=0)
    o_l = [o * lax.rsqrt(jnp.mean(o * o, axis=-1, keepdims=True) + EPS) * ng for o in o_l]
    o_ref[...] = (jnp.concatenate(o_l, axis=1) * _silu(z)).astype(o_ref.dtype)

    @pl.when(n == pl.num_programs(2) - 1)
    def _():
        sout_ref[...] = s_scr[...].astype(sout_ref.dtype)


def _gdn(p, row0, nseq, t, chunk, conv_w, g, beta, conv_buf, s0, norm_g):
    hg = GDN_HEADS_PER_STEP
    assert hg == H_A
    w = hg * DK_A
    nch = t // chunk
    nhg = H_A // hg
    rb0 = row0 // chunk
    gspec = pl.BlockSpec((chunk, H_A), lambda b, h, n: (rb0 + b * nch + n, 0))
    tail = jnp.pad(conv_buf.astype(F32), ((0, 0), (8 - (CONV_A - 1), 0), (0, 0)))
    qb0, kb0, vb0, zb0 = (PM_OFF[s] // w for s in ("qa", "ka", "va", "za"))
    pspec = lambda b0: pl.BlockSpec((chunk, w), lambda b, h, n: (rb0 + b * nch + n, b0 + h))
    wspec = lambda b0: pl.BlockSpec((CONV_A, w), lambda b, h, n: (0, b0 + h))
    tspec = lambda b0: pl.BlockSpec((None, 8, w), lambda b, h, n: (b, 0, b0 + h))
    o, s_out = pl.pallas_call(
        functools.partial(_gdn_body, chunk=chunk, hg=hg),
        grid=(nseq, nhg, nch),
        in_specs=[pspec(qb0), pspec(kb0), pspec(vb0), pspec(zb0),
                  wspec(0), wspec(nhg), wspec(2 * nhg),
                  tspec(0), tspec(nhg), tspec(2 * nhg),
                  gspec, gspec,
                  pl.BlockSpec((None, hg, DK_A, DV_A), lambda b, h, n: (b, h, 0, 0)),
                  pl.BlockSpec((1, DV_A), lambda b, h, n: (0, 0))],
        out_specs=[pl.BlockSpec((chunk, w), lambda b, h, n: (b * nch + n, h)),
                   pl.BlockSpec((None, hg, DK_A, DV_A), lambda b, h, n: (b, h, 0, 0))],
        out_shape=[jax.ShapeDtypeStruct((nseq * t, WIDTH_A), BF16),
                   jax.ShapeDtypeStruct((nseq, H_A, DK_A, DV_A), s0.dtype)],
        scratch_shapes=[pltpu.VMEM((hg, DK_A, DV_A), F32), pltpu.VMEM((8, 3 * w), F32)],
        compiler_params=_cparams("parallel", "parallel", "arbitrary"),
        name=f"gdn_c{chunk}",
    )(p, p, p, p, conv_w, conv_w, conv_w, tail, tail, tail, g, beta, s0,
      norm_g.reshape(1, DV_A).astype(F32))
    return o, s_out


SCONV_COLS = 256


def _sconv_body(gb_ref, gc_ref, hc_ref, w_ref, buf_ref, o_ref, st_ref, *, t):
    pr = gc_ref[...] * hc_ref[...]
    cw = w_ref[...]
    xfull = jnp.concatenate([buf_ref[...], pr], axis=0)
    y = pr * cw[CONV_C - 1:CONV_C, :]
    for j in range(1, CONV_C):
        y = y + pltpu.roll(xfull, j, axis=0)[8:, :] * cw[CONV_C - 1 - j:CONV_C - j, :]
    o_ref[...] = (gb_ref[...] * y).astype(o_ref.dtype)
    st_ref[...] = xfull[t:t + 8, :]


def _sconv(p, row0, nseq, t, conv_w, buf):
    cols = SCONV_COLS
    ncb = WIDTH_C // cols
    rb0 = row0 // t
    b0, c0, h0 = (PM_OFF[s] // cols for s in ("gate_b", "gate_c", "hc"))
    buf8 = jnp.pad(buf.astype(F32), ((0, 0), (8 - (CONV_C - 1), 0), (0, 0)))
    pspec = lambda o: pl.BlockSpec((t, cols), lambda b, j: (rb0 + b, o + j))
    o, st = pl.pallas_call(
        functools.partial(_sconv_body, t=t),
        grid=(nseq, ncb),
        in_specs=[pspec(b0), pspec(c0), pspec(h0),
                  pl.BlockSpec((CONV_C, cols), lambda b, j: (0, j)),
                  pl.BlockSpec((None, 8, cols), lambda b, j: (b, 0, j))],
        out_specs=[pl.BlockSpec((t, cols), lambda b, j: (b, j)),
                   pl.BlockSpec((None, 8, cols), lambda b, j: (b, 0, j))],
        out_shape=[jax.ShapeDtypeStruct((nseq * t, WIDTH_C), BF16),
                   jax.ShapeDtypeStruct((nseq, 8, WIDTH_C), F32)],
        compiler_params=_cparams("parallel", "parallel"),
        name=f"sconv_t{t}",
    )(p, p, p, conv_w, buf8)
    return o, st[:, 8 - (CONV_C - 1):, :]


BISECT_EVERY = 8
MAX_SEARCH = 2400


def _select_threshold(count_ge, lo0, hi0, n_valid, k):
    kf = jnp.float32(k)

    def cond(st):
        return jnp.logical_and(st[0] < MAX_SEARCH, jnp.min(st[-1]) < 0.5)

    def step(st):
        it, lo, hi, glo, ghi, t, side, done = st
        half = 0.5 * lo + 0.5 * hi
        cand = lo + (hi - lo) * (glo / jnp.maximum(glo - ghi, 1e-9))
        mid = jnp.where(it % BISECT_EVERY == BISECT_EVERY - 1, half, cand)
        mid = jnp.where(jnp.logical_and(mid > lo, mid < hi), mid, half)
        adjacent = jnp.logical_not(jnp.logical_and(mid > lo, mid < hi))
        first = it == 0
        mid = jnp.where(first, hi, mid)
        adjacent = jnp.logical_and(adjacent, jnp.logical_not(first))
        g = count_ge(mid) - kf
        ge = g >= 0.0
        finish = jnp.logical_or(jnp.logical_or(g == 0.0, adjacent), jnp.logical_and(first, ge))
        newly = jnp.logical_and(finish, done < 0.5)
        t = jnp.where(newly, jnp.where(adjacent, lo, mid), t)
        new_side = jnp.where(ge, 1.0, -1.0)
        damp = jnp.where(jnp.logical_and(new_side == side, jnp.logical_not(first)), 0.5, 1.0)
        glo = jnp.where(ge, g, glo * damp)
        ghi = jnp.where(ge, ghi * damp, g)
        lo = jnp.where(ge, mid, lo)
        hi = jnp.where(ge, hi, mid)
        done = jnp.where(finish, 1.0, done)
        return it + 1, lo, hi, glo, ghi, t, new_side, done

    done0 = jnp.where(n_valid > kf, 0.0, 1.0)
    st = lax.while_loop(cond, lambda st: step(step(st)), (jnp.int32(0), lo0, hi0, n_valid - kf, jnp.full_like(lo0, -kf),
                                     lo0, jnp.zeros_like(lo0), done0))
    return st[5]


KEY_CHUNK = 512


def _dsa_prompt_body(qi_ref, tq_ref, qb_ref, kb_ref, vb_ref, tk_ref, o_ref,
                     sc_scr, m_scr, l_scr, acc_scr, *, tq, sc, topk):
    i = pl.program_id(1)
    nck = (i * tq + tq - 1) // sc + 1
    wt = tq_ref[...].T[TAIL_WI:TAIL_WI + H_I, :] * ((D_I ** -0.5) * (H_I ** -0.5))
    qi = qi_ref[...]
    pairs = [jnp.concatenate([qi[:, (2 * p) * D_I:(2 * p + 1) * D_I],
                              qi[:, (2 * p + 1) * D_I:(2 * p + 2) * D_I]], axis=0).astype(BF16)
             for p in range(H_I // 2)]
    tpos = i * tq + lax.broadcasted_iota(jnp.int32, (1, tq), 1)

    def score_chunk(c, carry):
        r0 = pl.multiple_of(c * sc, sc)
        kic = tk_ref[pl.ds(r0, sc), :].astype(BF16)
        acc = jnp.zeros((sc, tq), F32)
        for p in range(H_I // 2):
            d = _dot_nt(kic, pairs[p])
            acc = acc + jnp.maximum(d[:, :tq], 0.0) * wt[2 * p:2 * p + 1, :]
            acc = acc + jnp.maximum(d[:, tq:], 0.0) * wt[2 * p + 1:2 * p + 2, :]
        kpos = r0 + lax.broadcasted_iota(jnp.int32, (sc, tq), 0)
        sc_scr[pl.ds(r0, sc), :] = jnp.where(kpos <= tpos, acc, -jnp.inf)
        return carry

    lax.fori_loop(0, nck, score_chunk, 0)

    def minmax_chunk(c, carry):
        mn, mx = carry
        x = sc_scr[pl.ds(pl.multiple_of(c * sc, sc), sc), :]
        mx = jnp.maximum(mx, jnp.max(x, axis=0, keepdims=True))
        mn = jnp.minimum(mn, jnp.min(jnp.where(x == -jnp.inf, jnp.inf, x), axis=0, keepdims=True))
        return mn, mx

    mn, mx = lax.fori_loop(0, nck, minmax_chunk,
                           (jnp.full((1, tq), jnp.inf, F32), jnp.full((1, tq), -jnp.inf, F32)))

    def count_ge(mid):
        def body(c, acc):
            x = sc_scr[pl.ds(pl.multiple_of(c * sc, sc), sc), :]
            hit = jnp.where(x >= mid, 1.0, 0.0).reshape(8, sc // 64, 8, tq)
            return acc + jnp.sum(jnp.sum(hit, axis=1), axis=0)
        return jnp.sum(lax.fori_loop(0, nck, body, jnp.zeros((8, tq), F32)), axis=0, keepdims=True)

    thr = _select_threshold(count_ge, mn, mx, (tpos + 1).astype(F32), topk)

    def count_gt(c, acc):
        x = sc_scr[pl.ds(pl.multiple_of(c * sc, sc), sc), :]
        hit = jnp.where(x > thr, 1.0, 0.0).reshape(8, sc // 64, 8, tq)
        return acc + jnp.sum(jnp.sum(hit, axis=1), axis=0)
    n_above = jnp.sum(lax.fori_loop(0, nck, count_gt, jnp.zeros((8, tq), F32)), axis=0, keepdims=True)
    crowded = count_ge(thr) > topk
    places = topk - n_above

    @pl.when(jnp.max(jnp.where(crowded, 1.0, 0.0)) > 0.5)
    def _():
        earlier = (lax.broadcasted_iota(jnp.int32, (sc, sc), 1)
                   < lax.broadcasted_iota(jnp.int32, (sc, sc), 0)).astype(BF16)

        def retire(c, seen):
            rows = pl.ds(pl.multiple_of(c * sc, sc), sc)
            x = sc_scr[rows, :]
            tie = jnp.logical_and(x == thr, crowded)
            tie_f = jnp.where(tie, 1.0, 0.0)
            rank = seen + _dot(earlier, tie_f.astype(BF16))
            sc_scr[rows, :] = jnp.where(jnp.logical_and(tie, rank >= places), -jnp.inf, x)
            return seen + jnp.sum(tie_f, axis=0, keepdims=True)

        lax.fori_loop(0, nck, retire, jnp.zeros((1, tq), F32))

    m_scr[...] = jnp.full(m_scr.shape, NEG_BIG, F32)
    l_scr[...] = jnp.zeros(l_scr.shape, F32)
    acc_scr[...] = jnp.zeros(acc_scr.shape, F32)
    q = qb_ref[...].astype(BF16)
    scale = HEAD_DIM_B ** -0.5

    def attn_chunk(c, carry):
        r0 = pl.multiple_of(c * sc, sc)
        sel = sc_scr[pl.ds(r0, sc), :] >= thr
        for n in range(N_KV_B):
            kc = kb_ref[pl.ds(r0, sc), n * HEAD_DIM_B:(n + 1) * HEAD_DIM_B].astype(BF16)
            vc = vb_ref[pl.ds(r0, sc), n * HEAD_DIM_B:(n + 1) * HEAD_DIM_B].astype(BF16)
            for g in range(GROUP_B):
                h = n * GROUP_B + g
                hs = slice(h * HEAD_DIM_B, (h + 1) * HEAD_DIM_B)
                s = jnp.where(sel, _dot_nt(kc, q[:, hs]) * scale, NEG_BIG)
                m_old = m_scr[h:h + 1, :]
                m_new = jnp.maximum(m_old, jnp.max(s, axis=0, keepdims=True))
                pexp = jnp.exp(s - m_new)
                alpha = jnp.exp(m_old - m_new)
                l_scr[h:h + 1, :] = alpha * l_scr[h:h + 1, :] + jnp.sum(pexp, axis=0, keepdims=True)
                acc_scr[hs, :] = alpha * acc_scr[hs, :] + _dot_tn(vc, pexp.astype(BF16))
                m_scr[h:h + 1, :] = m_new
        return carry

    lax.fori_loop(0, nck, attn_chunk, 0)

    for h in range(H_B):
        hs = slice(h * HEAD_DIM_B, (h + 1) * HEAD_DIM_B)
        ot = acc_scr[hs, :] / l_scr[h:h + 1, :]
        o_ref[:, hs] = ot.T.astype(o_ref.dtype)


def _dsa_prompt(p, k_new, v_new, ki_new, nseq, s):
    tq, sc = Q_BLOCK, min(KEY_CHUNK, s)
    topk = min(TOPK_MAX, s // 4)
    nqb = s // tq
    qi0 = PD_OFF["qi"] // (H_I * D_I)
    qb0 = PD_OFF["qb"] // WIDTH_B
    tl0 = TAIL_OFF // LANES
    return pl.pallas_call(
        functools.partial(_dsa_prompt_body, tq=tq, sc=sc, topk=topk),
        grid=(nseq, nqb),
        in_specs=[pl.BlockSpec((tq, H_I * D_I), lambda b, i: (b * nqb + i, qi0)),
                  pl.BlockSpec((tq, LANES), lambda b, i: (b * nqb + i, tl0)),
                  pl.BlockSpec((tq, WIDTH_B), lambda b, i: (b * nqb + i, qb0)),
                  pl.BlockSpec((s, KV_WIDTH_B), lambda b, i: (b, 0)),
                  pl.BlockSpec((s, KV_WIDTH_B), lambda b, i: (b, 0)),
                  pl.BlockSpec((s, D_I), lambda b, i: (b, 0))],
        out_specs=pl.BlockSpec((tq, WIDTH_B), lambda b, i: (b * nqb + i, 0)),
        out_shape=jax.ShapeDtypeStruct((nseq * s, WIDTH_B), BF16),
        scratch_shapes=[pltpu.VMEM((s, tq), F32), pltpu.VMEM((H_B, tq), F32),
                        pltpu.VMEM((H_B, tq), F32), pltpu.VMEM((WIDTH_B, tq), F32)],
        compiler_params=_cparams("parallel", "arbitrary"),
        name="dsa_prompt",
    )(p, p, p, k_new, v_new, ki_new)


PAGES_PER_STEP = 16
TIE_CHUNK = 512


def _dsa_sample_scores_body(pt_ref, q_ref, w_ref, *rest, nsteps, pps, t):
    page_refs, new_ref, o_ref, onew_ref = rest[:pps], rest[pps], rest[pps + 1], rest[pps + 2]
    j = pl.program_id(1)
    q = q_ref[...].astype(BF16)
    wv = w_ref[...] * ((D_I ** -0.5) * (H_I ** -0.5))

    def scores(keys_t):
        d = _dot(q, keys_t.astype(BF16))
        return jnp.sum((jnp.maximum(d, 0.0) * wv).reshape(t, H_I, PAGE_SIZE), axis=1)

    @pl.when(j < nsteps)
    def _():
        o_ref[...] = jnp.concatenate([scores(r[...]) for r in page_refs], axis=1)

    @pl.when(j == nsteps)
    def _():
        knew = lax.broadcasted_iota(jnp.int32, (t, PAGE_SIZE), 1)
        tnew = lax.broadcasted_iota(jnp.int32, (t, PAGE_SIZE), 0)
        onew_ref[...] = jnp.where(knew <= tnew, scores(new_ref[...]), -jnp.inf)


def _page_specs(block, layer, nsteps, pps):
    def spec(p):
        def index(b, j, pt):
            return (layer, pt[b, jnp.minimum(j, nsteps - 1) * pps + p]) + (0,) * (len(block) - 2)
        return pl.BlockSpec(block, index)
    return [spec(p) for p in range(pps)]


def _dsa_sample_scores(page_table, q, wv, cache_kidx, layer, ki_new):
    nseq, npages = page_table.shape
    t = q.shape[1] // H_I
    pps = math.gcd(PAGES_PER_STEP, npages)
    nsteps = npages // pps
    return pl.pallas_call(
        functools.partial(_dsa_sample_scores_body, nsteps=nsteps, pps=pps, t=t),
        grid_spec=pltpu.PrefetchScalarGridSpec(
            num_scalar_prefetch=1,
            grid=(nseq, nsteps + 1),
            in_specs=[pl.BlockSpec((None, t * H_I, D_I), lambda b, j, pt: (b, 0, 0)),
                      pl.BlockSpec((None, t * H_I, PAGE_SIZE), lambda b, j, pt: (b, 0, 0)),
                      *_page_specs((None, None, D_I, PAGE_SIZE), layer, nsteps, pps),
                      pl.BlockSpec((None, D_I, PAGE_SIZE), lambda b, j, pt: (b, 0, 0))],
            out_specs=[pl.BlockSpec((None, t, pps * PAGE_SIZE),
                                    lambda b, j, pt: (b, 0, jnp.minimum(j, nsteps - 1))),
                       pl.BlockSpec((None, t, PAGE_SIZE), lambda b, j, pt: (b, 0, 0))],
        ),
        out_shape=[jax.ShapeDtypeStruct((nseq, t, npages * PAGE_SIZE), F32),
                   jax.ShapeDtypeStruct((nseq, t, PAGE_SIZE), F32)],
        compiler_params=_cparams("parallel", "arbitrary"),
        name="dsa_sample_scores",
    )(page_table, q, wv, *([cache_kidx] * pps), ki_new)


def _dsa_sample_attn_body(pt_ref, sc_ref, scn_ref, q_ref, *rest, nsteps, pps, t, past, topk, tie_chunk):
    kpage_refs, vpage_refs = rest[:pps], rest[pps:2 * pps]
    knew_ref, vnew_ref, o_ref, scp_scr, scn_scr, thr_scr, m_scr, l_scr, acc_scr = rest[2 * pps:]
    j = pl.program_id(1)

    @pl.when(j == 0)
    def _():
        xp, xn = sc_ref[...], scn_ref[...]
        scp_scr[...] = xp
        scn_scr[...] = xn
        mx = jnp.maximum(jnp.max(xp, axis=1, keepdims=True), jnp.max(xn, axis=1, keepdims=True))
        mn = jnp.minimum(jnp.min(xp, axis=1, keepdims=True),
                         jnp.min(jnp.where(xn == -jnp.inf, jnp.inf, xn), axis=1, keepdims=True))
        cnt = lambda hit: jnp.sum(jnp.where(hit, 1.0, 0.0), axis=1, keepdims=True)

        def count_ge(mid):
            return cnt(sc_ref[...] >= mid) + cnt(scn_ref[...] >= mid)

        n_valid = (past + 1 + lax.broadcasted_iota(jnp.int32, (t, 1), 0)).astype(F32)
        thr = _select_threshold(count_ge, mn, mx, n_valid, topk)

        crowded = count_ge(thr) > topk
        places = topk - (cnt(xp > thr) + cnt(xn > thr))

        @pl.when(jnp.max(jnp.where(crowded, 1.0, 0.0)) > 0.5)
        def _():
            cw = tie_chunk
            earlier = (lax.broadcasted_iota(jnp.int32, (cw, cw), 0)
                       < lax.broadcasted_iota(jnp.int32, (cw, cw), 1)).astype(BF16)

            def retire(x, seen, width):
                tie = jnp.logical_and(x == thr, crowded)
                tie_f = jnp.where(tie, 1.0, 0.0)
                rank = seen + _dot(tie_f.astype(BF16), earlier[:width, :width])
                return (jnp.where(jnp.logical_and(tie, rank >= places), -jnp.inf, x),
                        seen + jnp.sum(tie_f, axis=1, keepdims=True))

            def past_chunk(c, seen):
                cols = pl.ds(pl.multiple_of(c * cw, cw), cw)
                x, seen = retire(scp_scr[:, cols], seen, cw)
                scp_scr[:, cols] = x
                return seen

            seen = lax.fori_loop(0, past // cw, past_chunk, jnp.zeros((t, 1), F32))
            scn_scr[...] = retire(scn_scr[...], seen, PAGE_SIZE)[0]

        thr_scr[...] = jnp.broadcast_to(thr, thr_scr.shape)
        m_scr[...] = jnp.full(m_scr.shape, NEG_BIG, F32)
        l_scr[...] = jnp.zeros(l_scr.shape, F32)
        acc_scr[...] = jnp.zeros(acc_scr.shape, F32)

    scale = HEAD_DIM_B ** -0.5

    def attend(x, keys, vals):
        sel_t = jnp.where(x >= thr_scr[:, 0:1], 1.0, 0.0)
        sel = jnp.concatenate([sel_t] * GROUP_B, axis=0) > 0.5
        for n in range(N_KV_B):
            s = jnp.where(sel, _dot_nt(q_ref[n].astype(BF16), keys(n).astype(BF16)) * scale, NEG_BIG)
            m_old = m_scr[n]
            m_new = jnp.maximum(m_old, jnp.max(s, axis=1, keepdims=True))
            pexp = jnp.exp(s - m_new)
            alpha = jnp.exp(m_old - m_new)
            l_scr[n] = alpha * l_scr[n] + jnp.sum(pexp, axis=1, keepdims=True)
            acc_scr[n] = alpha * acc_scr[n] + _dot(pexp.astype(BF16), vals(n).astype(BF16))
            m_scr[n] = m_new

    def page_head(r, n):
        return r[pl.ds(n, PAGE_SIZE, stride=N_KV_B), :]

    @pl.when(j < nsteps)
    def _():
        width = pps * PAGE_SIZE
        x = scp_scr[:, pl.ds(pl.multiple_of(j * width, width), width)]
        attend(x,
               lambda n: jnp.concatenate([page_head(r, n) for r in kpage_refs], axis=0),
               lambda n: jnp.concatenate([page_head(r, n) for r in vpage_refs], axis=0))

    @pl.when(j == nsteps)
    def _():
        hs = lambda n: slice(n * HEAD_DIM_B, (n + 1) * HEAD_DIM_B)
        attend(scn_scr[...], lambda n: knew_ref[:, hs(n)], lambda n: vnew_ref[:, hs(n)])
        for n in range(N_KV_B):
            o_ref[n] = (acc_scr[n] / l_scr[n]).astype(o_ref.dtype)


def _dsa_sample_attn(page_table, scores, scores_new, q, cache_k, cache_v, layer, k_new, v_new):
    nseq, npages = page_table.shape
    t = scores.shape[1]
    past = npages * PAGE_SIZE
    topk = min(TOPK_MAX, (past + t) // 4)
    rows = GROUP_B * t
    pps = math.gcd(PAGES_PER_STEP, npages)
    nsteps = npages // pps
    page_block = (None, None, PAGE_SIZE * N_KV_B, HEAD_DIM_B)
    const3 = lambda b, j, pt: (b, 0, 0)
    return pl.pallas_call(
        functools.partial(_dsa_sample_attn_body, nsteps=nsteps, pps=pps, t=t, past=past, topk=topk,
                          tie_chunk=math.gcd(TIE_CHUNK, past)),
        grid_spec=pltpu.PrefetchScalarGridSpec(
            num_scalar_prefetch=1,
            grid=(nseq, nsteps + 1),
            in_specs=[pl.BlockSpec((None, t, past), const3),
                      pl.BlockSpec((None, t, PAGE_SIZE), const3),
                      pl.BlockSpec((None, N_KV_B, rows, HEAD_DIM_B), lambda b, j, pt: (b, 0, 0, 0)),
                      *_page_specs(page_block, layer, nsteps, pps),
                      *_page_specs(page_block, layer, nsteps, pps),
                      pl.BlockSpec((None, PAGE_SIZE, KV_WIDTH_B), const3),
                      pl.BlockSpec((None, PAGE_SIZE, KV_WIDTH_B), const3)],
            out_specs=pl.BlockSpec((None, N_KV_B, rows, HEAD_DIM_B), lambda b, j, pt: (b, 0, 0, 0)),
            scratch_shapes=[pltpu.VMEM((t, past), F32),
                            pltpu.VMEM((t, PAGE_SIZE), F32),
                            pltpu.VMEM((t, PAGE_SIZE), F32),
                            pltpu.VMEM((N_KV_B, rows, 1), F32),
                            pltpu.VMEM((N_KV_B, rows, 1), F32),
                            pltpu.VMEM((N_KV_B, rows, HEAD_DIM_B), F32)],
        ),
        out_shape=jax.ShapeDtypeStruct((nseq, N_KV_B, rows, HEAD_DIM_B), BF16),
        compiler_params=_cparams("parallel", "arbitrary"),
        name="dsa_sample_attn",
    )(page_table, scores, scores_new, q, *([cache_k] * pps), *([cache_v] * pps), k_new, v_new)


def _dsa_sample(ps, k_s, v_s, ki_s, page_table, cache_k, cache_v, cache_kidx, layer):
    nseq = page_table.shape[0]
    t = ps.shape[0] // nseq
    seg = lambda name, width: ps[:, PD_OFF[name]:PD_OFF[name] + width]
    qi = seg("qi", H_I * D_I).reshape(nseq, t * H_I, D_I)
    wi = seg("wi", H_I).reshape(nseq, t * H_I, 1)
    wv = jnp.broadcast_to(wi, (nseq, t * H_I, PAGE_SIZE))
    pad_rows = lambda a: jnp.pad(a.reshape(nseq, t, -1), ((0, 0), (0, PAGE_SIZE - t), (0, 0)))
    kidx_t = cache_kidx.transpose(0, 1, 3, 2)
    scores, scores_new = _dsa_sample_scores(page_table, qi, wv, kidx_t, layer,
                                            pad_rows(ki_s).transpose(0, 2, 1))
    q = seg("qb", WIDTH_B).reshape(nseq, t, N_KV_B, GROUP_B, HEAD_DIM_B)
    q = q.transpose(0, 2, 3, 1, 4).reshape(nseq, N_KV_B, GROUP_B * t, HEAD_DIM_B)
    pool_rows = lambda c: c.reshape(c.shape[0], c.shape[1], PAGE_SIZE * N_KV_B, HEAD_DIM_B)
    o = _dsa_sample_attn(page_table, scores, scores_new, q, pool_rows(cache_k), pool_rows(cache_v),
                         layer, pad_rows(k_s), pad_rows(v_s))
    o = o.reshape(nseq, N_KV_B, GROUP_B, t, HEAD_DIM_B).transpose(0, 3, 1, 2, 4)
    return o.reshape(nseq * t, WIDTH_B)


def _prep_w_mid(w_in):
    sizes = dict(zip(IN_NAMES, IN_SIZES))
    z0 = min(SRC_OFF[n] for n in MID_ORDER + ("aa", "ba"))
    z1 = max(SRC_OFF[n] + sizes[n] for n in MID_ORDER)
    zone = lax.optimization_barrier(lax.slice_in_dim(w_in, z0, z1, axis=2))
    cols = [zone[:, :, SRC_OFF[n] - z0:SRC_OFF[n] - z0 + sizes[n]].astype(BF16) for n in MID_ORDER]
    cols.append(jnp.zeros(w_in.shape[:2] + (PD_WIDTH - sum(sizes[n] for n in MID_ORDER),), BF16))
    return jnp.concatenate(cols, axis=2)


def kernel(x_prompt, x_sample, cache_k, cache_v, cache_kidx, page_table, state_gdn, state_gdn_conv,
           state_sconv, final_norm, norm1, norm2, w_in, conv_a, a_log, dt_bias, gdn_norm, conv_c,
           w_branch_a, w_branch_b, w_branch_c, w_o, w_gate, w_up, w_down):
    bp, tp, d = x_prompt.shape
    bs, ts = x_sample.shape[:2]
    mp, ms = bp * tp, bs * ts
    x = jnp.concatenate([x_prompt.reshape(mp, d), x_sample.reshape(ms, d)], axis=0)
    new_p = [[] for _ in range(6)]
    new_s = [[] for _ in range(6)]
    w_mid = _prep_w_mid(w_in)
    w_down16 = w_down.astype(BF16)
    for l in range(DEPTH):
        xn = _rmsnorm(x, norm1[l], BF16)
        p = _in_proj_main(xn, w_in, l)
        pd, k_new, v_new, ki_new = _in_proj_mid(xn, w_mid, l)
        g, beta = _gdn_gates(pd, a_log[l], dt_bias[l])
        conv_w = conv_a[l].astype(F32)

        oa_p, gdn_p = _gdn(p, 0, bp, tp, min(GDN_CHUNK, tp), conv_w, g, beta,
                           jnp.zeros((bp, CONV_A - 1, 3 * WIDTH_A), F32),
                           jnp.zeros((bp, H_A, DK_A, DV_A), F32), gdn_norm[l])
        oa_s, gdn_s = _gdn(p, mp, bs, ts, ts, conv_w, g, beta,
                           state_gdn_conv[l], state_gdn[l], gdn_norm[l])
        ob_p = _dsa_prompt(pd, k_new, v_new, ki_new, bp, tp)
        ob_s = _dsa_sample(pd[mp:], k_new[mp:], v_new[mp:], ki_new[mp:], page_table,
                           cache_k, cache_v, cache_kidx, l)
        cw = conv_c[l].astype(F32)
        oc_p, sconv_p = _sconv(p, 0, bp, tp, cw, jnp.zeros((bp, CONV_C - 1, WIDTH_C), F32))
        oc_s, sconv_s = _sconv(p, mp, bs, ts, cw, state_sconv[l])

        oa = jnp.concatenate([oa_p, oa_s], axis=0)
        ob = jnp.concatenate([ob_p, ob_s], axis=0)
        oc = jnp.concatenate([oc_p, oc_s], axis=0)
        merged = _merge(oa, ob, oc, w_branch_a, w_branch_b, w_branch_c, l, p)
        x = _out_proj(merged, w_o, l, x)

        hn = _rmsnorm(x, norm2[l], BF16)
        h = _swiglu(hn, w_gate, w_up, l)
        x = _out_proj(h, w_down16, l, x, name="ffn_down")

        nb = CONV_A - 1
        tail_rows = lambda r0, t: lax.slice(p, (r0 + max(t - nb, 0), 0), (r0 + t, 3 * WIDTH_A))
        gconv_p = jnp.stack([tail_rows(b * tp, tp) for b in range(bp)])
        gconv_s = jnp.stack([tail_rows(mp + b * ts, ts) for b in range(bs)])
        if tp < nb:
            gconv_p = jnp.concatenate([jnp.zeros((bp, nb - tp, 3 * WIDTH_A), F32), gconv_p], axis=1)
        if ts < nb:
            gconv_s = jnp.concatenate([state_gdn_conv[l].astype(F32)[:, ts:], gconv_s], axis=1)
        kv = lambda a, rows, b, t: a[rows].reshape(b, t, N_KV_B, HEAD_DIM_B)
        rp, rs = slice(0, mp), slice(mp, mp + ms)
        for lst, val in zip(new_p, (kv(k_new, rp, bp, tp), kv(v_new, rp, bp, tp),
                                    ki_new[rp].reshape(bp, tp, D_I), gdn_p, gconv_p, sconv_p)):
            lst.append(val)
        for lst, val in zip(new_s, (kv(k_new, rs, bs, ts), kv(v_new, rs, bs, ts),
                                    ki_new[rs].reshape(bs, ts, D_I), gdn_s, gconv_s, sconv_s)):
            lst.append(val)

    y_prompt = _rmsnorm(x, final_norm, F32, 0, mp).reshape(bp, tp, d)
    y_sample = _rmsnorm(x, final_norm, F32, mp, ms).reshape(bs, ts, d)
    outs_p = [jnp.stack(a) for a in new_p]
    outs_s = [jnp.stack(a) for a in new_s]
    return (y_prompt, y_sample, *outs_p, *outs_s)
```

```python
import functools
import math

import jax
import jax.numpy as jnp
from jax import lax
from jax.experimental import pallas as pl
from jax.experimental.pallas import tpu as pltpu

F32 = jnp.float32
BF16 = jnp.bfloat16

D_MODEL = 4096
DEPTH = 2
PAGE_SIZE = 128
H_A = 16
DK_A = 128
DV_A = 128
WIDTH_A = H_A * DV_A
CONV_A = 4
GDN_CHUNK = 64
H_B = 8
N_KV_B = 2
GROUP_B = H_B // N_KV_B
HEAD_DIM_B = 128
WIDTH_B = H_B * HEAD_DIM_B
KV_WIDTH_B = N_KV_B * HEAD_DIM_B
H_I = 32
D_I = 64
TOPK_MAX = 256
Q_BLOCK = 128
WIDTH_C = 1024
CONV_C = 3
D_FF = -(-8 * D_MODEL // (3 * 256)) * 256
EPS = 1e-6

IN_NAMES = ("qa", "ka", "va", "za", "aa", "ba", "qb", "kb", "vb", "qi", "ki", "wi",
            "gate_b", "gate_c", "hc", "ga", "gb", "gc")
IN_SIZES = (WIDTH_A, WIDTH_A, WIDTH_A, WIDTH_A, H_A, H_A,
            WIDTH_B, KV_WIDTH_B, KV_WIDTH_B, H_I * D_I, D_I, H_I,
            WIDTH_C, WIDTH_C, WIDTH_C, D_MODEL, D_MODEL, D_MODEL)

LANES = 128
N_BLOCK = 512
VMEM_LIMIT = 56 * 1024 * 1024
ACT_VMEM_BYTES = 32 * 1024 * 1024
NEG_BIG = -1e30

MAIN_ORDER = ("qa", "ka", "va", "za", "gate_b", "gate_c", "hc", "ga", "gb", "gc")
MID_ORDER = ("qi", "qb", "kb", "vb", "ki", "aa", "ba", "wi")


def _offsets(order):
    sizes = dict(zip(IN_NAMES, IN_SIZES))
    off, out = 0, {}
    for name in order:
        out[name] = off
        off += sizes[name]
    return out, off


SRC_OFF, _ = _offsets(IN_NAMES)
PM_OFF, PM_WIDTH = _offsets(MAIN_ORDER)
PD_OFF, _mid_cols = _offsets(MID_ORDER)
PD_WIDTH = -(-_mid_cols // N_BLOCK) * N_BLOCK
MAIN_RUN0 = PM_OFF["gate_b"]
MAIN_SHIFT = SRC_OFF["gate_b"] - MAIN_RUN0
assert PM_WIDTH % N_BLOCK == 0 and MAIN_RUN0 % N_BLOCK == 0 and MAIN_SHIFT % LANES == 0
assert all(SRC_OFF[n] == PM_OFF[n] for n in MAIN_ORDER[:4])
assert all(SRC_OFF[n] == PM_OFF[n] + MAIN_SHIFT for n in MAIN_ORDER[4:])
TAIL_OFF = PD_OFF["ki"]
TAIL_AA = PD_OFF["aa"] - TAIL_OFF
TAIL_BA = PD_OFF["ba"] - TAIL_OFF
TAIL_WI = PD_OFF["wi"] - TAIL_OFF
assert TAIL_OFF % N_BLOCK == 0 and PD_OFF["wi"] + H_I - TAIL_OFF == LANES


def _cparams(*sem):
    return pltpu.CompilerParams(dimension_semantics=sem, vmem_limit_bytes=VMEM_LIMIT)


def _dot(a, b):
    return jnp.dot(a, b, preferred_element_type=F32)


def _dot_nt(a, b):
    return lax.dot_general(a, b, (((1,), (1,)), ((), ())), preferred_element_type=F32)


def _dot_tn(a, b):
    return lax.dot_general(a, b, (((0,), (0,)), ((), ())), preferred_element_type=F32)


def _dot_hi(a, b):
    return jnp.dot(a, b, preferred_element_type=F32, precision=lax.Precision.HIGHEST)


def _sigmoid(x):
    return jax.nn.sigmoid(x)


def _silu(x):
    return x * jax.nn.sigmoid(x)


def _rmsnorm_body(x_ref, g_ref, o_ref):
    x = x_ref[...]
    ms = jnp.mean(x * x, axis=-1, keepdims=True)
    o_ref[...] = (x * lax.rsqrt(ms + EPS) * g_ref[...]).astype(o_ref.dtype)


def _row_block(m):
    for bm in (192, 128, 64, 32, 16, 8):
        if m % bm == 0:
            return bm
    raise ValueError(f"unsupported row count {m}")


def _rmsnorm(x, g, out_dtype, row0=0, nrows=None):
    d = x.shape[1]
    m = x.shape[0] - row0 if nrows is None else nrows
    bm = _row_block(math.gcd(m, row0) if row0 else m)
    rb0 = row0 // bm
    return pl.pallas_call(
        _rmsnorm_body,
        grid=(m // bm,),
        in_specs=[pl.BlockSpec((bm, d), lambda i: (rb0 + i, 0)), pl.BlockSpec((1, d), lambda i: (0, 0))],
        out_specs=pl.BlockSpec((bm, d), lambda i: (i, 0)),
        out_shape=jax.ShapeDtypeStruct((m, d), out_dtype),
        compiler_params=_cparams("parallel"),
        name="rmsnorm",
    )(x, g.reshape(1, d).astype(F32))


def _gdn_gates_body(t_ref, alog_ref, dtb_ref, g_ref, b_ref):
    t = t_ref[...]
    aa = t[:, TAIL_AA:TAIL_AA + H_A]
    ba = t[:, TAIL_BA:TAIL_BA + H_A]
    x = aa + dtb_ref[...]
    softplus = jnp.maximum(x, 0.0) + jnp.log1p(jnp.exp(-jnp.abs(x)))
    g_ref[...] = -jnp.exp(alog_ref[...]) * softplus
    b_ref[...] = _sigmoid(ba)


def _gdn_gates(p, a_log, dt_bias):
    m = p.shape[0]
    bm = _row_block(m)
    tail_blk = TAIL_OFF // LANES
    return pl.pallas_call(
        _gdn_gates_body,
        grid=(m // bm,),
        in_specs=[pl.BlockSpec((bm, LANES), lambda i: (i, tail_blk)),
                  pl.BlockSpec((1, H_A), lambda i: (0, 0)),
                  pl.BlockSpec((1, H_A), lambda i: (0, 0))],
        out_specs=[pl.BlockSpec((bm, H_A), lambda i: (i, 0)), pl.BlockSpec((bm, H_A), lambda i: (i, 0))],
        out_shape=[jax.ShapeDtypeStruct((m, H_A), F32)] * 2,
        compiler_params=_cparams("parallel"),
        name="gdn_gates",
    )(p, a_log.reshape(1, H_A).astype(F32), dt_bias.reshape(1, H_A).astype(F32))


def _big_row_block(m, k=D_MODEL):
    for bm in (2064, 1376, 1024, 688, 512, 256, 128, 64):
        if m % bm == 0 and bm * k * 2 <= ACT_VMEM_BYTES:
            return bm
    raise ValueError(f"unsupported row count {m}")


def _resident_rows(bm, k):
    return pl.BlockSpec((bm, k), lambda i, j: (i, 0), pipeline_mode=pl.Buffered(1))


def _out_proj_body(a_ref, w_ref, r_ref, o_ref):
    o_ref[...] = r_ref[...] + _dot(a_ref[...], w_ref[...].astype(BF16))


def _out_proj(a, w, layer, residual, name="out_proj"):
    m, k = a.shape
    n = w.shape[2]
    bm, bn = _big_row_block(m, k), N_BLOCK // 2
    return pl.pallas_call(
        _out_proj_body,
        grid=(m // bm, n // bn),
        in_specs=[_resident_rows(bm, k),
                  pl.BlockSpec((None, k, bn), lambda i, j: (layer, 0, j)),
                  pl.BlockSpec((bm, bn), lambda i, j: (i, j))],
        out_specs=pl.BlockSpec((bm, bn), lambda i, j: (i, j)),
        out_shape=jax.ShapeDtypeStruct((m, n), F32),
        compiler_params=_cparams("parallel", "arbitrary"),
        name=name,
    )(a, w, residual)


IN_PROJ_PIECES = N_BLOCK // LANES


def _in_proj_main_body(a_ref, *rest):
    w_refs, p_ref = rest[:IN_PROJ_PIECES], rest[IN_PROJ_PIECES]
    w = jnp.concatenate([r[...].astype(BF16) for r in w_refs], axis=1)
    p_ref[...] = _dot(a_ref[...], w)


def _in_proj_main(a, w_in, layer):
    m, k = a.shape
    bm, bn = _big_row_block(m), N_BLOCK
    run0_blk, shift_blk = MAIN_RUN0 // bn, MAIN_SHIFT // LANES

    def piece(q):
        def index(i, j):
            return (layer, 0, IN_PROJ_PIECES * j + q + jnp.where(j >= run0_blk, shift_blk, 0))
        return pl.BlockSpec((None, k, LANES), index)

    return pl.pallas_call(
        _in_proj_main_body,
        grid=(m // bm, PM_WIDTH // bn),
        in_specs=[_resident_rows(bm, k)] + [piece(q) for q in range(IN_PROJ_PIECES)],
        out_specs=pl.BlockSpec((bm, bn), lambda i, j: (i, j)),
        out_shape=jax.ShapeDtypeStruct((m, PM_WIDTH), F32),
        compiler_params=_cparams("parallel", "arbitrary"),
        name="in_proj_main",
    )(a, *([w_in] * IN_PROJ_PIECES))


def _in_proj_mid_body(a_ref, w_ref, p_ref, k_ref, v_ref, ki_ref, *, jkv, jtail):
    acc = _dot(a_ref[...], w_ref[...])
    p_ref[...] = acc
    j = pl.program_id(1)

    @pl.when(j == jkv)
    def _():
        k_ref[...] = acc[:, :KV_WIDTH_B]
        v_ref[...] = acc[:, KV_WIDTH_B:2 * KV_WIDTH_B]

    @pl.when(j == jtail)
    def _():
        ki_ref[...] = acc[:, :D_I]


def _in_proj_mid(a, w_mid, layer):
    m, k = a.shape
    bm, bn = _big_row_block(m), N_BLOCK
    assert PD_OFF["kb"] % bn == 0 and PD_OFF["vb"] == PD_OFF["kb"] + KV_WIDTH_B
    row = lambda i, j: (i, 0)
    return pl.pallas_call(
        functools.partial(_in_proj_mid_body, jkv=PD_OFF["kb"] // bn, jtail=TAIL_OFF // bn),
        grid=(m // bm, PD_WIDTH // bn),
        in_specs=[_resident_rows(bm, k), pl.BlockSpec((None, k, bn), lambda i, j: (layer, 0, j))],
        out_specs=[pl.BlockSpec((bm, bn), lambda i, j: (i, j)),
                   pl.BlockSpec((bm, KV_WIDTH_B), row), pl.BlockSpec((bm, KV_WIDTH_B), row),
                   pl.BlockSpec((bm, D_I), row)],
        out_shape=[jax.ShapeDtypeStruct((m, PD_WIDTH), F32), jax.ShapeDtypeStruct((m, KV_WIDTH_B), F32),
                   jax.ShapeDtypeStruct((m, KV_WIDTH_B), F32), jax.ShapeDtypeStruct((m, D_I), F32)],
        compiler_params=_cparams("parallel", "arbitrary"),
        name="in_proj_mid",
    )(a, w_mid)


def _swiglu_body(a_ref, wg_ref, wu_ref, o_ref):
    a = a_ref[...]
    g = _dot(a, wg_ref[...].astype(BF16))
    u = _dot(a, wu_ref[...].astype(BF16))
    o_ref[...] = (_silu(g) * u).astype(o_ref.dtype)


def _swiglu(a, w_gate, w_up, layer):
    m, k = a.shape
    n = w_gate.shape[2]
    bm, bn = _big_row_block(m), N_BLOCK // 2
    assert n % bn == 0
    wspec = pl.BlockSpec((None, k, bn), lambda i, j: (layer, 0, j))
    return pl.pallas_call(
        _swiglu_body,
        grid=(m // bm, n // bn),
        in_specs=[_resident_rows(bm, k), wspec, wspec],
        out_specs=pl.BlockSpec((bm, bn), lambda i, j: (i, j)),
        out_shape=jax.ShapeDtypeStruct((m, n), BF16),
        compiler_params=_cparams("parallel", "arbitrary"),
        name="swiglu",
    )(a, w_gate, w_up)


def _merge_body(oa_ref, ob_ref, oc_ref, wa_ref, wb_ref, wc_ref, ga_ref, gb_ref, gc_ref, o_ref):
    acc = _sigmoid(ga_ref[...]) * _dot(oa_ref[...], wa_ref[...].astype(BF16))
    acc += _sigmoid(gb_ref[...]) * _dot(ob_ref[...], wb_ref[...].astype(BF16))
    acc += _sigmoid(gc_ref[...]) * _dot(oc_ref[...], wc_ref[...].astype(BF16))
    o_ref[...] = acc.astype(o_ref.dtype)


def _merge(oa, ob, oc, wa, wb, wc, layer, p):
    m = oa.shape[0]
    n = wa.shape[2]
    bm, bn = _big_row_block(m), N_BLOCK // 2
    ga0, gb0, gc0 = (PM_OFF[s] // bn for s in ("ga", "gb", "gc"))
    row = lambda i, j: (i, 0)
    wspec = lambda w: pl.BlockSpec((None, w.shape[1], bn), lambda i, j: (layer, 0, j))
    return pl.pallas_call(
        _merge_body,
        grid=(m // bm, n // bn),
        in_specs=[_resident_rows(bm, oa.shape[1]), _resident_rows(bm, ob.shape[1]),
                  _resident_rows(bm, oc.shape[1]),
                  wspec(wa), wspec(wb), wspec(wc),
                  pl.BlockSpec((bm, bn), lambda i, j: (i, ga0 + j)),
                  pl.BlockSpec((bm, bn), lambda i, j: (i, gb0 + j)),
                  pl.BlockSpec((bm, bn), lambda i, j: (i, gc0 + j))],
        out_specs=pl.BlockSpec((bm, bn), lambda i, j: (i, j)),
        out_shape=jax.ShapeDtypeStruct((m, n), BF16),
        compiler_params=_cparams("parallel", "arbitrary"),
        name="merge",
    )(oa, ob, oc, wa, wb, wc, p, p, p)


GDN_HEADS_PER_STEP = 16


def _gdn_body(q_ref, k_ref, v_ref, z_ref, wq_ref, wk_ref, wv_ref, tq_ref, tk_ref, tv_ref,
              g_ref, b_ref, s0_ref, ng_ref, o_ref, sout_ref, s_scr, tail_scr,
              *, chunk, hg):
    c = chunk
    w = hg * DK_A
    n = pl.program_id(2)

    @pl.when(n == 0)
    def _():
        s_scr[...] = s0_ref[...].astype(F32)
        tail_scr[:, 0:w] = tq_ref[...]
        tail_scr[:, w:2 * w] = tk_ref[...]
        tail_scr[:, 2 * w:3 * w] = tv_ref[...]

    act = []
    for idx, (x_ref, w_ref) in enumerate(((q_ref, wq_ref), (k_ref, wk_ref), (v_ref, wv_ref))):
        x = x_ref[...]
        cw = w_ref[...]
        xfull = jnp.concatenate([tail_scr[:, idx * w:(idx + 1) * w], x], axis=0)
        y = x * cw[CONV_A - 1:CONV_A, :]
        for j in range(1, CONV_A):
            y = y + pltpu.roll(xfull, j, axis=0)[8:, :] * cw[CONV_A - 1 - j:CONV_A - j, :]
        tail_scr[:, idx * w:(idx + 1) * w] = xfull[c:c + 8, :]
        act.append(_silu(y))
    qs, ks, vs = act
    z = z_ref[...]

    ii = lax.broadcasted_iota(jnp.int32, (c, c), 0)
    jj = lax.broadcasted_iota(jnp.int32, (c, c), 1)
    incl = jj <= ii
    strict = jj < ii
    eye = (ii == jj).astype(F32)
    n_double = int(math.log2(c)) - 1
    ng = ng_ref[...]
    b_blk = b_ref[...]
    gcum_all = _dot_hi(incl.astype(F32), g_ref[...])
    gcum_t = gcum_all.T

    heads = range(hg)
    hsl = [slice(hh * DK_A, (hh + 1) * DK_A) for hh in heads]
    q_l, k_l, k16_l, decay_l, egc_l, kdec_l, glast_l, pw_l, tinv_l, rhs_l = ([] for _ in range(10))
    for hh in heads:
        q = qs[:, hsl[hh]]
        k = ks[:, hsl[hh]]
        q = q * lax.rsqrt(jnp.sum(q * q, axis=-1, keepdims=True) + EPS) * (DK_A ** -0.5)
        k = k * lax.rsqrt(jnp.sum(k * k, axis=-1, keepdims=True) + EPS)
        bc = b_blk[:, hh:hh + 1]
        gcum_c = gcum_all[:, hh:hh + 1]
        gcum_r = gcum_t[hh:hh + 1, :]
        decay = jnp.exp(jnp.where(incl, gcum_c - gcum_r, -jnp.inf))
        kb = k * bc
        k16 = k.astype(BF16)
        egc = jnp.exp(gcum_c)
        glast = gcum_c[c - 1:c, :]
        mm = jnp.where(strict, _dot_nt(kb.astype(BF16), k16) * decay, 0.0)
        q_l.append(q)
        k_l.append(k)
        k16_l.append(k16)
        decay_l.append(decay)
        egc_l.append(egc)
        glast_l.append(glast)
        kdec_l.append(jnp.exp(glast - gcum_c))
        rhs_l.append(jnp.concatenate([vs[:, hsl[hh]] * bc, kb * egc], axis=1).astype(BF16))
        pw_l.append(-mm)
        tinv_l.append(eye - mm)
    for _ in range(n_double):
        for hh in heads:
            pw16 = pw_l[hh].astype(BF16)
            pw_l[hh] = _dot(pw16, pw16)
        for hh in heads:
            tinv_l[hh] = tinv_l[hh] + _dot(tinv_l[hh].astype(BF16), pw_l[hh].astype(BF16))
    sol_l = [_dot(tinv_l[hh].astype(BF16), rhs_l[hh]) for hh in heads]
    attn_l = [(_dot_nt(q_l[hh].astype(BF16), k16_l[hh]) * decay_l[hh]).astype(BF16) for hh in heads]
    s_l = [s_scr[hh] for hh in heads]
    s16_l = [s_l[hh].astype(BF16) for hh in heads]
    v16_l = [(sol_l[hh][:, :DV_A] - _dot(sol_l[hh][:, DV_A:].astype(BF16), s16_l[hh])).astype(BF16)
             for hh in heads]
    o_l = [_dot((q_l[hh] * egc_l[hh]).astype(BF16), s16_l[hh]) + _dot(attn_l[hh], v16_l[hh]) for hh in heads]
    snew_l = [s_l[hh] * jnp.exp(glast_l[hh]) + _dot_tn((k_l[hh] * kdec_l[hh]).astype(BF16), v16_l[hh])
              for hh in heads]
    s_scr[...] = jnp.stack(snew_l, axis=0)
    o_l = [o * lax.rsqrt(jnp.mean(o * o, axis=-1, keepdims=True) + EPS) * ng for o in o_l]
    o_ref[...] = (jnp.concatenate(o_l, axis=1) * _silu(z)).astype(o_ref.dtype)

    @pl.when(n == pl.num_programs(2) - 1)
    def _():
        sout_ref[...] = s_scr[...].astype(sout_ref.dtype)


def _gdn(p, row0, nseq, t, chunk, conv_w, g, beta, conv_buf, s0, norm_g, out_rows=None):
    hg = GDN_HEADS_PER_STEP
    assert hg == H_A
    w = hg * DK_A
    nch = t // chunk
    nhg = H_A // hg
    rb0 = row0 // chunk
    gspec = pl.BlockSpec((chunk, H_A), lambda b, h, n: (rb0 + b * nch + n, 0))
    tail = jnp.pad(conv_buf.astype(F32), ((0, 0), (8 - (CONV_A - 1), 0), (0, 0)))
    qb0, kb0, vb0, zb0 = (PM_OFF[s] // w for s in ("qa", "ka", "va", "za"))
    pspec = lambda b0: pl.BlockSpec((chunk, w), lambda b, h, n: (rb0 + b * nch + n, b0 + h))
    wspec = lambda b0: pl.BlockSpec((CONV_A, w), lambda b, h, n: (0, b0 + h))
    tspec = lambda b0: pl.BlockSpec((None, 8, w), lambda b, h, n: (b, 0, b0 + h))
    o, s_out = pl.pallas_call(
        functools.partial(_gdn_body, chunk=chunk, hg=hg),
        grid=(nseq, nhg, nch),
        in_specs=[pspec(qb0), pspec(kb0), pspec(vb0), pspec(zb0),
                  wspec(0), wspec(nhg), wspec(2 * nhg),
                  tspec(0), tspec(nhg), tspec(2 * nhg),
                  gspec, gspec,
                  pl.BlockSpec((None, hg, DK_A, DV_A), lambda b, h, n: (b, h, 0, 0)),
                  pl.BlockSpec((1, DV_A), lambda b, h, n: (0, 0))],
        out_specs=[pl.BlockSpec((chunk, w), lambda b, h, n: (b * nch + n, h)),
                   pl.BlockSpec((None, hg, DK_A, DV_A), lambda b, h, n: (b, h, 0, 0))],
        out_shape=[jax.ShapeDtypeStruct((out_rows or nseq * t, WIDTH_A), BF16),
                   jax.ShapeDtypeStruct((nseq, H_A, DK_A, DV_A), s0.dtype)],
        scratch_shapes=[pltpu.VMEM((hg, DK_A, DV_A), F32), pltpu.VMEM((8, 3 * w), F32)],
        compiler_params=_cparams("parallel", "parallel", "arbitrary"),
        name=f"gdn_c{chunk}",
    )(p, p, p, p, conv_w, conv_w, conv_w, tail, tail, tail, g, beta, s0,
      norm_g.reshape(1, DV_A).astype(F32))
    return o, s_out


SCONV_COLS = 256


def _sconv_body(gb_ref, gc_ref, hc_ref, w_ref, buf_ref, o_ref, st_ref, *, t):
    pr = gc_ref[...] * hc_ref[...]
    cw = w_ref[...]
    xfull = jnp.concatenate([buf_ref[...], pr], axis=0)
    y = pr * cw[CONV_C - 1:CONV_C, :]
    for j in range(1, CONV_C):
        y = y + pltpu.roll(xfull, j, axis=0)[8:, :] * cw[CONV_C - 1 - j:CONV_C - j, :]
    o_ref[...] = (gb_ref[...] * y).astype(o_ref.dtype)
    st_ref[...] = xfull[t:t + 8, :]


def _sconv(p, row0, nseq, t, conv_w, buf, out_rows=None):
    cols = SCONV_COLS
    ncb = WIDTH_C // cols
    rb0 = row0 // t
    b0, c0, h0 = (PM_OFF[s] // cols for s in ("gate_b", "gate_c", "hc"))
    buf8 = jnp.pad(buf.astype(F32), ((0, 0), (8 - (CONV_C - 1), 0), (0, 0)))
    pspec = lambda o: pl.BlockSpec((t, cols), lambda b, j: (rb0 + b, o + j))
    o, st = pl.pallas_call(
        functools.partial(_sconv_body, t=t),
        grid=(nseq, ncb),
        in_specs=[pspec(b0), pspec(c0), pspec(h0),
                  pl.BlockSpec((CONV_C, cols), lambda b, j: (0, j)),
                  pl.BlockSpec((None, 8, cols), lambda b, j: (b, 0, j))],
        out_specs=[pl.BlockSpec((t, cols), lambda b, j: (b, j)),
                   pl.BlockSpec((None, 8, cols), lambda b, j: (b, 0, j))],
        out_shape=[jax.ShapeDtypeStruct((out_rows or nseq * t, WIDTH_C), BF16),
                   jax.ShapeDtypeStruct((nseq, 8, WIDTH_C), F32)],
        compiler_params=_cparams("parallel", "parallel"),
        name=f"sconv_t{t}",
    )(p, p, p, conv_w, buf8)
    return o, st[:, 8 - (CONV_C - 1):, :]


BISECT_EVERY = 8
MAX_SEARCH = 2400


def _select_threshold(count_ge, lo0, hi0, n_valid, k):
    kf = jnp.float32(k)

    def cond(st):
        return jnp.logical_and(st[0] < MAX_SEARCH, jnp.min(st[-1]) < 0.5)

    def step(st):
        it, lo, hi, glo, ghi, t, side, done = st
        half = 0.5 * lo + 0.5 * hi
        cand = lo + (hi - lo) * (glo / jnp.maximum(glo - ghi, 1e-9))
        mid = jnp.where(it % BISECT_EVERY == BISECT_EVERY - 1, half, cand)
        mid = jnp.where(jnp.logical_and(mid > lo, mid < hi), mid, half)
        adjacent = jnp.logical_not(jnp.logical_and(mid > lo, mid < hi))
        first = it == 0
        mid = jnp.where(first, hi, mid)
        adjacent = jnp.logical_and(adjacent, jnp.logical_not(first))
        g = count_ge(mid) - kf
        ge = g >= 0.0
        finish = jnp.logical_or(jnp.logical_or(g == 0.0, adjacent), jnp.logical_and(first, ge))
        newly = jnp.logical_and(finish, done < 0.5)
        t = jnp.where(newly, jnp.where(adjacent, lo, mid), t)
        new_side = jnp.where(ge, 1.0, -1.0)
        damp = jnp.where(jnp.logical_and(new_side == side, jnp.logical_not(first)), 0.5, 1.0)
        glo = jnp.where(ge, g, glo * damp)
        ghi = jnp.where(ge, ghi * damp, g)
        lo = jnp.where(ge, mid, lo)
        hi = jnp.where(ge, hi, mid)
        done = jnp.where(finish, 1.0, done)
        return it + 1, lo, hi, glo, ghi, t, new_side, done

    done0 = jnp.where(n_valid > kf, 0.0, 1.0)
    st = lax.while_loop(cond, lambda st: step(step(st)), (jnp.int32(0), lo0, hi0, n_valid - kf, jnp.full_like(lo0, -kf),
                                     lo0, jnp.zeros_like(lo0), done0))
    return st[5]


KEY_CHUNK = 512


def _dsa_prompt_body(qi_ref, tq_ref, qb_ref, kb_ref, vb_ref, tk_ref, o_ref,
                     sc_scr, m_scr, l_scr, acc_scr, *, tq, sc, topk):
    i = pl.program_id(1)
    nck = (i * tq + tq - 1) // sc + 1
    wt = tq_ref[...].T[TAIL_WI:TAIL_WI + H_I, :] * ((D_I ** -0.5) * (H_I ** -0.5))
    qi = qi_ref[...]
    pairs = [jnp.concatenate([qi[:, (2 * p) * D_I:(2 * p + 1) * D_I],
                              qi[:, (2 * p + 1) * D_I:(2 * p + 2) * D_I]], axis=0).astype(BF16)
             for p in range(H_I // 2)]
    tpos = i * tq + lax.broadcasted_iota(jnp.int32, (1, tq), 1)

    def score_chunk(c, carry):
        r0 = pl.multiple_of(c * sc, sc)
        kic = tk_ref[pl.ds(r0, sc), :].astype(BF16)
        acc = jnp.zeros((sc, tq), F32)
        for p in range(H_I // 2):
            d = _dot_nt(kic, pairs[p])
            acc = acc + jnp.maximum(d[:, :tq], 0.0) * wt[2 * p:2 * p + 1, :]
            acc = acc + jnp.maximum(d[:, tq:], 0.0) * wt[2 * p + 1:2 * p + 2, :]
        kpos = r0 + lax.broadcasted_iota(jnp.int32, (sc, tq), 0)
        sc_scr[pl.ds(r0, sc), :] = jnp.where(kpos <= tpos, acc, -jnp.inf)
        return carry

    lax.fori_loop(0, nck, score_chunk, 0)

    def minmax_chunk(c, carry):
        mn, mx = carry
        x = sc_scr[pl.ds(pl.multiple_of(c * sc, sc), sc), :]
        mx = jnp.maximum(mx, jnp.max(x, axis=0, keepdims=True))
        mn = jnp.minimum(mn, jnp.min(jnp.where(x == -jnp.inf, jnp.inf, x), axis=0, keepdims=True))
        return mn, mx

    mn, mx = lax.fori_loop(0, nck, minmax_chunk,
                           (jnp.full((1, tq), jnp.inf, F32), jnp.full((1, tq), -jnp.inf, F32)))

    def count_ge(mid):
        def body(c, acc):
            x = sc_scr[pl.ds(pl.multiple_of(c * sc, sc), sc), :]
            hit = jnp.where(x >= mid, 1.0, 0.0).reshape(8, sc // 64, 8, tq)
            return acc + jnp.sum(jnp.sum(hit, axis=1), axis=0)
        return jnp.sum(lax.fori_loop(0, nck, body, jnp.zeros((8, tq), F32)), axis=0, keepdims=True)

    thr = _select_threshold(count_ge, mn, mx, (tpos + 1).astype(F32), topk)

    def count_gt(c, acc):
        x = sc_scr[pl.ds(pl.multiple_of(c * sc, sc), sc), :]
        hit = jnp.where(x > thr, 1.0, 0.0).reshape(8, sc // 64, 8, tq)
        return acc + jnp.sum(jnp.sum(hit, axis=1), axis=0)
    n_above = jnp.sum(lax.fori_loop(0, nck, count_gt, jnp.zeros((8, tq), F32)), axis=0, keepdims=True)
    crowded = count_ge(thr) > topk
    places = topk - n_above

    @pl.when(jnp.max(jnp.where(crowded, 1.0, 0.0)) > 0.5)
    def _():
        earlier = (lax.broadcasted_iota(jnp.int32, (sc, sc), 1)
                   < lax.broadcasted_iota(jnp.int32, (sc, sc), 0)).astype(BF16)

        def retire(c, seen):
            rows = pl.ds(pl.multiple_of(c * sc, sc), sc)
            x = sc_scr[rows, :]
            tie = jnp.logical_and(x == thr, crowded)
            tie_f = jnp.where(tie, 1.0, 0.0)
            rank = seen + _dot(earlier, tie_f.astype(BF16))
            sc_scr[rows, :] = jnp.where(jnp.logical_and(tie, rank >= places), -jnp.inf, x)
            return seen + jnp.sum(tie_f, axis=0, keepdims=True)

        lax.fori_loop(0, nck, retire, jnp.zeros((1, tq), F32))

    m_scr[...] = jnp.full(m_scr.shape, NEG_BIG, F32)
    l_scr[...] = jnp.zeros(l_scr.shape, F32)
    acc_scr[...] = jnp.zeros(acc_scr.shape, F32)
    q = qb_ref[...].astype(BF16)
    scale = HEAD_DIM_B ** -0.5

    def attn_chunk(c, carry):
        r0 = pl.multiple_of(c * sc, sc)
        sel = sc_scr[pl.ds(r0, sc), :] >= thr
        for n in range(N_KV_B):
            kc = kb_ref[pl.ds(r0, sc), n * HEAD_DIM_B:(n + 1) * HEAD_DIM_B].astype(BF16)
            vc = vb_ref[pl.ds(r0, sc), n * HEAD_DIM_B:(n + 1) * HEAD_DIM_B].astype(BF16)
            for g in range(GROUP_B):
                h = n * GROUP_B + g
                hs = slice(h * HEAD_DIM_B, (h + 1) * HEAD_DIM_B)
                s = jnp.where(sel, _dot_nt(kc, q[:, hs]) * scale, NEG_BIG)
                m_old = m_scr[h:h + 1, :]
                m_new = jnp.maximum(m_old, jnp.max(s, axis=0, keepdims=True))
                pexp = jnp.exp(s - m_new)
                alpha = jnp.exp(m_old - m_new)
                l_scr[h:h + 1, :] = alpha * l_scr[h:h + 1, :] + jnp.sum(pexp, axis=0, keepdims=True)
                acc_scr[hs, :] = alpha * acc_scr[hs, :] + _dot_tn(vc, pexp.astype(BF16))
                m_scr[h:h + 1, :] = m_new
        return carry

    lax.fori_loop(0, nck, attn_chunk, 0)

    for h in range(H_B):
        hs = slice(h * HEAD_DIM_B, (h + 1) * HEAD_DIM_B)
        ot = acc_scr[hs, :] / l_scr[h:h + 1, :]
        o_ref[:, hs] = ot.T.astype(o_ref.dtype)


def _dsa_prompt(p, k_new, v_new, ki_new, nseq, s, out_rows=None):
    tq, sc = Q_BLOCK, min(KEY_CHUNK, s)
    topk = min(TOPK_MAX, s // 4)
    nqb = s // tq
    qi0 = PD_OFF["qi"] // (H_I * D_I)
    qb0 = PD_OFF["qb"] // WIDTH_B
    tl0 = TAIL_OFF // LANES
    return pl.pallas_call(
        functools.partial(_dsa_prompt_body, tq=tq, sc=sc, topk=topk),
        grid=(nseq, nqb),
        in_specs=[pl.BlockSpec((tq, H_I * D_I), lambda b, i: (b * nqb + i, qi0)),
                  pl.BlockSpec((tq, LANES), lambda b, i: (b * nqb + i, tl0)),
                  pl.BlockSpec((tq, WIDTH_B), lambda b, i: (b * nqb + i, qb0)),
                  pl.BlockSpec((s, KV_WIDTH_B), lambda b, i: (b, 0)),
                  pl.BlockSpec((s, KV_WIDTH_B), lambda b, i: (b, 0)),
                  pl.BlockSpec((s, D_I), lambda b, i: (b, 0))],
        out_specs=pl.BlockSpec((tq, WIDTH_B), lambda b, i: (b * nqb + i, 0)),
        out_shape=jax.ShapeDtypeStruct((out_rows or nseq * s, WIDTH_B), BF16),
        scratch_shapes=[pltpu.VMEM((s, tq), F32), pltpu.VMEM((H_B, tq), F32),
                        pltpu.VMEM((H_B, tq), F32), pltpu.VMEM((WIDTH_B, tq), F32)],
        compiler_params=_cparams("parallel", "arbitrary"),
        name="dsa_prompt",
    )(p, p, p, k_new, v_new, ki_new)


PAGES_PER_STEP = 16
TIE_CHUNK = 512


def _dsa_sample_scores_body(pt_ref, q_ref, w_ref, *rest, nsteps, pps, t):
    page_refs, new_ref, o_ref, onew_ref = rest[:pps], rest[pps], rest[pps + 1], rest[pps + 2]
    j = pl.program_id(1)
    q = q_ref[...].astype(BF16)
    wv = w_ref[...] * ((D_I ** -0.5) * (H_I ** -0.5))

    def scores(keys_t):
        d = _dot(q, keys_t.astype(BF16))
        return jnp.sum((jnp.maximum(d, 0.0) * wv).reshape(t, H_I, PAGE_SIZE), axis=1)

    @pl.when(j < nsteps)
    def _():
        o_ref[...] = jnp.concatenate([scores(r[...]) for r in page_refs], axis=1)

    @pl.when(j == nsteps)
    def _():
        knew = lax.broadcasted_iota(jnp.int32, (t, PAGE_SIZE), 1)
        tnew = lax.broadcasted_iota(jnp.int32, (t, PAGE_SIZE), 0)
        onew_ref[...] = jnp.where(knew <= tnew, scores(new_ref[...]), -jnp.inf)


def _page_specs(block, layer, nsteps, pps):
    def spec(p):
        def index(b, j, pt):
            return (layer, pt[b, jnp.minimum(j, nsteps - 1) * pps + p]) + (0,) * (len(block) - 2)
        return pl.BlockSpec(block, index)
    return [spec(p) for p in range(pps)]


def _dsa_sample_scores(page_table, q, wv, cache_kidx, layer, ki_new):
    nseq, npages = page_table.shape
    t = q.shape[1] // H_I
    pps = math.gcd(PAGES_PER_STEP, npages)
    nsteps = npages // pps
    return pl.pallas_call(
        functools.partial(_dsa_sample_scores_body, nsteps=nsteps, pps=pps, t=t),
        grid_spec=pltpu.PrefetchScalarGridSpec(
            num_scalar_prefetch=1,
            grid=(nseq, nsteps + 1),
            in_specs=[pl.BlockSpec((None, t * H_I, D_I), lambda b, j, pt: (b, 0, 0)),
                      pl.BlockSpec((None, t * H_I, PAGE_SIZE), lambda b, j, pt: (b, 0, 0)),
                      *_page_specs((None, None, D_I, PAGE_SIZE), layer, nsteps, pps),
                      pl.BlockSpec((None, D_I, PAGE_SIZE), lambda b, j, pt: (b, 0, 0))],
            out_specs=[pl.BlockSpec((None, t, pps * PAGE_SIZE),
                                    lambda b, j, pt: (b, 0, jnp.minimum(j, nsteps - 1))),
                       pl.BlockSpec((None, t, PAGE_SIZE), lambda b, j, pt: (b, 0, 0))],
        ),
        out_shape=[jax.ShapeDtypeStruct((nseq, t, npages * PAGE_SIZE), F32),
                   jax.ShapeDtypeStruct((nseq, t, PAGE_SIZE), F32)],
        compiler_params=_cparams("parallel", "arbitrary"),
        name="dsa_sample_scores",
    )(page_table, q, wv, *([cache_kidx] * pps), ki_new)


def _dsa_sample_attn_body(pt_ref, sc_ref, scn_ref, q_ref, *rest, nsteps, pps, t, past, topk, tie_chunk):
    kpage_refs, vpage_refs = rest[:pps], rest[pps:2 * pps]
    knew_ref, vnew_ref, o_ref, scp_scr, scn_scr, thr_scr, m_scr, l_scr, acc_scr = rest[2 * pps:]
    j = pl.program_id(1)

    @pl.when(j == 0)
    def _():
        xp, xn = sc_ref[...], scn_ref[...]
        scp_scr[...] = xp
        scn_scr[...] = xn
        mx = jnp.maximum(jnp.max(xp, axis=1, keepdims=True), jnp.max(xn, axis=1, keepdims=True))
        mn = jnp.minimum(jnp.min(xp, axis=1, keepdims=True),
                         jnp.min(jnp.where(xn == -jnp.inf, jnp.inf, xn), axis=1, keepdims=True))
        cnt = lambda hit: jnp.sum(jnp.where(hit, 1.0, 0.0), axis=1, keepdims=True)

        def count_ge(mid):
            return cnt(sc_ref[...] >= mid) + cnt(scn_ref[...] >= mid)

        n_valid = (past + 1 + lax.broadcasted_iota(jnp.int32, (t, 1), 0)).astype(F32)
        thr = _select_threshold(count_ge, mn, mx, n_valid, topk)

        crowded = count_ge(thr) > topk
        places = topk - (cnt(xp > thr) + cnt(xn > thr))

        @pl.when(jnp.max(jnp.where(crowded, 1.0, 0.0)) > 0.5)
        def _():
            cw = tie_chunk
            earlier = (lax.broadcasted_iota(jnp.int32, (cw, cw), 0)
                       < lax.broadcasted_iota(jnp.int32, (cw, cw), 1)).astype(BF16)

            def retire(x, seen, width):
                tie = jnp.logical_and(x == thr, crowded)
                tie_f = jnp.where(tie, 1.0, 0.0)
                rank = seen + _dot(tie_f.astype(BF16), earlier[:width, :width])
                return (jnp.where(jnp.logical_and(tie, rank >= places), -jnp.inf, x),
                        seen + jnp.sum(tie_f, axis=1, keepdims=True))

            def past_chunk(c, seen):
                cols = pl.ds(pl.multiple_of(c * cw, cw), cw)
                x, seen = retire(scp_scr[:, cols], seen, cw)
                scp_scr[:, cols] = x
                return seen

            seen = lax.fori_loop(0, past // cw, past_chunk, jnp.zeros((t, 1), F32))
            scn_scr[...] = retire(scn_scr[...], seen, PAGE_SIZE)[0]

        thr_scr[...] = jnp.broadcast_to(thr, thr_scr.shape)
        m_scr[...] = jnp.full(m_scr.shape, NEG_BIG, F32)
        l_scr[...] = jnp.zeros(l_scr.shape, F32)
        acc_scr[...] = jnp.zeros(acc_scr.shape, F32)

    scale = HEAD_DIM_B ** -0.5

    def attend(x, keys, vals):
        sel_t = jnp.where(x >= thr_scr[:, 0:1], 1.0, 0.0)
        sel = jnp.concatenate([sel_t] * GROUP_B, axis=0) > 0.5
        for n in range(N_KV_B):
            s = jnp.where(sel, _dot_nt(q_ref[n].astype(BF16), keys(n).astype(BF16)) * scale, NEG_BIG)
            m_old = m_scr[n]
            m_new = jnp.maximum(m_old, jnp.max(s, axis=1, keepdims=True))
            pexp = jnp.exp(s - m_new)
            alpha = jnp.exp(m_old - m_new)
            l_scr[n] = alpha * l_scr[n] + jnp.sum(pexp, axis=1, keepdims=True)
            acc_scr[n] = alpha * acc_scr[n] + _dot(pexp.astype(BF16), vals(n).astype(BF16))
            m_scr[n] = m_new

    def page_head(r, n):
        return r[pl.ds(n, PAGE_SIZE, stride=N_KV_B), :]

    @pl.when(j < nsteps)
    def _():
        width = pps * PAGE_SIZE
        x = scp_scr[:, pl.ds(pl.multiple_of(j * width, width), width)]
        attend(x,
               lambda n: jnp.concatenate([page_head(r, n) for r in kpage_refs], axis=0),
               lambda n: jnp.concatenate([page_head(r, n) for r in vpage_refs], axis=0))

    @pl.when(j == nsteps)
    def _():
        hs = lambda n: slice(n * HEAD_DIM_B, (n + 1) * HEAD_DIM_B)
        attend(scn_scr[...], lambda n: knew_ref[:, hs(n)], lambda n: vnew_ref[:, hs(n)])
        for n in range(N_KV_B):
            o_ref[n] = (acc_scr[n] / l_scr[n]).astype(o_ref.dtype)


def _dsa_sample_attn(page_table, scores, scores_new, q, cache_k, cache_v, layer, k_new, v_new):
    nseq, npages = page_table.shape
    t = scores.shape[1]
    past = npages * PAGE_SIZE
    topk = min(TOPK_MAX, (past + t) // 4)
    rows = GROUP_B * t
    pps = math.gcd(PAGES_PER_STEP, npages)
    nsteps = npages // pps
    page_block = (None, None, PAGE_SIZE * N_KV_B, HEAD_DIM_B)
    const3 = lambda b, j, pt: (b, 0, 0)
    return pl.pallas_call(
        functools.partial(_dsa_sample_attn_body, nsteps=nsteps, pps=pps, t=t, past=past, topk=topk,
                          tie_chunk=math.gcd(TIE_CHUNK, past)),
        grid_spec=pltpu.PrefetchScalarGridSpec(
            num_scalar_prefetch=1,
            grid=(nseq, nsteps + 1),
            in_specs=[pl.BlockSpec((None, t, past), const3),
                      pl.BlockSpec((None, t, PAGE_SIZE), const3),
                      pl.BlockSpec((None, N_KV_B, rows, HEAD_DIM_B), lambda b, j, pt: (b, 0, 0, 0)),
                      *_page_specs(page_block, layer, nsteps, pps),
                      *_page_specs(page_block, layer, nsteps, pps),
                      pl.BlockSpec((None, PAGE_SIZE, KV_WIDTH_B), const3),
                      pl.BlockSpec((None, PAGE_SIZE, KV_WIDTH_B), const3)],
            out_specs=pl.BlockSpec((None, N_KV_B, rows, HEAD_DIM_B), lambda b, j, pt: (b, 0, 0, 0)),
            scratch_shapes=[pltpu.VMEM((t, past), F32),
                            pltpu.VMEM((t, PAGE_SIZE), F32),
                            pltpu.VMEM((t, PAGE_SIZE), F32),
                            pltpu.VMEM((N_KV_B, rows, 1), F32),
                            pltpu.VMEM((N_KV_B, rows, 1), F32),
                            pltpu.VMEM((N_KV_B, rows, HEAD_DIM_B), F32)],
        ),
        out_shape=jax.ShapeDtypeStruct((nseq, N_KV_B, rows, HEAD_DIM_B), BF16),
        compiler_params=_cparams("parallel", "arbitrary"),
        name="dsa_sample_attn",
    )(page_table, scores, scores_new, q, *([cache_k] * pps), *([cache_v] * pps), k_new, v_new)


def _dsa_sample(ps, k_s, v_s, ki_s, page_table, cache_k, cache_v, cache_kidx, layer):
    nseq = page_table.shape[0]
    t = ps.shape[0] // nseq
    seg = lambda name, width: ps[:, PD_OFF[name]:PD_OFF[name] + width]
    qi = seg("qi", H_I * D_I).reshape(nseq, t * H_I, D_I)
    wi = seg("wi", H_I).reshape(nseq, t * H_I, 1)
    wv = jnp.broadcast_to(wi, (nseq, t * H_I, PAGE_SIZE))
    pad_rows = lambda a: jnp.pad(a.reshape(nseq, t, -1), ((0, 0), (0, PAGE_SIZE - t), (0, 0)))
    kidx_t = cache_kidx.transpose(0, 1, 3, 2)
    scores, scores_new = _dsa_sample_scores(page_table, qi, wv, kidx_t, layer,
                                            pad_rows(ki_s).transpose(0, 2, 1))
    q = seg("qb", WIDTH_B).reshape(nseq, t, N_KV_B, GROUP_B, HEAD_DIM_B)
    q = q.transpose(0, 2, 3, 1, 4).reshape(nseq, N_KV_B, GROUP_B * t, HEAD_DIM_B)
    pool_rows = lambda c: c.reshape(c.shape[0], c.shape[1], PAGE_SIZE * N_KV_B, HEAD_DIM_B)
    o = _dsa_sample_attn(page_table, scores, scores_new, q, pool_rows(cache_k), pool_rows(cache_v),
                         layer, pad_rows(k_s), pad_rows(v_s))
    o = o.reshape(nseq, N_KV_B, GROUP_B, t, HEAD_DIM_B).transpose(0, 3, 1, 2, 4)
    return o.reshape(nseq * t, WIDTH_B)


def _prep_w_mid(w_in):
    sizes = dict(zip(IN_NAMES, IN_SIZES))
    z0 = min(SRC_OFF[n] for n in MID_ORDER + ("aa", "ba"))
    z1 = max(SRC_OFF[n] + sizes[n] for n in MID_ORDER)
    zone = lax.optimization_barrier(lax.slice_in_dim(w_in, z0, z1, axis=2))
    cols = [zone[:, :, SRC_OFF[n] - z0:SRC_OFF[n] - z0 + sizes[n]].astype(BF16) for n in MID_ORDER]
    cols.append(jnp.zeros(w_in.shape[:2] + (PD_WIDTH - sum(sizes[n] for n in MID_ORDER),), BF16))
    return jnp.concatenate(cols, axis=2)


def kernel(x_prompt, x_sample, cache_k, cache_v, cache_kidx, page_table, state_gdn, state_gdn_conv,
           state_sconv, final_norm, norm1, norm2, w_in, conv_a, a_log, dt_bias, gdn_norm, conv_c,
           w_branch_a, w_branch_b, w_branch_c, w_o, w_gate, w_up, w_down):
    bp, tp, d = x_prompt.shape
    bs, ts = x_sample.shape[:2]
    mp, ms = bp * tp, bs * ts
    x = jnp.concatenate([x_prompt.reshape(mp, d), x_sample.reshape(ms, d)], axis=0)
    new_p = [[] for _ in range(6)]
    new_s = [[] for _ in range(6)]
    w_mid = _prep_w_mid(w_in)
    w_down16 = w_down.astype(BF16)
    for l in range(DEPTH):
        xn = _rmsnorm(x, norm1[l], BF16)
        p = _in_proj_main(xn, w_in, l)
        pd, k_new, v_new, ki_new = _in_proj_mid(xn, w_mid, l)
        g, beta = _gdn_gates(pd, a_log[l], dt_bias[l])
        conv_w = conv_a[l].astype(F32)

        put_sample = lambda full, part: lax.dynamic_update_slice(full, part, (mp, 0))
        oa, gdn_p = _gdn(p, 0, bp, tp, min(GDN_CHUNK, tp), conv_w, g, beta,
                         jnp.zeros((bp, CONV_A - 1, 3 * WIDTH_A), F32),
                         jnp.zeros((bp, H_A, DK_A, DV_A), F32), gdn_norm[l], out_rows=mp + ms)
        oa_s, gdn_s = _gdn(p, mp, bs, ts, ts, conv_w, g, beta,
                           state_gdn_conv[l], state_gdn[l], gdn_norm[l])
        oa = put_sample(oa, oa_s)
        ob = _dsa_prompt(pd, k_new, v_new, ki_new, bp, tp, out_rows=mp + ms)
        ob = put_sample(ob, _dsa_sample(pd[mp:], k_new[mp:], v_new[mp:], ki_new[mp:], page_table,
                                        cache_k, cache_v, cache_kidx, l))
        cw = conv_c[l].astype(F32)
        oc, sconv_p = _sconv(p, 0, bp, tp, cw, jnp.zeros((bp, CONV_C - 1, WIDTH_C), F32),
                             out_rows=mp + ms)
        oc_s, sconv_s = _sconv(p, mp, bs, ts, cw, state_sconv[l])
        oc = put_sample(oc, oc_s)
        merged = _merge(oa, ob, oc, w_branch_a, w_branch_b, w_branch_c, l, p)
        x = _out_proj(merged, w_o, l, x)

        hn = _rmsnorm(x, norm2[l], BF16)
        h = _swiglu(hn, w_gate, w_up, l)
        x = _out_proj(h, w_down16, l, x, name="ffn_down")

        nb = CONV_A - 1
        tail_rows = lambda r0, t: lax.slice(p, (r0 + max(t - nb, 0), 0), (r0 + t, 3 * WIDTH_A))
        gconv_p = jnp.stack([tail_rows(b * tp, tp) for b in range(bp)])
        gconv_s = jnp.stack([tail_rows(mp + b * ts, ts) for b in range(bs)])
        if tp < nb:
            gconv_p = jnp.concatenate([jnp.zeros((bp, nb - tp, 3 * WIDTH_A), F32), gconv_p], axis=1)
        if ts < nb:
            gconv_s = jnp.concatenate([state_gdn_conv[l].astype(F32)[:, ts:], gconv_s], axis=1)
        kv = lambda a, rows, b, t: a[rows].reshape(b, t, N_KV_B, HEAD_DIM_B)
        rp, rs = slice(0, mp), slice(mp, mp + ms)
        for lst, val in zip(new_p, (kv(k_new, rp, bp, tp), kv(v_new, rp, bp, tp),
                                    ki_new[rp].reshape(bp, tp, D_I), gdn_p, gconv_p, sconv_p)):
            lst.append(val)
        for lst, val in zip(new_s, (kv(k_new, rs, bs, ts), kv(v_new, rs, bs, ts),
                                    ki_new[rs].reshape(bs, ts, D_I), gdn_s, gconv_s, sconv_s)):
            lst.append(val)

    y_prompt = _rmsnorm(x, final_norm, F32, 0, mp).reshape(bp, tp, d)
    y_sample = _rmsnorm(x, final_norm, F32, mp, ms).reshape(bs, ts, d)
    outs_p = [jnp.stack(a) for a in new_p]
    outs_s = [jnp.stack(a) for a in new_s]
    return (y_prompt, y_sample, *outs_p, *outs_s)
```

```python
import functools
import math

import jax
import jax.numpy as jnp
from jax import lax
from jax.experimental import pallas as pl
from jax.experimental.pallas import tpu as pltpu

F32 = jnp.float32
BF16 = jnp.bfloat16

D_MODEL = 4096
DEPTH = 2
PAGE_SIZE = 128
H_A = 16
DK_A = 128
DV_A = 128
WIDTH_A = H_A * DV_A
CONV_A = 4
GDN_CHUNK = 64
H_B = 8
N_KV_B = 2
GROUP_B = H_B // N_KV_B
HEAD_DIM_B = 128
WIDTH_B = H_B * HEAD_DIM_B
KV_WIDTH_B = N_KV_B * HEAD_DIM_B
H_I = 32
D_I = 64
TOPK_MAX = 256
Q_BLOCK = 128
WIDTH_C = 1024
CONV_C = 3
D_FF = -(-8 * D_MODEL // (3 * 256)) * 256
EPS = 1e-6

IN_NAMES = ("qa", "ka", "va", "za", "aa", "ba", "qb", "kb", "vb", "qi", "ki", "wi",
            "gate_b", "gate_c", "hc", "ga", "gb", "gc")
IN_SIZES = (WIDTH_A, WIDTH_A, WIDTH_A, WIDTH_A, H_A, H_A,
            WIDTH_B, KV_WIDTH_B, KV_WIDTH_B, H_I * D_I, D_I, H_I,
            WIDTH_C, WIDTH_C, WIDTH_C, D_MODEL, D_MODEL, D_MODEL)

LANES = 128
N_BLOCK = 512
VMEM_LIMIT = 56 * 1024 * 1024
ACT_VMEM_BYTES = 32 * 1024 * 1024
NEG_BIG = -1e30

MAIN_ORDER = ("qa", "ka", "va", "za", "gate_b", "gate_c", "hc", "ga", "gb", "gc")
MID_ORDER = ("qi", "qb", "kb", "vb", "ki", "aa", "ba", "wi")


def _offsets(order):
    sizes = dict(zip(IN_NAMES, IN_SIZES))
    off, out = 0, {}
    for name in order:
        out[name] = off
        off += sizes[name]
    return out, off


SRC_OFF, _ = _offsets(IN_NAMES)
PM_OFF, PM_WIDTH = _offsets(MAIN_ORDER)
PD_OFF, _mid_cols = _offsets(MID_ORDER)
PD_WIDTH = -(-_mid_cols // N_BLOCK) * N_BLOCK
MAIN_RUN0 = PM_OFF["gate_b"]
MAIN_SHIFT = SRC_OFF["gate_b"] - MAIN_RUN0
assert PM_WIDTH % N_BLOCK == 0 and MAIN_RUN0 % N_BLOCK == 0 and MAIN_SHIFT % LANES == 0
assert all(SRC_OFF[n] == PM_OFF[n] for n in MAIN_ORDER[:4])
assert all(SRC_OFF[n] == PM_OFF[n] + MAIN_SHIFT for n in MAIN_ORDER[4:])
TAIL_OFF = PD_OFF["ki"]
TAIL_AA = PD_OFF["aa"] - TAIL_OFF
TAIL_BA = PD_OFF["ba"] - TAIL_OFF
TAIL_WI = PD_OFF["wi"] - TAIL_OFF
assert TAIL_OFF % N_BLOCK == 0 and PD_OFF["wi"] + H_I - TAIL_OFF == LANES


def _cparams(*sem):
    return pltpu.CompilerParams(dimension_semantics=sem, vmem_limit_bytes=VMEM_LIMIT)


def _dot(a, b):
    return jnp.dot(a, b, preferred_element_type=F32)


def _dot_nt(a, b):
    return lax.dot_general(a, b, (((1,), (1,)), ((), ())), preferred_element_type=F32)


def _dot_tn(a, b):
    return lax.dot_general(a, b, (((0,), (0,)), ((), ())), preferred_element_type=F32)


def _dot_hi(a, b):
    return jnp.dot(a, b, preferred_element_type=F32, precision=lax.Precision.HIGHEST)


def _sigmoid(x):
    return jax.nn.sigmoid(x)


def _silu(x):
    return x * jax.nn.sigmoid(x)


def _rmsnorm_body(x_ref, g_ref, o_ref):
    x = x_ref[...]
    ms = jnp.mean(x * x, axis=-1, keepdims=True)
    o_ref[...] = (x * lax.rsqrt(ms + EPS) * g_ref[...]).astype(o_ref.dtype)


def _row_block(m):
    for bm in (192, 128, 64, 32, 16, 8):
        if m % bm == 0:
            return bm
    raise ValueError(f"unsupported row count {m}")


def _rmsnorm(x, g, out_dtype, row0=0, nrows=None):
    d = x.shape[1]
    m = x.shape[0] - row0 if nrows is None else nrows
    bm = _row_block(math.gcd(m, row0) if row0 else m)
    rb0 = row0 // bm
    return pl.pallas_call(
        _rmsnorm_body,
        grid=(m // bm,),
        in_specs=[pl.BlockSpec((bm, d), lambda i: (rb0 + i, 0)), pl.BlockSpec((1, d), lambda i: (0, 0))],
        out_specs=pl.BlockSpec((bm, d), lambda i: (i, 0)),
        out_shape=jax.ShapeDtypeStruct((m, d), out_dtype),
        compiler_params=_cparams("parallel"),
        name="rmsnorm",
    )(x, g.reshape(1, d).astype(F32))


def _gdn_gates_body(t_ref, alog_ref, dtb_ref, g_ref, b_ref):
    t = t_ref[...]
    aa = t[:, TAIL_AA:TAIL_AA + H_A]
    ba = t[:, TAIL_BA:TAIL_BA + H_A]
    x = aa + dtb_ref[...]
    softplus = jnp.maximum(x, 0.0) + jnp.log1p(jnp.exp(-jnp.abs(x)))
    g_ref[...] = -jnp.exp(alog_ref[...]) * softplus
    b_ref[...] = _sigmoid(ba)


def _gdn_gates(p, a_log, dt_bias):
    m = p.shape[0]
    bm = _row_block(m)
    tail_blk = TAIL_OFF // LANES
    return pl.pallas_call(
        _gdn_gates_body,
        grid=(m // bm,),
        in_specs=[pl.BlockSpec((bm, LANES), lambda i: (i, tail_blk)),
                  pl.BlockSpec((1, H_A), lambda i: (0, 0)),
                  pl.BlockSpec((1, H_A), lambda i: (0, 0))],
        out_specs=[pl.BlockSpec((bm, H_A), lambda i: (i, 0)), pl.BlockSpec((bm, H_A), lambda i: (i, 0))],
        out_shape=[jax.ShapeDtypeStruct((m, H_A), F32)] * 2,
        compiler_params=_cparams("parallel"),
        name="gdn_gates",
    )(p, a_log.reshape(1, H_A).astype(F32), dt_bias.reshape(1, H_A).astype(F32))


def _big_row_block(m, k=D_MODEL, buffers=2):
    for bm in (2064, 1376, 1024, 688, 512, 256, 128, 64):
        if m % bm == 0 and buffers * bm * k * 2 <= ACT_VMEM_BYTES:
            return bm
    raise ValueError(f"unsupported row count {m}")


def _resident_rows(bm, k, buffers=2):
    if buffers == 1:
        return pl.BlockSpec((bm, k), lambda i, j: (i, 0), pipeline_mode=pl.Buffered(1))
    return pl.BlockSpec((bm, k), lambda i, j: (i, 0))


def _out_proj_body(a_ref, w_ref, r_ref, o_ref):
    o_ref[...] = r_ref[...] + _dot(a_ref[...], w_ref[...].astype(BF16))


def _out_proj(a, w, layer, residual, name="out_proj"):
    m, k = a.shape
    n = w.shape[2]
    bm, bn = _big_row_block(m, k), N_BLOCK // 2
    return pl.pallas_call(
        _out_proj_body,
        grid=(m // bm, n // bn),
        in_specs=[_resident_rows(bm, k),
                  pl.BlockSpec((None, k, bn), lambda i, j: (layer, 0, j)),
                  pl.BlockSpec((bm, bn), lambda i, j: (i, j))],
        out_specs=pl.BlockSpec((bm, bn), lambda i, j: (i, j)),
        out_shape=jax.ShapeDtypeStruct((m, n), F32),
        compiler_params=_cparams("parallel", "arbitrary"),
        name=name,
    )(a, w, residual)


IN_PROJ_PIECES = N_BLOCK // LANES


def _in_proj_main_body(a_ref, *rest):
    w_refs, p_ref = rest[:IN_PROJ_PIECES], rest[IN_PROJ_PIECES]
    w = jnp.concatenate([r[...].astype(BF16) for r in w_refs], axis=1)
    p_ref[...] = _dot(a_ref[...], w)


def _in_proj_main(a, w_in, layer):
    m, k = a.shape
    bm, bn = _big_row_block(m, k, buffers=1), N_BLOCK
    run0_blk, shift_blk = MAIN_RUN0 // bn, MAIN_SHIFT // LANES

    def piece(q):
        def index(i, j):
            return (layer, 0, IN_PROJ_PIECES * j + q + jnp.where(j >= run0_blk, shift_blk, 0))
        return pl.BlockSpec((None, k, LANES), index)

    return pl.pallas_call(
        _in_proj_main_body,
        grid=(m // bm, PM_WIDTH // bn),
        in_specs=[_resident_rows(bm, k, buffers=1)] + [piece(q) for q in range(IN_PROJ_PIECES)],
        out_specs=pl.BlockSpec((bm, bn), lambda i, j: (i, j)),
        out_shape=jax.ShapeDtypeStruct((m, PM_WIDTH), F32),
        compiler_params=_cparams("parallel", "arbitrary"),
        name="in_proj_main",
    )(a, *([w_in] * IN_PROJ_PIECES))


def _in_proj_mid_body(a_ref, w_ref, p_ref, k_ref, v_ref, ki_ref, *, jkv, jtail):
    acc = _dot(a_ref[...], w_ref[...])
    p_ref[...] = acc
    j = pl.program_id(1)

    @pl.when(j == jkv)
    def _():
        k_ref[...] = acc[:, :KV_WIDTH_B]
        v_ref[...] = acc[:, KV_WIDTH_B:2 * KV_WIDTH_B]

    @pl.when(j == jtail)
    def _():
        ki_ref[...] = acc[:, :D_I]


def _in_proj_mid(a, w_mid, layer):
    m, k = a.shape
    bm, bn = _big_row_block(m), N_BLOCK
    assert PD_OFF["kb"] % bn == 0 and PD_OFF["vb"] == PD_OFF["kb"] + KV_WIDTH_B
    row = lambda i, j: (i, 0)
    return pl.pallas_call(
        functools.partial(_in_proj_mid_body, jkv=PD_OFF["kb"] // bn, jtail=TAIL_OFF // bn),
        grid=(m // bm, PD_WIDTH // bn),
        in_specs=[_resident_rows(bm, k), pl.BlockSpec((None, k, bn), lambda i, j: (layer, 0, j))],
        out_specs=[pl.BlockSpec((bm, bn), lambda i, j: (i, j)),
                   pl.BlockSpec((bm, KV_WIDTH_B), row), pl.BlockSpec((bm, KV_WIDTH_B), row),
                   pl.BlockSpec((bm, D_I), row)],
        out_shape=[jax.ShapeDtypeStruct((m, PD_WIDTH), F32), jax.ShapeDtypeStruct((m, KV_WIDTH_B), F32),
                   jax.ShapeDtypeStruct((m, KV_WIDTH_B), F32), jax.ShapeDtypeStruct((m, D_I), F32)],
        compiler_params=_cparams("parallel", "arbitrary"),
        name="in_proj_mid",
    )(a, w_mid)


def _swiglu_body(a_ref, wg_ref, wu_ref, o_ref):
    a = a_ref[...]
    g = _dot(a, wg_ref[...].astype(BF16))
    u = _dot(a, wu_ref[...].astype(BF16))
    o_ref[...] = (_silu(g) * u).astype(o_ref.dtype)


def _swiglu(a, w_gate, w_up, layer):
    m, k = a.shape
    n = w_gate.shape[2]
    bm, bn = _big_row_block(m, k, buffers=1), N_BLOCK // 2
    assert n % bn == 0
    wspec = pl.BlockSpec((None, k, bn), lambda i, j: (layer, 0, j))
    return pl.pallas_call(
        _swiglu_body,
        grid=(m // bm, n // bn),
        in_specs=[_resident_rows(bm, k, buffers=1), wspec, wspec],
        out_specs=pl.BlockSpec((bm, bn), lambda i, j: (i, j)),
        out_shape=jax.ShapeDtypeStruct((m, n), BF16),
        compiler_params=_cparams("parallel", "arbitrary"),
        name="swiglu",
    )(a, w_gate, w_up)


def _merge_body(oa_ref, ob_ref, oc_ref, wa_ref, wb_ref, wc_ref, ga_ref, gb_ref, gc_ref, o_ref):
    acc = _sigmoid(ga_ref[...]) * _dot(oa_ref[...], wa_ref[...].astype(BF16))
    acc += _sigmoid(gb_ref[...]) * _dot(ob_ref[...], wb_ref[...].astype(BF16))
    acc += _sigmoid(gc_ref[...]) * _dot(oc_ref[...], wc_ref[...].astype(BF16))
    o_ref[...] = acc.astype(o_ref.dtype)


def _merge(oa, ob, oc, wa, wb, wc, layer, p):
    m = oa.shape[0]
    n = wa.shape[2]
    bm, bn = _big_row_block(m), N_BLOCK // 2
    ga0, gb0, gc0 = (PM_OFF[s] // bn for s in ("ga", "gb", "gc"))
    row = lambda i, j: (i, 0)
    wspec = lambda w: pl.BlockSpec((None, w.shape[1], bn), lambda i, j: (layer, 0, j))
    return pl.pallas_call(
        _merge_body,
        grid=(m // bm, n // bn),
        in_specs=[_resident_rows(bm, oa.shape[1]), _resident_rows(bm, ob.shape[1]),
                  _resident_rows(bm, oc.shape[1]),
                  wspec(wa), wspec(wb), wspec(wc),
                  pl.BlockSpec((bm, bn), lambda i, j: (i, ga0 + j)),
                  pl.BlockSpec((bm, bn), lambda i, j: (i, gb0 + j)),
                  pl.BlockSpec((bm, bn), lambda i, j: (i, gc0 + j))],
        out_specs=pl.BlockSpec((bm, bn), lambda i, j: (i, j)),
        out_shape=jax.ShapeDtypeStruct((m, n), BF16),
        compiler_params=_cparams("parallel", "arbitrary"),
        name="merge",
    )(oa, ob, oc, wa, wb, wc, p, p, p)


GDN_HEADS_PER_STEP = 16
GDN_INV_BASE = 8


def _gdn_body(q_ref, k_ref, v_ref, z_ref, wq_ref, wk_ref, wv_ref, tq_ref, tk_ref, tv_ref,
              g_ref, b_ref, s0_ref, ng_ref, o_ref, sout_ref, s_scr, tail_scr,
              *, chunk, hg):
    c = chunk
    w = hg * DK_A
    n = pl.program_id(2)

    @pl.when(n == 0)
    def _():
        s_scr[...] = s0_ref[...].astype(F32)
        tail_scr[:, 0:w] = tq_ref[...]
        tail_scr[:, w:2 * w] = tk_ref[...]
        tail_scr[:, 2 * w:3 * w] = tv_ref[...]

    act = []
    for idx, (x_ref, w_ref) in enumerate(((q_ref, wq_ref), (k_ref, wk_ref), (v_ref, wv_ref))):
        x = x_ref[...]
        cw = w_ref[...]
        xfull = jnp.concatenate([tail_scr[:, idx * w:(idx + 1) * w], x], axis=0)
        y = x * cw[CONV_A - 1:CONV_A, :]
        for j in range(1, CONV_A):
            y = y + pltpu.roll(xfull, j, axis=0)[8:, :] * cw[CONV_A - 1 - j:CONV_A - j, :]
        tail_scr[:, idx * w:(idx + 1) * w] = xfull[c:c + 8, :]
        act.append(_silu(y))
    qs, ks, vs = act
    z = z_ref[...]

    ii = lax.broadcasted_iota(jnp.int32, (c, c), 0)
    jj = lax.broadcasted_iota(jnp.int32, (c, c), 1)
    incl = jj <= ii
    strict = jj < ii
    eye = (ii == jj).astype(F32)
    ng = ng_ref[...]
    b_blk = b_ref[...]
    gcum_all = _dot_hi(incl.astype(F32), g_ref[...])
    gcum_t = gcum_all.T

    heads = range(hg)
    hsl = [slice(hh * DK_A, (hh + 1) * DK_A) for hh in heads]
    q_l, k_l, k16_l, decay_l, egc_l, kdec_l, glast_l, mm_l, rhs_l = ([] for _ in range(9))
    for hh in heads:
        q = qs[:, hsl[hh]]
        k = ks[:, hsl[hh]]
        q = q * lax.rsqrt(jnp.sum(q * q, axis=-1, keepdims=True) + EPS) * (DK_A ** -0.5)
        k = k * lax.rsqrt(jnp.sum(k * k, axis=-1, keepdims=True) + EPS)
        bc = b_blk[:, hh:hh + 1]
        gcum_c = gcum_all[:, hh:hh + 1]
        gcum_r = gcum_t[hh:hh + 1, :]
        decay = jnp.exp(jnp.where(incl, gcum_c - gcum_r, -jnp.inf))
        kb = k * bc
        k16 = k.astype(BF16)
        egc = jnp.exp(gcum_c)
        glast = gcum_c[c - 1:c, :]
        mm = jnp.where(strict, _dot_nt(kb.astype(BF16), k16) * decay, 0.0)
        q_l.append(q)
        k_l.append(k)
        k16_l.append(k16)
        decay_l.append(decay)
        egc_l.append(egc)
        glast_l.append(glast)
        kdec_l.append(jnp.exp(glast - gcum_c))
        rhs_l.append(jnp.concatenate([vs[:, hsl[hh]] * bc, kb * egc], axis=1).astype(BF16))
        mm_l.append(mm)
    base = min(c, GDN_INV_BASE)
    log_base = base.bit_length() - 1
    in_base = jnp.right_shift(ii, log_base) == jnp.right_shift(jj, log_base)
    pw_l = [jnp.where(in_base, -mm, 0.0) for mm in mm_l]
    tinv_l = [eye + pw for pw in pw_l]
    for _ in range(log_base - 1):
        for hh in heads:
            pw16 = pw_l[hh].astype(BF16)
            pw_l[hh] = _dot(pw16, pw16)
        for hh in heads:
            tinv_l[hh] = tinv_l[hh] + _dot(tinv_l[hh].astype(BF16), pw_l[hh].astype(BF16))
    log_size = log_base
    while (1 << log_size) < c:
        lower_left = jnp.logical_and(
            jnp.right_shift(ii, log_size + 1) == jnp.right_shift(jj, log_size + 1),
            jnp.logical_and(jnp.bitwise_and(jnp.right_shift(ii, log_size), 1) == 1,
                            jnp.bitwise_and(jnp.right_shift(jj, log_size), 1) == 0))
        for hh in heads:
            t16 = tinv_l[hh].astype(BF16)
            c16 = jnp.where(lower_left, mm_l[hh], 0.0).astype(BF16)
            tinv_l[hh] = tinv_l[hh] - _dot(_dot(t16, c16).astype(BF16), t16)
        log_size += 1
    sol_l =[_dot(tinv_l[hh].astype(BF16), rhs_l[hh]) for hh in heads]
    attn_l = [(_dot_nt(q_l[hh].astype(BF16), k16_l[hh]) * decay_l[hh]).astype(BF16) for hh in heads]
    s_l = [s_scr[hh] for hh in heads]
    s16_l = [s_l[hh].astype(BF16) for hh in heads]
    v16_l = [(sol_l[hh][:, :DV_A] - _dot(sol_l[hh][:, DV_A:].astype(BF16), s16_l[hh])).astype(BF16)
             for hh in heads]
    o_l = [_dot((q_l[hh] * egc_l[hh]).astype(BF16), s16_l[hh]) + _dot(attn_l[hh], v16_l[hh]) for hh in heads]
    snew_l = [s_l[hh] * jnp.exp(glast_l[hh]) + _dot_tn((k_l[hh] * kdec_l[hh]).astype(BF16), v16_l[hh])
              for hh in heads]
    s_scr[...] = jnp.stack(snew_l, axis=0)
    o_l = [o * lax.rsqrt(jnp.mean(o * o, axis=-1, keepdims=True) + EPS) * ng for o in o_l]
    o_ref[...] = (jnp.concatenate(o_l, axis=1) * _silu(z)).astype(o_ref.dtype)

    @pl.when(n == pl.num_programs(2) - 1)
    def _():
        sout_ref[...] = s_scr[...].astype(sout_ref.dtype)


def _gdn(p, row0, nseq, t, chunk, conv_w, g, beta, conv_buf, s0, norm_g, out_rows=None):
    hg = GDN_HEADS_PER_STEP
    assert hg == H_A
    w = hg * DK_A
    nch = t // chunk
    nhg = H_A // hg
    rb0 = row0 // chunk
    gspec = pl.BlockSpec((chunk, H_A), lambda b, h, n: (rb0 + b * nch + n, 0))
    tail = jnp.pad(conv_buf.astype(F32), ((0, 0), (8 - (CONV_A - 1), 0), (0, 0)))
    qb0, kb0, vb0, zb0 = (PM_OFF[s] // w for s in ("qa", "ka", "va", "za"))
    pspec = lambda b0: pl.BlockSpec((chunk, w), lambda b, h, n: (rb0 + b * nch + n, b0 + h))
    wspec = lambda b0: pl.BlockSpec((CONV_A, w), lambda b, h, n: (0, b0 + h))
    tspec = lambda b0: pl.BlockSpec((None, 8, w), lambda b, h, n: (b, 0, b0 + h))
    o, s_out = pl.pallas_call(
        functools.partial(_gdn_body, chunk=chunk, hg=hg),
        grid=(nseq, nhg, nch),
        in_specs=[pspec(qb0), pspec(kb0), pspec(vb0), pspec(zb0),
                  wspec(0), wspec(nhg), wspec(2 * nhg),
                  tspec(0), tspec(nhg), tspec(2 * nhg),
                  gspec, gspec,
                  pl.BlockSpec((None, hg, DK_A, DV_A), lambda b, h, n: (b, h, 0, 0)),
                  pl.BlockSpec((1, DV_A), lambda b, h, n: (0, 0))],
        out_specs=[pl.BlockSpec((chunk, w), lambda b, h, n: (b * nch + n, h)),
                   pl.BlockSpec((None, hg, DK_A, DV_A), lambda b, h, n: (b, h, 0, 0))],
        out_shape=[jax.ShapeDtypeStruct((out_rows or nseq * t, WIDTH_A), BF16),
                   jax.ShapeDtypeStruct((nseq, H_A, DK_A, DV_A), s0.dtype)],
        scratch_shapes=[pltpu.VMEM((hg, DK_A, DV_A), F32), pltpu.VMEM((8, 3 * w), F32)],
        compiler_params=_cparams("parallel", "parallel", "arbitrary"),
        name=f"gdn_c{chunk}",
    )(p, p, p, p, conv_w, conv_w, conv_w, tail, tail, tail, g, beta, s0,
      norm_g.reshape(1, DV_A).astype(F32))
    return o, s_out


SCONV_COLS = 256


def _sconv_body(gb_ref, gc_ref, hc_ref, w_ref, buf_ref, o_ref, st_ref, *, t):
    pr = gc_ref[...] * hc_ref[...]
    cw = w_ref[...]
    xfull = jnp.concatenate([buf_ref[...], pr], axis=0)
    y = pr * cw[CONV_C - 1:CONV_C, :]
    for j in range(1, CONV_C):
        y = y + pltpu.roll(xfull, j, axis=0)[8:, :] * cw[CONV_C - 1 - j:CONV_C - j, :]
    o_ref[...] = (gb_ref[...] * y).astype(o_ref.dtype)
    st_ref[...] = xfull[t:t + 8, :]


def _sconv(p, row0, nseq, t, conv_w, buf, out_rows=None):
    cols = SCONV_COLS
    ncb = WIDTH_C // cols
    rb0 = row0 // t
    b0, c0, h0 = (PM_OFF[s] // cols for s in ("gate_b", "gate_c", "hc"))
    buf8 = jnp.pad(buf.astype(F32), ((0, 0), (8 - (CONV_C - 1), 0), (0, 0)))
    pspec = lambda o: pl.BlockSpec((t, cols), lambda b, j: (rb0 + b, o + j))
    o, st = pl.pallas_call(
        functools.partial(_sconv_body, t=t),
        grid=(nseq, ncb),
        in_specs=[pspec(b0), pspec(c0), pspec(h0),
                  pl.BlockSpec((CONV_C, cols), lambda b, j: (0, j)),
                  pl.BlockSpec((None, 8, cols), lambda b, j: (b, 0, j))],
        out_specs=[pl.BlockSpec((t, cols), lambda b, j: (b, j)),
                   pl.BlockSpec((None, 8, cols), lambda b, j: (b, 0, j))],
        out_shape=[jax.ShapeDtypeStruct((out_rows or nseq * t, WIDTH_C), BF16),
                   jax.ShapeDtypeStruct((nseq, 8, WIDTH_C), F32)],
        compiler_params=_cparams("parallel", "parallel"),
        name=f"sconv_t{t}",
    )(p, p, p, conv_w, buf8)
    return o, st[:, 8 - (CONV_C - 1):, :]


BISECT_EVERY = 8
MAX_SEARCH = 2400


def _select_threshold(count_ge, lo0, hi0, n_valid, k):
    kf = jnp.float32(k)

    def cond(st):
        return jnp.logical_and(st[0] < MAX_SEARCH, jnp.min(st[-1]) < 0.5)

    def step(st):
        it, lo, hi, glo, ghi, t, side, done = st
        half = 0.5 * lo + 0.5 * hi
        cand = lo + (hi - lo) * (glo / jnp.maximum(glo - ghi, 1e-9))
        mid = jnp.where(it % BISECT_EVERY == BISECT_EVERY - 1, half, cand)
        mid = jnp.where(jnp.logical_and(mid > lo, mid < hi), mid, half)
        adjacent = jnp.logical_not(jnp.logical_and(mid > lo, mid < hi))
        first = it == 0
        mid = jnp.where(first, hi, mid)
        adjacent = jnp.logical_and(adjacent, jnp.logical_not(first))
        g = count_ge(mid) - kf
        ge = g >= 0.0
        finish = jnp.logical_or(jnp.logical_or(g == 0.0, adjacent), jnp.logical_and(first, ge))
        newly = jnp.logical_and(finish, done < 0.5)
        t = jnp.where(newly, jnp.where(adjacent, lo, mid), t)
        new_side = jnp.where(ge, 1.0, -1.0)
        damp = jnp.where(jnp.logical_and(new_side == side, jnp.logical_not(first)), 0.5, 1.0)
        glo = jnp.where(ge, g, glo * damp)
        ghi = jnp.where(ge, ghi * damp, g)
        lo = jnp.where(ge, mid, lo)
        hi = jnp.where(ge, hi, mid)
        done = jnp.where(finish, 1.0, done)
        return it + 1, lo, hi, glo, ghi, t, new_side, done

    done0 = jnp.where(n_valid > kf, 0.0, 1.0)
    st = lax.while_loop(cond, lambda st: step(step(st)), (jnp.int32(0), lo0, hi0, n_valid - kf, jnp.full_like(lo0, -kf),
                                     lo0, jnp.zeros_like(lo0), done0))
    return st[5]


KEY_CHUNK = 512


def _dsa_prompt_body(qi_ref, tq_ref, qb_ref, kb_ref, vb_ref, tk_ref, o_ref,
                     sc_scr, m_scr, l_scr, acc_scr, *, tq, sc, topk):
    i = pl.program_id(1)
    nck = (i * tq + tq - 1) // sc + 1
    wt = tq_ref[...].T[TAIL_WI:TAIL_WI + H_I, :] * ((D_I ** -0.5) * (H_I ** -0.5))
    qi = qi_ref[...]
    pairs = [jnp.concatenate([qi[:, (2 * p) * D_I:(2 * p + 1) * D_I],
                              qi[:, (2 * p + 1) * D_I:(2 * p + 2) * D_I]], axis=0).astype(BF16)
             for p in range(H_I // 2)]
    tpos = i * tq + lax.broadcasted_iota(jnp.int32, (1, tq), 1)

    def score_chunk(c, carry):
        r0 = pl.multiple_of(c * sc, sc)
        kic = tk_ref[pl.ds(r0, sc), :].astype(BF16)
        acc = jnp.zeros((sc, tq), F32)
        for p in range(H_I // 2):
            d = _dot_nt(kic, pairs[p])
            acc = acc + jnp.maximum(d[:, :tq], 0.0) * wt[2 * p:2 * p + 1, :]
            acc = acc + jnp.maximum(d[:, tq:], 0.0) * wt[2 * p + 1:2 * p + 2, :]
        kpos = r0 + lax.broadcasted_iota(jnp.int32, (sc, tq), 0)
        sc_scr[pl.ds(r0, sc), :] = jnp.where(kpos <= tpos, acc, -jnp.inf)
        return carry

    lax.fori_loop(0, nck, score_chunk, 0)

    def minmax_chunk(c, carry):
        mn, mx = carry
        x = sc_scr[pl.ds(pl.multiple_of(c * sc, sc), sc), :]
        mx = jnp.maximum(mx, jnp.max(x, axis=0, keepdims=True))
        mn = jnp.minimum(mn, jnp.min(jnp.where(x == -jnp.inf, jnp.inf, x), axis=0, keepdims=True))
        return mn, mx

    mn, mx = lax.fori_loop(0, nck, minmax_chunk,
                           (jnp.full((1, tq), jnp.inf, F32), jnp.full((1, tq), -jnp.inf, F32)))

    def count_ge(mid):
        def body(c, acc):
            x = sc_scr[pl.ds(pl.multiple_of(c * sc, sc), sc), :]
            hit = jnp.where(x >= mid, 1.0, 0.0).reshape(8, sc // 64, 8, tq)
            return acc + jnp.sum(jnp.sum(hit, axis=1), axis=0)
        return jnp.sum(lax.fori_loop(0, nck, body, jnp.zeros((8, tq), F32)), axis=0, keepdims=True)

    thr = _select_threshold(count_ge, mn, mx, (tpos + 1).astype(F32), topk)

    def count_gt(c, acc):
        x = sc_scr[pl.ds(pl.multiple_of(c * sc, sc), sc), :]
        hit = jnp.where(x > thr, 1.0, 0.0).reshape(8, sc // 64, 8, tq)
        return acc + jnp.sum(jnp.sum(hit, axis=1), axis=0)
    n_above = jnp.sum(lax.fori_loop(0, nck, count_gt, jnp.zeros((8, tq), F32)), axis=0, keepdims=True)
    crowded = count_ge(thr) > topk
    places = topk - n_above

    @pl.when(jnp.max(jnp.where(crowded, 1.0, 0.0)) > 0.5)
    def _():
        earlier = (lax.broadcasted_iota(jnp.int32, (sc, sc), 1)
                   < lax.broadcasted_iota(jnp.int32, (sc, sc), 0)).astype(BF16)

        def retire(c, seen):
            rows = pl.ds(pl.multiple_of(c * sc, sc), sc)
            x = sc_scr[rows, :]
            tie = jnp.logical_and(x == thr, crowded)
            tie_f = jnp.where(tie, 1.0, 0.0)
            rank = seen + _dot(earlier, tie_f.astype(BF16))
            sc_scr[rows, :] = jnp.where(jnp.logical_and(tie, rank >= places), -jnp.inf, x)
            return seen + jnp.sum(tie_f, axis=0, keepdims=True)

        lax.fori_loop(0, nck, retire, jnp.zeros((1, tq), F32))

    m_scr[...] = jnp.full(m_scr.shape, NEG_BIG, F32)
    l_scr[...] = jnp.zeros(l_scr.shape, F32)
    acc_scr[...] = jnp.zeros(acc_scr.shape, F32)
    q = qb_ref[...].astype(BF16)
    scale = HEAD_DIM_B ** -0.5

    def attn_chunk(c, carry):
        r0 = pl.multiple_of(c * sc, sc)
        sel = sc_scr[pl.ds(r0, sc), :] >= thr
        for n in range(N_KV_B):
            kc = kb_ref[pl.ds(r0, sc), n * HEAD_DIM_B:(n + 1) * HEAD_DIM_B].astype(BF16)
            vc = vb_ref[pl.ds(r0, sc), n * HEAD_DIM_B:(n + 1) * HEAD_DIM_B].astype(BF16)
            for g in range(GROUP_B):
                h = n * GROUP_B + g
                hs = slice(h * HEAD_DIM_B, (h + 1) * HEAD_DIM_B)
                s = jnp.where(sel, _dot_nt(kc, q[:, hs]) * scale, NEG_BIG)
                m_old = m_scr[h:h + 1, :]
                m_new = jnp.maximum(m_old, jnp.max(s, axis=0, keepdims=True))
                pexp = jnp.exp(s - m_new)
                alpha = jnp.exp(m_old - m_new)
                l_scr[h:h + 1, :] = alpha * l_scr[h:h + 1, :] + jnp.sum(pexp, axis=0, keepdims=True)
                acc_scr[hs, :] = alpha * acc_scr[hs, :] + _dot_tn(vc, pexp.astype(BF16))
                m_scr[h:h + 1, :] = m_new
        return carry

    lax.fori_loop(0, nck, attn_chunk, 0)

    for h in range(H_B):
        hs = slice(h * HEAD_DIM_B, (h + 1) * HEAD_DIM_B)
        ot = acc_scr[hs, :] / l_scr[h:h + 1, :]
        o_ref[:, hs] = ot.T.astype(o_ref.dtype)


def _dsa_prompt(p, k_new, v_new, ki_new, nseq, s, out_rows=None):
    tq, sc = Q_BLOCK, min(KEY_CHUNK, s)
    topk = min(TOPK_MAX, s // 4)
    nqb = s // tq
    qi0 = PD_OFF["qi"] // (H_I * D_I)
    qb0 = PD_OFF["qb"] // WIDTH_B
    tl0 = TAIL_OFF // LANES
    return pl.pallas_call(
        functools.partial(_dsa_prompt_body, tq=tq, sc=sc, topk=topk),
        grid=(nseq, nqb),
        in_specs=[pl.BlockSpec((tq, H_I * D_I), lambda b, i: (b * nqb + i, qi0)),
                  pl.BlockSpec((tq, LANES), lambda b, i: (b * nqb + i, tl0)),
                  pl.BlockSpec((tq, WIDTH_B), lambda b, i: (b * nqb + i, qb0)),
                  pl.BlockSpec((s, KV_WIDTH_B), lambda b, i: (b, 0)),
                  pl.BlockSpec((s, KV_WIDTH_B), lambda b, i: (b, 0)),
                  pl.BlockSpec((s, D_I), lambda b, i: (b, 0))],
        out_specs=pl.BlockSpec((tq, WIDTH_B), lambda b, i: (b * nqb + i, 0)),
        out_shape=jax.ShapeDtypeStruct((out_rows or nseq * s, WIDTH_B), BF16),
        scratch_shapes=[pltpu.VMEM((s, tq), F32), pltpu.VMEM((H_B, tq), F32),
                        pltpu.VMEM((H_B, tq), F32), pltpu.VMEM((WIDTH_B, tq), F32)],
        compiler_params=_cparams("parallel", "arbitrary"),
        name="dsa_prompt",
    )(p, p, p, k_new, v_new, ki_new)


PAGES_PER_STEP = 16
TIE_CHUNK = 512


def _dsa_sample_scores_body(pt_ref, q_ref, w_ref, *rest, nsteps, pps, t):
    page_refs, new_ref, o_ref, onew_ref = rest[:pps], rest[pps], rest[pps + 1], rest[pps + 2]
    j = pl.program_id(1)
    q = q_ref[...].astype(BF16)
    wv = w_ref[...] * ((D_I ** -0.5) * (H_I ** -0.5))

    def scores(keys_t):
        d = _dot(q, keys_t.astype(BF16))
        return jnp.sum((jnp.maximum(d, 0.0) * wv).reshape(t, H_I, PAGE_SIZE), axis=1)

    @pl.when(j < nsteps)
    def _():
        o_ref[...] = jnp.concatenate([scores(r[...]) for r in page_refs], axis=1)

    @pl.when(j == nsteps)
    def _():
        knew = lax.broadcasted_iota(jnp.int32, (t, PAGE_SIZE), 1)
        tnew = lax.broadcasted_iota(jnp.int32, (t, PAGE_SIZE), 0)
        onew_ref[...] = jnp.where(knew <= tnew, scores(new_ref[...]), -jnp.inf)


def _page_specs(block, layer, nsteps, pps):
    def spec(p):
        def index(b, j, pt):
            return (layer, pt[b, jnp.minimum(j, nsteps - 1) * pps + p]) + (0,) * (len(block) - 2)
        return pl.BlockSpec(block, index)
    return [spec(p) for p in range(pps)]


def _dsa_sample_scores(page_table, q, wv, cache_kidx, layer, ki_new):
    nseq, npages = page_table.shape
    t = q.shape[1] // H_I
    pps = math.gcd(PAGES_PER_STEP, npages)
    nsteps = npages // pps
    return pl.pallas_call(
        functools.partial(_dsa_sample_scores_body, nsteps=nsteps, pps=pps, t=t),
        grid_spec=pltpu.PrefetchScalarGridSpec(
            num_scalar_prefetch=1,
            grid=(nseq, nsteps + 1),
            in_specs=[pl.BlockSpec((None, t * H_I, D_I), lambda b, j, pt: (b, 0, 0)),
                      pl.BlockSpec((None, t * H_I, PAGE_SIZE), lambda b, j, pt: (b, 0, 0)),
                      *_page_specs((None, None, D_I, PAGE_SIZE), layer, nsteps, pps),
                      pl.BlockSpec((None, D_I, PAGE_SIZE), lambda b, j, pt: (b, 0, 0))],
            out_specs=[pl.BlockSpec((None, t, pps * PAGE_SIZE),
                                    lambda b, j, pt: (b, 0, jnp.minimum(j, nsteps - 1))),
                       pl.BlockSpec((None, t, PAGE_SIZE), lambda b, j, pt: (b, 0, 0))],
        ),
        out_shape=[jax.ShapeDtypeStruct((nseq, t, npages * PAGE_SIZE), F32),
                   jax.ShapeDtypeStruct((nseq, t, PAGE_SIZE), F32)],
        compiler_params=_cparams("parallel", "arbitrary"),
        name="dsa_sample_scores",
    )(page_table, q, wv, *([cache_kidx] * pps), ki_new)


def _dsa_sample_attn_body(pt_ref, sc_ref, scn_ref, q_ref, *rest, nsteps, pps, t, past, topk, tie_chunk):
    kpage_refs, vpage_refs = rest[:pps], rest[pps:2 * pps]
    knew_ref, vnew_ref, o_ref, scp_scr, scn_scr, thr_scr, m_scr, l_scr, acc_scr = rest[2 * pps:]
    j = pl.program_id(1)

    @pl.when(j == 0)
    def _():
        xp, xn = sc_ref[...], scn_ref[...]
        scp_scr[...] = xp
        scn_scr[...] = xn
        mx = jnp.maximum(jnp.max(xp, axis=1, keepdims=True), jnp.max(xn, axis=1, keepdims=True))
        mn = jnp.minimum(jnp.min(xp, axis=1, keepdims=True),
                         jnp.min(jnp.where(xn == -jnp.inf, jnp.inf, xn), axis=1, keepdims=True))
        cnt = lambda hit: jnp.sum(jnp.where(hit, 1.0, 0.0), axis=1, keepdims=True)

        def count_ge(mid):
            return cnt(sc_ref[...] >= mid) + cnt(scn_ref[...] >= mid)

        n_valid = (past + 1 + lax.broadcasted_iota(jnp.int32, (t, 1), 0)).astype(F32)
        thr = _select_threshold(count_ge, mn, mx, n_valid, topk)

        crowded = count_ge(thr) > topk
        places = topk - (cnt(xp > thr) + cnt(xn > thr))

        @pl.when(jnp.max(jnp.where(crowded, 1.0, 0.0)) > 0.5)
        def _():
            cw = tie_chunk
            earlier = (lax.broadcasted_iota(jnp.int32, (cw, cw), 0)
                       < lax.broadcasted_iota(jnp.int32, (cw, cw), 1)).astype(BF16)

            def retire(x, seen, width):
                tie = jnp.logical_and(x == thr, crowded)
                tie_f = jnp.where(tie, 1.0, 0.0)
                rank = seen + _dot(tie_f.astype(BF16), earlier[:width, :width])
                return (jnp.where(jnp.logical_and(tie, rank >= places), -jnp.inf, x),
                        seen + jnp.sum(tie_f, axis=1, keepdims=True))

            def past_chunk(c, seen):
                cols = pl.ds(pl.multiple_of(c * cw, cw), cw)
                x, seen = retire(scp_scr[:, cols], seen, cw)
                scp_scr[:, cols] = x
                return seen

            seen = lax.fori_loop(0, past // cw, past_chunk, jnp.zeros((t, 1), F32))
            scn_scr[...] = retire(scn_scr[...], seen, PAGE_SIZE)[0]

        thr_scr[...] = jnp.broadcast_to(thr, thr_scr.shape)
        m_scr[...] = jnp.full(m_scr.shape, NEG_BIG, F32)
        l_scr[...] = jnp.zeros(l_scr.shape, F32)
        acc_scr[...] = jnp.zeros(acc_scr.shape, F32)

    scale = HEAD_DIM_B ** -0.5

    def attend(x, keys, vals):
        sel_t = jnp.where(x >= thr_scr[:, 0:1], 1.0, 0.0)
        sel = jnp.concatenate([sel_t] * GROUP_B, axis=0) > 0.5
        for n in range(N_KV_B):
            s = jnp.where(sel, _dot_nt(q_ref[n].astype(BF16), keys(n).astype(BF16)) * scale, NEG_BIG)
            m_old = m_scr[n]
            m_new = jnp.maximum(m_old, jnp.max(s, axis=1, keepdims=True))
            pexp = jnp.exp(s - m_new)
            alpha = jnp.exp(m_old - m_new)
            l_scr[n] = alpha * l_scr[n] + jnp.sum(pexp, axis=1, keepdims=True)
            acc_scr[n] = alpha * acc_scr[n] + _dot(pexp.astype(BF16), vals(n).astype(BF16))
            m_scr[n] = m_new

    def page_head(r, n):
        return r[pl.ds(n, PAGE_SIZE, stride=N_KV_B), :]

    @pl.when(j < nsteps)
    def _():
        width = pps * PAGE_SIZE
        x = scp_scr[:, pl.ds(pl.multiple_of(j * width, width), width)]
        attend(x,
               lambda n: jnp.concatenate([page_head(r, n) for r in kpage_refs], axis=0),
               lambda n: jnp.concatenate([page_head(r, n) for r in vpage_refs], axis=0))

    @pl.when(j == nsteps)
    def _():
        hs = lambda n: slice(n * HEAD_DIM_B, (n + 1) * HEAD_DIM_B)
        attend(scn_scr[...], lambda n: knew_ref[:, hs(n)], lambda n: vnew_ref[:, hs(n)])
        for n in range(N_KV_B):
            o_ref[n] = (acc_scr[n] / l_scr[n]).astype(o_ref.dtype)


def _dsa_sample_attn(page_table, scores, scores_new, q, cache_k, cache_v, layer, k_new, v_new):
    nseq, npages = page_table.shape
    t = scores.shape[1]
    past = npages * PAGE_SIZE
    topk = min(TOPK_MAX, (past + t) // 4)
    rows = GROUP_B * t
    pps = math.gcd(PAGES_PER_STEP, npages)
    nsteps = npages // pps
    page_block = (None, None, PAGE_SIZE * N_KV_B, HEAD_DIM_B)
    const3 = lambda b, j, pt: (b, 0, 0)
    return pl.pallas_call(
        functools.partial(_dsa_sample_attn_body, nsteps=nsteps, pps=pps, t=t, past=past, topk=topk,
                          tie_chunk=math.gcd(TIE_CHUNK, past)),
        grid_spec=pltpu.PrefetchScalarGridSpec(
            num_scalar_prefetch=1,
            grid=(nseq, nsteps + 1),
            in_specs=[pl.BlockSpec((None, t, past), const3),
                      pl.BlockSpec((None, t, PAGE_SIZE), const3),
                      pl.BlockSpec((None, N_KV_B, rows, HEAD_DIM_B), lambda b, j, pt: (b, 0, 0, 0)),
                      *_page_specs(page_block, layer, nsteps, pps),
                      *_page_specs(page_block, layer, nsteps, pps),
                      pl.BlockSpec((None, PAGE_SIZE, KV_WIDTH_B), const3),
                      pl.BlockSpec((None, PAGE_SIZE, KV_WIDTH_B), const3)],
            out_specs=pl.BlockSpec((None, N_KV_B, rows, HEAD_DIM_B), lambda b, j, pt: (b, 0, 0, 0)),
            scratch_shapes=[pltpu.VMEM((t, past), F32),
                            pltpu.VMEM((t, PAGE_SIZE), F32),
                            pltpu.VMEM((t, PAGE_SIZE), F32),
                            pltpu.VMEM((N_KV_B, rows, 1), F32),
                            pltpu.VMEM((N_KV_B, rows, 1), F32),
                            pltpu.VMEM((N_KV_B, rows, HEAD_DIM_B), F32)],
        ),
        out_shape=jax.ShapeDtypeStruct((nseq, N_KV_B, rows, HEAD_DIM_B), BF16),
        compiler_params=_cparams("parallel", "arbitrary"),
        name="dsa_sample_attn",
    )(page_table, scores, scores_new, q, *([cache_k] * pps), *([cache_v] * pps), k_new, v_new)


def _dsa_sample(ps, k_s, v_s, ki_s, page_table, cache_k, cache_v, cache_kidx, layer):
    nseq = page_table.shape[0]
    t = ps.shape[0] // nseq
    seg = lambda name, width: ps[:, PD_OFF[name]:PD_OFF[name] + width]
    qi = seg("qi", H_I * D_I).reshape(nseq, t * H_I, D_I)
    wi = seg("wi", H_I).reshape(nseq, t * H_I, 1)
    wv = jnp.broadcast_to(wi, (nseq, t * H_I, PAGE_SIZE))
    pad_rows = lambda a: jnp.pad(a.reshape(nseq, t, -1), ((0, 0), (0, PAGE_SIZE - t), (0, 0)))
    kidx_t = cache_kidx.transpose(0, 1, 3, 2)
    scores, scores_new = _dsa_sample_scores(page_table, qi, wv, kidx_t, layer,
                                            pad_rows(ki_s).transpose(0, 2, 1))
    q = seg("qb", WIDTH_B).reshape(nseq, t, N_KV_B, GROUP_B, HEAD_DIM_B)
    q = q.transpose(0, 2, 3, 1, 4).reshape(nseq, N_KV_B, GROUP_B * t, HEAD_DIM_B)
    pool_rows = lambda c: c.reshape(c.shape[0], c.shape[1], PAGE_SIZE * N_KV_B, HEAD_DIM_B)
    o = _dsa_sample_attn(page_table, scores, scores_new, q, pool_rows(cache_k), pool_rows(cache_v),
                         layer, pad_rows(k_s), pad_rows(v_s))
    o = o.reshape(nseq, N_KV_B, GROUP_B, t, HEAD_DIM_B).transpose(0, 3, 1, 2, 4)
    return o.reshape(nseq * t, WIDTH_B)


def _prep_w_mid(w_in):
    sizes = dict(zip(IN_NAMES, IN_SIZES))
    z0 = min(SRC_OFF[n] for n in MID_ORDER + ("aa", "ba"))
    z1 = max(SRC_OFF[n] + sizes[n] for n in MID_ORDER)
    zone = lax.optimization_barrier(lax.slice_in_dim(w_in, z0, z1, axis=2))
    cols = [zone[:, :, SRC_OFF[n] - z0:SRC_OFF[n] - z0 + sizes[n]].astype(BF16) for n in MID_ORDER]
    cols.append(jnp.zeros(w_in.shape[:2] + (PD_WIDTH - sum(sizes[n] for n in MID_ORDER),), BF16))
    return jnp.concatenate(cols, axis=2)


def kernel(x_prompt, x_sample, cache_k, cache_v, cache_kidx, page_table, state_gdn, state_gdn_conv,
           state_sconv, final_norm, norm1, norm2, w_in, conv_a, a_log, dt_bias, gdn_norm, conv_c,
           w_branch_a, w_branch_b, w_branch_c, w_o, w_gate, w_up, w_down):
    bp, tp, d = x_prompt.shape
    bs, ts = x_sample.shape[:2]
    mp, ms = bp * tp, bs * ts
    x = jnp.concatenate([x_prompt.reshape(mp, d), x_sample.reshape(ms, d)], axis=0)
    new_p = [[] for _ in range(6)]
    new_s = [[] for _ in range(6)]
    w_mid = _prep_w_mid(w_in)
    w_down16 = w_down.astype(BF16)
    for l in range(DEPTH):
        xn = _rmsnorm(x, norm1[l], BF16)
        p = _in_proj_main(xn, w_in, l)
        pd, k_new, v_new, ki_new = _in_proj_mid(xn, w_mid, l)
        g, beta = _gdn_gates(pd, a_log[l], dt_bias[l])
        conv_w = conv_a[l].astype(F32)

        put_sample = lambda full, part: lax.dynamic_update_slice(full, part, (mp, 0))
        oa, gdn_p = _gdn(p, 0, bp, tp, min(GDN_CHUNK, tp), conv_w, g, beta,
                         jnp.zeros((bp, CONV_A - 1, 3 * WIDTH_A), F32),
                         jnp.zeros((bp, H_A, DK_A, DV_A), F32), gdn_norm[l], out_rows=mp + ms)
        oa_s, gdn_s = _gdn(p, mp, bs, ts, ts, conv_w, g, beta,
                           state_gdn_conv[l], state_gdn[l], gdn_norm[l])
        oa = put_sample(oa, oa_s)
        ob = _dsa_prompt(pd, k_new, v_new, ki_new, bp, tp, out_rows=mp + ms)
        ob = put_sample(ob, _dsa_sample(pd[mp:], k_new[mp:], v_new[mp:], ki_new[mp:], page_table,
                                        cache_k, cache_v, cache_kidx, l))
        cw = conv_c[l].astype(F32)
        oc, sconv_p = _sconv(p, 0, bp, tp, cw, jnp.zeros((bp, CONV_C - 1, WIDTH_C), F32),
                             out_rows=mp + ms)
        oc_s, sconv_s = _sconv(p, mp, bs, ts, cw, state_sconv[l])
        oc = put_sample(oc, oc_s)
        merged = _merge(oa, ob, oc, w_branch_a, w_branch_b, w_branch_c, l, p)
        x = _out_proj(merged, w_o, l, x)

        hn = _rmsnorm(x, norm2[l], BF16)
        h = _swiglu(hn, w_gate, w_up, l)
        x = _out_proj(h, w_down16, l, x, name="ffn_down")

        nb = CONV_A - 1
        tail_rows = lambda r0, t: lax.slice(p, (r0 + max(t - nb, 0), 0), (r0 + t, 3 * WIDTH_A))
        gconv_p = jnp.stack([tail_rows(b * tp, tp) for b in range(bp)])
        gconv_s = jnp.stack([tail_rows(mp + b * ts, ts) for b in range(bs)])
        if tp < nb:
            gconv_p = jnp.concatenate([jnp.zeros((bp, nb - tp, 3 * WIDTH_A), F32), gconv_p], axis=1)
        if ts < nb:
            gconv_s = jnp.concatenate([state_gdn_conv[l].astype(F32)[:, ts:], gconv_s], axis=1)
        kv = lambda a, rows, b, t: a[rows].reshape(b, t, N_KV_B, HEAD_DIM_B)
        rp, rs = slice(0, mp), slice(mp, mp + ms)
        for lst, val in zip(new_p, (kv(k_new, rp, bp, tp), kv(v_new, rp, bp, tp),
                                    ki_new[rp].reshape(bp, tp, D_I), gdn_p, gconv_p, sconv_p)):
            lst.append(val)
        for lst, val in zip(new_s, (kv(k_new, rs, bs, ts), kv(v_new, rs, bs, ts),
                                    ki_new[rs].reshape(bs, ts, D_I), gdn_s, gconv_s, sconv_s)):
            lst.append(val)

    y_prompt = _rmsnorm(x, final_norm, F32, 0, mp).reshape(bp, tp, d)
    y_sample = _rmsnorm(x, final_norm, F32, mp, ms).reshape(bs, ts, d)
    outs_p = [jnp.stack(a) for a in new_p]
    outs_s = [jnp.stack(a) for a in new_s]
    return (y_prompt, y_sample, *outs_p, *outs_s)
```

```python
import functools
import math

import jax
import jax.numpy as jnp
from jax import lax
from jax.experimental import pallas as pl
from jax.experimental.pallas import tpu as pltpu

F32 = jnp.float32
BF16 = jnp.bfloat16

D_MODEL = 4096
DEPTH = 2
PAGE_SIZE = 128
H_A = 16
DK_A = 128
DV_A = 128
WIDTH_A = H_A * DV_A
CONV_A = 4
GDN_CHUNK = 64
H_B = 8
N_KV_B = 2
GROUP_B = H_B // N_KV_B
HEAD_DIM_B = 128
WIDTH_B = H_B * HEAD_DIM_B
KV_WIDTH_B = N_KV_B * HEAD_DIM_B
H_I = 32
D_I = 64
TOPK_MAX = 256
Q_BLOCK = 128
WIDTH_C = 1024
CONV_C = 3
D_FF = -(-8 * D_MODEL // (3 * 256)) * 256
EPS = 1e-6

IN_NAMES = ("qa", "ka", "va", "za", "aa", "ba", "qb", "kb", "vb", "qi", "ki", "wi",
            "gate_b", "gate_c", "hc", "ga", "gb", "gc")
IN_SIZES = (WIDTH_A, WIDTH_A, WIDTH_A, WIDTH_A, H_A, H_A,
            WIDTH_B, KV_WIDTH_B, KV_WIDTH_B, H_I * D_I, D_I, H_I,
            WIDTH_C, WIDTH_C, WIDTH_C, D_MODEL, D_MODEL, D_MODEL)

LANES = 128
N_BLOCK = 512
VMEM_LIMIT = 56 * 1024 * 1024
ACT_VMEM_BYTES = 32 * 1024 * 1024
NEG_BIG = -1e30

MAIN_ORDER = ("qa", "ka", "va", "za", "gate_b", "gate_c", "hc", "ga", "gb", "gc")
MID_ORDER = ("qb", "kb", "vb", "qi", "ki", "wi", "aa", "ba")


def _offsets(order):
    sizes = dict(zip(IN_NAMES, IN_SIZES))
    off, out = 0, {}
    for name in order:
        out[name] = off
        off += sizes[name]
    return out, off


SRC_OFF, _ = _offsets(IN_NAMES)
PM_OFF, PM_WIDTH = _offsets(MAIN_ORDER)
PD_OFF, _mid_cols = _offsets(MID_ORDER)
PD_WIDTH = -(-_mid_cols // N_BLOCK) * N_BLOCK
MAIN_RUN0 = PM_OFF["gate_b"]
MAIN_SHIFT = SRC_OFF["gate_b"] - MAIN_RUN0
assert PM_WIDTH % N_BLOCK == 0 and MAIN_RUN0 % N_BLOCK == 0 and MAIN_SHIFT % LANES == 0
assert all(SRC_OFF[n] == PM_OFF[n] for n in MAIN_ORDER[:4])
assert all(SRC_OFF[n] == PM_OFF[n] + MAIN_SHIFT for n in MAIN_ORDER[4:])
TAIL_OFF = PD_OFF["ki"]
TAIL_AA = PD_OFF["aa"] - TAIL_OFF
TAIL_BA = PD_OFF["ba"] - TAIL_OFF
TAIL_WI = PD_OFF["wi"] - TAIL_OFF
assert TAIL_OFF % N_BLOCK == 0 and _mid_cols - TAIL_OFF == LANES


def _cparams(*sem):
    return pltpu.CompilerParams(dimension_semantics=sem, vmem_limit_bytes=VMEM_LIMIT)


def _dot(a, b):
    return jnp.dot(a, b, preferred_element_type=F32)


def _dot_nt(a, b):
    return lax.dot_general(a, b, (((1,), (1,)), ((), ())), preferred_element_type=F32)


def _dot_tn(a, b):
    return lax.dot_general(a, b, (((0,), (0,)), ((), ())), preferred_element_type=F32)


def _dot_hi(a, b):
    return jnp.dot(a, b, preferred_element_type=F32, precision=lax.Precision.HIGHEST)


def _sigmoid(x):
    return jax.nn.sigmoid(x)


def _silu(x):
    return x * jax.nn.sigmoid(x)


def _rmsnorm_body(x_ref, g_ref, o_ref):
    x = x_ref[...]
    ms = jnp.mean(x * x, axis=-1, keepdims=True)
    o_ref[...] = (x * lax.rsqrt(ms + EPS) * g_ref[...]).astype(o_ref.dtype)


def _row_block(m):
    for bm in (192, 128, 64, 32, 16, 8):
        if m % bm == 0:
            return bm
    raise ValueError(f"unsupported row count {m}")


def _rmsnorm(x, g, out_dtype, row0=0, nrows=None):
    d = x.shape[1]
    m = x.shape[0] - row0 if nrows is None else nrows
    bm = _row_block(math.gcd(m, row0) if row0 else m)
    rb0 = row0 // bm
    return pl.pallas_call(
        _rmsnorm_body,
        grid=(m // bm,),
        in_specs=[pl.BlockSpec((bm, d), lambda i: (rb0 + i, 0)), pl.BlockSpec((1, d), lambda i: (0, 0))],
        out_specs=pl.BlockSpec((bm, d), lambda i: (i, 0)),
        out_shape=jax.ShapeDtypeStruct((m, d), out_dtype),
        compiler_params=_cparams("parallel"),
        name="rmsnorm",
    )(x, g.reshape(1, d).astype(F32))


def _gdn_gates_body(t_ref, alog_ref, dtb_ref, g_ref, b_ref):
    t = t_ref[...]
    aa = t[:, TAIL_AA:TAIL_AA + H_A]
    ba = t[:, TAIL_BA:TAIL_BA + H_A]
    x = aa + dtb_ref[...]
    softplus = jnp.maximum(x, 0.0) + jnp.log1p(jnp.exp(-jnp.abs(x)))
    g_ref[...] = -jnp.exp(alog_ref[...]) * softplus
    b_ref[...] = _sigmoid(ba)


def _gdn_gates(p, a_log, dt_bias):
    m = p.shape[0]
    bm = _row_block(m)
    tail_blk = TAIL_OFF // LANES
    return pl.pallas_call(
        _gdn_gates_body,
        grid=(m // bm,),
        in_specs=[pl.BlockSpec((bm, LANES), lambda i: (i, tail_blk)),
                  pl.BlockSpec((1, H_A), lambda i: (0, 0)),
                  pl.BlockSpec((1, H_A), lambda i: (0, 0))],
        out_specs=[pl.BlockSpec((bm, H_A), lambda i: (i, 0)), pl.BlockSpec((bm, H_A), lambda i: (i, 0))],
        out_shape=[jax.ShapeDtypeStruct((m, H_A), F32)] * 2,
        compiler_params=_cparams("parallel"),
        name="gdn_gates",
    )(p, a_log.reshape(1, H_A).astype(F32), dt_bias.reshape(1, H_A).astype(F32))


def _big_row_block(m, k=D_MODEL, buffers=2):
    for bm in (2064, 1376, 1024, 688, 512, 256, 128, 64):
        if m % bm == 0 and buffers * bm * k * 2 <= ACT_VMEM_BYTES:
            return bm
    raise ValueError(f"unsupported row count {m}")


def _resident_rows(bm, k, buffers=2):
    if buffers == 1:
        return pl.BlockSpec((bm, k), lambda i, j: (i, 0), pipeline_mode=pl.Buffered(1))
    return pl.BlockSpec((bm, k), lambda i, j: (i, 0))


def _out_proj_body(a_ref, w_ref, r_ref, o_ref):
    o_ref[...] = r_ref[...] + _dot(a_ref[...], w_ref[...].astype(BF16))


def _out_proj(a, w, layer, residual, name="out_proj"):
    m, k = a.shape
    n = w.shape[2]
    bm, bn = _big_row_block(m, k), N_BLOCK // 2
    return pl.pallas_call(
        _out_proj_body,
        grid=(m // bm, n // bn),
        in_specs=[_resident_rows(bm, k),
                  pl.BlockSpec((None, k, bn), lambda i, j: (layer, 0, j)),
                  pl.BlockSpec((bm, bn), lambda i, j: (i, j))],
        out_specs=pl.BlockSpec((bm, bn), lambda i, j: (i, j)),
        out_shape=jax.ShapeDtypeStruct((m, n), F32),
        compiler_params=_cparams("parallel", "arbitrary"),
        name=name,
    )(a, w, residual)


IN_PROJ_PIECES = N_BLOCK // LANES


def _in_proj_main_body(a_ref, *rest):
    w_refs, p_ref = rest[:IN_PROJ_PIECES], rest[IN_PROJ_PIECES]
    w = jnp.concatenate([r[...].astype(BF16) for r in w_refs], axis=1)
    p_ref[...] = _dot(a_ref[...], w)


def _in_proj_main(a, w_in, layer):
    m, k = a.shape
    bm, bn = _big_row_block(m, k, buffers=1), N_BLOCK
    run0_blk, shift_blk = MAIN_RUN0 // bn, MAIN_SHIFT // LANES

    def piece(q):
        def index(i, j):
            return (layer, 0, IN_PROJ_PIECES * j + q + jnp.where(j >= run0_blk, shift_blk, 0))
        return pl.BlockSpec((None, k, LANES), index)

    return pl.pallas_call(
        _in_proj_main_body,
        grid=(m // bm, PM_WIDTH // bn),
        in_specs=[_resident_rows(bm, k, buffers=1)] + [piece(q) for q in range(IN_PROJ_PIECES)],
        out_specs=pl.BlockSpec((bm, bn), lambda i, j: (i, j)),
        out_shape=jax.ShapeDtypeStruct((m, PM_WIDTH), F32),
        compiler_params=_cparams("parallel", "arbitrary"),
        name="in_proj_main",
    )(a, *([w_in] * IN_PROJ_PIECES))


NORM_ROWS = 16


def _in_proj_mid_body(x_ref, g_ref, w_ref, p_ref, k_ref, v_ref, ki_ref, xn_ref, *, jkv, jtail):
    j = pl.program_id(1)

    @pl.when(j == 0)
    def _():
        def norm_rows(r, carry):
            rows = pl.ds(pl.multiple_of(r * NORM_ROWS, NORM_ROWS), NORM_ROWS)
            x = x_ref[rows, :]
            ms = jnp.mean(x * x, axis=-1, keepdims=True)
            xn_ref[rows, :] = (x * lax.rsqrt(ms + EPS) * g_ref[...]).astype(xn_ref.dtype)
            return carry
        lax.fori_loop(0, x_ref.shape[0] // NORM_ROWS, norm_rows, 0)

    acc = _dot(xn_ref[...], w_ref[...])
    p_ref[...] = acc

    @pl.when(j == jkv)
    def _():
        k_ref[...] = acc[:, :KV_WIDTH_B]
        v_ref[...] = acc[:, KV_WIDTH_B:2 * KV_WIDTH_B]

    @pl.when(j == jtail)
    def _():
        ki_ref[...] = acc[:, :D_I]


def _in_proj_mid(x, g, w_mid, layer):
    m, k = x.shape
    bn = N_BLOCK
    vmem = lambda b: 2 * (b * k * (4 + 2) + k * bn * 2 + b * (bn + 2 * KV_WIDTH_B + D_I) * 4)
    bm = next(b for b in (1376, 1024, 688, 512, 256, 128, 64) if m % b == 0 and vmem(b) <= VMEM_LIMIT)
    assert PD_OFF["kb"] % bn == 0 and PD_OFF["vb"] == PD_OFF["kb"] + KV_WIDTH_B
    row = lambda i, j: (i, 0)
    return pl.pallas_call(
        functools.partial(_in_proj_mid_body, jkv=PD_OFF["kb"] // bn, jtail=TAIL_OFF // bn),
        grid=(m // bm, PD_WIDTH // bn),
        in_specs=[pl.BlockSpec((bm, k), row), pl.BlockSpec((1, k), lambda i, j: (0, 0)),
                  pl.BlockSpec((None, k, bn), lambda i, j: (layer, 0, j))],
        out_specs=[pl.BlockSpec((bm, bn), lambda i, j: (i, j)),
                   pl.BlockSpec((bm, KV_WIDTH_B), row), pl.BlockSpec((bm, KV_WIDTH_B), row),
                   pl.BlockSpec((bm, D_I), row), pl.BlockSpec((bm, k), row)],
        out_shape=[jax.ShapeDtypeStruct((m, PD_WIDTH), F32), jax.ShapeDtypeStruct((m, KV_WIDTH_B), F32),
                   jax.ShapeDtypeStruct((m, KV_WIDTH_B), F32), jax.ShapeDtypeStruct((m, D_I), F32),
                   jax.ShapeDtypeStruct((m, k), BF16)],
        compiler_params=_cparams("parallel", "arbitrary"),
        name="in_proj_mid",
    )(x, g.reshape(1, k).astype(F32), w_mid)


def _swiglu_body(a_ref, wg_ref, wu_ref, o_ref):
    a = a_ref[...]
    g = _dot(a, wg_ref[...].astype(BF16))
    u = _dot(a, wu_ref[...].astype(BF16))
    o_ref[...] = (_silu(g) * u).astype(o_ref.dtype)


def _swiglu(a, w_gate, w_up, layer):
    m, k = a.shape
    n = w_gate.shape[2]
    bm, bn = _big_row_block(m, k, buffers=1), N_BLOCK // 2
    assert n % bn == 0
    wspec = pl.BlockSpec((None, k, bn), lambda i, j: (layer, 0, j))
    return pl.pallas_call(
        _swiglu_body,
        grid=(m // bm, n // bn),
        in_specs=[_resident_rows(bm, k, buffers=1), wspec, wspec],
        out_specs=pl.BlockSpec((bm, bn), lambda i, j: (i, j)),
        out_shape=jax.ShapeDtypeStruct((m, n), BF16),
        compiler_params=_cparams("parallel", "arbitrary"),
        name="swiglu",
    )(a, w_gate, w_up)


def _merge_body(oa_ref, ob_ref, oc_ref, wa_ref, wb_ref, wc_ref, ga_ref, gb_ref, gc_ref, o_ref):
    acc = _sigmoid(ga_ref[...]) * _dot(oa_ref[...], wa_ref[...].astype(BF16))
    acc += _sigmoid(gb_ref[...]) * _dot(ob_ref[...], wb_ref[...].astype(BF16))
    acc += _sigmoid(gc_ref[...]) * _dot(oc_ref[...], wc_ref[...].astype(BF16))
    o_ref[...] = acc.astype(o_ref.dtype)


def _merge(oa, ob, oc, wa, wb, wc, layer, p):
    m = oa.shape[0]
    n = wa.shape[2]
    bm, bn = _big_row_block(m), N_BLOCK // 2
    ga0, gb0, gc0 = (PM_OFF[s] // bn for s in ("ga", "gb", "gc"))
    row = lambda i, j: (i, 0)
    wspec = lambda w: pl.BlockSpec((None, w.shape[1], bn), lambda i, j: (layer, 0, j))
    return pl.pallas_call(
        _merge_body,
        grid=(m // bm, n // bn),
        in_specs=[_resident_rows(bm, oa.shape[1]), _resident_rows(bm, ob.shape[1]),
                  _resident_rows(bm, oc.shape[1]),
                  wspec(wa), wspec(wb), wspec(wc),
                  pl.BlockSpec((bm, bn), lambda i, j: (i, ga0 + j)),
                  pl.BlockSpec((bm, bn), lambda i, j: (i, gb0 + j)),
                  pl.BlockSpec((bm, bn), lambda i, j: (i, gc0 + j))],
        out_specs=pl.BlockSpec((bm, bn), lambda i, j: (i, j)),
        out_shape=jax.ShapeDtypeStruct((m, n), BF16),
        compiler_params=_cparams("parallel", "arbitrary"),
        name="merge",
    )(oa, ob, oc, wa, wb, wc, p, p, p)


GDN_HEADS_PER_STEP = 16
GDN_INV_BASE = 8


def _gdn_body(q_ref, k_ref, v_ref, z_ref, wq_ref, wk_ref, wv_ref, tq_ref, tk_ref, tv_ref,
              g_ref, b_ref, s0_ref, ng_ref, o_ref, sout_ref, s_scr, tail_scr,
              *, chunk, hg):
    c = chunk
    w = hg * DK_A
    n = pl.program_id(2)

    @pl.when(n == 0)
    def _():
        s_scr[...] = s0_ref[...].astype(F32)
        tail_scr[:, 0:w] = tq_ref[...]
        tail_scr[:, w:2 * w] = tk_ref[...]
        tail_scr[:, 2 * w:3 * w] = tv_ref[...]

    act = []
    for idx, (x_ref, w_ref) in enumerate(((q_ref, wq_ref), (k_ref, wk_ref), (v_ref, wv_ref))):
        x = x_ref[...]
        cw = w_ref[...]
        xfull = jnp.concatenate([tail_scr[:, idx * w:(idx + 1) * w], x], axis=0)
        y = x * cw[CONV_A - 1:CONV_A, :]
        for j in range(1, CONV_A):
            y = y + pltpu.roll(xfull, j, axis=0)[8:, :] * cw[CONV_A - 1 - j:CONV_A - j, :]
        tail_scr[:, idx * w:(idx + 1) * w] = xfull[c:c + 8, :]
        act.append(_silu(y))
    qs, ks, vs = act
    z = z_ref[...]

    ii = lax.broadcasted_iota(jnp.int32, (c, c), 0)
    jj = lax.broadcasted_iota(jnp.int32, (c, c), 1)
    incl = jj <= ii
    strict = jj < ii
    eye = (ii == jj).astype(F32)
    ng = ng_ref[...]
    b_blk = b_ref[...]
    gcum_all = _dot_hi(incl.astype(F32), g_ref[...])
    gcum_t = gcum_all.T

    heads = range(hg)
    hsl = [slice(hh * DK_A, (hh + 1) * DK_A) for hh in heads]
    q_l, k_l, k16_l, decay_l, egc_l, kdec_l, glast_l, mm_l, rhs_l = ([] for _ in range(9))
    for hh in heads:
        q = qs[:, hsl[hh]]
        k = ks[:, hsl[hh]]
        q = q * lax.rsqrt(jnp.sum(q * q, axis=-1, keepdims=True) + EPS) * (DK_A ** -0.5)
        k = k * lax.rsqrt(jnp.sum(k * k, axis=-1, keepdims=True) + EPS)
        bc = b_blk[:, hh:hh + 1]
        gcum_c = gcum_all[:, hh:hh + 1]
        gcum_r = gcum_t[hh:hh + 1, :]
        decay = jnp.exp(jnp.where(incl, gcum_c - gcum_r, -jnp.inf))
        kb = k * bc
        k16 = k.astype(BF16)
        egc = jnp.exp(gcum_c)
        glast = gcum_c[c - 1:c, :]
        mm = jnp.where(strict, _dot_nt(kb.astype(BF16), k16) * decay, 0.0)
        q_l.append(q)
        k_l.append(k)
        k16_l.append(k16)
        decay_l.append(decay)
        egc_l.append(egc)
        glast_l.append(glast)
        kdec_l.append(jnp.exp(glast - gcum_c))
        rhs_l.append(jnp.concatenate([vs[:, hsl[hh]] * bc, kb * egc], axis=1).astype(BF16))
        mm_l.append(mm)
    base = min(c, GDN_INV_BASE)
    log_base = base.bit_length() - 1
    in_base = jnp.right_shift(ii, log_base) == jnp.right_shift(jj, log_base)
    pw_l = [jnp.where(in_base, -mm, 0.0) for mm in mm_l]
    tinv_l = [eye + pw for pw in pw_l]
    for _ in range(log_base - 1):
        for hh in heads:
            pw16 = pw_l[hh].astype(BF16)
            pw_l[hh] = _dot(pw16, pw16)
        for hh in heads:
            tinv_l[hh] = tinv_l[hh] + _dot(tinv_l[hh].astype(BF16), pw_l[hh].astype(BF16))
    log_size = log_base
    while (1 << log_size) < c:
        lower_left = jnp.logical_and(
            jnp.right_shift(ii, log_size + 1) == jnp.right_shift(jj, log_size + 1),
            jnp.logical_and(jnp.bitwise_and(jnp.right_shift(ii, log_size), 1) == 1,
                            jnp.bitwise_and(jnp.right_shift(jj, log_size), 1) == 0))
        for hh in heads:
            t16 = tinv_l[hh].astype(BF16)
            c16 = jnp.where(lower_left, mm_l[hh], 0.0).astype(BF16)
            tinv_l[hh] = tinv_l[hh] - _dot(_dot(t16, c16).astype(BF16), t16)
        log_size += 1
    sol_l =[_dot(tinv_l[hh].astype(BF16), rhs_l[hh]) for hh in heads]
    attn_l = [(_dot_nt(q_l[hh].astype(BF16), k16_l[hh]) * decay_l[hh]).astype(BF16) for hh in heads]
    s_l = [s_scr[hh] for hh in heads]
    s16_l = [s_l[hh].astype(BF16) for hh in heads]
    v16_l = [(sol_l[hh][:, :DV_A] - _dot(sol_l[hh][:, DV_A:].astype(BF16), s16_l[hh])).astype(BF16)
             for hh in heads]
    o_l = [_dot((q_l[hh] * egc_l[hh]).astype(BF16), s16_l[hh]) + _dot(attn_l[hh], v16_l[hh]) for hh in heads]
    snew_l = [s_l[hh] * jnp.exp(glast_l[hh]) + _dot_tn((k_l[hh] * kdec_l[hh]).astype(BF16), v16_l[hh])
              for hh in heads]
    s_scr[...] = jnp.stack(snew_l, axis=0)
    o_l = [o * lax.rsqrt(jnp.mean(o * o, axis=-1, keepdims=True) + EPS) * ng for o in o_l]
    o_ref[...] = (jnp.concatenate(o_l, axis=1) * _silu(z)).astype(o_ref.dtype)

    @pl.when(n == pl.num_programs(2) - 1)
    def _():
        sout_ref[...] = s_scr[...].astype(sout_ref.dtype)


def _gdn(p, row0, nseq, t, chunk, conv_w, g, beta, conv_buf, s0, norm_g, out_rows=None):
    hg = GDN_HEADS_PER_STEP
    assert hg == H_A
    w = hg * DK_A
    nch = t // chunk
    nhg = H_A // hg
    rb0 = row0 // chunk
    gspec = pl.BlockSpec((chunk, H_A), lambda b, h, n: (rb0 + b * nch + n, 0))
    tail = jnp.pad(conv_buf.astype(F32), ((0, 0), (8 - (CONV_A - 1), 0), (0, 0)))
    qb0, kb0, vb0, zb0 = (PM_OFF[s] // w for s in ("qa", "ka", "va", "za"))
    pspec = lambda b0: pl.BlockSpec((chunk, w), lambda b, h, n: (rb0 + b * nch + n, b0 + h))
    wspec = lambda b0: pl.BlockSpec((CONV_A, w), lambda b, h, n: (0, b0 + h))
    tspec = lambda b0: pl.BlockSpec((None, 8, w), lambda b, h, n: (b, 0, b0 + h))
    o, s_out = pl.pallas_call(
        functools.partial(_gdn_body, chunk=chunk, hg=hg),
        grid=(nseq, nhg, nch),
        in_specs=[pspec(qb0), pspec(kb0), pspec(vb0), pspec(zb0),
                  wspec(0), wspec(nhg), wspec(2 * nhg),
                  tspec(0), tspec(nhg), tspec(2 * nhg),
                  gspec, gspec,
                  pl.BlockSpec((None, hg, DK_A, DV_A), lambda b, h, n: (b, h, 0, 0)),
                  pl.BlockSpec((1, DV_A), lambda b, h, n: (0, 0))],
        out_specs=[pl.BlockSpec((chunk, w), lambda b, h, n: (b * nch + n, h)),
                   pl.BlockSpec((None, hg, DK_A, DV_A), lambda b, h, n: (b, h, 0, 0))],
        out_shape=[jax.ShapeDtypeStruct((out_rows or nseq * t, WIDTH_A), BF16),
                   jax.ShapeDtypeStruct((nseq, H_A, DK_A, DV_A), s0.dtype)],
        scratch_shapes=[pltpu.VMEM((hg, DK_A, DV_A), F32), pltpu.VMEM((8, 3 * w), F32)],
        compiler_params=_cparams("parallel", "parallel", "arbitrary"),
        name=f"gdn_c{chunk}",
    )(p, p, p, p, conv_w, conv_w, conv_w, tail, tail, tail, g, beta, s0,
      norm_g.reshape(1, DV_A).astype(F32))
    return o, s_out


SCONV_COLS = 256


def _sconv_body(gb_ref, gc_ref, hc_ref, w_ref, buf_ref, o_ref, st_ref, *, t):
    pr = gc_ref[...] * hc_ref[...]
    cw = w_ref[...]
    xfull = jnp.concatenate([buf_ref[...], pr], axis=0)
    y = pr * cw[CONV_C - 1:CONV_C, :]
    for j in range(1, CONV_C):
        y = y + pltpu.roll(xfull, j, axis=0)[8:, :] * cw[CONV_C - 1 - j:CONV_C - j, :]
    o_ref[...] = (gb_ref[...] * y).astype(o_ref.dtype)
    st_ref[...] = xfull[t:t + 8, :]


def _sconv(p, row0, nseq, t, conv_w, buf, out_rows=None):
    cols = SCONV_COLS
    ncb = WIDTH_C // cols
    rb0 = row0 // t
    b0, c0, h0 = (PM_OFF[s] // cols for s in ("gate_b", "gate_c", "hc"))
    buf8 = jnp.pad(buf.astype(F32), ((0, 0), (8 - (CONV_C - 1), 0), (0, 0)))
    pspec = lambda o: pl.BlockSpec((t, cols), lambda b, j: (rb0 + b, o + j))
    o, st = pl.pallas_call(
        functools.partial(_sconv_body, t=t),
        grid=(nseq, ncb),
        in_specs=[pspec(b0), pspec(c0), pspec(h0),
                  pl.BlockSpec((CONV_C, cols), lambda b, j: (0, j)),
                  pl.BlockSpec((None, 8, cols), lambda b, j: (b, 0, j))],
        out_specs=[pl.BlockSpec((t, cols), lambda b, j: (b, j)),
                   pl.BlockSpec((None, 8, cols), lambda b, j: (b, 0, j))],
        out_shape=[jax.ShapeDtypeStruct((out_rows or nseq * t, WIDTH_C), BF16),
                   jax.ShapeDtypeStruct((nseq, 8, WIDTH_C), F32)],
        compiler_params=_cparams("parallel", "parallel"),
        name=f"sconv_t{t}",
    )(p, p, p, conv_w, buf8)
    return o, st[:, 8 - (CONV_C - 1):, :]


BISECT_EVERY = 8
MAX_SEARCH = 2400


def _select_threshold(count_ge, lo0, hi0, n_valid, k):
    kf = jnp.float32(k)

    def cond(st):
        return jnp.logical_and(st[0] < MAX_SEARCH, jnp.min(st[-1]) < 0.5)

    def step(st):
        it, lo, hi, glo, ghi, t, side, done = st
        half = 0.5 * lo + 0.5 * hi
        cand = lo + (hi - lo) * (glo / jnp.maximum(glo - ghi, 1e-9))
        mid = jnp.where(it % BISECT_EVERY == BISECT_EVERY - 1, half, cand)
        mid = jnp.where(jnp.logical_and(mid > lo, mid < hi), mid, half)
        adjacent = jnp.logical_not(jnp.logical_and(mid > lo, mid < hi))
        first = it == 0
        mid = jnp.where(first, hi, mid)
        adjacent = jnp.logical_and(adjacent, jnp.logical_not(first))
        g = count_ge(mid) - kf
        ge = g >= 0.0
        finish = jnp.logical_or(jnp.logical_or(g == 0.0, adjacent), jnp.logical_and(first, ge))
        newly = jnp.logical_and(finish, done < 0.5)
        t = jnp.where(newly, jnp.where(adjacent, lo, mid), t)
        new_side = jnp.where(ge, 1.0, -1.0)
        damp = jnp.where(jnp.logical_and(new_side == side, jnp.logical_not(first)), 0.5, 1.0)
        glo = jnp.where(ge, g, glo * damp)
        ghi = jnp.where(ge, ghi * damp, g)
        lo = jnp.where(ge, mid, lo)
        hi = jnp.where(ge, hi, mid)
        done = jnp.where(finish, 1.0, done)
        return it + 1, lo, hi, glo, ghi, t, new_side, done

    done0 = jnp.where(n_valid > kf, 0.0, 1.0)
    st = lax.while_loop(cond, lambda st: step(step(st)), (jnp.int32(0), lo0, hi0, n_valid - kf, jnp.full_like(lo0, -kf),
                                     lo0, jnp.zeros_like(lo0), done0))
    return st[5]


KEY_CHUNK = 512


QI_PIECE = 512


def _dsa_prompt_body(*refs, tq, sc, topk):
    nqi = H_I * D_I // QI_PIECE
    qi_refs = refs[:nqi]
    tq_ref, qb_ref, kb_ref, vb_ref, tk_ref, o_ref, sc_scr, m_scr, l_scr, acc_scr = refs[nqi:]
    i = pl.program_id(1)
    nck = (i * tq + tq - 1) // sc + 1
    wt = tq_ref[...].T[TAIL_WI:TAIL_WI + H_I, :] * ((D_I ** -0.5) * (H_I ** -0.5))

    def head_q(h):
        piece, off = divmod(h * D_I, QI_PIECE)
        return qi_refs[piece][:, off:off + D_I]

    pairs = [jnp.concatenate([head_q(2 * p), head_q(2 * p + 1)], axis=0).astype(BF16)
             for p in range(H_I // 2)]
    tpos = i * tq + lax.broadcasted_iota(jnp.int32, (1, tq), 1)

    def score_chunk(c, carry):
        r0 = pl.multiple_of(c * sc, sc)
        kic = tk_ref[pl.ds(r0, sc), :].astype(BF16)
        acc = jnp.zeros((sc, tq), F32)
        for p in range(H_I // 2):
            d = _dot_nt(kic, pairs[p])
            acc = acc + jnp.maximum(d[:, :tq], 0.0) * wt[2 * p:2 * p + 1, :]
            acc = acc + jnp.maximum(d[:, tq:], 0.0) * wt[2 * p + 1:2 * p + 2, :]
        kpos = r0 + lax.broadcasted_iota(jnp.int32, (sc, tq), 0)
        sc_scr[pl.ds(r0, sc), :] = jnp.where(kpos <= tpos, acc, -jnp.inf)
        return carry

    lax.fori_loop(0, nck, score_chunk, 0)

    def minmax_chunk(c, carry):
        mn, mx = carry
        x = sc_scr[pl.ds(pl.multiple_of(c * sc, sc), sc), :]
        mx = jnp.maximum(mx, jnp.max(x, axis=0, keepdims=True))
        mn = jnp.minimum(mn, jnp.min(jnp.where(x == -jnp.inf, jnp.inf, x), axis=0, keepdims=True))
        return mn, mx

    mn, mx = lax.fori_loop(0, nck, minmax_chunk,
                           (jnp.full((1, tq), jnp.inf, F32), jnp.full((1, tq), -jnp.inf, F32)))

    def count_ge(mid):
        def body(c, acc):
            x = sc_scr[pl.ds(pl.multiple_of(c * sc, sc), sc), :]
            hit = jnp.where(x >= mid, 1.0, 0.0).reshape(8, sc // 64, 8, tq)
            return acc + jnp.sum(jnp.sum(hit, axis=1), axis=0)
        return jnp.sum(lax.fori_loop(0, nck, body, jnp.zeros((8, tq), F32)), axis=0, keepdims=True)

    thr = _select_threshold(count_ge, mn, mx, (tpos + 1).astype(F32), topk)

    def count_gt(c, acc):
        x = sc_scr[pl.ds(pl.multiple_of(c * sc, sc), sc), :]
        hit = jnp.where(x > thr, 1.0, 0.0).reshape(8, sc // 64, 8, tq)
        return acc + jnp.sum(jnp.sum(hit, axis=1), axis=0)
    n_above = jnp.sum(lax.fori_loop(0, nck, count_gt, jnp.zeros((8, tq), F32)), axis=0, keepdims=True)
    crowded = count_ge(thr) > topk
    places = topk - n_above

    @pl.when(jnp.max(jnp.where(crowded, 1.0, 0.0)) > 0.5)
    def _():
        earlier = (lax.broadcasted_iota(jnp.int32, (sc, sc), 1)
                   < lax.broadcasted_iota(jnp.int32, (sc, sc), 0)).astype(BF16)

        def retire(c, seen):
            rows = pl.ds(pl.multiple_of(c * sc, sc), sc)
            x = sc_scr[rows, :]
            tie = jnp.logical_and(x == thr, crowded)
            tie_f = jnp.where(tie, 1.0, 0.0)
            rank = seen + _dot(earlier, tie_f.astype(BF16))
            sc_scr[rows, :] = jnp.where(jnp.logical_and(tie, rank >= places), -jnp.inf, x)
            return seen + jnp.sum(tie_f, axis=0, keepdims=True)

        lax.fori_loop(0, nck, retire, jnp.zeros((1, tq), F32))

    m_scr[...] = jnp.full(m_scr.shape, NEG_BIG, F32)
    l_scr[...] = jnp.zeros(l_scr.shape, F32)
    acc_scr[...] = jnp.zeros(acc_scr.shape, F32)
    q = qb_ref[...].astype(BF16)
    scale = HEAD_DIM_B ** -0.5

    def attn_chunk(c, carry):
        r0 = pl.multiple_of(c * sc, sc)
        sel = sc_scr[pl.ds(r0, sc), :] >= thr
        for n in range(N_KV_B):
            kc = kb_ref[pl.ds(r0, sc), n * HEAD_DIM_B:(n + 1) * HEAD_DIM_B].astype(BF16)
            vc = vb_ref[pl.ds(r0, sc), n * HEAD_DIM_B:(n + 1) * HEAD_DIM_B].astype(BF16)
            for g in range(GROUP_B):
                h = n * GROUP_B + g
                hs = slice(h * HEAD_DIM_B, (h + 1) * HEAD_DIM_B)
                s = jnp.where(sel, _dot_nt(kc, q[:, hs]) * scale, NEG_BIG)
                m_old = m_scr[h:h + 1, :]
                m_new = jnp.maximum(m_old, jnp.max(s, axis=0, keepdims=True))
                pexp = jnp.exp(s - m_new)
                alpha = jnp.exp(m_old - m_new)
                l_scr[h:h + 1, :] = alpha * l_scr[h:h + 1, :] + jnp.sum(pexp, axis=0, keepdims=True)
                acc_scr[hs, :] = alpha * acc_scr[hs, :] + _dot_tn(vc, pexp.astype(BF16))
                m_scr[h:h + 1, :] = m_new
        return carry

    lax.fori_loop(0, nck, attn_chunk, 0)

    for h in range(H_B):
        hs = slice(h * HEAD_DIM_B, (h + 1) * HEAD_DIM_B)
        ot = acc_scr[hs, :] / l_scr[h:h + 1, :]
        o_ref[:, hs] = ot.T.astype(o_ref.dtype)


def _dsa_prompt(p, k_new, v_new, ki_new, nseq, s, out_rows=None):
    tq, sc = Q_BLOCK, min(KEY_CHUNK, s)
    topk = min(TOPK_MAX, s // 4)
    nqb = s // tq
    assert PD_OFF["qi"] % QI_PIECE == 0 and PD_OFF["qb"] % WIDTH_B == 0
    nqi = H_I * D_I // QI_PIECE
    qi0 = PD_OFF["qi"] // QI_PIECE
    qb0 = PD_OFF["qb"] // WIDTH_B
    tl0 = TAIL_OFF // LANES
    qi_spec = lambda q: pl.BlockSpec((tq, QI_PIECE), lambda b, i: (b * nqb + i, qi0 + q))
    return pl.pallas_call(
        functools.partial(_dsa_prompt_body, tq=tq, sc=sc, topk=topk),
        grid=(nseq, nqb),
        in_specs=[*(qi_spec(q) for q in range(nqi)),
                  pl.BlockSpec((tq, LANES), lambda b, i: (b * nqb + i, tl0)),
                  pl.BlockSpec((tq, WIDTH_B), lambda b, i: (b * nqb + i, qb0)),
                  pl.BlockSpec((s, KV_WIDTH_B), lambda b, i: (b, 0)),
                  pl.BlockSpec((s, KV_WIDTH_B), lambda b, i: (b, 0)),
                  pl.BlockSpec((s, D_I), lambda b, i: (b, 0))],
        out_specs=pl.BlockSpec((tq, WIDTH_B), lambda b, i: (b * nqb + i, 0)),
        out_shape=jax.ShapeDtypeStruct((out_rows or nseq * s, WIDTH_B), BF16),
        scratch_shapes=[pltpu.VMEM((s, tq), F32), pltpu.VMEM((H_B, tq), F32),
                        pltpu.VMEM((H_B, tq), F32), pltpu.VMEM((WIDTH_B, tq), F32)],
        compiler_params=_cparams("parallel", "arbitrary"),
        name="dsa_prompt",
    )(*([p] * nqi), p, p, k_new, v_new, ki_new)


PAGES_PER_STEP = 16
TIE_CHUNK = 512


def _dsa_sample_scores_body(pt_ref, q_ref, w_ref, *rest, nsteps, pps, t):
    page_refs, new_ref, o_ref, onew_ref = rest[:pps], rest[pps], rest[pps + 1], rest[pps + 2]
    j = pl.program_id(1)
    q = q_ref[...].astype(BF16)
    wv = w_ref[...] * ((D_I ** -0.5) * (H_I ** -0.5))

    def scores(keys_t):
        d = _dot(q, keys_t.astype(BF16))
        return jnp.sum((jnp.maximum(d, 0.0) * wv).reshape(t, H_I, PAGE_SIZE), axis=1)

    @pl.when(j < nsteps)
    def _():
        o_ref[...] = jnp.concatenate([scores(r[...]) for r in page_refs], axis=1)

    @pl.when(j == nsteps)
    def _():
        knew = lax.broadcasted_iota(jnp.int32, (t, PAGE_SIZE), 1)
        tnew = lax.broadcasted_iota(jnp.int32, (t, PAGE_SIZE), 0)
        onew_ref[...] = jnp.where(knew <= tnew, scores(new_ref[...]), -jnp.inf)


def _page_specs(block, layer, nsteps, pps):
    def spec(p):
        def index(b, j, pt):
            return (layer, pt[b, jnp.minimum(j, nsteps - 1) * pps + p]) + (0,) * (len(block) - 2)
        return pl.BlockSpec(block, index)
    return [spec(p) for p in range(pps)]


def _dsa_sample_scores(page_table, q, wv, cache_kidx, layer, ki_new):
    nseq, npages = page_table.shape
    t = q.shape[1] // H_I
    pps = math.gcd(PAGES_PER_STEP, npages)
    nsteps = npages // pps
    return pl.pallas_call(
        functools.partial(_dsa_sample_scores_body, nsteps=nsteps, pps=pps, t=t),
        grid_spec=pltpu.PrefetchScalarGridSpec(
            num_scalar_prefetch=1,
            grid=(nseq, nsteps + 1),
            in_specs=[pl.BlockSpec((None, t * H_I, D_I), lambda b, j, pt: (b, 0, 0)),
                      pl.BlockSpec((None, t * H_I, PAGE_SIZE), lambda b, j, pt: (b, 0, 0)),
                      *_page_specs((None, None, D_I, PAGE_SIZE), layer, nsteps, pps),
                      pl.BlockSpec((None, D_I, PAGE_SIZE), lambda b, j, pt: (b, 0, 0))],
            out_specs=[pl.BlockSpec((None, t, pps * PAGE_SIZE),
                                    lambda b, j, pt: (b, 0, jnp.minimum(j, nsteps - 1))),
                       pl.BlockSpec((None, t, PAGE_SIZE), lambda b, j, pt: (b, 0, 0))],
        ),
        out_shape=[jax.ShapeDtypeStruct((nseq, t, npages * PAGE_SIZE), F32),
                   jax.ShapeDtypeStruct((nseq, t, PAGE_SIZE), F32)],
        compiler_params=_cparams("parallel", "arbitrary"),
        name="dsa_sample_scores",
    )(page_table, q, wv, *([cache_kidx] * pps), ki_new)


def _dsa_sample_attn_body(pt_ref, sc_ref, scn_ref, q_ref, *rest, nsteps, pps, t, past, topk, tie_chunk):
    kpage_refs, vpage_refs = rest[:pps], rest[pps:2 * pps]
    knew_ref, vnew_ref, o_ref, scp_scr, scn_scr, thr_scr, m_scr, l_scr, acc_scr = rest[2 * pps:]
    j = pl.program_id(1)

    @pl.when(j == 0)
    def _():
        xp, xn = sc_ref[...], scn_ref[...]
        scp_scr[...] = xp
        scn_scr[...] = xn
        mx = jnp.maximum(jnp.max(xp, axis=1, keepdims=True), jnp.max(xn, axis=1, keepdims=True))
        mn = jnp.minimum(jnp.min(xp, axis=1, keepdims=True),
                         jnp.min(jnp.where(xn == -jnp.inf, jnp.inf, xn), axis=1, keepdims=True))
        cnt = lambda hit: jnp.sum(jnp.where(hit, 1.0, 0.0), axis=1, keepdims=True)

        def count_ge(mid):
            return cnt(sc_ref[...] >= mid) + cnt(scn_ref[...] >= mid)

        n_valid = (past + 1 + lax.broadcasted_iota(jnp.int32, (t, 1), 0)).astype(F32)
        thr = _select_threshold(count_ge, mn, mx, n_valid, topk)

        crowded = count_ge(thr) > topk
        places = topk - (cnt(xp > thr) + cnt(xn > thr))

        @pl.when(jnp.max(jnp.where(crowded, 1.0, 0.0)) > 0.5)
        def _():
            cw = tie_chunk
            earlier = (lax.broadcasted_iota(jnp.int32, (cw, cw), 0)
                       < lax.broadcasted_iota(jnp.int32, (cw, cw), 1)).astype(BF16)

            def retire(x, seen, width):
                tie = jnp.logical_and(x == thr, crowded)
                tie_f = jnp.where(tie, 1.0, 0.0)
                rank = seen + _dot(tie_f.astype(BF16), earlier[:width, :width])
                return (jnp.where(jnp.logical_and(tie, rank >= places), -jnp.inf, x),
                        seen + jnp.sum(tie_f, axis=1, keepdims=True))

            def past_chunk(c, seen):
                cols = pl.ds(pl.multiple_of(c * cw, cw), cw)
                x, seen = retire(scp_scr[:, cols], seen, cw)
                scp_scr[:, cols] = x
                return seen

            seen = lax.fori_loop(0, past // cw, past_chunk, jnp.zeros((t, 1), F32))
            scn_scr[...] = retire(scn_scr[...], seen, PAGE_SIZE)[0]

        thr_scr[...] = jnp.broadcast_to(thr, thr_scr.shape)
        m_scr[...] = jnp.full(m_scr.shape, NEG_BIG, F32)
        l_scr[...] = jnp.zeros(l_scr.shape, F32)
        acc_scr[...] = jnp.zeros(acc_scr.shape, F32)

    scale = HEAD_DIM_B ** -0.5

    def attend(x, keys, vals):
        sel_t = jnp.where(x >= thr_scr[:, 0:1], 1.0, 0.0)
        sel = jnp.concatenate([sel_t] * GROUP_B, axis=0) > 0.5
        for n in range(N_KV_B):
            s = jnp.where(sel, _dot_nt(q_ref[n].astype(BF16), keys(n).astype(BF16)) * scale, NEG_BIG)
            m_old = m_scr[n]
            m_new = jnp.maximum(m_old, jnp.max(s, axis=1, keepdims=True))
            pexp = jnp.exp(s - m_new)
            alpha = jnp.exp(m_old - m_new)
            l_scr[n] = alpha * l_scr[n] + jnp.sum(pexp, axis=1, keepdims=True)
            acc_scr[n] = alpha * acc_scr[n] + _dot(pexp.astype(BF16), vals(n).astype(BF16))
            m_scr[n] = m_new

    def page_head(r, n):
        return r[pl.ds(n, PAGE_SIZE, stride=N_KV_B), :]

    @pl.when(j < nsteps)
    def _():
        width = pps * PAGE_SIZE
        x = scp_scr[:, pl.ds(pl.multiple_of(j * width, width), width)]
        attend(x,
               lambda n: jnp.concatenate([page_head(r, n) for r in kpage_refs], axis=0),
               lambda n: jnp.concatenate([page_head(r, n) for r in vpage_refs], axis=0))

    @pl.when(j == nsteps)
    def _():
        hs = lambda n: slice(n * HEAD_DIM_B, (n + 1) * HEAD_DIM_B)
        attend(scn_scr[...], lambda n: knew_ref[:, hs(n)], lambda n: vnew_ref[:, hs(n)])
        for n in range(N_KV_B):
            o_ref[n] = (acc_scr[n] / l_scr[n]).astype(o_ref.dtype)


def _dsa_sample_attn(page_table, scores, scores_new, q, cache_k, cache_v, layer, k_new, v_new):
    nseq, npages = page_table.shape
    t = scores.shape[1]
    past = npages * PAGE_SIZE
    topk = min(TOPK_MAX, (past + t) // 4)
    rows = GROUP_B * t
    pps = math.gcd(PAGES_PER_STEP, npages)
    nsteps = npages // pps
    page_block = (None, None, PAGE_SIZE * N_KV_B, HEAD_DIM_B)
    const3 = lambda b, j, pt: (b, 0, 0)
    return pl.pallas_call(
        functools.partial(_dsa_sample_attn_body, nsteps=nsteps, pps=pps, t=t, past=past, topk=topk,
                          tie_chunk=math.gcd(TIE_CHUNK, past)),
        grid_spec=pltpu.PrefetchScalarGridSpec(
            num_scalar_prefetch=1,
            grid=(nseq, nsteps + 1),
            in_specs=[pl.BlockSpec((None, t, past), const3),
                      pl.BlockSpec((None, t, PAGE_SIZE), const3),
                      pl.BlockSpec((None, N_KV_B, rows, HEAD_DIM_B), lambda b, j, pt: (b, 0, 0, 0)),
                      *_page_specs(page_block, layer, nsteps, pps),
                      *_page_specs(page_block, layer, nsteps, pps),
                      pl.BlockSpec((None, PAGE_SIZE, KV_WIDTH_B), const3),
                      pl.BlockSpec((None, PAGE_SIZE, KV_WIDTH_B), const3)],
            out_specs=pl.BlockSpec((None, N_KV_B, rows, HEAD_DIM_B), lambda b, j, pt: (b, 0, 0, 0)),
            scratch_shapes=[pltpu.VMEM((t, past), F32),
                            pltpu.VMEM((t, PAGE_SIZE), F32),
                            pltpu.VMEM((t, PAGE_SIZE), F32),
                            pltpu.VMEM((N_KV_B, rows, 1), F32),
                            pltpu.VMEM((N_KV_B, rows, 1), F32),
                            pltpu.VMEM((N_KV_B, rows, HEAD_DIM_B), F32)],
        ),
        out_shape=jax.ShapeDtypeStruct((nseq, N_KV_B, rows, HEAD_DIM_B), BF16),
        compiler_params=_cparams("parallel", "arbitrary"),
        name="dsa_sample_attn",
    )(page_table, scores, scores_new, q, *([cache_k] * pps), *([cache_v] * pps), k_new, v_new)


def _dsa_sample(ps, k_s, v_s, ki_s, page_table, cache_k, cache_v, cache_kidx, layer):
    nseq = page_table.shape[0]
    t = ps.shape[0] // nseq
    seg = lambda name, width: ps[:, PD_OFF[name]:PD_OFF[name] + width]
    qi = seg("qi", H_I * D_I).reshape(nseq, t * H_I, D_I)
    wi = seg("wi", H_I).reshape(nseq, t * H_I, 1)
    wv = jnp.broadcast_to(wi, (nseq, t * H_I, PAGE_SIZE))
    pad_rows = lambda a: jnp.pad(a.reshape(nseq, t, -1), ((0, 0), (0, PAGE_SIZE - t), (0, 0)))
    kidx_t = cache_kidx.transpose(0, 1, 3, 2)
    scores, scores_new = _dsa_sample_scores(page_table, qi, wv, kidx_t, layer,
                                            pad_rows(ki_s).transpose(0, 2, 1))
    q = seg("qb", WIDTH_B).reshape(nseq, t, N_KV_B, GROUP_B, HEAD_DIM_B)
    q = q.transpose(0, 2, 3, 1, 4).reshape(nseq, N_KV_B, GROUP_B * t, HEAD_DIM_B)
    pool_rows = lambda c: c.reshape(c.shape[0], c.shape[1], PAGE_SIZE * N_KV_B, HEAD_DIM_B)
    o = _dsa_sample_attn(page_table, scores, scores_new, q, pool_rows(cache_k), pool_rows(cache_v),
                         layer, pad_rows(k_s), pad_rows(v_s))
    o = o.reshape(nseq, N_KV_B, GROUP_B, t, HEAD_DIM_B).transpose(0, 3, 1, 2, 4)
    return o.reshape(nseq * t, WIDTH_B)


def _prep_w_mid(w_in):
    sizes = dict(zip(IN_NAMES, IN_SIZES))
    z0 = min(SRC_OFF[n] for n in MID_ORDER + ("aa", "ba"))
    z1 = max(SRC_OFF[n] + sizes[n] for n in MID_ORDER)
    zone = lax.optimization_barrier(lax.slice_in_dim(w_in, z0, z1, axis=2))
    runs = []
    for n in MID_ORDER:
        a, b = SRC_OFF[n] - z0, SRC_OFF[n] - z0 + sizes[n]
        if runs and runs[-1][1] == a:
            runs[-1][1] = b
        else:
            runs.append([a, b])
    cols = [zone[:, :, a:b].astype(BF16) for a, b in runs]
    cols.append(jnp.zeros(w_in.shape[:2] + (PD_WIDTH - sum(sizes[n] for n in MID_ORDER),), BF16))
    return jnp.concatenate(cols, axis=2)


def kernel(x_prompt, x_sample, cache_k, cache_v, cache_kidx, page_table, state_gdn, state_gdn_conv,
           state_sconv, final_norm, norm1, norm2, w_in, conv_a, a_log, dt_bias, gdn_norm, conv_c,
           w_branch_a, w_branch_b, w_branch_c, w_o, w_gate, w_up, w_down):
    bp, tp, d = x_prompt.shape
    bs, ts = x_sample.shape[:2]
    mp, ms = bp * tp, bs * ts
    x = jnp.concatenate([x_prompt.reshape(mp, d), x_sample.reshape(ms, d)], axis=0)
    new_p = [[] for _ in range(6)]
    new_s = [[] for _ in range(6)]
    w_mid = _prep_w_mid(w_in)
    w_down16 = w_down.astype(BF16)
    for l in range(DEPTH):
        pd, k_new, v_new, ki_new, xn = _in_proj_mid(x, norm1[l], w_mid, l)
        p = _in_proj_main(xn, w_in, l)
        g, beta = _gdn_gates(pd, a_log[l], dt_bias[l])
        conv_w = conv_a[l].astype(F32)

        put_sample = lambda full, part: lax.dynamic_update_slice(full, part, (mp, 0))
        oa, gdn_p = _gdn(p, 0, bp, tp, min(GDN_CHUNK, tp), conv_w, g, beta,
                         jnp.zeros((bp, CONV_A - 1, 3 * WIDTH_A), F32),
                         jnp.zeros((bp, H_A, DK_A, DV_A), F32), gdn_norm[l], out_rows=mp + ms)
        oa_s, gdn_s = _gdn(p, mp, bs, ts, ts, conv_w, g, beta,
                           state_gdn_conv[l], state_gdn[l], gdn_norm[l])
        oa = put_sample(oa, oa_s)
        ob = _dsa_prompt(pd, k_new, v_new, ki_new, bp, tp, out_rows=mp + ms)
        ob = put_sample(ob, _dsa_sample(pd[mp:], k_new[mp:], v_new[mp:], ki_new[mp:], page_table,
                                        cache_k, cache_v, cache_kidx, l))
        cw = conv_c[l].astype(F32)
        oc, sconv_p = _sconv(p, 0, bp, tp, cw, jnp.zeros((bp, CONV_C - 1, WIDTH_C), F32),
                             out_rows=mp + ms)
        oc_s, sconv_s = _sconv(p, mp, bs, ts, cw, state_sconv[l])
        oc = put_sample(oc, oc_s)
        merged = _merge(oa, ob, oc, w_branch_a, w_branch_b, w_branch_c, l, p)
        x = _out_proj(merged, w_o, l, x)

        hn = _rmsnorm(x, norm2[l], BF16)
        h = _swiglu(hn, w_gate, w_up, l)
        x = _out_proj(h, w_down16, l, x, name="ffn_down")

        nb = CONV_A - 1
        tail_rows = lambda r0, t: lax.slice(p, (r0 + max(t - nb, 0), 0), (r0 + t, 3 * WIDTH_A))
        gconv_p = jnp.stack([tail_rows(b * tp, tp) for b in range(bp)])
        gconv_s = jnp.stack([tail_rows(mp + b * ts, ts) for b in range(bs)])
        if tp < nb:
            gconv_p = jnp.concatenate([jnp.zeros((bp, nb - tp, 3 * WIDTH_A), F32), gconv_p], axis=1)
        if ts < nb:
            gconv_s = jnp.concatenate([state_gdn_conv[l].astype(F32)[:, ts:], gconv_s], axis=1)
        kv = lambda a, rows, b, t: a[rows].reshape(b, t, N_KV_B, HEAD_DIM_B)
        rp, rs = slice(0, mp), slice(mp, mp + ms)
        for lst, val in zip(new_p, (kv(k_new, rp, bp, tp), kv(v_new, rp, bp, tp),
                                    ki_new[rp].reshape(bp, tp, D_I), gdn_p, gconv_p, sconv_p)):
            lst.append(val)
        for lst, val in zip(new_s, (kv(k_new, rs, bs, ts), kv(v_new, rs, bs, ts),
                                    ki_new[rs].reshape(bs, ts, D_I), gdn_s, gconv_s, sconv_s)):
            lst.append(val)

    y_prompt = _rmsnorm(x, final_norm, F32, 0, mp).reshape(bp, tp, d)
    y_sample = _rmsnorm(x, final_norm, F32, mp, ms).reshape(bs, ts, d)
    outs_p = [jnp.stack(a) for a in new_p]
    outs_s = [jnp.stack(a) for a in new_s]
    return (y_prompt, y_sample, *outs_p, *outs_s)
```

```python
import functools
import math

import jax
import jax.numpy as jnp
from jax import lax
from jax.experimental import pallas as pl
from jax.experimental.pallas import tpu as pltpu

F32 = jnp.float32
BF16 = jnp.bfloat16

D_MODEL = 4096
DEPTH = 2
PAGE_SIZE = 128
H_A = 16
DK_A = 128
DV_A = 128
WIDTH_A = H_A * DV_A
CONV_A = 4
GDN_CHUNK = 64
H_B = 8
N_KV_B = 2
GROUP_B = H_B // N_KV_B
HEAD_DIM_B = 128
WIDTH_B = H_B * HEAD_DIM_B
KV_WIDTH_B = N_KV_B * HEAD_DIM_B
H_I = 32
D_I = 64
TOPK_MAX = 256
Q_BLOCK = 128
WIDTH_C = 1024
CONV_C = 3
D_FF = -(-8 * D_MODEL // (3 * 256)) * 256
EPS = 1e-6

IN_NAMES = ("qa", "ka", "va", "za", "aa", "ba", "qb", "kb", "vb", "qi", "ki", "wi",
            "gate_b", "gate_c", "hc", "ga", "gb", "gc")
IN_SIZES = (WIDTH_A, WIDTH_A, WIDTH_A, WIDTH_A, H_A, H_A,
            WIDTH_B, KV_WIDTH_B, KV_WIDTH_B, H_I * D_I, D_I, H_I,
            WIDTH_C, WIDTH_C, WIDTH_C, D_MODEL, D_MODEL, D_MODEL)

LANES = 128
N_BLOCK = 512
VMEM_LIMIT = 56 * 1024 * 1024
ACT_VMEM_BYTES = 32 * 1024 * 1024
NEG_BIG = -1e30

MAIN_ORDER = ("qa", "ka", "va", "za", "gate_b", "gate_c", "hc", "ga", "gb", "gc")
MID_ORDER = ("qb", "kb", "vb", "qi", "ki", "wi", "aa", "ba")


def _offsets(order):
    sizes = dict(zip(IN_NAMES, IN_SIZES))
    off, out = 0, {}
    for name in order:
        out[name] = off
        off += sizes[name]
    return out, off


SRC_OFF, _ = _offsets(IN_NAMES)
PM_OFF, PM_WIDTH = _offsets(MAIN_ORDER)
PD_OFF, _mid_cols = _offsets(MID_ORDER)
PD_WIDTH = -(-_mid_cols // N_BLOCK) * N_BLOCK
MAIN_RUN0 = PM_OFF["gate_b"]
MAIN_SHIFT = SRC_OFF["gate_b"] - MAIN_RUN0
assert PM_WIDTH % N_BLOCK == 0 and MAIN_RUN0 % N_BLOCK == 0 and MAIN_SHIFT % LANES == 0
assert all(SRC_OFF[n] == PM_OFF[n] for n in MAIN_ORDER[:4])
assert all(SRC_OFF[n] == PM_OFF[n] + MAIN_SHIFT for n in MAIN_ORDER[4:])
TAIL_OFF = PD_OFF["ki"]
TAIL_AA = PD_OFF["aa"] - TAIL_OFF
TAIL_BA = PD_OFF["ba"] - TAIL_OFF
TAIL_WI = PD_OFF["wi"] - TAIL_OFF
assert TAIL_OFF % N_BLOCK == 0 and _mid_cols - TAIL_OFF == LANES


def _cparams(*sem):
    return pltpu.CompilerParams(dimension_semantics=sem, vmem_limit_bytes=VMEM_LIMIT)


def _dot(a, b):
    return jnp.dot(a, b, preferred_element_type=F32)


def _dot_nt(a, b):
    return lax.dot_general(a, b, (((1,), (1,)), ((), ())), preferred_element_type=F32)


def _dot_tn(a, b):
    return lax.dot_general(a, b, (((0,), (0,)), ((), ())), preferred_element_type=F32)


def _dot_hi(a, b):
    return jnp.dot(a, b, preferred_element_type=F32, precision=lax.Precision.HIGHEST)


def _sigmoid(x):
    return jax.nn.sigmoid(x)


def _silu(x):
    return x * jax.nn.sigmoid(x)


def _rmsnorm_body(x_ref, g_ref, o_ref):
    x = x_ref[...]
    ms = jnp.mean(x * x, axis=-1, keepdims=True)
    o_ref[...] = (x * lax.rsqrt(ms + EPS) * g_ref[...]).astype(o_ref.dtype)


def _row_block(m):
    for bm in (192, 128, 64, 32, 16, 8):
        if m % bm == 0:
            return bm
    raise ValueError(f"unsupported row count {m}")


def _rmsnorm(x, g, out_dtype, row0=0, nrows=None):
    d = x.shape[1]
    m = x.shape[0] - row0 if nrows is None else nrows
    bm = _row_block(math.gcd(m, row0) if row0 else m)
    rb0 = row0 // bm
    return pl.pallas_call(
        _rmsnorm_body,
        grid=(m // bm,),
        in_specs=[pl.BlockSpec((bm, d), lambda i: (rb0 + i, 0)), pl.BlockSpec((1, d), lambda i: (0, 0))],
        out_specs=pl.BlockSpec((bm, d), lambda i: (i, 0)),
        out_shape=jax.ShapeDtypeStruct((m, d), out_dtype),
        compiler_params=_cparams("parallel"),
        name="rmsnorm",
    )(x, g.reshape(1, d).astype(F32))


def _gdn_gates_body(t_ref, alog_ref, dtb_ref, g_ref, b_ref):
    t = t_ref[...]
    aa = t[:, TAIL_AA:TAIL_AA + H_A]
    ba = t[:, TAIL_BA:TAIL_BA + H_A]
    x = aa + dtb_ref[...]
    softplus = jnp.maximum(x, 0.0) + jnp.log1p(jnp.exp(-jnp.abs(x)))
    g_ref[...] = -jnp.exp(alog_ref[...]) * softplus
    b_ref[...] = _sigmoid(ba)


def _gdn_gates(p, a_log, dt_bias):
    m = p.shape[0]
    bm = _row_block(m)
    tail_blk = TAIL_OFF // LANES
    return pl.pallas_call(
        _gdn_gates_body,
        grid=(m // bm,),
        in_specs=[pl.BlockSpec((bm, LANES), lambda i: (i, tail_blk)),
                  pl.BlockSpec((1, H_A), lambda i: (0, 0)),
                  pl.BlockSpec((1, H_A), lambda i: (0, 0))],
        out_specs=[pl.BlockSpec((bm, H_A), lambda i: (i, 0)), pl.BlockSpec((bm, H_A), lambda i: (i, 0))],
        out_shape=[jax.ShapeDtypeStruct((m, H_A), F32)] * 2,
        compiler_params=_cparams("parallel"),
        name="gdn_gates",
    )(p, a_log.reshape(1, H_A).astype(F32), dt_bias.reshape(1, H_A).astype(F32))


def _big_row_block(m, k=D_MODEL, buffers=2):
    for bm in (2064, 1376, 1024, 688, 512, 256, 128, 64):
        if m % bm == 0 and buffers * bm * k * 2 <= ACT_VMEM_BYTES:
            return bm
    raise ValueError(f"unsupported row count {m}")


def _resident_rows(bm, k, buffers=2):
    if buffers == 1:
        return pl.BlockSpec((bm, k), lambda i, j: (i, 0), pipeline_mode=pl.Buffered(1))
    return pl.BlockSpec((bm, k), lambda i, j: (i, 0))


def _out_proj_body(a_ref, w_ref, r_ref, o_ref):
    o_ref[...] = r_ref[...] + _dot(a_ref[...], w_ref[...].astype(BF16))


def _out_proj(a, w, layer, residual, name="out_proj"):
    m, k = a.shape
    n = w.shape[2]
    bm, bn = _big_row_block(m, k), N_BLOCK // 2
    return pl.pallas_call(
        _out_proj_body,
        grid=(m // bm, n // bn),
        in_specs=[_resident_rows(bm, k),
                  pl.BlockSpec((None, k, bn), lambda i, j: (layer, 0, j)),
                  pl.BlockSpec((bm, bn), lambda i, j: (i, j))],
        out_specs=pl.BlockSpec((bm, bn), lambda i, j: (i, j)),
        out_shape=jax.ShapeDtypeStruct((m, n), F32),
        compiler_params=_cparams("parallel", "arbitrary"),
        name=name,
    )(a, w, residual)


IN_PROJ_PIECES = N_BLOCK // LANES


def _in_proj_main_body(a_ref, *rest):
    w_refs, p_ref = rest[:IN_PROJ_PIECES], rest[IN_PROJ_PIECES]
    w = jnp.concatenate([r[...].astype(BF16) for r in w_refs], axis=1)
    p_ref[...] = _dot(a_ref[...], w)


def _in_proj_main(a, w_in, layer):
    m, k = a.shape
    bm, bn = _big_row_block(m, k, buffers=1), N_BLOCK
    run0_blk, shift_blk = MAIN_RUN0 // bn, MAIN_SHIFT // LANES

    def piece(q):
        def index(i, j):
            return (layer, 0, IN_PROJ_PIECES * j + q + jnp.where(j >= run0_blk, shift_blk, 0))
        return pl.BlockSpec((None, k, LANES), index)

    return pl.pallas_call(
        _in_proj_main_body,
        grid=(m // bm, PM_WIDTH // bn),
        in_specs=[_resident_rows(bm, k, buffers=1)] + [piece(q) for q in range(IN_PROJ_PIECES)],
        out_specs=pl.BlockSpec((bm, bn), lambda i, j: (i, j)),
        out_shape=jax.ShapeDtypeStruct((m, PM_WIDTH), F32),
        compiler_params=_cparams("parallel", "arbitrary"),
        name="in_proj_main",
    )(a, *([w_in] * IN_PROJ_PIECES))


MID_SHIFT = SRC_OFF["qb"] % LANES
MID_PIECE0 = SRC_OFF["qb"] // LANES
MID_PIECES = IN_PROJ_PIECES + 1
SMALL_PIECE = SRC_OFF["aa"] // LANES
SMALL_AT = PD_OFF["aa"] - TAIL_OFF
assert [SRC_OFF[n] - SRC_OFF["qb"] for n in MID_ORDER[:6]] == [PD_OFF[n] for n in MID_ORDER[:6]]
assert SRC_OFF["aa"] % LANES == 0 and SRC_OFF["ba"] == SRC_OFF["aa"] + H_A and PD_OFF["ba"] == PD_OFF["aa"] + H_A
assert SMALL_AT + 2 * H_A == LANES


def _in_proj_mid_body(a_ref, *rest, jkv, jtail):
    w_refs, small_ref = rest[:MID_PIECES], rest[MID_PIECES]
    p_ref, k_ref, v_ref, ki_ref = rest[MID_PIECES + 1:]
    j = pl.program_id(1)
    wide = jnp.concatenate([r[...].astype(BF16) for r in w_refs], axis=1)
    w = pltpu.roll(wide, wide.shape[1] - MID_SHIFT, axis=1)[:, :N_BLOCK]
    small = pltpu.roll(small_ref[...].astype(BF16), SMALL_AT, axis=1)
    lane = lax.broadcasted_iota(jnp.int32, (1, LANES), 1)
    first = jnp.where(jnp.logical_and(j == jtail, lane >= SMALL_AT), small, w[:, :LANES])
    w = jnp.concatenate([first, w[:, LANES:]], axis=1)
    acc = _dot(a_ref[...], w)
    p_ref[...] = acc

    @pl.when(j == jkv)
    def _():
        k_ref[...] = acc[:, :KV_WIDTH_B]
        v_ref[...] = acc[:, KV_WIDTH_B:2 * KV_WIDTH_B]

    @pl.when(j == jtail)
    def _():
        ki_ref[...] = acc[:, :D_I]


def _in_proj_mid(a, w_in, layer):
    m, k = a.shape
    bm, bn = _big_row_block(m, k, buffers=4), N_BLOCK
    assert PD_OFF["kb"] % bn == 0 and PD_OFF["vb"] == PD_OFF["kb"] + KV_WIDTH_B
    row = lambda i, j: (i, 0)
    piece = lambda q: pl.BlockSpec((None, k, LANES), lambda i, j: (layer, 0, MID_PIECE0 + IN_PROJ_PIECES * j + q))
    return pl.pallas_call(
        functools.partial(_in_proj_mid_body, jkv=PD_OFF["kb"] // bn, jtail=TAIL_OFF // bn),
        grid=(m // bm, PD_WIDTH // bn),
        in_specs=[_resident_rows(bm, k)] + [piece(q) for q in range(MID_PIECES)]
                 + [pl.BlockSpec((None, k, LANES), lambda i, j: (layer, 0, SMALL_PIECE))],
        out_specs=[pl.BlockSpec((bm, bn), lambda i, j: (i, j)),
                   pl.BlockSpec((bm, KV_WIDTH_B), row), pl.BlockSpec((bm, KV_WIDTH_B), row),
                   pl.BlockSpec((bm, D_I), row)],
        out_shape=[jax.ShapeDtypeStruct((m, PD_WIDTH), F32), jax.ShapeDtypeStruct((m, KV_WIDTH_B), F32),
                   jax.ShapeDtypeStruct((m, KV_WIDTH_B), F32), jax.ShapeDtypeStruct((m, D_I), F32)],
        compiler_params=_cparams("parallel", "arbitrary"),
        name="in_proj_mid",
    )(a, *([w_in] * (MID_PIECES + 1)))


def _swiglu_body(a_ref, wg_ref, wu_ref, o_ref):
    a = a_ref[...]
    g = _dot(a, wg_ref[...].astype(BF16))
    u = _dot(a, wu_ref[...].astype(BF16))
    o_ref[...] = (_silu(g) * u).astype(o_ref.dtype)


def _swiglu(a, w_gate, w_up, layer):
    m, k = a.shape
    n = w_gate.shape[2]
    bm, bn = _big_row_block(m, k, buffers=1), N_BLOCK // 2
    assert n % bn == 0
    wspec = pl.BlockSpec((None, k, bn), lambda i, j: (layer, 0, j))
    return pl.pallas_call(
        _swiglu_body,
        grid=(m // bm, n // bn),
        in_specs=[_resident_rows(bm, k, buffers=1), wspec, wspec],
        out_specs=pl.BlockSpec((bm, bn), lambda i, j: (i, j)),
        out_shape=jax.ShapeDtypeStruct((m, n), BF16),
        compiler_params=_cparams("parallel", "arbitrary"),
        name="swiglu",
    )(a, w_gate, w_up)


def _merge_body(oa_ref, ob_ref, oc_ref, wa_ref, wb_ref, wc_ref, ga_ref, gb_ref, gc_ref, o_ref):
    acc = _sigmoid(ga_ref[...]) * _dot(oa_ref[...], wa_ref[...].astype(BF16))
    acc += _sigmoid(gb_ref[...]) * _dot(ob_ref[...], wb_ref[...].astype(BF16))
    acc += _sigmoid(gc_ref[...]) * _dot(oc_ref[...], wc_ref[...].astype(BF16))
    o_ref[...] = acc.astype(o_ref.dtype)


def _merge(oa, ob, oc, wa, wb, wc, layer, p):
    m = oa.shape[0]
    n = wa.shape[2]
    bm, bn = _big_row_block(m), N_BLOCK // 2
    ga0, gb0, gc0 = (PM_OFF[s] // bn for s in ("ga", "gb", "gc"))
    row = lambda i, j: (i, 0)
    wspec = lambda w: pl.BlockSpec((None, w.shape[1], bn), lambda i, j: (layer, 0, j))
    return pl.pallas_call(
        _merge_body,
        grid=(m // bm, n // bn),
        in_specs=[_resident_rows(bm, oa.shape[1]), _resident_rows(bm, ob.shape[1]),
                  _resident_rows(bm, oc.shape[1]),
                  wspec(wa), wspec(wb), wspec(wc),
                  pl.BlockSpec((bm, bn), lambda i, j: (i, ga0 + j)),
                  pl.BlockSpec((bm, bn), lambda i, j: (i, gb0 + j)),
                  pl.BlockSpec((bm, bn), lambda i, j: (i, gc0 + j))],
        out_specs=pl.BlockSpec((bm, bn), lambda i, j: (i, j)),
        out_shape=jax.ShapeDtypeStruct((m, n), BF16),
        compiler_params=_cparams("parallel", "arbitrary"),
        name="merge",
    )(oa, ob, oc, wa, wb, wc, p, p, p)


GDN_HEADS_PER_STEP = 16
GDN_INV_BASE = 8


def _gdn_body(q_ref, k_ref, v_ref, z_ref, wq_ref, wk_ref, wv_ref, tq_ref, tk_ref, tv_ref,
              g_ref, b_ref, s0_ref, ng_ref, o_ref, sout_ref, s_scr, tail_scr,
              *, chunk, hg):
    c = chunk
    w = hg * DK_A
    n = pl.program_id(2)

    @pl.when(n == 0)
    def _():
        s_scr[...] = s0_ref[...].astype(F32)
        tail_scr[:, 0:w] = tq_ref[...]
        tail_scr[:, w:2 * w] = tk_ref[...]
        tail_scr[:, 2 * w:3 * w] = tv_ref[...]

    act = []
    for idx, (x_ref, w_ref) in enumerate(((q_ref, wq_ref), (k_ref, wk_ref), (v_ref, wv_ref))):
        x = x_ref[...]
        cw = w_ref[...]
        xfull = jnp.concatenate([tail_scr[:, idx * w:(idx + 1) * w], x], axis=0)
        y = x * cw[CONV_A - 1:CONV_A, :]
        for j in range(1, CONV_A):
            y = y + pltpu.roll(xfull, j, axis=0)[8:, :] * cw[CONV_A - 1 - j:CONV_A - j, :]
        tail_scr[:, idx * w:(idx + 1) * w] = xfull[c:c + 8, :]
        act.append(_silu(y))
    qs, ks, vs = act
    z = z_ref[...]

    ii = lax.broadcasted_iota(jnp.int32, (c, c), 0)
    jj = lax.broadcasted_iota(jnp.int32, (c, c), 1)
    incl = jj <= ii
    strict = jj < ii
    eye = (ii == jj).astype(F32)
    ng = ng_ref[...]
    b_blk = b_ref[...]
    gcum_all = _dot_hi(incl.astype(F32), g_ref[...])
    gcum_t = gcum_all.T

    heads = range(hg)
    hsl = [slice(hh * DK_A, (hh + 1) * DK_A) for hh in heads]
    q_l, k_l, k16_l, decay_l, egc_l, kdec_l, glast_l, mm_l, rhs_l = ([] for _ in range(9))
    for hh in heads:
        q = qs[:, hsl[hh]]
        k = ks[:, hsl[hh]]
        q = q * lax.rsqrt(jnp.sum(q * q, axis=-1, keepdims=True) + EPS) * (DK_A ** -0.5)
        k = k * lax.rsqrt(jnp.sum(k * k, axis=-1, keepdims=True) + EPS)
        bc = b_blk[:, hh:hh + 1]
        gcum_c = gcum_all[:, hh:hh + 1]
        gcum_r = gcum_t[hh:hh + 1, :]
        decay = jnp.exp(jnp.where(incl, gcum_c - gcum_r, -jnp.inf))
        kb = k * bc
        k16 = k.astype(BF16)
        egc = jnp.exp(gcum_c)
        glast = gcum_c[c - 1:c, :]
        mm = jnp.where(strict, _dot_nt(kb.astype(BF16), k16) * decay, 0.0)
        q_l.append(q)
        k_l.append(k)
        k16_l.append(k16)
        decay_l.append(decay)
        egc_l.append(egc)
        glast_l.append(glast)
        kdec_l.append(jnp.exp(glast - gcum_c))
        rhs_l.append(jnp.concatenate([vs[:, hsl[hh]] * bc, kb * egc], axis=1).astype(BF16))
        mm_l.append(mm)
    base = min(c, GDN_INV_BASE)
    log_base = base.bit_length() - 1
    in_base = jnp.right_shift(ii, log_base) == jnp.right_shift(jj, log_base)
    pw_l = [jnp.where(in_base, -mm, 0.0) for mm in mm_l]
    tinv_l = [eye + pw for pw in pw_l]
    for _ in range(log_base - 1):
        for hh in heads:
            pw16 = pw_l[hh].astype(BF16)
            pw_l[hh] = _dot(pw16, pw16)
        for hh in heads:
            tinv_l[hh] = tinv_l[hh] + _dot(tinv_l[hh].astype(BF16), pw_l[hh].astype(BF16))
    log_size = log_base
    while (1 << log_size) < c:
        lower_left = jnp.logical_and(
            jnp.right_shift(ii, log_size + 1) == jnp.right_shift(jj, log_size + 1),
            jnp.logical_and(jnp.bitwise_and(jnp.right_shift(ii, log_size), 1) == 1,
                            jnp.bitwise_and(jnp.right_shift(jj, log_size), 1) == 0))
        for hh in heads:
            t16 = tinv_l[hh].astype(BF16)
            c16 = jnp.where(lower_left, mm_l[hh], 0.0).astype(BF16)
            tinv_l[hh] = tinv_l[hh] - _dot(_dot(t16, c16).astype(BF16), t16)
        log_size += 1
    sol_l =[_dot(tinv_l[hh].astype(BF16), rhs_l[hh]) for hh in heads]
    attn_l = [(_dot_nt(q_l[hh].astype(BF16), k16_l[hh]) * decay_l[hh]).astype(BF16) for hh in heads]
    s_l = [s_scr[hh] for hh in heads]
    s16_l = [s_l[hh].astype(BF16) for hh in heads]
    v16_l = [(sol_l[hh][:, :DV_A] - _dot(sol_l[hh][:, DV_A:].astype(BF16), s16_l[hh])).astype(BF16)
             for hh in heads]
    o_l = [_dot((q_l[hh] * egc_l[hh]).astype(BF16), s16_l[hh]) + _dot(attn_l[hh], v16_l[hh]) for hh in heads]
    snew_l = [s_l[hh] * jnp.exp(glast_l[hh]) + _dot_tn((k_l[hh] * kdec_l[hh]).astype(BF16), v16_l[hh])
              for hh in heads]
    s_scr[...] = jnp.stack(snew_l, axis=0)
    o_l = [o * lax.rsqrt(jnp.mean(o * o, axis=-1, keepdims=True) + EPS) * ng for o in o_l]
    o_ref[...] = (jnp.concatenate(o_l, axis=1) * _silu(z)).astype(o_ref.dtype)

    @pl.when(n == pl.num_programs(2) - 1)
    def _():
        sout_ref[...] = s_scr[...].astype(sout_ref.dtype)


def _gdn(p, row0, nseq, t, chunk, conv_w, g, beta, conv_buf, s0, norm_g, out_rows=None):
    hg = GDN_HEADS_PER_STEP
    assert hg == H_A
    w = hg * DK_A
    nch = t // chunk
    nhg = H_A // hg
    rb0 = row0 // chunk
    gspec = pl.BlockSpec((chunk, H_A), lambda b, h, n: (rb0 + b * nch + n, 0))
    tail = jnp.pad(conv_buf.astype(F32), ((0, 0), (8 - (CONV_A - 1), 0), (0, 0)))
    qb0, kb0, vb0, zb0 = (PM_OFF[s] // w for s in ("qa", "ka", "va", "za"))
    pspec = lambda b0: pl.BlockSpec((chunk, w), lambda b, h, n: (rb0 + b * nch + n, b0 + h))
    wspec = lambda b0: pl.BlockSpec((CONV_A, w), lambda b, h, n: (0, b0 + h))
    tspec = lambda b0: pl.BlockSpec((None, 8, w), lambda b, h, n: (b, 0, b0 + h))
    o, s_out = pl.pallas_call(
        functools.partial(_gdn_body, chunk=chunk, hg=hg),
        grid=(nseq, nhg, nch),
        in_specs=[pspec(qb0), pspec(kb0), pspec(vb0), pspec(zb0),
                  wspec(0), wspec(nhg), wspec(2 * nhg),
                  tspec(0), tspec(nhg), tspec(2 * nhg),
                  gspec, gspec,
                  pl.BlockSpec((None, hg, DK_A, DV_A), lambda b, h, n: (b, h, 0, 0)),
                  pl.BlockSpec((1, DV_A), lambda b, h, n: (0, 0))],
        out_specs=[pl.BlockSpec((chunk, w), lambda b, h, n: (b * nch + n, h)),
                   pl.BlockSpec((None, hg, DK_A, DV_A), lambda b, h, n: (b, h, 0, 0))],
        out_shape=[jax.ShapeDtypeStruct((out_rows or nseq * t, WIDTH_A), BF16),
                   jax.ShapeDtypeStruct((nseq, H_A, DK_A, DV_A), s0.dtype)],
        scratch_shapes=[pltpu.VMEM((hg, DK_A, DV_A), F32), pltpu.VMEM((8, 3 * w), F32)],
        compiler_params=_cparams("parallel", "parallel", "arbitrary"),
        name=f"gdn_c{chunk}",
    )(p, p, p, p, conv_w, conv_w, conv_w, tail, tail, tail, g, beta, s0,
      norm_g.reshape(1, DV_A).astype(F32))
    return o, s_out


SCONV_COLS = 256


def _sconv_body(gb_ref, gc_ref, hc_ref, w_ref, buf_ref, o_ref, st_ref, *, t):
    pr = gc_ref[...] * hc_ref[...]
    cw = w_ref[...]
    xfull = jnp.concatenate([buf_ref[...], pr], axis=0)
    y = pr * cw[CONV_C - 1:CONV_C, :]
    for j in range(1, CONV_C):
        y = y + pltpu.roll(xfull, j, axis=0)[8:, :] * cw[CONV_C - 1 - j:CONV_C - j, :]
    o_ref[...] = (gb_ref[...] * y).astype(o_ref.dtype)
    st_ref[...] = xfull[t:t + 8, :]


def _sconv(p, row0, nseq, t, conv_w, buf, out_rows=None):
    cols = SCONV_COLS
    ncb = WIDTH_C // cols
    rb0 = row0 // t
    b0, c0, h0 = (PM_OFF[s] // cols for s in ("gate_b", "gate_c", "hc"))
    buf8 = jnp.pad(buf.astype(F32), ((0, 0), (8 - (CONV_C - 1), 0), (0, 0)))
    pspec = lambda o: pl.BlockSpec((t, cols), lambda b, j: (rb0 + b, o + j))
    o, st = pl.pallas_call(
        functools.partial(_sconv_body, t=t),
        grid=(nseq, ncb),
        in_specs=[pspec(b0), pspec(c0), pspec(h0),
                  pl.BlockSpec((CONV_C, cols), lambda b, j: (0, j)),
                  pl.BlockSpec((None, 8, cols), lambda b, j: (b, 0, j))],
        out_specs=[pl.BlockSpec((t, cols), lambda b, j: (b, j)),
                   pl.BlockSpec((None, 8, cols), lambda b, j: (b, 0, j))],
        out_shape=[jax.ShapeDtypeStruct((out_rows or nseq * t, WIDTH_C), BF16),
                   jax.ShapeDtypeStruct((nseq, 8, WIDTH_C), F32)],
        compiler_params=_cparams("parallel", "parallel"),
        name=f"sconv_t{t}",
    )(p, p, p, conv_w, buf8)
    return o, st[:, 8 - (CONV_C - 1):, :]


BISECT_EVERY = 8
MAX_SEARCH = 2400


def _select_threshold(count_ge, lo0, hi0, n_valid, k):
    kf = jnp.float32(k)

    def cond(st):
        return jnp.logical_and(st[0] < MAX_SEARCH, jnp.min(st[-1]) < 0.5)

    def step(st):
        it, lo, hi, glo, ghi, t, side, done = st
        half = 0.5 * lo + 0.5 * hi
        cand = lo + (hi - lo) * (glo / jnp.maximum(glo - ghi, 1e-9))
        mid = jnp.where(it % BISECT_EVERY == BISECT_EVERY - 1, half, cand)
        mid = jnp.where(jnp.logical_and(mid > lo, mid < hi), mid, half)
        adjacent = jnp.logical_not(jnp.logical_and(mid > lo, mid < hi))
        first = it == 0
        mid = jnp.where(first, hi, mid)
        adjacent = jnp.logical_and(adjacent, jnp.logical_not(first))
        g = count_ge(mid) - kf
        ge = g >= 0.0
        finish = jnp.logical_or(jnp.logical_or(g == 0.0, adjacent), jnp.logical_and(first, ge))
        newly = jnp.logical_and(finish, done < 0.5)
        t = jnp.where(newly, jnp.where(adjacent, lo, mid), t)
        new_side = jnp.where(ge, 1.0, -1.0)
        damp = jnp.where(jnp.logical_and(new_side == side, jnp.logical_not(first)), 0.5, 1.0)
        glo = jnp.where(ge, g, glo * damp)
        ghi = jnp.where(ge, ghi * damp, g)
        lo = jnp.where(ge, mid, lo)
        hi = jnp.where(ge, hi, mid)
        done = jnp.where(finish, 1.0, done)
        return it + 1, lo, hi, glo, ghi, t, new_side, done

    done0 = jnp.where(n_valid > kf, 0.0, 1.0)
    st = lax.while_loop(cond, lambda st: step(step(st)), (jnp.int32(0), lo0, hi0, n_valid - kf, jnp.full_like(lo0, -kf),
                                     lo0, jnp.zeros_like(lo0), done0))
    return st[5]


KEY_CHUNK = 512


QI_PIECE = 512


def _dsa_prompt_body(*refs, tq, sc, topk):
    nqi = H_I * D_I // QI_PIECE
    qi_refs = refs[:nqi]
    tq_ref, qb_ref, kb_ref, vb_ref, tk_ref, o_ref, sc_scr, m_scr, l_scr, acc_scr = refs[nqi:]
    i = pl.program_id(1)
    nck = (i * tq + tq - 1) // sc + 1
    wt = tq_ref[...].T[TAIL_WI:TAIL_WI + H_I, :] * ((D_I ** -0.5) * (H_I ** -0.5))

    def head_q(h):
        piece, off = divmod(h * D_I, QI_PIECE)
        return qi_refs[piece][:, off:off + D_I]

    pairs = [jnp.concatenate([head_q(2 * p), head_q(2 * p + 1)], axis=0).astype(BF16)
             for p in range(H_I // 2)]
    tpos = i * tq + lax.broadcasted_iota(jnp.int32, (1, tq), 1)

    def score_chunk(c, carry):
        r0 = pl.multiple_of(c * sc, sc)
        kic = tk_ref[pl.ds(r0, sc), :].astype(BF16)
        acc = jnp.zeros((sc, tq), F32)
        for p in range(H_I // 2):
            d = _dot_nt(kic, pairs[p])
            acc = acc + jnp.maximum(d[:, :tq], 0.0) * wt[2 * p:2 * p + 1, :]
            acc = acc + jnp.maximum(d[:, tq:], 0.0) * wt[2 * p + 1:2 * p + 2, :]
        kpos = r0 + lax.broadcasted_iota(jnp.int32, (sc, tq), 0)
        sc_scr[pl.ds(r0, sc), :] = jnp.where(kpos <= tpos, acc, -jnp.inf)
        return carry

    lax.fori_loop(0, nck, score_chunk, 0)

    def minmax_chunk(c, carry):
        mn, mx = carry
        x = sc_scr[pl.ds(pl.multiple_of(c * sc, sc), sc), :]
        mx = jnp.maximum(mx, jnp.max(x, axis=0, keepdims=True))
        mn = jnp.minimum(mn, jnp.min(jnp.where(x == -jnp.inf, jnp.inf, x), axis=0, keepdims=True))
        return mn, mx

    mn, mx = lax.fori_loop(0, nck, minmax_chunk,
                           (jnp.full((1, tq), jnp.inf, F32), jnp.full((1, tq), -jnp.inf, F32)))

    def count_ge(mid):
        def body(c, acc):
            x = sc_scr[pl.ds(pl.multiple_of(c * sc, sc), sc), :]
            hit = jnp.where(x >= mid, 1.0, 0.0).reshape(8, sc // 64, 8, tq)
            return acc + jnp.sum(jnp.sum(hit, axis=1), axis=0)
        return jnp.sum(lax.fori_loop(0, nck, body, jnp.zeros((8, tq), F32)), axis=0, keepdims=True)

    thr = _select_threshold(count_ge, mn, mx, (tpos + 1).astype(F32), topk)

    def count_gt(c, acc):
        x = sc_scr[pl.ds(pl.multiple_of(c * sc, sc), sc), :]
        hit = jnp.where(x > thr, 1.0, 0.0).reshape(8, sc // 64, 8, tq)
        return acc + jnp.sum(jnp.sum(hit, axis=1), axis=0)
    n_above = jnp.sum(lax.fori_loop(0, nck, count_gt, jnp.zeros((8, tq), F32)), axis=0, keepdims=True)
    crowded = count_ge(thr) > topk
    places = topk - n_above

    @pl.when(jnp.max(jnp.where(crowded, 1.0, 0.0)) > 0.5)
    def _():
        earlier = (lax.broadcasted_iota(jnp.int32, (sc, sc), 1)
                   < lax.broadcasted_iota(jnp.int32, (sc, sc), 0)).astype(BF16)

        def retire(c, seen):
            rows = pl.ds(pl.multiple_of(c * sc, sc), sc)
            x = sc_scr[rows, :]
            tie = jnp.logical_and(x == thr, crowded)
            tie_f = jnp.where(tie, 1.0, 0.0)
            rank = seen + _dot(earlier, tie_f.astype(BF16))
            sc_scr[rows, :] = jnp.where(jnp.logical_and(tie, rank >= places), -jnp.inf, x)
            return seen + jnp.sum(tie_f, axis=0, keepdims=True)

        lax.fori_loop(0, nck, retire, jnp.zeros((1, tq), F32))

    m_scr[...] = jnp.full(m_scr.shape, NEG_BIG, F32)
    l_scr[...] = jnp.zeros(l_scr.shape, F32)
    acc_scr[...] = jnp.zeros(acc_scr.shape, F32)
    q = qb_ref[...].astype(BF16)
    scale = HEAD_DIM_B ** -0.5

    def attn_chunk(c, carry):
        r0 = pl.multiple_of(c * sc, sc)
        sel = sc_scr[pl.ds(r0, sc), :] >= thr
        for n in range(N_KV_B):
            kc = kb_ref[pl.ds(r0, sc), n * HEAD_DIM_B:(n + 1) * HEAD_DIM_B].astype(BF16)
            vc = vb_ref[pl.ds(r0, sc), n * HEAD_DIM_B:(n + 1) * HEAD_DIM_B].astype(BF16)
            for g in range(GROUP_B):
                h = n * GROUP_B + g
                hs = slice(h * HEAD_DIM_B, (h + 1) * HEAD_DIM_B)
                s = jnp.where(sel, _dot_nt(kc, q[:, hs]) * scale, NEG_BIG)
                m_old = m_scr[h:h + 1, :]
                m_new = jnp.maximum(m_old, jnp.max(s, axis=0, keepdims=True))
                pexp = jnp.exp(s - m_new)
                alpha = jnp.exp(m_old - m_new)
                l_scr[h:h + 1, :] = alpha * l_scr[h:h + 1, :] + jnp.sum(pexp, axis=0, keepdims=True)
                acc_scr[hs, :] = alpha * acc_scr[hs, :] + _dot_tn(vc, pexp.astype(BF16))
                m_scr[h:h + 1, :] = m_new
        return carry

    lax.fori_loop(0, nck, attn_chunk, 0)

    for h in range(H_B):
        hs = slice(h * HEAD_DIM_B, (h + 1) * HEAD_DIM_B)
        ot = acc_scr[hs, :] / l_scr[h:h + 1, :]
        o_ref[:, hs] = ot.T.astype(o_ref.dtype)


def _dsa_prompt(p, k_new, v_new, ki_new, nseq, s, out_rows=None):
    tq, sc = Q_BLOCK, min(KEY_CHUNK, s)
    topk = min(TOPK_MAX, s // 4)
    nqb = s // tq
    assert PD_OFF["qi"] % QI_PIECE == 0 and PD_OFF["qb"] % WIDTH_B == 0
    nqi = H_I * D_I // QI_PIECE
    qi0 = PD_OFF["qi"] // QI_PIECE
    qb0 = PD_OFF["qb"] // WIDTH_B
    tl0 = TAIL_OFF // LANES
    qi_spec = lambda q: pl.BlockSpec((tq, QI_PIECE), lambda b, i: (b * nqb + i, qi0 + q))
    return pl.pallas_call(
        functools.partial(_dsa_prompt_body, tq=tq, sc=sc, topk=topk),
        grid=(nseq, nqb),
        in_specs=[*(qi_spec(q) for q in range(nqi)),
                  pl.BlockSpec((tq, LANES), lambda b, i: (b * nqb + i, tl0)),
                  pl.BlockSpec((tq, WIDTH_B), lambda b, i: (b * nqb + i, qb0)),
                  pl.BlockSpec((s, KV_WIDTH_B), lambda b, i: (b, 0)),
                  pl.BlockSpec((s, KV_WIDTH_B), lambda b, i: (b, 0)),
                  pl.BlockSpec((s, D_I), lambda b, i: (b, 0))],
        out_specs=pl.BlockSpec((tq, WIDTH_B), lambda b, i: (b * nqb + i, 0)),
        out_shape=jax.ShapeDtypeStruct((out_rows or nseq * s, WIDTH_B), BF16),
        scratch_shapes=[pltpu.VMEM((s, tq), F32), pltpu.VMEM((H_B, tq), F32),
                        pltpu.VMEM((H_B, tq), F32), pltpu.VMEM((WIDTH_B, tq), F32)],
        compiler_params=_cparams("parallel", "arbitrary"),
        name="dsa_prompt",
    )(*([p] * nqi), p, p, k_new, v_new, ki_new)


PAGES_PER_STEP = 16
TIE_CHUNK = 512


def _dsa_sample_scores_body(pt_ref, q_ref, w_ref, *rest, nsteps, pps, t):
    page_refs, new_ref, o_ref, onew_ref = rest[:pps], rest[pps], rest[pps + 1], rest[pps + 2]
    j = pl.program_id(1)
    q = q_ref[...].astype(BF16)
    wv = w_ref[...] * ((D_I ** -0.5) * (H_I ** -0.5))

    def scores(keys_t):
        d = _dot(q, keys_t.astype(BF16))
        return jnp.sum((jnp.maximum(d, 0.0) * wv).reshape(t, H_I, PAGE_SIZE), axis=1)

    @pl.when(j < nsteps)
    def _():
        o_ref[...] = jnp.concatenate([scores(r[...]) for r in page_refs], axis=1)

    @pl.when(j == nsteps)
    def _():
        knew = lax.broadcasted_iota(jnp.int32, (t, PAGE_SIZE), 1)
        tnew = lax.broadcasted_iota(jnp.int32, (t, PAGE_SIZE), 0)
        onew_ref[...] = jnp.where(knew <= tnew, scores(new_ref[...]), -jnp.inf)


def _page_specs(block, layer, nsteps, pps):
    def spec(p):
        def index(b, j, pt):
            return (layer, pt[b, jnp.minimum(j, nsteps - 1) * pps + p]) + (0,) * (len(block) - 2)
        return pl.BlockSpec(block, index)
    return [spec(p) for p in range(pps)]


def _dsa_sample_scores(page_table, q, wv, cache_kidx, layer, ki_new):
    nseq, npages = page_table.shape
    t = q.shape[1] // H_I
    pps = math.gcd(PAGES_PER_STEP, npages)
    nsteps = npages // pps
    return pl.pallas_call(
        functools.partial(_dsa_sample_scores_body, nsteps=nsteps, pps=pps, t=t),
        grid_spec=pltpu.PrefetchScalarGridSpec(
            num_scalar_prefetch=1,
            grid=(nseq, nsteps + 1),
            in_specs=[pl.BlockSpec((None, t * H_I, D_I), lambda b, j, pt: (b, 0, 0)),
                      pl.BlockSpec((None, t * H_I, PAGE_SIZE), lambda b, j, pt: (b, 0, 0)),
                      *_page_specs((None, None, D_I, PAGE_SIZE), layer, nsteps, pps),
                      pl.BlockSpec((None, D_I, PAGE_SIZE), lambda b, j, pt: (b, 0, 0))],
            out_specs=[pl.BlockSpec((None, t, pps * PAGE_SIZE),
                                    lambda b, j, pt: (b, 0, jnp.minimum(j, nsteps - 1))),
                       pl.BlockSpec((None, t, PAGE_SIZE), lambda b, j, pt: (b, 0, 0))],
        ),
        out_shape=[jax.ShapeDtypeStruct((nseq, t, npages * PAGE_SIZE), F32),
                   jax.ShapeDtypeStruct((nseq, t, PAGE_SIZE), F32)],
        compiler_params=_cparams("parallel", "arbitrary"),
        name="dsa_sample_scores",
    )(page_table, q, wv, *([cache_kidx] * pps), ki_new)


def _dsa_sample_attn_body(pt_ref, sc_ref, scn_ref, q_ref, *rest, nsteps, pps, t, past, topk, tie_chunk):
    kpage_refs, vpage_refs = rest[:pps], rest[pps:2 * pps]
    knew_ref, vnew_ref, o_ref, scp_scr, scn_scr, thr_scr, m_scr, l_scr, acc_scr = rest[2 * pps:]
    j = pl.program_id(1)

    @pl.when(j == 0)
    def _():
        xp, xn = sc_ref[...], scn_ref[...]
        scp_scr[...] = xp
        scn_scr[...] = xn
        mx = jnp.maximum(jnp.max(xp, axis=1, keepdims=True), jnp.max(xn, axis=1, keepdims=True))
        mn = jnp.minimum(jnp.min(xp, axis=1, keepdims=True),
                         jnp.min(jnp.where(xn == -jnp.inf, jnp.inf, xn), axis=1, keepdims=True))
        cnt = lambda hit: jnp.sum(jnp.where(hit, 1.0, 0.0), axis=1, keepdims=True)

        def count_ge(mid):
            return cnt(sc_ref[...] >= mid) + cnt(scn_ref[...] >= mid)

        n_valid = (past + 1 + lax.broadcasted_iota(jnp.int32, (t, 1), 0)).astype(F32)
        thr = _select_threshold(count_ge, mn, mx, n_valid, topk)

        crowded = count_ge(thr) > topk
        places = topk - (cnt(xp > thr) + cnt(xn > thr))

        @pl.when(jnp.max(jnp.where(crowded, 1.0, 0.0)) > 0.5)
        def _():
            cw = tie_chunk
            earlier = (lax.broadcasted_iota(jnp.int32, (cw, cw), 0)
                       < lax.broadcasted_iota(jnp.int32, (cw, cw), 1)).astype(BF16)

            def retire(x, seen, width):
                tie = jnp.logical_and(x == thr, crowded)
                tie_f = jnp.where(tie, 1.0, 0.0)
                rank = seen + _dot(tie_f.astype(BF16), earlier[:width, :width])
                return (jnp.where(jnp.logical_and(tie, rank >= places), -jnp.inf, x),
                        seen + jnp.sum(tie_f, axis=1, keepdims=True))

            def past_chunk(c, seen):
                cols = pl.ds(pl.multiple_of(c * cw, cw), cw)
                x, seen = retire(scp_scr[:, cols], seen, cw)
                scp_scr[:, cols] = x
                return seen

            seen = lax.fori_loop(0, past // cw, past_chunk, jnp.zeros((t, 1), F32))
            scn_scr[...] = retire(scn_scr[...], seen, PAGE_SIZE)[0]

        thr_scr[...] = jnp.broadcast_to(thr, thr_scr.shape)
        m_scr[...] = jnp.full(m_scr.shape, NEG_BIG, F32)
        l_scr[...] = jnp.zeros(l_scr.shape, F32)
        acc_scr[...] = jnp.zeros(acc_scr.shape, F32)

    scale = HEAD_DIM_B ** -0.5

    def attend(x, keys, vals):
        sel_t = jnp.where(x >= thr_scr[:, 0:1], 1.0, 0.0)
        sel = jnp.concatenate([sel_t] * GROUP_B, axis=0) > 0.5
        for n in range(N_KV_B):
            s = jnp.where(sel, _dot_nt(q_ref[n].astype(BF16), keys(n).astype(BF16)) * scale, NEG_BIG)
            m_old = m_scr[n]
            m_new = jnp.maximum(m_old, jnp.max(s, axis=1, keepdims=True))
            pexp = jnp.exp(s - m_new)
            alpha = jnp.exp(m_old - m_new)
            l_scr[n] = alpha * l_scr[n] + jnp.sum(pexp, axis=1, keepdims=True)
            acc_scr[n] = alpha * acc_scr[n] + _dot(pexp.astype(BF16), vals(n).astype(BF16))
            m_scr[n] = m_new

    def page_head(r, n):
        return r[pl.ds(n, PAGE_SIZE, stride=N_KV_B), :]

    @pl.when(j < nsteps)
    def _():
        width = pps * PAGE_SIZE
        x = scp_scr[:, pl.ds(pl.multiple_of(j * width, width), width)]
        attend(x,
               lambda n: jnp.concatenate([page_head(r, n) for r in kpage_refs], axis=0),
               lambda n: jnp.concatenate([page_head(r, n) for r in vpage_refs], axis=0))

    @pl.when(j == nsteps)
    def _():
        hs = lambda n: slice(n * HEAD_DIM_B, (n + 1) * HEAD_DIM_B)
        attend(scn_scr[...], lambda n: knew_ref[:, hs(n)], lambda n: vnew_ref[:, hs(n)])
        for n in range(N_KV_B):
            o_ref[n] = (acc_scr[n] / l_scr[n]).astype(o_ref.dtype)


def _dsa_sample_attn(page_table, scores, scores_new, q, cache_k, cache_v, layer, k_new, v_new):
    nseq, npages = page_table.shape
    t = scores.shape[1]
    past = npages * PAGE_SIZE
    topk = min(TOPK_MAX, (past + t) // 4)
    rows = GROUP_B * t
    pps = math.gcd(PAGES_PER_STEP, npages)
    nsteps = npages // pps
    page_block = (None, None, PAGE_SIZE * N_KV_B, HEAD_DIM_B)
    const3 = lambda b, j, pt: (b, 0, 0)
    return pl.pallas_call(
        functools.partial(_dsa_sample_attn_body, nsteps=nsteps, pps=pps, t=t, past=past, topk=topk,
                          tie_chunk=math.gcd(TIE_CHUNK, past)),
        grid_spec=pltpu.PrefetchScalarGridSpec(
            num_scalar_prefetch=1,
            grid=(nseq, nsteps + 1),
            in_specs=[pl.BlockSpec((None, t, past), const3),
                      pl.BlockSpec((None, t, PAGE_SIZE), const3),
                      pl.BlockSpec((None, N_KV_B, rows, HEAD_DIM_B), lambda b, j, pt: (b, 0, 0, 0)),
                      *_page_specs(page_block, layer, nsteps, pps),
                      *_page_specs(page_block, layer, nsteps, pps),
                      pl.BlockSpec((None, PAGE_SIZE, KV_WIDTH_B), const3),
                      pl.BlockSpec((None, PAGE_SIZE, KV_WIDTH_B), const3)],
            out_specs=pl.BlockSpec((None, N_KV_B, rows, HEAD_DIM_B), lambda b, j, pt: (b, 0, 0, 0)),
            scratch_shapes=[pltpu.VMEM((t, past), F32),
                            pltpu.VMEM((t, PAGE_SIZE), F32),
                            pltpu.VMEM((t, PAGE_SIZE), F32),
                            pltpu.VMEM((N_KV_B, rows, 1), F32),
                            pltpu.VMEM((N_KV_B, rows, 1), F32),
                            pltpu.VMEM((N_KV_B, rows, HEAD_DIM_B), F32)],
        ),
        out_shape=jax.ShapeDtypeStruct((nseq, N_KV_B, rows, HEAD_DIM_B), BF16),
        compiler_params=_cparams("parallel", "arbitrary"),
        name="dsa_sample_attn",
    )(page_table, scores, scores_new, q, *([cache_k] * pps), *([cache_v] * pps), k_new, v_new)


def _dsa_sample(ps, k_s, v_s, ki_s, page_table, cache_k, cache_v, cache_kidx, layer):
    nseq = page_table.shape[0]
    t = ps.shape[0] // nseq
    seg = lambda name, width: ps[:, PD_OFF[name]:PD_OFF[name] + width]
    qi = seg("qi", H_I * D_I).reshape(nseq, t * H_I, D_I)
    wi = seg("wi", H_I).reshape(nseq, t * H_I, 1)
    wv = jnp.broadcast_to(wi, (nseq, t * H_I, PAGE_SIZE))
    pad_rows = lambda a: jnp.pad(a.reshape(nseq, t, -1), ((0, 0), (0, PAGE_SIZE - t), (0, 0)))
    kidx_t = cache_kidx.transpose(0, 1, 3, 2)
    scores, scores_new = _dsa_sample_scores(page_table, qi, wv, kidx_t, layer,
                                            pad_rows(ki_s).transpose(0, 2, 1))
    q = seg("qb", WIDTH_B).reshape(nseq, t, N_KV_B, GROUP_B, HEAD_DIM_B)
    q = q.transpose(0, 2, 3, 1, 4).reshape(nseq, N_KV_B, GROUP_B * t, HEAD_DIM_B)
    pool_rows = lambda c: c.reshape(c.shape[0], c.shape[1], PAGE_SIZE * N_KV_B, HEAD_DIM_B)
    o = _dsa_sample_attn(page_table, scores, scores_new, q, pool_rows(cache_k), pool_rows(cache_v),
                         layer, pad_rows(k_s), pad_rows(v_s))
    o = o.reshape(nseq, N_KV_B, GROUP_B, t, HEAD_DIM_B).transpose(0, 3, 1, 2, 4)
    return o.reshape(nseq * t, WIDTH_B)


def kernel(x_prompt, x_sample, cache_k, cache_v, cache_kidx, page_table, state_gdn, state_gdn_conv,
           state_sconv, final_norm, norm1, norm2, w_in, conv_a, a_log, dt_bias, gdn_norm, conv_c,
           w_branch_a, w_branch_b, w_branch_c, w_o, w_gate, w_up, w_down):
    bp, tp, d = x_prompt.shape
    bs, ts = x_sample.shape[:2]
    mp, ms = bp * tp, bs * ts
    x = jnp.concatenate([x_prompt.reshape(mp, d), x_sample.reshape(ms, d)], axis=0)
    new_p = [[] for _ in range(6)]
    new_s = [[] for _ in range(6)]
    w_down16 = w_down.astype(BF16)
    for l in range(DEPTH):
        xn = _rmsnorm(x, norm1[l], BF16)
        p = _in_proj_main(xn, w_in, l)
        pd, k_new, v_new, ki_new = _in_proj_mid(xn, w_in, l)
        g, beta = _gdn_gates(pd, a_log[l], dt_bias[l])
        conv_w = conv_a[l].astype(F32)

        put_sample = lambda full, part: lax.dynamic_update_slice(full, part, (mp, 0))
        oa, gdn_p = _gdn(p, 0, bp, tp, min(GDN_CHUNK, tp), conv_w, g, beta,
                         jnp.zeros((bp, CONV_A - 1, 3 * WIDTH_A), F32),
                         jnp.zeros((bp, H_A, DK_A, DV_A), F32), gdn_norm[l], out_rows=mp + ms)
        oa_s, gdn_s = _gdn(p, mp, bs, ts, ts, conv_w, g, beta,
                           state_gdn_conv[l], state_gdn[l], gdn_norm[l])
        oa = put_sample(oa, oa_s)
        ob = _dsa_prompt(pd, k_new, v_new, ki_new, bp, tp, out_rows=mp + ms)
        ob = put_sample(ob, _dsa_sample(pd[mp:], k_new[mp:], v_new[mp:], ki_new[mp:], page_table,
                                        cache_k, cache_v, cache_kidx, l))
        cw = conv_c[l].astype(F32)
        oc, sconv_p = _sconv(p, 0, bp, tp, cw, jnp.zeros((bp, CONV_C - 1, WIDTH_C), F32),
                             out_rows=mp + ms)
        oc_s, sconv_s = _sconv(p, mp, bs, ts, cw, state_sconv[l])
        oc = put_sample(oc, oc_s)
        merged = _merge(oa, ob, oc, w_branch_a, w_branch_b, w_branch_c, l, p)
        x = _out_proj(merged, w_o, l, x)

        hn = _rmsnorm(x, norm2[l], BF16)
        h = _swiglu(hn, w_gate, w_up, l)
        x = _out_proj(h, w_down16, l, x, name="ffn_down")

        nb = CONV_A - 1
        tail_rows = lambda r0, t: lax.slice(p, (r0 + max(t - nb, 0), 0), (r0 + t, 3 * WIDTH_A))
        gconv_p = jnp.stack([tail_rows(b * tp, tp) for b in range(bp)])
        gconv_s = jnp.stack([tail_rows(mp + b * ts, ts) for b in range(bs)])
        if tp < nb:
            gconv_p = jnp.concatenate([jnp.zeros((bp, nb - tp, 3 * WIDTH_A), F32), gconv_p], axis=1)
        if ts < nb:
            gconv_s = jnp.concatenate([state_gdn_conv[l].astype(F32)[:, ts:], gconv_s], axis=1)
        kv = lambda a, rows, b, t: a[rows].reshape(b, t, N_KV_B, HEAD_DIM_B)
        rp, rs = slice(0, mp), slice(mp, mp + ms)
        for lst, val in zip(new_p, (kv(k_new, rp, bp, tp), kv(v_new, rp, bp, tp),
                                    ki_new[rp].reshape(bp, tp, D_I), gdn_p, gconv_p, sconv_p)):
            lst.append(val)
        for lst, val in zip(new_s, (kv(k_new, rs, bs, ts), kv(v_new, rs, bs, ts),
                                    ki_new[rs].reshape(bs, ts, D_I), gdn_s, gconv_s, sconv_s)):
            lst.append(val)

    y_prompt = _rmsnorm(x, final_norm, F32, 0, mp).reshape(bp, tp, d)
    y_sample = _rmsnorm(x, final_norm, F32, mp, ms).reshape(bs, ts, d)
    outs_p = [jnp.stack(a) for a in new_p]
    outs_s = [jnp.stack(a) for a in new_s]
    return (y_prompt, y_sample, *outs_p, *outs_s)
```

```python
import functools
import math

import jax
import jax.numpy as jnp
from jax import lax
from jax.experimental import pallas as pl
from jax.experimental.pallas import tpu as pltpu

F32 = jnp.float32
BF16 = jnp.bfloat16

D_MODEL = 4096
DEPTH = 2
PAGE_SIZE = 128
H_A = 16
DK_A = 128
DV_A = 128
WIDTH_A = H_A * DV_A
CONV_A = 4
GDN_CHUNK = 64
H_B = 8
N_KV_B = 2
GROUP_B = H_B // N_KV_B
HEAD_DIM_B = 128
WIDTH_B = H_B * HEAD_DIM_B
KV_WIDTH_B = N_KV_B * HEAD_DIM_B
H_I = 32
D_I = 64
TOPK_MAX = 256
Q_BLOCK = 128
WIDTH_C = 1024
CONV_C = 3
D_FF = -(-8 * D_MODEL // (3 * 256)) * 256
EPS = 1e-6

IN_NAMES = ("qa", "ka", "va", "za", "aa", "ba", "qb", "kb", "vb", "qi", "ki", "wi",
            "gate_b", "gate_c", "hc", "ga", "gb", "gc")
IN_SIZES = (WIDTH_A, WIDTH_A, WIDTH_A, WIDTH_A, H_A, H_A,
            WIDTH_B, KV_WIDTH_B, KV_WIDTH_B, H_I * D_I, D_I, H_I,
            WIDTH_C, WIDTH_C, WIDTH_C, D_MODEL, D_MODEL, D_MODEL)

LANES = 128
N_BLOCK = 512
VMEM_LIMIT = 56 * 1024 * 1024
ACT_VMEM_BYTES = 32 * 1024 * 1024
NEG_BIG = -1e30

MAIN_ORDER = ("qa", "ka", "va", "za", "gate_b", "gate_c", "hc", "ga", "gb", "gc")
MID_ORDER = ("qi", "qb", "kb", "vb", "ki", "aa", "ba", "wi")


def _offsets(order):
    sizes = dict(zip(IN_NAMES, IN_SIZES))
    off, out = 0, {}
    for name in order:
        out[name] = off
        off += sizes[name]
    return out, off


SRC_OFF, _ = _offsets(IN_NAMES)
PM_OFF, PM_WIDTH = _offsets(MAIN_ORDER)
PD_OFF, _mid_cols = _offsets(MID_ORDER)
PD_WIDTH = -(-_mid_cols // N_BLOCK) * N_BLOCK
MAIN_RUN0 = PM_OFF["gate_b"]
MAIN_SHIFT = SRC_OFF["gate_b"] - MAIN_RUN0
assert PM_WIDTH % N_BLOCK == 0 and MAIN_RUN0 % N_BLOCK == 0 and MAIN_SHIFT % LANES == 0
assert all(SRC_OFF[n] == PM_OFF[n] for n in MAIN_ORDER[:4])
assert all(SRC_OFF[n] == PM_OFF[n] + MAIN_SHIFT for n in MAIN_ORDER[4:])
TAIL_OFF = PD_OFF["ki"]
TAIL_AA = PD_OFF["aa"] - TAIL_OFF
TAIL_BA = PD_OFF["ba"] - TAIL_OFF
TAIL_WI = PD_OFF["wi"] - TAIL_OFF
assert TAIL_OFF % N_BLOCK == 0 and PD_OFF["wi"] + H_I - TAIL_OFF == LANES


def _cparams(*sem):
    return pltpu.CompilerParams(dimension_semantics=sem, vmem_limit_bytes=VMEM_LIMIT)


def _dot(a, b):
    return jnp.dot(a, b, preferred_element_type=F32)


def _dot_nt(a, b):
    return lax.dot_general(a, b, (((1,), (1,)), ((), ())), preferred_element_type=F32)


def _dot_tn(a, b):
    return lax.dot_general(a, b, (((0,), (0,)), ((), ())), preferred_element_type=F32)


def _dot_hi(a, b):
    return jnp.dot(a, b, preferred_element_type=F32, precision=lax.Precision.HIGHEST)


def _sigmoid(x):
    return jax.nn.sigmoid(x)


def _silu(x):
    return x * jax.nn.sigmoid(x)


def _rmsnorm_body(x_ref, g_ref, o_ref):
    x = x_ref[...]
    ms = jnp.mean(x * x, axis=-1, keepdims=True)
    o_ref[...] = (x * lax.rsqrt(ms + EPS) * g_ref[...]).astype(o_ref.dtype)


def _row_block(m):
    for bm in (256, 192, 128, 64, 32, 16, 8):
        if m % bm == 0:
            return bm
    raise ValueError(f"unsupported row count {m}")


def _rmsnorm(x, g, out_dtype, row0=0, nrows=None):
    d = x.shape[1]
    m = x.shape[0] - row0 if nrows is None else nrows
    bm = _row_block(math.gcd(m, row0) if row0 else m)
    rb0 = row0 // bm
    return pl.pallas_call(
        _rmsnorm_body,
        grid=(m // bm,),
        in_specs=[pl.BlockSpec((bm, d), lambda i: (rb0 + i, 0)), pl.BlockSpec((1, d), lambda i: (0, 0))],
        out_specs=pl.BlockSpec((bm, d), lambda i: (i, 0)),
        out_shape=jax.ShapeDtypeStruct((m, d), out_dtype),
        compiler_params=_cparams("parallel"),
        name="rmsnorm",
    )(x, g.reshape(1, d).astype(F32))


def _gdn_gates_body(t_ref, alog_ref, dtb_ref, g_ref, b_ref):
    t = t_ref[...]
    aa = t[:, TAIL_AA:TAIL_AA + H_A]
    ba = t[:, TAIL_BA:TAIL_BA + H_A]
    x = aa + dtb_ref[...]
    softplus = jnp.maximum(x, 0.0) + jnp.log1p(jnp.exp(-jnp.abs(x)))
    g_ref[...] = -jnp.exp(alog_ref[...]) * softplus
    b_ref[...] = _sigmoid(ba)


def _gdn_gates(p, a_log, dt_bias):
    m = p.shape[0]
    bm = _row_block(m)
    tail_blk = TAIL_OFF // LANES
    return pl.pallas_call(
        _gdn_gates_body,
        grid=(m // bm,),
        in_specs=[pl.BlockSpec((bm, LANES), lambda i: (i, tail_blk)),
                  pl.BlockSpec((1, H_A), lambda i: (0, 0)),
                  pl.BlockSpec((1, H_A), lambda i: (0, 0))],
        out_specs=[pl.BlockSpec((bm, H_A), lambda i: (i, 0)), pl.BlockSpec((bm, H_A), lambda i: (i, 0))],
        out_shape=[jax.ShapeDtypeStruct((m, H_A), F32)] * 2,
        compiler_params=_cparams("parallel"),
        name="gdn_gates",
    )(p, a_log.reshape(1, H_A).astype(F32), dt_bias.reshape(1, H_A).astype(F32))


def _big_row_block(m, k=D_MODEL, buffers=2):
    for bm in (2064, 1376, 1024, 688, 512, 256, 128, 64):
        if m % bm == 0 and buffers * bm * k * 2 <= ACT_VMEM_BYTES:
            return bm
    raise ValueError(f"unsupported row count {m}")


def _resident_rows(bm, k, buffers=2):
    if buffers == 1:
        return pl.BlockSpec((bm, k), lambda i, j: (i, 0), pipeline_mode=pl.Buffered(1))
    return pl.BlockSpec((bm, k), lambda i, j: (i, 0))


def _out_proj_body(a_ref, w_ref, r_ref, o_ref):
    o_ref[...] = r_ref[...] + _dot(a_ref[...], w_ref[...].astype(BF16))


def _out_proj(a, w, layer, residual, name="out_proj"):
    m, k = a.shape
    n = w.shape[2]
    bm, bn = _big_row_block(m, k), N_BLOCK // 2
    return pl.pallas_call(
        _out_proj_body,
        grid=(m // bm, n // bn),
        in_specs=[_resident_rows(bm, k),
                  pl.BlockSpec((None, k, bn), lambda i, j: (layer, 0, j)),
                  pl.BlockSpec((bm, bn), lambda i, j: (i, j))],
        out_specs=pl.BlockSpec((bm, bn), lambda i, j: (i, j)),
        out_shape=jax.ShapeDtypeStruct((m, n), F32),
        compiler_params=_cparams("parallel", "arbitrary"),
        name=name,
    )(a, w, residual)


IN_PROJ_PIECES = N_BLOCK // LANES


def _in_proj_main_body(a_ref, *rest):
    w_refs, p_ref = rest[:IN_PROJ_PIECES], rest[IN_PROJ_PIECES]
    w = jnp.concatenate([r[...].astype(BF16) for r in w_refs], axis=1)
    p_ref[...] = _dot(a_ref[...], w)


def _in_proj_main(a, w_in, layer):
    m, k = a.shape
    bm, bn = _big_row_block(m, k, buffers=1), N_BLOCK
    run0_blk, shift_blk = MAIN_RUN0 // bn, MAIN_SHIFT // LANES

    def piece(q):
        def index(i, j):
            return (layer, 0, IN_PROJ_PIECES * j + q + jnp.where(j >= run0_blk, shift_blk, 0))
        return pl.BlockSpec((None, k, LANES), index)

    return pl.pallas_call(
        _in_proj_main_body,
        grid=(m // bm, PM_WIDTH // bn),
        in_specs=[_resident_rows(bm, k, buffers=1)] + [piece(q) for q in range(IN_PROJ_PIECES)],
        out_specs=pl.BlockSpec((bm, bn), lambda i, j: (i, j)),
        out_shape=jax.ShapeDtypeStruct((m, PM_WIDTH), F32),
        compiler_params=_cparams("parallel", "arbitrary"),
        name="in_proj_main",
    )(a, *([w_in] * IN_PROJ_PIECES))


def _in_proj_mid_body(a_ref, w_ref, p_ref, k_ref, v_ref, ki_ref, *, jkv, jtail):
    acc = _dot(a_ref[...], w_ref[...])
    p_ref[...] = acc
    j = pl.program_id(1)

    @pl.when(j == jkv)
    def _():
        k_ref[...] = acc[:, :KV_WIDTH_B]
        v_ref[...] = acc[:, KV_WIDTH_B:2 * KV_WIDTH_B]

    @pl.when(j == jtail)
    def _():
        ki_ref[...] = acc[:, :D_I]


def _in_proj_mid(a, w_mid, layer):
    m, k = a.shape
    bm, bn = _big_row_block(m), N_BLOCK
    assert PD_OFF["kb"] % bn == 0 and PD_OFF["vb"] == PD_OFF["kb"] + KV_WIDTH_B
    row = lambda i, j: (i, 0)
    return pl.pallas_call(
        functools.partial(_in_proj_mid_body, jkv=PD_OFF["kb"] // bn, jtail=TAIL_OFF // bn),
        grid=(m // bm, PD_WIDTH // bn),
        in_specs=[_resident_rows(bm, k), pl.BlockSpec((None, k, bn), lambda i, j: (layer, 0, j))],
        out_specs=[pl.BlockSpec((bm, bn), lambda i, j: (i, j)),
                   pl.BlockSpec((bm, KV_WIDTH_B), row), pl.BlockSpec((bm, KV_WIDTH_B), row),
                   pl.BlockSpec((bm, D_I), row)],
        out_shape=[jax.ShapeDtypeStruct((m, PD_WIDTH), F32), jax.ShapeDtypeStruct((m, KV_WIDTH_B), F32),
                   jax.ShapeDtypeStruct((m, KV_WIDTH_B), F32), jax.ShapeDtypeStruct((m, D_I), F32)],
        compiler_params=_cparams("parallel", "arbitrary"),
        name="in_proj_mid",
    )(a, w_mid)


def _swiglu_body(a_ref, wg_ref, wu_ref, o_ref):
    a = a_ref[...]
    g = _dot(a, wg_ref[...].astype(BF16))
    u = _dot(a, wu_ref[...].astype(BF16))
    o_ref[...] = (_silu(g) * u).astype(o_ref.dtype)


def _swiglu(a, w_gate, w_up, layer):
    m, k = a.shape
    n = w_gate.shape[2]
    bm, bn = _big_row_block(m, k, buffers=1), N_BLOCK // 2
    assert n % bn == 0
    wspec = pl.BlockSpec((None, k, bn), lambda i, j: (layer, 0, j))
    return pl.pallas_call(
        _swiglu_body,
        grid=(m // bm, n // bn),
        in_specs=[_resident_rows(bm, k, buffers=1), wspec, wspec],
        out_specs=pl.BlockSpec((bm, bn), lambda i, j: (i, j)),
        out_shape=jax.ShapeDtypeStruct((m, n), BF16),
        compiler_params=_cparams("parallel", "arbitrary"),
        name="swiglu",
    )(a, w_gate, w_up)


def _merge_body(oa_ref, ob_ref, oc_ref, wa_ref, wb_ref, wc_ref, ga_ref, gb_ref, gc_ref, o_ref):
    acc = _sigmoid(ga_ref[...]) * _dot(oa_ref[...], wa_ref[...].astype(BF16))
    acc += _sigmoid(gb_ref[...]) * _dot(ob_ref[...], wb_ref[...].astype(BF16))
    acc += _sigmoid(gc_ref[...]) * _dot(oc_ref[...], wc_ref[...].astype(BF16))
    o_ref[...] = acc.astype(o_ref.dtype)


def _merge(oa, ob, oc, wa, wb, wc, layer, p):
    m = oa.shape[0]
    n = wa.shape[2]
    bm, bn = _big_row_block(m), N_BLOCK // 2
    ga0, gb0, gc0 = (PM_OFF[s] // bn for s in ("ga", "gb", "gc"))
    row = lambda i, j: (i, 0)
    wspec = lambda w: pl.BlockSpec((None, w.shape[1], bn), lambda i, j: (layer, 0, j))
    return pl.pallas_call(
        _merge_body,
        grid=(m // bm, n // bn),
        in_specs=[_resident_rows(bm, oa.shape[1]), _resident_rows(bm, ob.shape[1]),
                  _resident_rows(bm, oc.shape[1]),
                  wspec(wa), wspec(wb), wspec(wc),
                  pl.BlockSpec((bm, bn), lambda i, j: (i, ga0 + j)),
                  pl.BlockSpec((bm, bn), lambda i, j: (i, gb0 + j)),
                  pl.BlockSpec((bm, bn), lambda i, j: (i, gc0 + j))],
        out_specs=pl.BlockSpec((bm, bn), lambda i, j: (i, j)),
        out_shape=jax.ShapeDtypeStruct((m, n), BF16),
        compiler_params=_cparams("parallel", "arbitrary"),
        name="merge",
    )(oa, ob, oc, wa, wb, wc, p, p, p)


GDN_HEADS_PER_STEP = 16
GDN_INV_BASE = 8


def _gdn_body(q_ref, k_ref, v_ref, z_ref, wq_ref, wk_ref, wv_ref, tq_ref, tk_ref, tv_ref,
              g_ref, b_ref, s0_ref, ng_ref, o_ref, sout_ref, s_scr, tail_scr,
              *, chunk, hg):
    c = chunk
    w = hg * DK_A
    n = pl.program_id(2)

    @pl.when(n == 0)
    def _():
        s_scr[...] = s0_ref[...].astype(F32)
        tail_scr[:, 0:w] = tq_ref[...]
        tail_scr[:, w:2 * w] = tk_ref[...]
        tail_scr[:, 2 * w:3 * w] = tv_ref[...]

    act = []
    for idx, (x_ref, w_ref) in enumerate(((q_ref, wq_ref), (k_ref, wk_ref), (v_ref, wv_ref))):
        x = x_ref[...]
        cw = w_ref[...]
        xfull = jnp.concatenate([tail_scr[:, idx * w:(idx + 1) * w], x], axis=0)
        y = x * cw[CONV_A - 1:CONV_A, :]
        for j in range(1, CONV_A):
            y = y + pltpu.roll(xfull, j, axis=0)[8:, :] * cw[CONV_A - 1 - j:CONV_A - j, :]
        tail_scr[:, idx * w:(idx + 1) * w] = xfull[c:c + 8, :]
        act.append(_silu(y))
    qs, ks, vs = act
    z = z_ref[...]

    ii = lax.broadcasted_iota(jnp.int32, (c, c), 0)
    jj = lax.broadcasted_iota(jnp.int32, (c, c), 1)
    incl = jj <= ii
    strict = jj < ii
    eye = (ii == jj).astype(F32)
    ng = ng_ref[...]
    b_blk = b_ref[...]
    gcum_all = _dot_hi(incl.astype(F32), g_ref[...])
    gcum_t = gcum_all.T

    heads = range(hg)
    hsl = [slice(hh * DK_A, (hh + 1) * DK_A) for hh in heads]
    q_l, k_l, k16_l, decay_l, egc_l, kdec_l, glast_l, mm_l, rhs_l = ([] for _ in range(9))
    for hh in heads:
        q = qs[:, hsl[hh]]
        k = ks[:, hsl[hh]]
        q = q * lax.rsqrt(jnp.sum(q * q, axis=-1, keepdims=True) + EPS) * (DK_A ** -0.5)
        k = k * lax.rsqrt(jnp.sum(k * k, axis=-1, keepdims=True) + EPS)
        bc = b_blk[:, hh:hh + 1]
        gcum_c = gcum_all[:, hh:hh + 1]
        gcum_r = gcum_t[hh:hh + 1, :]
        decay = jnp.exp(jnp.where(incl, gcum_c - gcum_r, -jnp.inf))
        kb = k * bc
        k16 = k.astype(BF16)
        egc = jnp.exp(gcum_c)
        glast = gcum_c[c - 1:c, :]
        mm = jnp.where(strict, _dot_nt(kb.astype(BF16), k16) * decay, 0.0)
        q_l.append(q)
        k_l.append(k)
        k16_l.append(k16)
        decay_l.append(decay)
        egc_l.append(egc)
        glast_l.append(glast)
        kdec_l.append(jnp.exp(glast - gcum_c))
        rhs_l.append(jnp.concatenate([vs[:, hsl[hh]] * bc, kb * egc], axis=1).astype(BF16))
        mm_l.append(mm)
    base = min(c, GDN_INV_BASE)
    log_base = base.bit_length() - 1
    in_base = jnp.right_shift(ii, log_base) == jnp.right_shift(jj, log_base)
    pw_l = [jnp.where(in_base, -mm, 0.0) for mm in mm_l]
    tinv_l = [eye + pw for pw in pw_l]
    for _ in range(log_base - 1):
        for hh in heads:
            pw16 = pw_l[hh].astype(BF16)
            pw_l[hh] = _dot(pw16, pw16)
        for hh in heads:
            tinv_l[hh] = tinv_l[hh] + _dot(tinv_l[hh].astype(BF16), pw_l[hh].astype(BF16))
    log_size = log_base
    while (1 << log_size) < c:
        lower_left = jnp.logical_and(
            jnp.right_shift(ii, log_size + 1) == jnp.right_shift(jj, log_size + 1),
            jnp.logical_and(jnp.bitwise_and(jnp.right_shift(ii, log_size), 1) == 1,
                            jnp.bitwise_and(jnp.right_shift(jj, log_size), 1) == 0))
        for hh in heads:
            t16 = tinv_l[hh].astype(BF16)
            c16 = jnp.where(lower_left, mm_l[hh], 0.0).astype(BF16)
            tinv_l[hh] = tinv_l[hh] - _dot(_dot(t16, c16).astype(BF16), t16)
        log_size += 1
    sol_l = [_dot(tinv_l[hh].astype(BF16), rhs_l[hh]) for hh in heads]
    attn_l = [(_dot_nt(q_l[hh].astype(BF16), k16_l[hh]) * decay_l[hh]).astype(BF16) for hh in heads]
    s_l = [s_scr[hh] for hh in heads]
    s16_l = [s_l[hh].astype(BF16) for hh in heads]
    v16_l = [(sol_l[hh][:, :DV_A] - _dot(sol_l[hh][:, DV_A:].astype(BF16), s16_l[hh])).astype(BF16)
             for hh in heads]
    o_l = [_dot((q_l[hh] * egc_l[hh]).astype(BF16), s16_l[hh]) + _dot(attn_l[hh], v16_l[hh]) for hh in heads]
    snew_l = [s_l[hh] * jnp.exp(glast_l[hh]) + _dot_tn((k_l[hh] * kdec_l[hh]).astype(BF16), v16_l[hh])
              for hh in heads]
    s_scr[...] = jnp.stack(snew_l, axis=0)
    o_l = [o * lax.rsqrt(jnp.mean(o * o, axis=-1, keepdims=True) + EPS) * ng for o in o_l]
    o_ref[...] = (jnp.concatenate(o_l, axis=1) * _silu(z)).astype(o_ref.dtype)

    @pl.when(n == pl.num_programs(2) - 1)
    def _():
        sout_ref[...] = s_scr[...].astype(sout_ref.dtype)


def _gdn(p, row0, nseq, t, chunk, conv_w, g, beta, conv_buf, s0, norm_g, out_rows=None):
    hg = GDN_HEADS_PER_STEP
    assert hg == H_A
    w = hg * DK_A
    nch = t // chunk
    nhg = H_A // hg
    rb0 = row0 // chunk
    gspec = pl.BlockSpec((chunk, H_A), lambda b, h, n: (rb0 + b * nch + n, 0))
    tail = jnp.pad(conv_buf.astype(F32), ((0, 0), (8 - (CONV_A - 1), 0), (0, 0)))
    qb0, kb0, vb0, zb0 = (PM_OFF[s] // w for s in ("qa", "ka", "va", "za"))
    pspec = lambda b0: pl.BlockSpec((chunk, w), lambda b, h, n: (rb0 + b * nch + n, b0 + h))
    wspec = lambda b0: pl.BlockSpec((CONV_A, w), lambda b, h, n: (0, b0 + h))
    tspec = lambda b0: pl.BlockSpec((None, 8, w), lambda b, h, n: (b, 0, b0 + h))
    o, s_out = pl.pallas_call(
        functools.partial(_gdn_body, chunk=chunk, hg=hg),
        grid=(nseq, nhg, nch),
        in_specs=[pspec(qb0), pspec(kb0), pspec(vb0), pspec(zb0),
                  wspec(0), wspec(nhg), wspec(2 * nhg),
                  tspec(0), tspec(nhg), tspec(2 * nhg),
                  gspec, gspec,
                  pl.BlockSpec((None, hg, DK_A, DV_A), lambda b, h, n: (b, h, 0, 0)),
                  pl.BlockSpec((1, DV_A), lambda b, h, n: (0, 0))],
        out_specs=[pl.BlockSpec((chunk, w), lambda b, h, n: (b * nch + n, h)),
                   pl.BlockSpec((None, hg, DK_A, DV_A), lambda b, h, n: (b, h, 0, 0))],
        out_shape=[jax.ShapeDtypeStruct((out_rows or nseq * t, WIDTH_A), BF16),
                   jax.ShapeDtypeStruct((nseq, H_A, DK_A, DV_A), s0.dtype)],
        scratch_shapes=[pltpu.VMEM((hg, DK_A, DV_A), F32), pltpu.VMEM((8, 3 * w), F32)],
        compiler_params=_cparams("parallel", "parallel", "arbitrary"),
        name=f"gdn_c{chunk}",
    )(p, p, p, p, conv_w, conv_w, conv_w, tail, tail, tail, g, beta, s0,
      norm_g.reshape(1, DV_A).astype(F32))
    return o, s_out


SCONV_COLS = 256


def _sconv_body(gb_ref, gc_ref, hc_ref, w_ref, buf_ref, o_ref, st_ref, *, t):
    pr = gc_ref[...] * hc_ref[...]
    cw = w_ref[...]
    xfull = jnp.concatenate([buf_ref[...], pr], axis=0)
    y = pr * cw[CONV_C - 1:CONV_C, :]
    for j in range(1, CONV_C):
        y = y + pltpu.roll(xfull, j, axis=0)[8:, :] * cw[CONV_C - 1 - j:CONV_C - j, :]
    o_ref[...] = (gb_ref[...] * y).astype(o_ref.dtype)
    st_ref[...] = xfull[t:t + 8, :]


def _sconv(p, row0, nseq, t, conv_w, buf, out_rows=None):
    cols = SCONV_COLS
    ncb = WIDTH_C // cols
    rb0 = row0 // t
    b0, c0, h0 = (PM_OFF[s] // cols for s in ("gate_b", "gate_c", "hc"))
    buf8 = jnp.pad(buf.astype(F32), ((0, 0), (8 - (CONV_C - 1), 0), (0, 0)))
    pspec = lambda o: pl.BlockSpec((t, cols), lambda b, j: (rb0 + b, o + j))
    o, st = pl.pallas_call(
        functools.partial(_sconv_body, t=t),
        grid=(nseq, ncb),
        in_specs=[pspec(b0), pspec(c0), pspec(h0),
                  pl.BlockSpec((CONV_C, cols), lambda b, j: (0, j)),
                  pl.BlockSpec((None, 8, cols), lambda b, j: (b, 0, j))],
        out_specs=[pl.BlockSpec((t, cols), lambda b, j: (b, j)),
                   pl.BlockSpec((None, 8, cols), lambda b, j: (b, 0, j))],
        out_shape=[jax.ShapeDtypeStruct((out_rows or nseq * t, WIDTH_C), BF16),
                   jax.ShapeDtypeStruct((nseq, 8, WIDTH_C), F32)],
        compiler_params=_cparams("parallel", "parallel"),
        name=f"sconv_t{t}",
    )(p, p, p, conv_w, buf8)
    return o, st[:, 8 - (CONV_C - 1):, :]


BISECT_EVERY = 8
MAX_SEARCH = 2400


def _select_threshold(count_ge, lo0, hi0, n_valid, k):
    kf = jnp.float32(k)

    def cond(st):
        return jnp.logical_and(st[0] < MAX_SEARCH, jnp.min(st[-1]) < 0.5)

    def step(st):
        it, lo, hi, glo, ghi, t, side, done = st
        half = 0.5 * lo + 0.5 * hi
        cand = lo + (hi - lo) * (glo / jnp.maximum(glo - ghi, 1e-9))
        mid = jnp.where(it % BISECT_EVERY == BISECT_EVERY - 1, half, cand)
        mid = jnp.where(jnp.logical_and(mid > lo, mid < hi), mid, half)
        adjacent = jnp.logical_not(jnp.logical_and(mid > lo, mid < hi))
        first = it == 0
        mid = jnp.where(first, hi, mid)
        adjacent = jnp.logical_and(adjacent, jnp.logical_not(first))
        g = count_ge(mid) - kf
        ge = g >= 0.0
        finish = jnp.logical_or(jnp.logical_or(g == 0.0, adjacent), jnp.logical_and(first, ge))
        newly = jnp.logical_and(finish, done < 0.5)
        t = jnp.where(newly, jnp.where(adjacent, lo, mid), t)
        new_side = jnp.where(ge, 1.0, -1.0)
        damp = jnp.where(jnp.logical_and(new_side == side, jnp.logical_not(first)), 0.5, 1.0)
        glo = jnp.where(ge, g, glo * damp)
        ghi = jnp.where(ge, ghi * damp, g)
        lo = jnp.where(ge, mid, lo)
        hi = jnp.where(ge, hi, mid)
        done = jnp.where(finish, 1.0, done)
        return it + 1, lo, hi, glo, ghi, t, new_side, done

    done0 = jnp.where(n_valid > kf, 0.0, 1.0)
    st = lax.while_loop(cond, lambda st: step(step(st)), (jnp.int32(0), lo0, hi0, n_valid - kf, jnp.full_like(lo0, -kf),
                                     lo0, jnp.zeros_like(lo0), done0))
    return st[5]


KEY_CHUNK = 512


def _dsa_prompt_body(qi_ref, tq_ref, qb_ref, kb_ref, vb_ref, tk_ref, o_ref,
                     sc_scr, m_scr, l_scr, acc_scr, *, tq, sc, topk):
    i = pl.program_id(1)
    nck = (i * tq + tq - 1) // sc + 1
    wt = tq_ref[...].T[TAIL_WI:TAIL_WI + H_I, :] * ((D_I ** -0.5) * (H_I ** -0.5))
    qi = qi_ref[...]
    pairs = [jnp.concatenate([qi[:, (2 * p) * D_I:(2 * p + 1) * D_I],
                              qi[:, (2 * p + 1) * D_I:(2 * p + 2) * D_I]], axis=0).astype(BF16)
             for p in range(H_I // 2)]
    tpos = i * tq + lax.broadcasted_iota(jnp.int32, (1, tq), 1)

    def score_chunk(c, carry):
        r0 = pl.multiple_of(c * sc, sc)
        kic = tk_ref[pl.ds(r0, sc), :].astype(BF16)
        acc = jnp.zeros((sc, tq), F32)
        for p in range(H_I // 2):
            d = _dot_nt(kic, pairs[p])
            acc = acc + jnp.maximum(d[:, :tq], 0.0) * wt[2 * p:2 * p + 1, :]
            acc = acc + jnp.maximum(d[:, tq:], 0.0) * wt[2 * p + 1:2 * p + 2, :]
        kpos = r0 + lax.broadcasted_iota(jnp.int32, (sc, tq), 0)
        sc_scr[pl.ds(r0, sc), :] = jnp.where(kpos <= tpos, acc, -jnp.inf)
        return carry

    lax.fori_loop(0, nck, score_chunk, 0)

    def minmax_chunk(c, carry):
        mn, mx = carry
        x = sc_scr[pl.ds(pl.multiple_of(c * sc, sc), sc), :]
        mx = jnp.maximum(mx, jnp.max(x, axis=0, keepdims=True))
        mn = jnp.minimum(mn, jnp.min(jnp.where(x == -jnp.inf, jnp.inf, x), axis=0, keepdims=True))
        return mn, mx

    mn, mx = lax.fori_loop(0, nck, minmax_chunk,
                           (jnp.full((1, tq), jnp.inf, F32), jnp.full((1, tq), -jnp.inf, F32)))

    def count_ge(mid):
        def body(c, acc):
            x = sc_scr[pl.ds(pl.multiple_of(c * sc, sc), sc), :]
            hit = jnp.where(x >= mid, 1.0, 0.0).reshape(8, sc // 64, 8, tq)
            return acc + jnp.sum(jnp.sum(hit, axis=1), axis=0)
        return jnp.sum(lax.fori_loop(0, nck, body, jnp.zeros((8, tq), F32)), axis=0, keepdims=True)

    thr = _select_threshold(count_ge, mn, mx, (tpos + 1).astype(F32), topk)

    def count_gt(c, acc):
        x = sc_scr[pl.ds(pl.multiple_of(c * sc, sc), sc), :]
        hit = jnp.where(x > thr, 1.0, 0.0).reshape(8, sc // 64, 8, tq)
        return acc + jnp.sum(jnp.sum(hit, axis=1), axis=0)
    n_above = jnp.sum(lax.fori_loop(0, nck, count_gt, jnp.zeros((8, tq), F32)), axis=0, keepdims=True)
    crowded = count_ge(thr) > topk
    places = topk - n_above

    @pl.when(jnp.max(jnp.where(crowded, 1.0, 0.0)) > 0.5)
    def _():
        earlier = (lax.broadcasted_iota(jnp.int32, (sc, sc), 1)
                   < lax.broadcasted_iota(jnp.int32, (sc, sc), 0)).astype(BF16)

        def retire(c, seen):
            rows = pl.ds(pl.multiple_of(c * sc, sc), sc)
            x = sc_scr[rows, :]
            tie = jnp.logical_and(x == thr, crowded)
            tie_f = jnp.where(tie, 1.0, 0.0)
            rank = seen + _dot(earlier, tie_f.astype(BF16))
            sc_scr[rows, :] = jnp.where(jnp.logical_and(tie, rank >= places), -jnp.inf, x)
            return seen + jnp.sum(tie_f, axis=0, keepdims=True)

        lax.fori_loop(0, nck, retire, jnp.zeros((1, tq), F32))

    m_scr[...] = jnp.full(m_scr.shape, NEG_BIG, F32)
    l_scr[...] = jnp.zeros(l_scr.shape, F32)
    acc_scr[...] = jnp.zeros(acc_scr.shape, F32)
    q = qb_ref[...].astype(BF16)
    scale = HEAD_DIM_B ** -0.5

    def attn_chunk(c, carry):
        r0 = pl.multiple_of(c * sc, sc)
        sel = sc_scr[pl.ds(r0, sc), :] >= thr
        for n in range(N_KV_B):
            kc = kb_ref[pl.ds(r0, sc), n * HEAD_DIM_B:(n + 1) * HEAD_DIM_B].astype(BF16)
            vc = vb_ref[pl.ds(r0, sc), n * HEAD_DIM_B:(n + 1) * HEAD_DIM_B].astype(BF16)
            for g in range(GROUP_B):
                h = n * GROUP_B + g
                hs = slice(h * HEAD_DIM_B, (h + 1) * HEAD_DIM_B)
                s = jnp.where(sel, _dot_nt(kc, q[:, hs]) * scale, NEG_BIG)
                m_old = m_scr[h:h + 1, :]
                m_new = jnp.maximum(m_old, jnp.max(s, axis=0, keepdims=True))
                pexp = jnp.exp(s - m_new)
                alpha = jnp.exp(m_old - m_new)
                l_scr[h:h + 1, :] = alpha * l_scr[h:h + 1, :] + jnp.sum(pexp, axis=0, keepdims=True)
                acc_scr[hs, :] = alpha * acc_scr[hs, :] + _dot_tn(vc, pexp.astype(BF16))
                m_scr[h:h + 1, :] = m_new
        return carry

    lax.fori_loop(0, nck, attn_chunk, 0)

    for h in range(H_B):
        hs = slice(h * HEAD_DIM_B, (h + 1) * HEAD_DIM_B)
        ot = acc_scr[hs, :] / l_scr[h:h + 1, :]
        o_ref[:, hs] = ot.T.astype(o_ref.dtype)


def _dsa_prompt(p, k_new, v_new, ki_new, nseq, s, out_rows=None):
    tq, sc = Q_BLOCK, min(KEY_CHUNK, s)
    topk = min(TOPK_MAX, s // 4)
    nqb = s // tq
    qi0 = PD_OFF["qi"] // (H_I * D_I)
    qb0 = PD_OFF["qb"] // WIDTH_B
    tl0 = TAIL_OFF // LANES
    return pl.pallas_call(
        functools.partial(_dsa_prompt_body, tq=tq, sc=sc, topk=topk),
        grid=(nseq, nqb),
        in_specs=[pl.BlockSpec((tq, H_I * D_I), lambda b, i: (b * nqb + i, qi0)),
                  pl.BlockSpec((tq, LANES), lambda b, i: (b * nqb + i, tl0)),
                  pl.BlockSpec((tq, WIDTH_B), lambda b, i: (b * nqb + i, qb0)),
                  pl.BlockSpec((s, KV_WIDTH_B), lambda b, i: (b, 0)),
                  pl.BlockSpec((s, KV_WIDTH_B), lambda b, i: (b, 0)),
                  pl.BlockSpec((s, D_I), lambda b, i: (b, 0))],
        out_specs=pl.BlockSpec((tq, WIDTH_B), lambda b, i: (b * nqb + i, 0)),
        out_shape=jax.ShapeDtypeStruct((out_rows or nseq * s, WIDTH_B), BF16),
        scratch_shapes=[pltpu.VMEM((s, tq), F32), pltpu.VMEM((H_B, tq), F32),
                        pltpu.VMEM((H_B, tq), F32), pltpu.VMEM((WIDTH_B, tq), F32)],
        compiler_params=_cparams("parallel", "arbitrary"),
        name="dsa_prompt",
    )(p, p, p, k_new, v_new, ki_new)


PAGES_PER_STEP = 32
TIE_CHUNK = 512


def _dsa_sample_scores_body(pt_ref, q_ref, w_ref, *rest, nsteps, pps, t):
    page_refs, new_ref, o_ref, onew_ref = rest[:pps], rest[pps], rest[pps + 1], rest[pps + 2]
    j = pl.program_id(1)
    q = q_ref[...].astype(BF16)
    wv = w_ref[...] * ((D_I ** -0.5) * (H_I ** -0.5))

    def scores(keys_t):
        d = _dot(q, keys_t.astype(BF16))
        return jnp.sum((jnp.maximum(d, 0.0) * wv).reshape(t, H_I, PAGE_SIZE), axis=1)

    @pl.when(j < nsteps)
    def _():
        o_ref[...] = jnp.concatenate([scores(r[...]) for r in page_refs], axis=1)

    @pl.when(j == nsteps)
    def _():
        knew = lax.broadcasted_iota(jnp.int32, (t, PAGE_SIZE), 1)
        tnew = lax.broadcasted_iota(jnp.int32, (t, PAGE_SIZE), 0)
        onew_ref[...] = jnp.where(knew <= tnew, scores(new_ref[...]), -jnp.inf)


def _page_specs(block, layer, nsteps, pps):
    def spec(p):
        def index(b, j, pt):
            return (layer, pt[b, jnp.minimum(j, nsteps - 1) * pps + p]) + (0,) * (len(block) - 2)
        return pl.BlockSpec(block, index)
    return [spec(p) for p in range(pps)]


def _dsa_sample_scores(page_table, q, wv, cache_kidx, layer, ki_new):
    nseq, npages = page_table.shape
    t = q.shape[1] // H_I
    pps = math.gcd(PAGES_PER_STEP, npages)
    nsteps = npages // pps
    return pl.pallas_call(
        functools.partial(_dsa_sample_scores_body, nsteps=nsteps, pps=pps, t=t),
        grid_spec=pltpu.PrefetchScalarGridSpec(
            num_scalar_prefetch=1,
            grid=(nseq, nsteps + 1),
            in_specs=[pl.BlockSpec((None, t * H_I, D_I), lambda b, j, pt: (b, 0, 0)),
                      pl.BlockSpec((None, t * H_I, PAGE_SIZE), lambda b, j, pt: (b, 0, 0)),
                      *_page_specs((None, None, D_I, PAGE_SIZE), layer, nsteps, pps),
                      pl.BlockSpec((None, D_I, PAGE_SIZE), lambda b, j, pt: (b, 0, 0))],
            out_specs=[pl.BlockSpec((None, t, pps * PAGE_SIZE),
                                    lambda b, j, pt: (b, 0, jnp.minimum(j, nsteps - 1))),
                       pl.BlockSpec((None, t, PAGE_SIZE), lambda b, j, pt: (b, 0, 0))],
        ),
        out_shape=[jax.ShapeDtypeStruct((nseq, t, npages * PAGE_SIZE), F32),
                   jax.ShapeDtypeStruct((nseq, t, PAGE_SIZE), F32)],
        compiler_params=_cparams("parallel", "arbitrary"),
        name="dsa_sample_scores",
    )(page_table, q, wv, *([cache_kidx] * pps), ki_new)


def _dsa_sample_attn_body(pt_ref, sc_ref, scn_ref, q_ref, *rest, nsteps, pps, t, past, topk, tie_chunk):
    kpage_refs, vpage_refs = rest[:pps], rest[pps:2 * pps]
    knew_ref, vnew_ref, o_ref, scp_scr, scn_scr, thr_scr, m_scr, l_scr, acc_scr = rest[2 * pps:]
    j = pl.program_id(1)

    @pl.when(j == 0)
    def _():
        xp, xn = sc_ref[...], scn_ref[...]
        scp_scr[...] = xp
        scn_scr[...] = xn
        mx = jnp.maximum(jnp.max(xp, axis=1, keepdims=True), jnp.max(xn, axis=1, keepdims=True))
        mn = jnp.minimum(jnp.min(xp, axis=1, keepdims=True),
                         jnp.min(jnp.where(xn == -jnp.inf, jnp.inf, xn), axis=1, keepdims=True))
        cnt = lambda hit: jnp.sum(jnp.where(hit, 1.0, 0.0), axis=1, keepdims=True)

        def count_ge(mid):
            return cnt(sc_ref[...] >= mid) + cnt(scn_ref[...] >= mid)

        n_valid = (past + 1 + lax.broadcasted_iota(jnp.int32, (t, 1), 0)).astype(F32)
        thr = _select_threshold(count_ge, mn, mx, n_valid, topk)

        crowded = count_ge(thr) > topk
        places = topk - (cnt(xp > thr) + cnt(xn > thr))

        @pl.when(jnp.max(jnp.where(crowded, 1.0, 0.0)) > 0.5)
        def _():
            cw = tie_chunk
            earlier = (lax.broadcasted_iota(jnp.int32, (cw, cw), 0)
                       < lax.broadcasted_iota(jnp.int32, (cw, cw), 1)).astype(BF16)

            def retire(x, seen, width):
                tie = jnp.logical_and(x == thr, crowded)
                tie_f = jnp.where(tie, 1.0, 0.0)
                rank = seen + _dot(tie_f.astype(BF16), earlier[:width, :width])
                return (jnp.where(jnp.logical_and(tie, rank >= places), -jnp.inf, x),
                        seen + jnp.sum(tie_f, axis=1, keepdims=True))

            def past_chunk(c, seen):
                cols = pl.ds(pl.multiple_of(c * cw, cw), cw)
                x, seen = retire(scp_scr[:, cols], seen, cw)
                scp_scr[:, cols] = x
                return seen

            seen = lax.fori_loop(0, past // cw, past_chunk, jnp.zeros((t, 1), F32))
            scn_scr[...] = retire(scn_scr[...], seen, PAGE_SIZE)[0]

        thr_scr[...] = jnp.broadcast_to(thr, thr_scr.shape)
        m_scr[...] = jnp.full(m_scr.shape, NEG_BIG, F32)
        l_scr[...] = jnp.zeros(l_scr.shape, F32)
        acc_scr[...] = jnp.zeros(acc_scr.shape, F32)

    scale = HEAD_DIM_B ** -0.5

    def attend(x, keys, vals):
        sel_t = jnp.where(x >= thr_scr[:, 0:1], 1.0, 0.0)
        sel = jnp.concatenate([sel_t] * GROUP_B, axis=0) > 0.5
        for n in range(N_KV_B):
            s = jnp.where(sel, _dot_nt(q_ref[n].astype(BF16), keys(n).astype(BF16)) * scale, NEG_BIG)
            m_old = m_scr[n]
            m_new = jnp.maximum(m_old, jnp.max(s, axis=1, keepdims=True))
            pexp = jnp.exp(s - m_new)
            alpha = jnp.exp(m_old - m_new)
            l_scr[n] = alpha * l_scr[n] + jnp.sum(pexp, axis=1, keepdims=True)
            acc_scr[n] = alpha * acc_scr[n] + _dot(pexp.astype(BF16), vals(n).astype(BF16))
            m_scr[n] = m_new

    def page_head(r, n):
        return r[pl.ds(n, PAGE_SIZE, stride=N_KV_B), :]

    @pl.when(j < nsteps)
    def _():
        width = pps * PAGE_SIZE
        x = scp_scr[:, pl.ds(pl.multiple_of(j * width, width), width)]
        attend(x,
               lambda n: jnp.concatenate([page_head(r, n) for r in kpage_refs], axis=0),
               lambda n: jnp.concatenate([page_head(r, n) for r in vpage_refs], axis=0))

    @pl.when(j == nsteps)
    def _():
        hs = lambda n: slice(n * HEAD_DIM_B, (n + 1) * HEAD_DIM_B)
        attend(scn_scr[...], lambda n: knew_ref[:, hs(n)], lambda n: vnew_ref[:, hs(n)])
        for n in range(N_KV_B):
            o_ref[n] = (acc_scr[n] / l_scr[n]).astype(o_ref.dtype)


def _dsa_sample_attn(page_table, scores, scores_new, q, cache_k, cache_v, layer, k_new, v_new):
    nseq, npages = page_table.shape
    t = scores.shape[1]
    past = npages * PAGE_SIZE
    topk = min(TOPK_MAX, (past + t) // 4)
    rows = GROUP_B * t
    pps = math.gcd(PAGES_PER_STEP, npages)
    nsteps = npages // pps
    page_block = (None, None, PAGE_SIZE * N_KV_B, HEAD_DIM_B)
    const3 = lambda b, j, pt: (b, 0, 0)
    return pl.pallas_call(
        functools.partial(_dsa_sample_attn_body, nsteps=nsteps, pps=pps, t=t, past=past, topk=topk,
                          tie_chunk=math.gcd(TIE_CHUNK, past)),
        grid_spec=pltpu.PrefetchScalarGridSpec(
            num_scalar_prefetch=1,
            grid=(nseq, nsteps + 1),
            in_specs=[pl.BlockSpec((None, t, past), const3),
                      pl.BlockSpec((None, t, PAGE_SIZE), const3),
                      pl.BlockSpec((None, N_KV_B, rows, HEAD_DIM_B), lambda b, j, pt: (b, 0, 0, 0)),
                      *_page_specs(page_block, layer, nsteps, pps),
                      *_page_specs(page_block, layer, nsteps, pps),
                      pl.BlockSpec((None, PAGE_SIZE, KV_WIDTH_B), const3),
                      pl.BlockSpec((None, PAGE_SIZE, KV_WIDTH_B), const3)],
            out_specs=pl.BlockSpec((None, N_KV_B, rows, HEAD_DIM_B), lambda b, j, pt: (b, 0, 0, 0)),
            scratch_shapes=[pltpu.VMEM((t, past), F32),
                            pltpu.VMEM((t, PAGE_SIZE), F32),
                            pltpu.VMEM((t, PAGE_SIZE), F32),
                            pltpu.VMEM((N_KV_B, rows, 1), F32),
                            pltpu.VMEM((N_KV_B, rows, 1), F32),
                            pltpu.VMEM((N_KV_B, rows, HEAD_DIM_B), F32)],
        ),
        out_shape=jax.ShapeDtypeStruct((nseq, N_KV_B, rows, HEAD_DIM_B), BF16),
        compiler_params=_cparams("parallel", "arbitrary"),
        name="dsa_sample_attn",
    )(page_table, scores, scores_new, q, *([cache_k] * pps), *([cache_v] * pps), k_new, v_new)


def _dsa_sample(ps, k_s, v_s, ki_s, page_table, cache_k, cache_v, cache_kidx, layer):
    nseq = page_table.shape[0]
    t = ps.shape[0] // nseq
    seg = lambda name, width: ps[:, PD_OFF[name]:PD_OFF[name] + width]
    qi = seg("qi", H_I * D_I).reshape(nseq, t * H_I, D_I)
    wi = seg("wi", H_I).reshape(nseq, t * H_I, 1)
    wv = jnp.broadcast_to(wi, (nseq, t * H_I, PAGE_SIZE))
    pad_rows = lambda a: jnp.pad(a.reshape(nseq, t, -1), ((0, 0), (0, PAGE_SIZE - t), (0, 0)))
    kidx_t = cache_kidx.transpose(0, 1, 3, 2)
    scores, scores_new = _dsa_sample_scores(page_table, qi, wv, kidx_t, layer,
                                            pad_rows(ki_s).transpose(0, 2, 1))
    q = seg("qb", WIDTH_B).reshape(nseq, t, N_KV_B, GROUP_B, HEAD_DIM_B)
    q = q.transpose(0, 2, 3, 1, 4).reshape(nseq, N_KV_B, GROUP_B * t, HEAD_DIM_B)
    pool_rows = lambda c: c.reshape(c.shape[0], c.shape[1], PAGE_SIZE * N_KV_B, HEAD_DIM_B)
    o = _dsa_sample_attn(page_table, scores, scores_new, q, pool_rows(cache_k), pool_rows(cache_v),
                         layer, pad_rows(k_s), pad_rows(v_s))
    o = o.reshape(nseq, N_KV_B, GROUP_B, t, HEAD_DIM_B).transpose(0, 3, 1, 2, 4)
    return o.reshape(nseq * t, WIDTH_B)


def _prep_w_mid(w_in):
    sizes = dict(zip(IN_NAMES, IN_SIZES))
    z0 = min(SRC_OFF[n] for n in MID_ORDER + ("aa", "ba"))
    z1 = max(SRC_OFF[n] + sizes[n] for n in MID_ORDER)
    zone = lax.optimization_barrier(lax.slice_in_dim(w_in, z0, z1, axis=2))
    cols = [zone[:, :, SRC_OFF[n] - z0:SRC_OFF[n] - z0 + sizes[n]].astype(BF16) for n in MID_ORDER]
    cols.append(jnp.zeros(w_in.shape[:2] + (PD_WIDTH - sum(sizes[n] for n in MID_ORDER),), BF16))
    return jnp.concatenate(cols, axis=2)


def kernel(x_prompt, x_sample, cache_k, cache_v, cache_kidx, page_table, state_gdn, state_gdn_conv,
           state_sconv, final_norm, norm1, norm2, w_in, conv_a, a_log, dt_bias, gdn_norm, conv_c,
           w_branch_a, w_branch_b, w_branch_c, w_o, w_gate, w_up, w_down):
    bp, tp, d = x_prompt.shape
    bs, ts = x_sample.shape[:2]
    mp, ms = bp * tp, bs * ts
    x = jnp.concatenate([x_prompt.reshape(mp, d), x_sample.reshape(ms, d)], axis=0)
    new_p = [[] for _ in range(6)]
    new_s = [[] for _ in range(6)]
    w_mid = _prep_w_mid(w_in)
    w_down16 = w_down.astype(BF16)
    for l in range(DEPTH):
        xn = _rmsnorm(x, norm1[l], BF16)
        p = _in_proj_main(xn, w_in, l)
        pd, k_new, v_new, ki_new = _in_proj_mid(xn, w_mid, l)
        g, beta = _gdn_gates(pd, a_log[l], dt_bias[l])
        conv_w = conv_a[l].astype(F32)

        put_sample = lambda full, part: lax.dynamic_update_slice(full, part, (mp, 0))
        oa, gdn_p = _gdn(p, 0, bp, tp, min(GDN_CHUNK, tp), conv_w, g, beta,
                         jnp.zeros((bp, CONV_A - 1, 3 * WIDTH_A), F32),
                         jnp.zeros((bp, H_A, DK_A, DV_A), F32), gdn_norm[l], out_rows=mp + ms)
        oa_s, gdn_s = _gdn(p, mp, bs, ts, ts, conv_w, g, beta,
                           state_gdn_conv[l], state_gdn[l], gdn_norm[l])
        oa = put_sample(oa, oa_s)
        ob = _dsa_prompt(pd, k_new, v_new, ki_new, bp, tp, out_rows=mp + ms)
        ob = put_sample(ob, _dsa_sample(pd[mp:], k_new[mp:], v_new[mp:], ki_new[mp:], page_table,
                                        cache_k, cache_v, cache_kidx, l))
        cw = conv_c[l].astype(F32)
        oc, sconv_p = _sconv(p, 0, bp, tp, cw, jnp.zeros((bp, CONV_C - 1, WIDTH_C), F32),
                             out_rows=mp + ms)
        oc_s, sconv_s = _sconv(p, mp, bs, ts, cw, state_sconv[l])
        oc = put_sample(oc, oc_s)
        merged = _merge(oa, ob, oc, w_branch_a, w_branch_b, w_branch_c, l, p)
        x = _out_proj(merged, w_o, l, x)

        hn = _rmsnorm(x, norm2[l], BF16)
        h = _swiglu(hn, w_gate, w_up, l)
        x = _out_proj(h, w_down16, l, x, name="ffn_down")

        nb = CONV_A - 1
        tail_rows = lambda r0, t: lax.slice(p, (r0 + max(t - nb, 0), 0), (r0 + t, 3 * WIDTH_A))
        gconv_p = jnp.stack([tail_rows(b * tp, tp) for b in range(bp)])
        gconv_s = jnp.stack([tail_rows(mp + b * ts, ts) for b in range(bs)])
        if tp < nb:
            gconv_p = jnp.concatenate([jnp.zeros((bp, nb - tp, 3 * WIDTH_A), F32), gconv_p], axis=1)
        if ts < nb:
            gconv_s = jnp.concatenate([state_gdn_conv[l].astype(F32)[:, ts:], gconv_s], axis=1)
        kv = lambda a, rows, b, t: a[rows].reshape(b, t, N_KV_B, HEAD_DIM_B)
        rp, rs = slice(0, mp), slice(mp, mp + ms)
        for lst, val in zip(new_p, (kv(k_new, rp, bp, tp), kv(v_new, rp, bp, tp),
                                    ki_new[rp].reshape(bp, tp, D_I), gdn_p, gconv_p, sconv_p)):
            lst.append(val)
        for lst, val in zip(new_s, (kv(k_new, rs, bs, ts), kv(v_new, rs, bs, ts),
                                    ki_new[rs].reshape(bs, ts, D_I), gdn_s, gconv_s, sconv_s)):
            lst.append(val)

    y_prompt = _rmsnorm(x, final_norm, F32, 0, mp).reshape(bp, tp, d)
    y_sample = _rmsnorm(x, final_norm, F32, mp, ms).reshape(bs, ts, d)
    outs_p = [jnp.stack(a) for a in new_p]
    outs_s = [jnp.stack(a) for a in new_s]
    return (y_prompt, y_sample, *outs_p, *outs_s)
```

```python
import functools
import math

import jax
import jax.numpy as jnp
from jax import lax
from jax.experimental import pallas as pl
from jax.experimental.pallas import tpu as pltpu

F32 = jnp.float32
BF16 = jnp.bfloat16

D_MODEL = 4096
DEPTH = 2
PAGE_SIZE = 128
H_A = 16
DK_A = 128
DV_A = 128
WIDTH_A = H_A * DV_A
CONV_A = 4
GDN_CHUNK = 64
H_B = 8
N_KV_B = 2
GROUP_B = H_B // N_KV_B
HEAD_DIM_B = 128
WIDTH_B = H_B * HEAD_DIM_B
KV_WIDTH_B = N_KV_B * HEAD_DIM_B
H_I = 32
D_I = 64
TOPK_MAX = 256
Q_BLOCK = 128
WIDTH_C = 1024
CONV_C = 3
D_FF = -(-8 * D_MODEL // (3 * 256)) * 256
EPS = 1e-6

IN_NAMES = ("qa", "ka", "va", "za", "aa", "ba", "qb", "kb", "vb", "qi", "ki", "wi",
            "gate_b", "gate_c", "hc", "ga", "gb", "gc")
IN_SIZES = (WIDTH_A, WIDTH_A, WIDTH_A, WIDTH_A, H_A, H_A,
            WIDTH_B, KV_WIDTH_B, KV_WIDTH_B, H_I * D_I, D_I, H_I,
            WIDTH_C, WIDTH_C, WIDTH_C, D_MODEL, D_MODEL, D_MODEL)

LANES = 128
N_BLOCK = 512
VMEM_LIMIT = 56 * 1024 * 1024
ACT_VMEM_BYTES = 32 * 1024 * 1024
NEG_BIG = -1e30

MAIN_ORDER = ("qa", "ka", "va", "za", "gate_b", "gate_c", "hc", "ga", "gb", "gc")
MID_ORDER = ("qi", "qb", "kb", "vb", "ki", "aa", "ba", "wi")


def _offsets(order):
    sizes = dict(zip(IN_NAMES, IN_SIZES))
    off, out = 0, {}
    for name in order:
        out[name] = off
        off += sizes[name]
    return out, off


SRC_OFF, _ = _offsets(IN_NAMES)
PM_OFF, PM_WIDTH = _offsets(MAIN_ORDER)
PD_OFF, _mid_cols = _offsets(MID_ORDER)
PD_WIDTH = -(-_mid_cols // N_BLOCK) * N_BLOCK
MAIN_RUN0 = PM_OFF["gate_b"]
MAIN_SHIFT = SRC_OFF["gate_b"] - MAIN_RUN0
assert PM_WIDTH % N_BLOCK == 0 and MAIN_RUN0 % N_BLOCK == 0 and MAIN_SHIFT % LANES == 0
assert all(SRC_OFF[n] == PM_OFF[n] for n in MAIN_ORDER[:4])
assert all(SRC_OFF[n] == PM_OFF[n] + MAIN_SHIFT for n in MAIN_ORDER[4:])
TAIL_OFF = PD_OFF["ki"]
TAIL_AA = PD_OFF["aa"] - TAIL_OFF
TAIL_BA = PD_OFF["ba"] - TAIL_OFF
TAIL_WI = PD_OFF["wi"] - TAIL_OFF
assert TAIL_OFF % N_BLOCK == 0 and PD_OFF["wi"] + H_I - TAIL_OFF == LANES


def _cparams(*sem):
    return pltpu.CompilerParams(dimension_semantics=sem, vmem_limit_bytes=VMEM_LIMIT)


def _dot(a, b):
    return jnp.dot(a, b, preferred_element_type=F32)


def _dot_nt(a, b):
    return lax.dot_general(a, b, (((1,), (1,)), ((), ())), preferred_element_type=F32)


def _dot_tn(a, b):
    return lax.dot_general(a, b, (((0,), (0,)), ((), ())), preferred_element_type=F32)


def _dot_hi(a, b):
    return jnp.dot(a, b, preferred_element_type=F32, precision=lax.Precision.HIGHEST)


def _sigmoid(x):
    return jax.nn.sigmoid(x)


def _silu(x):
    return x * jax.nn.sigmoid(x)


def _rmsnorm_body(x_ref, g_ref, o_ref):
    x = x_ref[...]
    ms = jnp.mean(x * x, axis=-1, keepdims=True)
    o_ref[...] = (x * lax.rsqrt(ms + EPS) * g_ref[...]).astype(o_ref.dtype)


def _row_block(m):
    for bm in (256, 192, 128, 64, 32, 16, 8):
        if m % bm == 0:
            return bm
    raise ValueError(f"unsupported row count {m}")


def _rmsnorm(x, g, out_dtype, row0=0, nrows=None):
    d = x.shape[1]
    m = x.shape[0] - row0 if nrows is None else nrows
    bm = _row_block(math.gcd(m, row0) if row0 else m)
    rb0 = row0 // bm
    return pl.pallas_call(
        _rmsnorm_body,
        grid=(m // bm,),
        in_specs=[pl.BlockSpec((bm, d), lambda i: (rb0 + i, 0)), pl.BlockSpec((1, d), lambda i: (0, 0))],
        out_specs=pl.BlockSpec((bm, d), lambda i: (i, 0)),
        out_shape=jax.ShapeDtypeStruct((m, d), out_dtype),
        compiler_params=_cparams("parallel"),
        name="rmsnorm",
    )(x, g.reshape(1, d).astype(F32))


def _gdn_gates_body(t_ref, alog_ref, dtb_ref, g_ref, b_ref):
    t = t_ref[...]
    aa = t[:, TAIL_AA:TAIL_AA + H_A]
    ba = t[:, TAIL_BA:TAIL_BA + H_A]
    x = aa + dtb_ref[...]
    softplus = jnp.maximum(x, 0.0) + jnp.log1p(jnp.exp(-jnp.abs(x)))
    g_ref[...] = -jnp.exp(alog_ref[...]) * softplus
    b_ref[...] = _sigmoid(ba)


def _gdn_gates(p, a_log, dt_bias):
    m = p.shape[0]
    bm = _row_block(m)
    tail_blk = TAIL_OFF // LANES
    return pl.pallas_call(
        _gdn_gates_body,
        grid=(m // bm,),
        in_specs=[pl.BlockSpec((bm, LANES), lambda i: (i, tail_blk)),
                  pl.BlockSpec((1, H_A), lambda i: (0, 0)),
                  pl.BlockSpec((1, H_A), lambda i: (0, 0))],
        out_specs=[pl.BlockSpec((bm, H_A), lambda i: (i, 0)), pl.BlockSpec((bm, H_A), lambda i: (i, 0))],
        out_shape=[jax.ShapeDtypeStruct((m, H_A), F32)] * 2,
        compiler_params=_cparams("parallel"),
        name="gdn_gates",
    )(p, a_log.reshape(1, H_A).astype(F32), dt_bias.reshape(1, H_A).astype(F32))


def _big_row_block(m, k=D_MODEL, buffers=2):
    for bm in (2064, 1376, 1024, 688, 512, 256, 128, 64):
        if m % bm == 0 and buffers * bm * k * 2 <= ACT_VMEM_BYTES:
            return bm
    raise ValueError(f"unsupported row count {m}")


def _resident_rows(bm, k, buffers=2):
    if buffers == 1:
        return pl.BlockSpec((bm, k), lambda i, j: (i, 0), pipeline_mode=pl.Buffered(1))
    return pl.BlockSpec((bm, k), lambda i, j: (i, 0))


def _out_proj_body(a_ref, w_ref, r_ref, o_ref):
    o_ref[...] = r_ref[...] + _dot(a_ref[...], w_ref[...].astype(BF16))


def _out_proj(a, w, layer, residual, name="out_proj"):
    m, k = a.shape
    n = w.shape[2]
    bm, bn = _big_row_block(m, k), N_BLOCK // 2
    return pl.pallas_call(
        _out_proj_body,
        grid=(m // bm, n // bn),
        in_specs=[_resident_rows(bm, k),
                  pl.BlockSpec((None, k, bn), lambda i, j: (layer, 0, j)),
                  pl.BlockSpec((bm, bn), lambda i, j: (i, j))],
        out_specs=pl.BlockSpec((bm, bn), lambda i, j: (i, j)),
        out_shape=jax.ShapeDtypeStruct((m, n), F32),
        compiler_params=_cparams("parallel", "arbitrary"),
        name=name,
    )(a, w, residual)


IN_PROJ_PIECES = N_BLOCK // LANES


def _in_proj_main_body(a_ref, *rest):
    w_refs, p_ref = rest[:IN_PROJ_PIECES], rest[IN_PROJ_PIECES]
    w = jnp.concatenate([r[...].astype(BF16) for r in w_refs], axis=1)
    p_ref[...] = _dot(a_ref[...], w)


def _in_proj_main(a, w_in, layer):
    m, k = a.shape
    bm, bn = _big_row_block(m, k, buffers=1), N_BLOCK
    run0_blk, shift_blk = MAIN_RUN0 // bn, MAIN_SHIFT // LANES

    def piece(q):
        def index(i, j):
            return (layer, 0, IN_PROJ_PIECES * j + q + jnp.where(j >= run0_blk, shift_blk, 0))
        return pl.BlockSpec((None, k, LANES), index)

    return pl.pallas_call(
        _in_proj_main_body,
        grid=(m // bm, PM_WIDTH // bn),
        in_specs=[_resident_rows(bm, k, buffers=1)] + [piece(q) for q in range(IN_PROJ_PIECES)],
        out_specs=pl.BlockSpec((bm, bn), lambda i, j: (i, j)),
        out_shape=jax.ShapeDtypeStruct((m, PM_WIDTH), F32),
        compiler_params=_cparams("parallel", "arbitrary"),
        name="in_proj_main",
    )(a, *([w_in] * IN_PROJ_PIECES))


def _in_proj_mid_body(a_ref, w_ref, p_ref, k_ref, v_ref, ki_ref, *, jkv, jtail):
    acc = _dot(a_ref[...], w_ref[...])
    p_ref[...] = acc
    j = pl.program_id(1)

    @pl.when(j == jkv)
    def _():
        k_ref[...] = acc[:, :KV_WIDTH_B]
        v_ref[...] = acc[:, KV_WIDTH_B:2 * KV_WIDTH_B]

    @pl.when(j == jtail)
    def _():
        ki_ref[...] = acc[:, :D_I]


def _in_proj_mid(a, w_mid, layer):
    m, k = a.shape
    bm, bn = _big_row_block(m), N_BLOCK
    assert PD_OFF["kb"] % bn == 0 and PD_OFF["vb"] == PD_OFF["kb"] + KV_WIDTH_B
    row = lambda i, j: (i, 0)
    return pl.pallas_call(
        functools.partial(_in_proj_mid_body, jkv=PD_OFF["kb"] // bn, jtail=TAIL_OFF // bn),
        grid=(m // bm, PD_WIDTH // bn),
        in_specs=[_resident_rows(bm, k), pl.BlockSpec((None, k, bn), lambda i, j: (layer, 0, j))],
        out_specs=[pl.BlockSpec((bm, bn), lambda i, j: (i, j)),
                   pl.BlockSpec((bm, KV_WIDTH_B), row), pl.BlockSpec((bm, KV_WIDTH_B), row),
                   pl.BlockSpec((bm, D_I), row)],
        out_shape=[jax.ShapeDtypeStruct((m, PD_WIDTH), F32), jax.ShapeDtypeStruct((m, KV_WIDTH_B), F32),
                   jax.ShapeDtypeStruct((m, KV_WIDTH_B), F32), jax.ShapeDtypeStruct((m, D_I), F32)],
        compiler_params=_cparams("parallel", "arbitrary"),
        name="in_proj_mid",
    )(a, w_mid)


def _swiglu_body(a_ref, wg_ref, wu_ref, o_ref):
    a = a_ref[...]
    g = _dot(a, wg_ref[...].astype(BF16))
    u = _dot(a, wu_ref[...].astype(BF16))
    o_ref[...] = (_silu(g) * u).astype(o_ref.dtype)


def _swiglu(a, w_gate, w_up, layer):
    m, k = a.shape
    n = w_gate.shape[2]
    bm, bn = _big_row_block(m, k, buffers=1), N_BLOCK // 2
    assert n % bn == 0
    wspec = pl.BlockSpec((None, k, bn), lambda i, j: (layer, 0, j))
    return pl.pallas_call(
        _swiglu_body,
        grid=(m // bm, n // bn),
        in_specs=[_resident_rows(bm, k, buffers=1), wspec, wspec],
        out_specs=pl.BlockSpec((bm, bn), lambda i, j: (i, j)),
        out_shape=jax.ShapeDtypeStruct((m, n), BF16),
        compiler_params=_cparams("parallel", "arbitrary"),
        name="swiglu",
    )(a, w_gate, w_up)


def _merge_body(oa_ref, ob_ref, oc_ref, wa_ref, wb_ref, wc_ref, ga_ref, gb_ref, gc_ref, o_ref):
    acc = _sigmoid(ga_ref[...]) * _dot(oa_ref[...], wa_ref[...].astype(BF16))
    acc += _sigmoid(gb_ref[...]) * _dot(ob_ref[...], wb_ref[...].astype(BF16))
    acc += _sigmoid(gc_ref[...]) * _dot(oc_ref[...], wc_ref[...].astype(BF16))
    o_ref[...] = acc.astype(o_ref.dtype)


def _merge(oa, ob, oc, wa, wb, wc, layer, p):
    m = oa.shape[0]
    n = wa.shape[2]
    bm, bn = _big_row_block(m), N_BLOCK // 2
    ga0, gb0, gc0 = (PM_OFF[s] // bn for s in ("ga", "gb", "gc"))
    row = lambda i, j: (i, 0)
    wspec = lambda w: pl.BlockSpec((None, w.shape[1], bn), lambda i, j: (layer, 0, j))
    return pl.pallas_call(
        _merge_body,
        grid=(m // bm, n // bn),
        in_specs=[_resident_rows(bm, oa.shape[1]), _resident_rows(bm, ob.shape[1]),
                  _resident_rows(bm, oc.shape[1]),
                  wspec(wa), wspec(wb), wspec(wc),
                  pl.BlockSpec((bm, bn), lambda i, j: (i, ga0 + j)),
                  pl.BlockSpec((bm, bn), lambda i, j: (i, gb0 + j)),
                  pl.BlockSpec((bm, bn), lambda i, j: (i, gc0 + j))],
        out_specs=pl.BlockSpec((bm, bn), lambda i, j: (i, j)),
        out_shape=jax.ShapeDtypeStruct((m, n), BF16),
        compiler_params=_cparams("parallel", "arbitrary"),
        name="merge",
    )(oa, ob, oc, wa, wb, wc, p, p, p)


GDN_HEADS_PER_STEP = 16
GDN_INV_BASE = 8


def _gdn_body(q_ref, k_ref, v_ref, z_ref, wq_ref, wk_ref, wv_ref, tq_ref, tk_ref, tv_ref,
              g_ref, b_ref, s0_ref, ng_ref, o_ref, sout_ref, s_scr, tail_scr,
              *, chunk, hg):
    c = chunk
    w = hg * DK_A
    n = pl.program_id(2)

    @pl.when(n == 0)
    def _():
        s_scr[...] = s0_ref[...].astype(F32)
        tail_scr[:, 0:w] = tq_ref[...]
        tail_scr[:, w:2 * w] = tk_ref[...]
        tail_scr[:, 2 * w:3 * w] = tv_ref[...]

    act = []
    for idx, (x_ref, w_ref) in enumerate(((q_ref, wq_ref), (k_ref, wk_ref), (v_ref, wv_ref))):
        x = x_ref[...]
        cw = w_ref[...]
        xfull = jnp.concatenate([tail_scr[:, idx * w:(idx + 1) * w], x], axis=0)
        y = x * cw[CONV_A - 1:CONV_A, :]
        for j in range(1, CONV_A):
            y = y + pltpu.roll(xfull, j, axis=0)[8:, :] * cw[CONV_A - 1 - j:CONV_A - j, :]
        tail_scr[:, idx * w:(idx + 1) * w] = xfull[c:c + 8, :]
        act.append(_silu(y))
    qs, ks, vs = act
    z = z_ref[...]

    ii = lax.broadcasted_iota(jnp.int32, (c, c), 0)
    jj = lax.broadcasted_iota(jnp.int32, (c, c), 1)
    incl = jj <= ii
    strict = jj < ii
    eye = (ii == jj).astype(F32)
    ng = ng_ref[...]
    b_blk = b_ref[...]
    gcum_all = _dot_hi(incl.astype(F32), g_ref[...])
    gcum_t = gcum_all.T

    heads = range(hg)
    hsl = [slice(hh * DK_A, (hh + 1) * DK_A) for hh in heads]
    q_l, k_l, k16_l, decay_l, egc_l, kdec_l, glast_l, mm_l, rhs_l = ([] for _ in range(9))
    for hh in heads:
        q = qs[:, hsl[hh]]
        k = ks[:, hsl[hh]]
        q = q * lax.rsqrt(jnp.sum(q * q, axis=-1, keepdims=True) + EPS) * (DK_A ** -0.5)
        k = k * lax.rsqrt(jnp.sum(k * k, axis=-1, keepdims=True) + EPS)
        bc = b_blk[:, hh:hh + 1]
        gcum_c = gcum_all[:, hh:hh + 1]
        gcum_r = gcum_t[hh:hh + 1, :]
        decay = jnp.exp(jnp.where(incl, gcum_c - gcum_r, -jnp.inf))
        kb = k * bc
        k16 = k.astype(BF16)
        egc = jnp.exp(gcum_c)
        glast = gcum_c[c - 1:c, :]
        mm = jnp.where(strict, _dot_nt(kb.astype(BF16), k16) * decay, 0.0)
        q_l.append(q)
        k_l.append(k)
        k16_l.append(k16)
        decay_l.append(decay)
        egc_l.append(egc)
        glast_l.append(glast)
        kdec_l.append(jnp.exp(glast - gcum_c))
        rhs_l.append(jnp.concatenate([vs[:, hsl[hh]] * bc, kb * egc], axis=1).astype(BF16))
        mm_l.append(mm)
    base = min(c, GDN_INV_BASE)
    log_base = base.bit_length() - 1
    in_base = jnp.right_shift(ii, log_base) == jnp.right_shift(jj, log_base)
    pw_l = [jnp.where(in_base, -mm, 0.0) for mm in mm_l]
    tinv_l = [eye + pw for pw in pw_l]
    for _ in range(log_base - 1):
        for hh in heads:
            pw16 = pw_l[hh].astype(BF16)
            pw_l[hh] = _dot(pw16, pw16)
        for hh in heads:
            tinv_l[hh] = tinv_l[hh] + _dot(tinv_l[hh].astype(BF16), pw_l[hh].astype(BF16))
    log_size = log_base
    while (1 << log_size) < c:
        lower_left = jnp.logical_and(
            jnp.right_shift(ii, log_size + 1) == jnp.right_shift(jj, log_size + 1),
            jnp.logical_and(jnp.bitwise_and(jnp.right_shift(ii, log_size), 1) == 1,
                            jnp.bitwise_and(jnp.right_shift(jj, log_size), 1) == 0))
        for hh in heads:
            t16 = tinv_l[hh].astype(BF16)
            c16 = jnp.where(lower_left, mm_l[hh], 0.0).astype(BF16)
            tinv_l[hh] = tinv_l[hh] - _dot(_dot(t16, c16).astype(BF16), t16)
        log_size += 1
    sol_l = [_dot(tinv_l[hh].astype(BF16), rhs_l[hh]) for hh in heads]
    attn_l = [(_dot_nt(q_l[hh].astype(BF16), k16_l[hh]) * decay_l[hh]).astype(BF16) for hh in heads]
    s_l = [s_scr[hh] for hh in heads]
    s16_l = [s_l[hh].astype(BF16) for hh in heads]
    v16_l = [(sol_l[hh][:, :DV_A] - _dot(sol_l[hh][:, DV_A:].astype(BF16), s16_l[hh])).astype(BF16)
             for hh in heads]
    o_l = [_dot((q_l[hh] * egc_l[hh]).astype(BF16), s16_l[hh]) + _dot(attn_l[hh], v16_l[hh]) for hh in heads]
    snew_l = [s_l[hh] * jnp.exp(glast_l[hh]) + _dot_tn((k_l[hh] * kdec_l[hh]).astype(BF16), v16_l[hh])
              for hh in heads]
    s_scr[...] = jnp.stack(snew_l, axis=0)
    o_l = [o * lax.rsqrt(jnp.mean(o * o, axis=-1, keepdims=True) + EPS) * ng for o in o_l]
    o_ref[...] = (jnp.concatenate(o_l, axis=1) * _silu(z)).astype(o_ref.dtype)

    @pl.when(n == pl.num_programs(2) - 1)
    def _():
        sout_ref[...] = s_scr[...].astype(sout_ref.dtype)


def _gdn(p, row0, nseq, t, chunk, conv_w, g, beta, conv_buf, s0, norm_g):
    hg = GDN_HEADS_PER_STEP
    assert hg == H_A
    w = hg * DK_A
    nch = t // chunk
    nhg = H_A // hg
    rb0 = row0 // chunk
    gspec = pl.BlockSpec((chunk, H_A), lambda b, h, n: (rb0 + b * nch + n, 0))
    tail = jnp.pad(conv_buf.astype(F32), ((0, 0), (8 - (CONV_A - 1), 0), (0, 0)))
    qb0, kb0, vb0, zb0 = (PM_OFF[s] // w for s in ("qa", "ka", "va", "za"))
    pspec = lambda b0: pl.BlockSpec((chunk, w), lambda b, h, n: (rb0 + b * nch + n, b0 + h))
    wspec = lambda b0: pl.BlockSpec((CONV_A, w), lambda b, h, n: (0, b0 + h))
    tspec = lambda b0: pl.BlockSpec((None, 8, w), lambda b, h, n: (b, 0, b0 + h))
    o, s_out = pl.pallas_call(
        functools.partial(_gdn_body, chunk=chunk, hg=hg),
        grid=(nseq, nhg, nch),
        in_specs=[pspec(qb0), pspec(kb0), pspec(vb0), pspec(zb0),
                  wspec(0), wspec(nhg), wspec(2 * nhg),
                  tspec(0), tspec(nhg), tspec(2 * nhg),
                  gspec, gspec,
                  pl.BlockSpec((None, hg, DK_A, DV_A), lambda b, h, n: (b, h, 0, 0)),
                  pl.BlockSpec((1, DV_A), lambda b, h, n: (0, 0))],
        out_specs=[pl.BlockSpec((chunk, w), lambda b, h, n: (b * nch + n, h)),
                   pl.BlockSpec((None, hg, DK_A, DV_A), lambda b, h, n: (b, h, 0, 0))],
        out_shape=[jax.ShapeDtypeStruct((nseq * t, WIDTH_A), BF16),
                   jax.ShapeDtypeStruct((nseq, H_A, DK_A, DV_A), s0.dtype)],
        scratch_shapes=[pltpu.VMEM((hg, DK_A, DV_A), F32), pltpu.VMEM((8, 3 * w), F32)],
        compiler_params=_cparams("parallel", "parallel", "arbitrary"),
        name=f"gdn_c{chunk}",
    )(p, p, p, p, conv_w, conv_w, conv_w, tail, tail, tail, g, beta, s0,
      norm_g.reshape(1, DV_A).astype(F32))
    return o, s_out


SCONV_COLS = 256


def _sconv_body(gb_ref, gc_ref, hc_ref, w_ref, buf_ref, o_ref, st_ref, *, t):
    pr = gc_ref[...] * hc_ref[...]
    cw = w_ref[...]
    xfull = jnp.concatenate([buf_ref[...], pr], axis=0)
    y = pr * cw[CONV_C - 1:CONV_C, :]
    for j in range(1, CONV_C):
        y = y + pltpu.roll(xfull, j, axis=0)[8:, :] * cw[CONV_C - 1 - j:CONV_C - j, :]
    o_ref[...] = (gb_ref[...] * y).astype(o_ref.dtype)
    st_ref[...] = xfull[t:t + 8, :]


def _sconv(p, row0, nseq, t, conv_w, buf):
    cols = SCONV_COLS
    ncb = WIDTH_C // cols
    rb0 = row0 // t
    b0, c0, h0 = (PM_OFF[s] // cols for s in ("gate_b", "gate_c", "hc"))
    buf8 = jnp.pad(buf.astype(F32), ((0, 0), (8 - (CONV_C - 1), 0), (0, 0)))
    pspec = lambda o: pl.BlockSpec((t, cols), lambda b, j: (rb0 + b, o + j))
    o, st = pl.pallas_call(
        functools.partial(_sconv_body, t=t),
        grid=(nseq, ncb),
        in_specs=[pspec(b0), pspec(c0), pspec(h0),
                  pl.BlockSpec((CONV_C, cols), lambda b, j: (0, j)),
                  pl.BlockSpec((None, 8, cols), lambda b, j: (b, 0, j))],
        out_specs=[pl.BlockSpec((t, cols), lambda b, j: (b, j)),
                   pl.BlockSpec((None, 8, cols), lambda b, j: (b, 0, j))],
        out_shape=[jax.ShapeDtypeStruct((nseq * t, WIDTH_C), BF16),
                   jax.ShapeDtypeStruct((nseq, 8, WIDTH_C), F32)],
        compiler_params=_cparams("parallel", "parallel"),
        name=f"sconv_t{t}",
    )(p, p, p, conv_w, buf8)
    return o, st[:, 8 - (CONV_C - 1):, :]


BISECT_EVERY = 8
MAX_SEARCH = 2400


def _select_threshold(count_ge, lo0, hi0, n_valid, k):
    kf = jnp.float32(k)

    def cond(st):
        return jnp.logical_and(st[0] < MAX_SEARCH, jnp.min(st[-1]) < 0.5)

    def step(st):
        it, lo, hi, glo, ghi, t, side, done = st
        half = 0.5 * lo + 0.5 * hi
        cand = lo + (hi - lo) * (glo / jnp.maximum(glo - ghi, 1e-9))
        mid = jnp.where(it % BISECT_EVERY == BISECT_EVERY - 1, half, cand)
        mid = jnp.where(jnp.logical_and(mid > lo, mid < hi), mid, half)
        adjacent = jnp.logical_not(jnp.logical_and(mid > lo, mid < hi))
        first = it == 0
        mid = jnp.where(first, hi, mid)
        adjacent = jnp.logical_and(adjacent, jnp.logical_not(first))
        g = count_ge(mid) - kf
        ge = g >= 0.0
        finish = jnp.logical_or(jnp.logical_or(g == 0.0, adjacent), jnp.logical_and(first, ge))
        newly = jnp.logical_and(finish, done < 0.5)
        t = jnp.where(newly, jnp.where(adjacent, lo, mid), t)
        new_side = jnp.where(ge, 1.0, -1.0)
        damp = jnp.where(jnp.logical_and(new_side == side, jnp.logical_not(first)), 0.5, 1.0)
        glo = jnp.where(ge, g, glo * damp)
        ghi = jnp.where(ge, ghi * damp, g)
        lo = jnp.where(ge, mid, lo)
        hi = jnp.where(ge, hi, mid)
        done = jnp.where(finish, 1.0, done)
        return it + 1, lo, hi, glo, ghi, t, new_side, done

    done0 = jnp.where(n_valid > kf, 0.0, 1.0)
    st = lax.while_loop(cond, lambda st: step(step(st)), (jnp.int32(0), lo0, hi0, n_valid - kf, jnp.full_like(lo0, -kf),
                                     lo0, jnp.zeros_like(lo0), done0))
    return st[5]


KEY_CHUNK = 512


def _dsa_prompt_body(qi_ref, tq_ref, qb_ref, kb_ref, vb_ref, tk_ref, o_ref,
                     sc_scr, m_scr, l_scr, acc_scr, *, tq, sc, topk):
    i = pl.program_id(1)
    nck = (i * tq + tq - 1) // sc + 1
    wt = tq_ref[...].T[TAIL_WI:TAIL_WI + H_I, :] * ((D_I ** -0.5) * (H_I ** -0.5))
    qi = qi_ref[...]
    pairs = [jnp.concatenate([qi[:, (2 * p) * D_I:(2 * p + 1) * D_I],
                              qi[:, (2 * p + 1) * D_I:(2 * p + 2) * D_I]], axis=0).astype(BF16)
             for p in range(H_I // 2)]
    tpos = i * tq + lax.broadcasted_iota(jnp.int32, (1, tq), 1)

    def score_chunk(c, carry):
        r0 = pl.multiple_of(c * sc, sc)
        kic = tk_ref[pl.ds(r0, sc), :].astype(BF16)
        acc = jnp.zeros((sc, tq), F32)
        for p in range(H_I // 2):
            d = _dot_nt(kic, pairs[p])
            acc = acc + jnp.maximum(d[:, :tq], 0.0) * wt[2 * p:2 * p + 1, :]
            acc = acc + jnp.maximum(d[:, tq:], 0.0) * wt[2 * p + 1:2 * p + 2, :]
        kpos = r0 + lax.broadcasted_iota(jnp.int32, (sc, tq), 0)
        sc_scr[pl.ds(r0, sc), :] = jnp.where(kpos <= tpos, acc, -jnp.inf)
        return carry

    lax.fori_loop(0, nck, score_chunk, 0)

    def minmax_chunk(c, carry):
        mn, mx = carry
        x = sc_scr[pl.ds(pl.multiple_of(c * sc, sc), sc), :]
        mx = jnp.maximum(mx, jnp.max(x, axis=0, keepdims=True))
        mn = jnp.minimum(mn, jnp.min(jnp.where(x == -jnp.inf, jnp.inf, x), axis=0, keepdims=True))
        return mn, mx

    mn, mx = lax.fori_loop(0, nck, minmax_chunk,
                           (jnp.full((1, tq), jnp.inf, F32), jnp.full((1, tq), -jnp.inf, F32)))

    def count_ge(mid):
        def body(c, acc):
            x = sc_scr[pl.ds(pl.multiple_of(c * sc, sc), sc), :]
            hit = jnp.where(x >= mid, 1.0, 0.0).reshape(8, sc // 64, 8, tq)
            return acc + jnp.sum(jnp.sum(hit, axis=1), axis=0)
        return jnp.sum(lax.fori_loop(0, nck, body, jnp.zeros((8, tq), F32)), axis=0, keepdims=True)

    thr = _select_threshold(count_ge, mn, mx, (tpos + 1).astype(F32), topk)

    def count_gt(c, acc):
        x = sc_scr[pl.ds(pl.multiple_of(c * sc, sc), sc), :]
        hit = jnp.where(x > thr, 1.0, 0.0).reshape(8, sc // 64, 8, tq)
        return acc + jnp.sum(jnp.sum(hit, axis=1), axis=0)
    n_above = jnp.sum(lax.fori_loop(0, nck, count_gt, jnp.zeros((8, tq), F32)), axis=0, keepdims=True)
    crowded = count_ge(thr) > topk
    places = topk - n_above

    @pl.when(jnp.max(jnp.where(crowded, 1.0, 0.0)) > 0.5)
    def _():
        earlier = (lax.broadcasted_iota(jnp.int32, (sc, sc), 1)
                   < lax.broadcasted_iota(jnp.int32, (sc, sc), 0)).astype(BF16)

        def retire(c, seen):
            rows = pl.ds(pl.multiple_of(c * sc, sc), sc)
            x = sc_scr[rows, :]
            tie = jnp.logical_and(x == thr, crowded)
            tie_f = jnp.where(tie, 1.0, 0.0)
            rank = seen + _dot(earlier, tie_f.astype(BF16))
            sc_scr[rows, :] = jnp.where(jnp.logical_and(tie, rank >= places), -jnp.inf, x)
            return seen + jnp.sum(tie_f, axis=0, keepdims=True)

        lax.fori_loop(0, nck, retire, jnp.zeros((1, tq), F32))

    m_scr[...] = jnp.full(m_scr.shape, NEG_BIG, F32)
    l_scr[...] = jnp.zeros(l_scr.shape, F32)
    acc_scr[...] = jnp.zeros(acc_scr.shape, F32)
    q = qb_ref[...].astype(BF16)
    scale = HEAD_DIM_B ** -0.5

    def attn_chunk(c, carry):
        r0 = pl.multiple_of(c * sc, sc)
        sel = sc_scr[pl.ds(r0, sc), :] >= thr
        for n in range(N_KV_B):
            kc = kb_ref[pl.ds(r0, sc), n * HEAD_DIM_B:(n + 1) * HEAD_DIM_B].astype(BF16)
            vc = vb_ref[pl.ds(r0, sc), n * HEAD_DIM_B:(n + 1) * HEAD_DIM_B].astype(BF16)
            for g in range(GROUP_B):
                h = n * GROUP_B + g
                hs = slice(h * HEAD_DIM_B, (h + 1) * HEAD_DIM_B)
                s = jnp.where(sel, _dot_nt(kc, q[:, hs]) * scale, NEG_BIG)
                m_old = m_scr[h:h + 1, :]
                m_new = jnp.maximum(m_old, jnp.max(s, axis=0, keepdims=True))
                pexp = jnp.exp(s - m_new)
                alpha = jnp.exp(m_old - m_new)
                l_scr[h:h + 1, :] = alpha * l_scr[h:h + 1, :] + jnp.sum(pexp, axis=0, keepdims=True)
                acc_scr[hs, :] = alpha * acc_scr[hs, :] + _dot_tn(vc, pexp.astype(BF16))
                m_scr[h:h + 1, :] = m_new
        return carry

    lax.fori_loop(0, nck, attn_chunk, 0)

    for h in range(H_B):
        hs = slice(h * HEAD_DIM_B, (h + 1) * HEAD_DIM_B)
        ot = acc_scr[hs, :] / l_scr[h:h + 1, :]
        o_ref[:, hs] = ot.T.astype(o_ref.dtype)


def _dsa_prompt(p, k_new, v_new, ki_new, nseq, s):
    tq, sc = Q_BLOCK, min(KEY_CHUNK, s)
    topk = min(TOPK_MAX, s // 4)
    nqb = s // tq
    qi0 = PD_OFF["qi"] // (H_I * D_I)
    qb0 = PD_OFF["qb"] // WIDTH_B
    tl0 = TAIL_OFF // LANES
    return pl.pallas_call(
        functools.partial(_dsa_prompt_body, tq=tq, sc=sc, topk=topk),
        grid=(nseq, nqb),
        in_specs=[pl.BlockSpec((tq, H_I * D_I), lambda b, i: (b * nqb + i, qi0)),
                  pl.BlockSpec((tq, LANES), lambda b, i: (b * nqb + i, tl0)),
                  pl.BlockSpec((tq, WIDTH_B), lambda b, i: (b * nqb + i, qb0)),
                  pl.BlockSpec((s, KV_WIDTH_B), lambda b, i: (b, 0)),
                  pl.BlockSpec((s, KV_WIDTH_B), lambda b, i: (b, 0)),
                  pl.BlockSpec((s, D_I), lambda b, i: (b, 0))],
        out_specs=pl.BlockSpec((tq, WIDTH_B), lambda b, i: (b * nqb + i, 0)),
        out_shape=jax.ShapeDtypeStruct((nseq * s, WIDTH_B), BF16),
        scratch_shapes=[pltpu.VMEM((s, tq), F32), pltpu.VMEM((H_B, tq), F32),
                        pltpu.VMEM((H_B, tq), F32), pltpu.VMEM((WIDTH_B, tq), F32)],
        compiler_params=_cparams("parallel", "arbitrary"),
        name="dsa_prompt",
    )(p, p, p, k_new, v_new, ki_new)


PAGES_PER_STEP = 32
TIE_CHUNK = 512


def _dsa_sample_scores_body(pt_ref, q_ref, w_ref, *rest, nsteps, pps, t):
    page_refs, new_ref, o_ref, onew_ref = rest[:pps], rest[pps], rest[pps + 1], rest[pps + 2]
    j = pl.program_id(1)
    q = q_ref[...].astype(BF16)
    wv = w_ref[...] * ((D_I ** -0.5) * (H_I ** -0.5))

    def scores(keys_t):
        d = _dot(q, keys_t.astype(BF16))
        return jnp.sum((jnp.maximum(d, 0.0) * wv).reshape(t, H_I, PAGE_SIZE), axis=1)

    @pl.when(j < nsteps)
    def _():
        o_ref[...] = jnp.concatenate([scores(r[...]) for r in page_refs], axis=1)

    @pl.when(j == nsteps)
    def _():
        knew = lax.broadcasted_iota(jnp.int32, (t, PAGE_SIZE), 1)
        tnew = lax.broadcasted_iota(jnp.int32, (t, PAGE_SIZE), 0)
        onew_ref[...] = jnp.where(knew <= tnew, scores(new_ref[...]), -jnp.inf)


def _page_specs(block, layer, nsteps, pps):
    def spec(p):
        def index(b, j, pt):
            return (layer, pt[b, jnp.minimum(j, nsteps - 1) * pps + p]) + (0,) * (len(block) - 2)
        return pl.BlockSpec(block, index)
    return [spec(p) for p in range(pps)]


def _dsa_sample_scores(page_table, q, wv, cache_kidx, layer, ki_new):
    nseq, npages = page_table.shape
    t = q.shape[1] // H_I
    pps = math.gcd(PAGES_PER_STEP, npages)
    nsteps = npages // pps
    return pl.pallas_call(
        functools.partial(_dsa_sample_scores_body, nsteps=nsteps, pps=pps, t=t),
        grid_spec=pltpu.PrefetchScalarGridSpec(
            num_scalar_prefetch=1,
            grid=(nseq, nsteps + 1),
            in_specs=[pl.BlockSpec((None, t * H_I, D_I), lambda b, j, pt: (b, 0, 0)),
                      pl.BlockSpec((None, t * H_I, PAGE_SIZE), lambda b, j, pt: (b, 0, 0)),
                      *_page_specs((None, None, D_I, PAGE_SIZE), layer, nsteps, pps),
                      pl.BlockSpec((None, D_I, PAGE_SIZE), lambda b, j, pt: (b, 0, 0))],
            out_specs=[pl.BlockSpec((None, t, pps * PAGE_SIZE),
                                    lambda b, j, pt: (b, 0, jnp.minimum(j, nsteps - 1))),
                       pl.BlockSpec((None, t, PAGE_SIZE), lambda b, j, pt: (b, 0, 0))],
        ),
        out_shape=[jax.ShapeDtypeStruct((nseq, t, npages * PAGE_SIZE), F32),
                   jax.ShapeDtypeStruct((nseq, t, PAGE_SIZE), F32)],
        compiler_params=_cparams("parallel", "arbitrary"),
        name="dsa_sample_scores",
    )(page_table, q, wv, *([cache_kidx] * pps), ki_new)


def _dsa_sample_attn_body(pt_ref, sc_ref, scn_ref, q_ref, *rest, nsteps, pps, t, past, topk, tie_chunk):
    kpage_refs, vpage_refs = rest[:pps], rest[pps:2 * pps]
    knew_ref, vnew_ref, o_ref, scp_scr, scn_scr, thr_scr, m_scr, l_scr, acc_scr = rest[2 * pps:]
    j = pl.program_id(1)

    @pl.when(j == 0)
    def _():
        xp, xn = sc_ref[...], scn_ref[...]
        scp_scr[...] = xp
        scn_scr[...] = xn
        mx = jnp.maximum(jnp.max(xp, axis=1, keepdims=True), jnp.max(xn, axis=1, keepdims=True))
        mn = jnp.minimum(jnp.min(xp, axis=1, keepdims=True),
                         jnp.min(jnp.where(xn == -jnp.inf, jnp.inf, xn), axis=1, keepdims=True))
        cnt = lambda hit: jnp.sum(jnp.where(hit, 1.0, 0.0), axis=1, keepdims=True)

        def count_ge(mid):
            return cnt(sc_ref[...] >= mid) + cnt(scn_ref[...] >= mid)

        n_valid = (past + 1 + lax.broadcasted_iota(jnp.int32, (t, 1), 0)).astype(F32)
        thr = _select_threshold(count_ge, mn, mx, n_valid, topk)

        crowded = count_ge(thr) > topk
        places = topk - (cnt(xp > thr) + cnt(xn > thr))

        @pl.when(jnp.max(jnp.where(crowded, 1.0, 0.0)) > 0.5)
        def _():
            cw = tie_chunk
            earlier = (lax.broadcasted_iota(jnp.int32, (cw, cw), 0)
                       < lax.broadcasted_iota(jnp.int32, (cw, cw), 1)).astype(BF16)

            def retire(x, seen, width):
                tie = jnp.logical_and(x == thr, crowded)
                tie_f = jnp.where(tie, 1.0, 0.0)
                rank = seen + _dot(tie_f.astype(BF16), earlier[:width, :width])
                return (jnp.where(jnp.logical_and(tie, rank >= places), -jnp.inf, x),
                        seen + jnp.sum(tie_f, axis=1, keepdims=True))

            def past_chunk(c, seen):
                cols = pl.ds(pl.multiple_of(c * cw, cw), cw)
                x, seen = retire(scp_scr[:, cols], seen, cw)
                scp_scr[:, cols] = x
                return seen

            seen = lax.fori_loop(0, past // cw, past_chunk, jnp.zeros((t, 1), F32))
            scn_scr[...] = retire(scn_scr[...], seen, PAGE_SIZE)[0]

        thr_scr[...] = jnp.broadcast_to(thr, thr_scr.shape)
        m_scr[...] = jnp.full(m_scr.shape, NEG_BIG, F32)
        l_scr[...] = jnp.zeros(l_scr.shape, F32)
        acc_scr[...] = jnp.zeros(acc_scr.shape, F32)

    scale = HEAD_DIM_B ** -0.5

    def attend(x, keys, vals):
        sel_t = jnp.where(x >= thr_scr[:, 0:1], 1.0, 0.0)
        sel = jnp.concatenate([sel_t] * GROUP_B, axis=0) > 0.5
        for n in range(N_KV_B):
            s = jnp.where(sel, _dot_nt(q_ref[n].astype(BF16), keys(n).astype(BF16)) * scale, NEG_BIG)
            m_old = m_scr[n]
            m_new = jnp.maximum(m_old, jnp.max(s, axis=1, keepdims=True))
            pexp = jnp.exp(s - m_new)
            alpha = jnp.exp(m_old - m_new)
            l_scr[n] = alpha * l_scr[n] + jnp.sum(pexp, axis=1, keepdims=True)
            acc_scr[n] = alpha * acc_scr[n] + _dot(pexp.astype(BF16), vals(n).astype(BF16))
            m_scr[n] = m_new

    def page_head(r, n):
        return r[pl.ds(n, PAGE_SIZE, stride=N_KV_B), :]

    @pl.when(j < nsteps)
    def _():
        width = pps * PAGE_SIZE
        x = scp_scr[:, pl.ds(pl.multiple_of(j * width, width), width)]
        attend(x,
               lambda n: jnp.concatenate([page_head(r, n) for r in kpage_refs], axis=0),
               lambda n: jnp.concatenate([page_head(r, n) for r in vpage_refs], axis=0))

    @pl.when(j == nsteps)
    def _():
        hs = lambda n: slice(n * HEAD_DIM_B, (n + 1) * HEAD_DIM_B)
        attend(scn_scr[...], lambda n: knew_ref[:, hs(n)], lambda n: vnew_ref[:, hs(n)])
        for n in range(N_KV_B):
            o_ref[n] = (acc_scr[n] / l_scr[n]).astype(o_ref.dtype)


def _dsa_sample_attn(page_table, scores, scores_new, q, cache_k, cache_v, layer, k_new, v_new):
    nseq, npages = page_table.shape
    t = scores.shape[1]
    past = npages * PAGE_SIZE
    topk = min(TOPK_MAX, (past + t) // 4)
    rows = GROUP_B * t
    pps = math.gcd(PAGES_PER_STEP, npages)
    nsteps = npages // pps
    page_block = (None, None, PAGE_SIZE * N_KV_B, HEAD_DIM_B)
    const3 = lambda b, j, pt: (b, 0, 0)
    return pl.pallas_call(
        functools.partial(_dsa_sample_attn_body, nsteps=nsteps, pps=pps, t=t, past=past, topk=topk,
                          tie_chunk=math.gcd(TIE_CHUNK, past)),
        grid_spec=pltpu.PrefetchScalarGridSpec(
            num_scalar_prefetch=1,
            grid=(nseq, nsteps + 1),
            in_specs=[pl.BlockSpec((None, t, past), const3),
                      pl.BlockSpec((None, t, PAGE_SIZE), const3),
                      pl.BlockSpec((None, N_KV_B, rows, HEAD_DIM_B), lambda b, j, pt: (b, 0, 0, 0)),
                      *_page_specs(page_block, layer, nsteps, pps),
                      *_page_specs(page_block, layer, nsteps, pps),
                      pl.BlockSpec((None, PAGE_SIZE, KV_WIDTH_B), const3),
                      pl.BlockSpec((None, PAGE_SIZE, KV_WIDTH_B), const3)],
            out_specs=pl.BlockSpec((None, N_KV_B, rows, HEAD_DIM_B), lambda b, j, pt: (b, 0, 0, 0)),
            scratch_shapes=[pltpu.VMEM((t, past), F32),
                            pltpu.VMEM((t, PAGE_SIZE), F32),
                            pltpu.VMEM((t, PAGE_SIZE), F32),
                            pltpu.VMEM((N_KV_B, rows, 1), F32),
                            pltpu.VMEM((N_KV_B, rows, 1), F32),
                            pltpu.VMEM((N_KV_B, rows, HEAD_DIM_B), F32)],
        ),
        out_shape=jax.ShapeDtypeStruct((nseq, N_KV_B, rows, HEAD_DIM_B), BF16),
        compiler_params=_cparams("parallel", "arbitrary"),
        name="dsa_sample_attn",
    )(page_table, scores, scores_new, q, *([cache_k] * pps), *([cache_v] * pps), k_new, v_new)


def _dsa_sample(ps, k_s, v_s, ki_s, page_table, cache_k, cache_v, cache_kidx, layer):
    nseq = page_table.shape[0]
    t = ps.shape[0] // nseq
    seg = lambda name, width: ps[:, PD_OFF[name]:PD_OFF[name] + width]
    qi = seg("qi", H_I * D_I).reshape(nseq, t * H_I, D_I)
    wi = seg("wi", H_I).reshape(nseq, t * H_I, 1)
    wv = jnp.broadcast_to(wi, (nseq, t * H_I, PAGE_SIZE))
    pad_rows = lambda a: jnp.pad(a.reshape(nseq, t, -1), ((0, 0), (0, PAGE_SIZE - t), (0, 0)))
    kidx_t = cache_kidx.transpose(0, 1, 3, 2)
    scores, scores_new = _dsa_sample_scores(page_table, qi, wv, kidx_t, layer,
                                            pad_rows(ki_s).transpose(0, 2, 1))
    q = seg("qb", WIDTH_B).reshape(nseq, t, N_KV_B, GROUP_B, HEAD_DIM_B)
    q = q.transpose(0, 2, 3, 1, 4).reshape(nseq, N_KV_B, GROUP_B * t, HEAD_DIM_B)
    pool_rows = lambda c: c.reshape(c.shape[0], c.shape[1], PAGE_SIZE * N_KV_B, HEAD_DIM_B)
    o = _dsa_sample_attn(page_table, scores, scores_new, q, pool_rows(cache_k), pool_rows(cache_v),
                         layer, pad_rows(k_s), pad_rows(v_s))
    o = o.reshape(nseq, N_KV_B, GROUP_B, t, HEAD_DIM_B).transpose(0, 3, 1, 2, 4)
    return o.reshape(nseq * t, WIDTH_B)


def _prep_w_mid(w_in):
    sizes = dict(zip(IN_NAMES, IN_SIZES))
    z0 = min(SRC_OFF[n] for n in MID_ORDER + ("aa", "ba"))
    z1 = max(SRC_OFF[n] + sizes[n] for n in MID_ORDER)
    zone = lax.optimization_barrier(lax.slice_in_dim(w_in, z0, z1, axis=2))
    cols = [zone[:, :, SRC_OFF[n] - z0:SRC_OFF[n] - z0 + sizes[n]].astype(BF16) for n in MID_ORDER]
    cols.append(jnp.zeros(w_in.shape[:2] + (PD_WIDTH - sum(sizes[n] for n in MID_ORDER),), BF16))
    return jnp.concatenate(cols, axis=2)


def kernel(x_prompt, x_sample, cache_k, cache_v, cache_kidx, page_table, state_gdn, state_gdn_conv,
           state_sconv, final_norm, norm1, norm2, w_in, conv_a, a_log, dt_bias, gdn_norm, conv_c,
           w_branch_a, w_branch_b, w_branch_c, w_o, w_gate, w_up, w_down):
    bp, tp, d = x_prompt.shape
    bs, ts = x_sample.shape[:2]
    mp, ms = bp * tp, bs * ts
    x = jnp.concatenate([x_prompt.reshape(mp, d), x_sample.reshape(ms, d)], axis=0)
    new_p = [[] for _ in range(6)]
    new_s = [[] for _ in range(6)]
    w_mid = _prep_w_mid(w_in)
    w_down16 = w_down.astype(BF16)
    for l in range(DEPTH):
        xn = _rmsnorm(x, norm1[l], BF16)
        p = _in_proj_main(xn, w_in, l)
        pd, k_new, v_new, ki_new = _in_proj_mid(xn, w_mid, l)
        g, beta = _gdn_gates(pd, a_log[l], dt_bias[l])
        conv_w = conv_a[l].astype(F32)

        oa_p, gdn_p = _gdn(p, 0, bp, tp, min(GDN_CHUNK, tp), conv_w, g, beta,
                           jnp.zeros((bp, CONV_A - 1, 3 * WIDTH_A), F32),
                           jnp.zeros((bp, H_A, DK_A, DV_A), F32), gdn_norm[l])
        oa_s, gdn_s = _gdn(p, mp, bs, ts, ts, conv_w, g, beta,
                           state_gdn_conv[l], state_gdn[l], gdn_norm[l])
        ob_p = _dsa_prompt(pd, k_new, v_new, ki_new, bp, tp)
        ob_s = _dsa_sample(pd[mp:], k_new[mp:], v_new[mp:], ki_new[mp:], page_table,
                           cache_k, cache_v, cache_kidx, l)
        cw = conv_c[l].astype(F32)
        oc_p, sconv_p = _sconv(p, 0, bp, tp, cw, jnp.zeros((bp, CONV_C - 1, WIDTH_C), F32))
        oc_s, sconv_s = _sconv(p, mp, bs, ts, cw, state_sconv[l])

        oa = jnp.concatenate([oa_p, oa_s], axis=0)
        ob = jnp.concatenate([ob_p, ob_s], axis=0)
        oc = jnp.concatenate([oc_p, oc_s], axis=0)
        merged = _merge(oa, ob, oc, w_branch_a, w_branch_b, w_branch_c, l, p)
        x = _out_proj(merged, w_o, l, x)

        hn = _rmsnorm(x, norm2[l], BF16)
        h = _swiglu(hn, w_gate, w_up, l)
        x = _out_proj(h, w_down16, l, x, name="ffn_down")

        nb = CONV_A - 1
        tail_rows = lambda r0, t: lax.slice(p, (r0 + max(t - nb, 0), 0), (r0 + t, 3 * WIDTH_A))
        gconv_p = jnp.stack([tail_rows(b * tp, tp) for b in range(bp)])
        gconv_s = jnp.stack([tail_rows(mp + b * ts, ts) for b in range(bs)])
        if tp < nb:
            gconv_p = jnp.concatenate([jnp.zeros((bp, nb - tp, 3 * WIDTH_A), F32), gconv_p], axis=1)
        if ts < nb:
            gconv_s = jnp.concatenate([state_gdn_conv[l].astype(F32)[:, ts:], gconv_s], axis=1)
        kv = lambda a, rows, b, t: a[rows].reshape(b, t, N_KV_B, HEAD_DIM_B)
        rp, rs = slice(0, mp), slice(mp, mp + ms)
        for lst, val in zip(new_p, (kv(k_new, rp, bp, tp), kv(v_new, rp, bp, tp),
                                    ki_new[rp].reshape(bp, tp, D_I), gdn_p, gconv_p, sconv_p)):
            lst.append(val)
        for lst, val in zip(new_s, (kv(k_new, rs, bs, ts), kv(v_new, rs, bs, ts),
                                    ki_new[rs].reshape(bs, ts, D_I), gdn_s, gconv_s, sconv_s)):
            lst.append(val)

    y_prompt = _rmsnorm(x, final_norm, F32, 0, mp).reshape(bp, tp, d)
    y_sample = _rmsnorm(x, final_norm, F32, mp, ms).reshape(bs, ts, d)
    outs_p = [jnp.stack(a) for a in new_p]
    outs_s = [jnp.stack(a) for a in new_s]
    return (y_prompt, y_sample, *outs_p, *outs_s)
```

```python
import functools
import math

import jax
import jax.numpy as jnp
from jax import lax
from jax.experimental import pallas as pl
from jax.experimental.pallas import tpu as pltpu

F32 = jnp.float32
BF16 = jnp.bfloat16

D_MODEL = 4096
DEPTH = 2
PAGE_SIZE = 128
H_A = 16
DK_A = 128
DV_A = 128
WIDTH_A = H_A * DV_A
CONV_A = 4
GDN_CHUNK = 64
H_B = 8
N_KV_B = 2
GROUP_B = H_B // N_KV_B
HEAD_DIM_B = 128
WIDTH_B = H_B * HEAD_DIM_B
KV_WIDTH_B = N_KV_B * HEAD_DIM_B
H_I = 32
D_I = 64
TOPK_MAX = 256
Q_BLOCK = 128
WIDTH_C = 1024
CONV_C = 3
D_FF = -(-8 * D_MODEL // (3 * 256)) * 256
EPS = 1e-6

IN_NAMES = ("qa", "ka", "va", "za", "aa", "ba", "qb", "kb", "vb", "qi", "ki", "wi",
            "gate_b", "gate_c", "hc", "ga", "gb", "gc")
IN_SIZES = (WIDTH_A, WIDTH_A, WIDTH_A, WIDTH_A, H_A, H_A,
            WIDTH_B, KV_WIDTH_B, KV_WIDTH_B, H_I * D_I, D_I, H_I,
            WIDTH_C, WIDTH_C, WIDTH_C, D_MODEL, D_MODEL, D_MODEL)

LANES = 128
N_BLOCK = 512
VMEM_LIMIT = 56 * 1024 * 1024
ACT_VMEM_BYTES = 32 * 1024 * 1024
NEG_BIG = -1e30

MAIN_ORDER = ("qa", "ka", "va", "za", "gate_b", "gate_c", "hc", "ga", "gb", "gc")
MID_ORDER = ("qi", "qb", "kb", "vb", "ki", "aa", "ba", "wi")


def _offsets(order):
    sizes = dict(zip(IN_NAMES, IN_SIZES))
    off, out = 0, {}
    for name in order:
        out[name] = off
        off += sizes[name]
    return out, off


SRC_OFF, _ = _offsets(IN_NAMES)
PM_OFF, PM_WIDTH = _offsets(MAIN_ORDER)
PD_OFF, _mid_cols = _offsets(MID_ORDER)
PD_WIDTH = -(-_mid_cols // N_BLOCK) * N_BLOCK
MAIN_RUN0 = PM_OFF["gate_b"]
MAIN_SHIFT = SRC_OFF["gate_b"] - MAIN_RUN0
assert PM_WIDTH % N_BLOCK == 0 and MAIN_RUN0 % N_BLOCK == 0 and MAIN_SHIFT % LANES == 0
assert all(SRC_OFF[n] == PM_OFF[n] for n in MAIN_ORDER[:4])
assert all(SRC_OFF[n] == PM_OFF[n] + MAIN_SHIFT for n in MAIN_ORDER[4:])
TAIL_OFF = PD_OFF["ki"]
TAIL_AA = PD_OFF["aa"] - TAIL_OFF
TAIL_BA = PD_OFF["ba"] - TAIL_OFF
TAIL_WI = PD_OFF["wi"] - TAIL_OFF
assert TAIL_OFF % N_BLOCK == 0 and PD_OFF["wi"] + H_I - TAIL_OFF == LANES


def _cparams(*sem):
    return pltpu.CompilerParams(dimension_semantics=sem, vmem_limit_bytes=VMEM_LIMIT)


def _dot(a, b):
    return jnp.dot(a, b, preferred_element_type=F32)


def _dot_nt(a, b):
    return lax.dot_general(a, b, (((1,), (1,)), ((), ())), preferred_element_type=F32)


def _dot_tn(a, b):
    return lax.dot_general(a, b, (((0,), (0,)), ((), ())), preferred_element_type=F32)


def _dot_hi(a, b):
    return jnp.dot(a, b, preferred_element_type=F32, precision=lax.Precision.HIGHEST)


def _sigmoid(x):
    return jax.nn.sigmoid(x)


def _silu(x):
    return x * jax.nn.sigmoid(x)


def _rmsnorm_body(x_ref, g_ref, o_ref):
    x = x_ref[...]
    ms = jnp.mean(x * x, axis=-1, keepdims=True)
    o_ref[...] = (x * lax.rsqrt(ms + EPS) * g_ref[...]).astype(o_ref.dtype)


def _row_block(m):
    for bm in (256, 192, 128, 64, 32, 16, 8):
        if m % bm == 0:
            return bm
    raise ValueError(f"unsupported row count {m}")


def _rmsnorm(x, g, out_dtype, row0=0, nrows=None):
    d = x.shape[1]
    m = x.shape[0] - row0 if nrows is None else nrows
    bm = _row_block(math.gcd(m, row0) if row0 else m)
    rb0 = row0 // bm
    return pl.pallas_call(
        _rmsnorm_body,
        grid=(m // bm,),
        in_specs=[pl.BlockSpec((bm, d), lambda i: (rb0 + i, 0)), pl.BlockSpec((1, d), lambda i: (0, 0))],
        out_specs=pl.BlockSpec((bm, d), lambda i: (i, 0)),
        out_shape=jax.ShapeDtypeStruct((m, d), out_dtype),
        compiler_params=_cparams("parallel"),
        name="rmsnorm",
    )(x, g.reshape(1, d).astype(F32))


def _big_row_block(m, k=D_MODEL, buffers=2):
    for bm in (2064, 1376, 1024, 688, 512, 256, 128, 64):
        if m % bm == 0 and buffers * bm * k * 2 <= ACT_VMEM_BYTES:
            return bm
    raise ValueError(f"unsupported row count {m}")


def _resident_rows(bm, k, buffers=2):
    if buffers == 1:
        return pl.BlockSpec((bm, k), lambda i, j: (i, 0), pipeline_mode=pl.Buffered(1))
    return pl.BlockSpec((bm, k), lambda i, j: (i, 0))


def _out_proj_body(a_ref, w_ref, r_ref, o_ref):
    o_ref[...] = r_ref[...] + _dot(a_ref[...], w_ref[...].astype(BF16))


def _out_proj(a, w, layer, residual, name="out_proj"):
    m, k = a.shape
    n = w.shape[2]
    bm, bn = _big_row_block(m, k), N_BLOCK // 2
    return pl.pallas_call(
        _out_proj_body,
        grid=(m // bm, n // bn),
        in_specs=[_resident_rows(bm, k),
                  pl.BlockSpec((None, k, bn), lambda i, j: (layer, 0, j)),
                  pl.BlockSpec((bm, bn), lambda i, j: (i, j))],
        out_specs=pl.BlockSpec((bm, bn), lambda i, j: (i, j)),
        out_shape=jax.ShapeDtypeStruct((m, n), F32),
        compiler_params=_cparams("parallel", "arbitrary"),
        name=name,
    )(a, w, residual)


IN_PROJ_PIECES = N_BLOCK // LANES


def _in_proj_main_body(a_ref, *rest):
    w_refs, p_ref = rest[:IN_PROJ_PIECES], rest[IN_PROJ_PIECES]
    w = jnp.concatenate([r[...].astype(BF16) for r in w_refs], axis=1)
    p_ref[...] = _dot(a_ref[...], w)


def _in_proj_main(a, w_in, layer):
    m, k = a.shape
    bm, bn = _big_row_block(m, k, buffers=1), N_BLOCK
    run0_blk, shift_blk = MAIN_RUN0 // bn, MAIN_SHIFT // LANES

    def piece(q):
        def index(i, j):
            return (layer, 0, IN_PROJ_PIECES * j + q + jnp.where(j >= run0_blk, shift_blk, 0))
        return pl.BlockSpec((None, k, LANES), index)

    return pl.pallas_call(
        _in_proj_main_body,
        grid=(m // bm, PM_WIDTH // bn),
        in_specs=[_resident_rows(bm, k, buffers=1)] + [piece(q) for q in range(IN_PROJ_PIECES)],
        out_specs=pl.BlockSpec((bm, bn), lambda i, j: (i, j)),
        out_shape=jax.ShapeDtypeStruct((m, PM_WIDTH), F32),
        compiler_params=_cparams("parallel", "arbitrary"),
        name="in_proj_main",
    )(a, *([w_in] * IN_PROJ_PIECES))


def _in_proj_mid_body(a_ref, w_ref, p_ref, k_ref, v_ref, ki_ref, *, jkv, jtail):
    acc = _dot(a_ref[...], w_ref[...])
    p_ref[...] = acc
    j = pl.program_id(1)

    @pl.when(j == jkv)
    def _():
        k_ref[...] = acc[:, :KV_WIDTH_B]
        v_ref[...] = acc[:, KV_WIDTH_B:2 * KV_WIDTH_B]

    @pl.when(j == jtail)
    def _():
        ki_ref[...] = acc[:, :D_I]


def _in_proj_mid(a, w_mid, layer):
    m, k = a.shape
    bm, bn = _big_row_block(m), N_BLOCK
    assert PD_OFF["kb"] % bn == 0 and PD_OFF["vb"] == PD_OFF["kb"] + KV_WIDTH_B
    row = lambda i, j: (i, 0)
    return pl.pallas_call(
        functools.partial(_in_proj_mid_body, jkv=PD_OFF["kb"] // bn, jtail=TAIL_OFF // bn),
        grid=(m // bm, PD_WIDTH // bn),
        in_specs=[_resident_rows(bm, k), pl.BlockSpec((None, k, bn), lambda i, j: (layer, 0, j))],
        out_specs=[pl.BlockSpec((bm, bn), lambda i, j: (i, j)),
                   pl.BlockSpec((bm, KV_WIDTH_B), row), pl.BlockSpec((bm, KV_WIDTH_B), row),
                   pl.BlockSpec((bm, D_I), row)],
        out_shape=[jax.ShapeDtypeStruct((m, PD_WIDTH), F32), jax.ShapeDtypeStruct((m, KV_WIDTH_B), F32),
                   jax.ShapeDtypeStruct((m, KV_WIDTH_B), F32), jax.ShapeDtypeStruct((m, D_I), F32)],
        compiler_params=_cparams("parallel", "arbitrary"),
        name="in_proj_mid",
    )(a, w_mid)


def _swiglu_body(a_ref, wg_ref, wu_ref, o_ref):
    a = a_ref[...]
    g = _dot(a, wg_ref[...].astype(BF16))
    u = _dot(a, wu_ref[...].astype(BF16))
    o_ref[...] = (_silu(g) * u).astype(o_ref.dtype)


def _swiglu(a, w_gate, w_up, layer):
    m, k = a.shape
    n = w_gate.shape[2]
    bm, bn = _big_row_block(m, k, buffers=1), N_BLOCK // 2
    assert n % bn == 0
    wspec = pl.BlockSpec((None, k, bn), lambda i, j: (layer, 0, j))
    return pl.pallas_call(
        _swiglu_body,
        grid=(m // bm, n // bn),
        in_specs=[_resident_rows(bm, k, buffers=1), wspec, wspec],
        out_specs=pl.BlockSpec((bm, bn), lambda i, j: (i, j)),
        out_shape=jax.ShapeDtypeStruct((m, n), BF16),
        compiler_params=_cparams("parallel", "arbitrary"),
        name="swiglu",
    )(a, w_gate, w_up)


def _merge_body(oa_ref, ob_ref, oc_ref, wa_ref, wb_ref, wc_ref, ga_ref, gb_ref, gc_ref, o_ref):
    acc = _sigmoid(ga_ref[...]) * _dot(oa_ref[...], wa_ref[...].astype(BF16))
    acc += _sigmoid(gb_ref[...]) * _dot(ob_ref[...], wb_ref[...].astype(BF16))
    acc += _sigmoid(gc_ref[...]) * _dot(oc_ref[...], wc_ref[...].astype(BF16))
    o_ref[...] = acc.astype(o_ref.dtype)


def _merge(oa, ob, oc, wa, wb, wc, layer, p):
    m = oa.shape[0]
    n = wa.shape[2]
    bm, bn = _big_row_block(m), N_BLOCK // 2
    ga0, gb0, gc0 = (PM_OFF[s] // bn for s in ("ga", "gb", "gc"))
    row = lambda i, j: (i, 0)
    wspec = lambda w: pl.BlockSpec((None, w.shape[1], bn), lambda i, j: (layer, 0, j))
    return pl.pallas_call(
        _merge_body,
        grid=(m // bm, n // bn),
        in_specs=[_resident_rows(bm, oa.shape[1]), _resident_rows(bm, ob.shape[1]),
                  _resident_rows(bm, oc.shape[1]),
                  wspec(wa), wspec(wb), wspec(wc),
                  pl.BlockSpec((bm, bn), lambda i, j: (i, ga0 + j)),
                  pl.BlockSpec((bm, bn), lambda i, j: (i, gb0 + j)),
                  pl.BlockSpec((bm, bn), lambda i, j: (i, gc0 + j))],
        out_specs=pl.BlockSpec((bm, bn), lambda i, j: (i, j)),
        out_shape=jax.ShapeDtypeStruct((m, n), BF16),
        compiler_params=_cparams("parallel", "arbitrary"),
        name="merge",
    )(oa, ob, oc, wa, wb, wc, p, p, p)


GDN_HEADS_PER_STEP = 16
GDN_INV_BASE = 8


def _gdn_body(q_ref, k_ref, v_ref, z_ref, wq_ref, wk_ref, wv_ref, tq_ref, tk_ref, tv_ref,
              g_ref, b_ref, s0_ref, ng_ref, o_ref, sout_ref, s_scr, tail_scr,
              *, chunk, hg):
    c = chunk
    w = hg * DK_A
    n = pl.program_id(2)

    @pl.when(n == 0)
    def _():
        s_scr[...] = s0_ref[...].astype(F32)
        tail_scr[:, 0:w] = tq_ref[...]
        tail_scr[:, w:2 * w] = tk_ref[...]
        tail_scr[:, 2 * w:3 * w] = tv_ref[...]

    act = []
    for idx, (x_ref, w_ref) in enumerate(((q_ref, wq_ref), (k_ref, wk_ref), (v_ref, wv_ref))):
        x = x_ref[...]
        cw = w_ref[...]
        xfull = jnp.concatenate([tail_scr[:, idx * w:(idx + 1) * w], x], axis=0)
        y = x * cw[CONV_A - 1:CONV_A, :]
        for j in range(1, CONV_A):
            y = y + pltpu.roll(xfull, j, axis=0)[8:, :] * cw[CONV_A - 1 - j:CONV_A - j, :]
        tail_scr[:, idx * w:(idx + 1) * w] = xfull[c:c + 8, :]
        act.append(_silu(y))
    qs, ks, vs = act
    z = z_ref[...]

    ii = lax.broadcasted_iota(jnp.int32, (c, c), 0)
    jj = lax.broadcasted_iota(jnp.int32, (c, c), 1)
    incl = jj <= ii
    strict = jj < ii
    eye = (ii == jj).astype(F32)
    ng = ng_ref[...]
    tail = g_ref[...]
    gate_p = b_ref[...]
    xg = tail[:, TAIL_AA:TAIL_AA + H_A] + gate_p[1:2, :]
    softplus = jnp.maximum(xg, 0.0) + jnp.log1p(jnp.exp(-jnp.abs(xg)))
    b_blk = _sigmoid(tail[:, TAIL_BA:TAIL_BA + H_A])
    gcum_all = _dot_hi(incl.astype(F32), -jnp.exp(gate_p[0:1, :]) * softplus)
    gcum_t = gcum_all.T

    heads = range(hg)
    hsl = [slice(hh * DK_A, (hh + 1) * DK_A) for hh in heads]
    q_l, k_l, k16_l, decay_l, egc_l, kdec_l, glast_l, mm_l, rhs_l = ([] for _ in range(9))
    for hh in heads:
        q = qs[:, hsl[hh]]
        k = ks[:, hsl[hh]]
        q = q * lax.rsqrt(jnp.sum(q * q, axis=-1, keepdims=True) + EPS) * (DK_A ** -0.5)
        k = k * lax.rsqrt(jnp.sum(k * k, axis=-1, keepdims=True) + EPS)
        bc = b_blk[:, hh:hh + 1]
        gcum_c = gcum_all[:, hh:hh + 1]
        gcum_r = gcum_t[hh:hh + 1, :]
        decay = jnp.exp(jnp.where(incl, gcum_c - gcum_r, -jnp.inf))
        kb = k * bc
        k16 = k.astype(BF16)
        egc = jnp.exp(gcum_c)
        glast = gcum_c[c - 1:c, :]
        mm = jnp.where(strict, _dot_nt(kb.astype(BF16), k16) * decay, 0.0)
        q_l.append(q)
        k_l.append(k)
        k16_l.append(k16)
        decay_l.append(decay)
        egc_l.append(egc)
        glast_l.append(glast)
        kdec_l.append(jnp.exp(glast - gcum_c))
        rhs_l.append(jnp.concatenate([vs[:, hsl[hh]] * bc, kb * egc], axis=1).astype(BF16))
        mm_l.append(mm)
    base = min(c, GDN_INV_BASE)
    log_base = base.bit_length() - 1
    in_base = jnp.right_shift(ii, log_base) == jnp.right_shift(jj, log_base)
    pw_l = [jnp.where(in_base, -mm, 0.0) for mm in mm_l]
    tinv_l = [eye + pw for pw in pw_l]
    for _ in range(log_base - 1):
        for hh in heads:
            pw16 = pw_l[hh].astype(BF16)
            pw_l[hh] = _dot(pw16, pw16)
        for hh in heads:
            tinv_l[hh] = tinv_l[hh] + _dot(tinv_l[hh].astype(BF16), pw_l[hh].astype(BF16))
    log_size = log_base
    while (1 << log_size) < c:
        lower_left = jnp.logical_and(
            jnp.right_shift(ii, log_size + 1) == jnp.right_shift(jj, log_size + 1),
            jnp.logical_and(jnp.bitwise_and(jnp.right_shift(ii, log_size), 1) == 1,
                            jnp.bitwise_and(jnp.right_shift(jj, log_size), 1) == 0))
        for hh in heads:
            t16 = tinv_l[hh].astype(BF16)
            c16 = jnp.where(lower_left, mm_l[hh], 0.0).astype(BF16)
            tinv_l[hh] = tinv_l[hh] - _dot(_dot(t16, c16).astype(BF16), t16)
        log_size += 1
    sol_l = [_dot(tinv_l[hh].astype(BF16), rhs_l[hh]) for hh in heads]
    attn_l = [(_dot_nt(q_l[hh].astype(BF16), k16_l[hh]) * decay_l[hh]).astype(BF16) for hh in heads]
    s_l = [s_scr[hh] for hh in heads]
    s16_l = [s_l[hh].astype(BF16) for hh in heads]
    v16_l = [(sol_l[hh][:, :DV_A] - _dot(sol_l[hh][:, DV_A:].astype(BF16), s16_l[hh])).astype(BF16)
             for hh in heads]
    o_l = [_dot((q_l[hh] * egc_l[hh]).astype(BF16), s16_l[hh]) + _dot(attn_l[hh], v16_l[hh]) for hh in heads]
    snew_l = [s_l[hh] * jnp.exp(glast_l[hh]) + _dot_tn((k_l[hh] * kdec_l[hh]).astype(BF16), v16_l[hh])
              for hh in heads]
    s_scr[...] = jnp.stack(snew_l, axis=0)
    o_l = [o * lax.rsqrt(jnp.mean(o * o, axis=-1, keepdims=True) + EPS) * ng for o in o_l]
    o_ref[...] = (jnp.concatenate(o_l, axis=1) * _silu(z)).astype(o_ref.dtype)

    @pl.when(n == pl.num_programs(2) - 1)
    def _():
        sout_ref[...] = s_scr[...].astype(sout_ref.dtype)


def _gdn(p, row0, nseq, t, chunk, conv_w, g, beta, conv_buf, s0, norm_g):
    hg = GDN_HEADS_PER_STEP
    assert hg == H_A
    w = hg * DK_A
    nch = t // chunk
    nhg = H_A // hg
    rb0 = row0 // chunk
    tail_blk = TAIL_OFF // LANES
    gspec = pl.BlockSpec((chunk, LANES), lambda b, h, n: (rb0 + b * nch + n, tail_blk))
    tail = jnp.pad(conv_buf.astype(F32), ((0, 0), (8 - (CONV_A - 1), 0), (0, 0)))
    qb0, kb0, vb0, zb0 = (PM_OFF[s] // w for s in ("qa", "ka", "va", "za"))
    pspec = lambda b0: pl.BlockSpec((chunk, w), lambda b, h, n: (rb0 + b * nch + n, b0 + h))
    wspec = lambda b0: pl.BlockSpec((CONV_A, w), lambda b, h, n: (0, b0 + h))
    tspec = lambda b0: pl.BlockSpec((None, 8, w), lambda b, h, n: (b, 0, b0 + h))
    o, s_out = pl.pallas_call(
        functools.partial(_gdn_body, chunk=chunk, hg=hg),
        grid=(nseq, nhg, nch),
        in_specs=[pspec(qb0), pspec(kb0), pspec(vb0), pspec(zb0),
                  wspec(0), wspec(nhg), wspec(2 * nhg),
                  tspec(0), tspec(nhg), tspec(2 * nhg),
                  gspec, pl.BlockSpec((2, H_A), lambda b, h, n: (0, 0)),
                  pl.BlockSpec((None, hg, DK_A, DV_A), lambda b, h, n: (b, h, 0, 0)),
                  pl.BlockSpec((1, DV_A), lambda b, h, n: (0, 0))],
        out_specs=[pl.BlockSpec((chunk, w), lambda b, h, n: (b * nch + n, h)),
                   pl.BlockSpec((None, hg, DK_A, DV_A), lambda b, h, n: (b, h, 0, 0))],
        out_shape=[jax.ShapeDtypeStruct((nseq * t, WIDTH_A), BF16),
                   jax.ShapeDtypeStruct((nseq, H_A, DK_A, DV_A), s0.dtype)],
        scratch_shapes=[pltpu.VMEM((hg, DK_A, DV_A), F32), pltpu.VMEM((8, 3 * w), F32)],
        compiler_params=_cparams("parallel", "parallel", "arbitrary"),
        name=f"gdn_c{chunk}",
    )(p, p, p, p, conv_w, conv_w, conv_w, tail, tail, tail, g, beta, s0,
      norm_g.reshape(1, DV_A).astype(F32))
    return o, s_out


SCONV_COLS = 256


def _sconv_body(gb_ref, gc_ref, hc_ref, w_ref, buf_ref, o_ref, st_ref, *, t):
    pr = gc_ref[...] * hc_ref[...]
    cw = w_ref[...]
    xfull = jnp.concatenate([buf_ref[...], pr], axis=0)
    y = pr * cw[CONV_C - 1:CONV_C, :]
    for j in range(1, CONV_C):
        y = y + pltpu.roll(xfull, j, axis=0)[8:, :] * cw[CONV_C - 1 - j:CONV_C - j, :]
    o_ref[...] = (gb_ref[...] * y).astype(o_ref.dtype)
    st_ref[...] = xfull[t:t + 8, :]


def _sconv(p, row0, nseq, t, conv_w, buf):
    cols = SCONV_COLS
    ncb = WIDTH_C // cols
    rb0 = row0 // t
    b0, c0, h0 = (PM_OFF[s] // cols for s in ("gate_b", "gate_c", "hc"))
    buf8 = jnp.pad(buf.astype(F32), ((0, 0), (8 - (CONV_C - 1), 0), (0, 0)))
    pspec = lambda o: pl.BlockSpec((t, cols), lambda b, j: (rb0 + b, o + j))
    o, st = pl.pallas_call(
        functools.partial(_sconv_body, t=t),
        grid=(nseq, ncb),
        in_specs=[pspec(b0), pspec(c0), pspec(h0),
                  pl.BlockSpec((CONV_C, cols), lambda b, j: (0, j)),
                  pl.BlockSpec((None, 8, cols), lambda b, j: (b, 0, j))],
        out_specs=[pl.BlockSpec((t, cols), lambda b, j: (b, j)),
                   pl.BlockSpec((None, 8, cols), lambda b, j: (b, 0, j))],
        out_shape=[jax.ShapeDtypeStruct((nseq * t, WIDTH_C), BF16),
                   jax.ShapeDtypeStruct((nseq, 8, WIDTH_C), F32)],
        compiler_params=_cparams("parallel", "parallel"),
        name=f"sconv_t{t}",
    )(p, p, p, conv_w, buf8)
    return o, st[:, 8 - (CONV_C - 1):, :]


BISECT_EVERY = 8
MAX_SEARCH = 2400


def _select_threshold(count_ge, lo0, hi0, n_valid, k):
    kf = jnp.float32(k)

    def cond(st):
        return jnp.logical_and(st[0] < MAX_SEARCH, jnp.min(st[-1]) < 0.5)

    def step(st):
        it, lo, hi, glo, ghi, t, side, done = st
        half = 0.5 * lo + 0.5 * hi
        cand = lo + (hi - lo) * (glo / jnp.maximum(glo - ghi, 1e-9))
        mid = jnp.where(it % BISECT_EVERY == BISECT_EVERY - 1, half, cand)
        mid = jnp.where(jnp.logical_and(mid > lo, mid < hi), mid, half)
        adjacent = jnp.logical_not(jnp.logical_and(mid > lo, mid < hi))
        first = it == 0
        mid = jnp.where(first, hi, mid)
        adjacent = jnp.logical_and(adjacent, jnp.logical_not(first))
        g = count_ge(mid) - kf
        ge = g >= 0.0
        finish = jnp.logical_or(jnp.logical_or(g == 0.0, adjacent), jnp.logical_and(first, ge))
        newly = jnp.logical_and(finish, done < 0.5)
        t = jnp.where(newly, jnp.where(adjacent, lo, mid), t)
        new_side = jnp.where(ge, 1.0, -1.0)
        damp = jnp.where(jnp.logical_and(new_side == side, jnp.logical_not(first)), 0.5, 1.0)
        glo = jnp.where(ge, g, glo * damp)
        ghi = jnp.where(ge, ghi * damp, g)
        lo = jnp.where(ge, mid, lo)
        hi = jnp.where(ge, hi, mid)
        done = jnp.where(finish, 1.0, done)
        return it + 1, lo, hi, glo, ghi, t, new_side, done

    done0 = jnp.where(n_valid > kf, 0.0, 1.0)
    st = lax.while_loop(cond, lambda st: step(step(st)), (jnp.int32(0), lo0, hi0, n_valid - kf, jnp.full_like(lo0, -kf),
                                     lo0, jnp.zeros_like(lo0), done0))
    return st[5]


KEY_CHUNK = 512


def _dsa_prompt_body(qi_ref, tq_ref, qb_ref, kb_ref, vb_ref, tk_ref, o_ref,
                     sc_scr, m_scr, l_scr, acc_scr, *, tq, sc, topk):
    i = pl.program_id(1)
    nck = (i * tq + tq - 1) // sc + 1
    wt = tq_ref[...].T[TAIL_WI:TAIL_WI + H_I, :] * ((D_I ** -0.5) * (H_I ** -0.5))
    qi = qi_ref[...]
    pairs = [jnp.concatenate([qi[:, (2 * p) * D_I:(2 * p + 1) * D_I],
                              qi[:, (2 * p + 1) * D_I:(2 * p + 2) * D_I]], axis=0).astype(BF16)
             for p in range(H_I // 2)]
    tpos = i * tq + lax.broadcasted_iota(jnp.int32, (1, tq), 1)

    def score_chunk(c, carry):
        r0 = pl.multiple_of(c * sc, sc)
        kic = tk_ref[pl.ds(r0, sc), :].astype(BF16)
        acc = jnp.zeros((sc, tq), F32)
        for p in range(H_I // 2):
            d = _dot_nt(kic, pairs[p])
            acc = acc + jnp.maximum(d[:, :tq], 0.0) * wt[2 * p:2 * p + 1, :]
            acc = acc + jnp.maximum(d[:, tq:], 0.0) * wt[2 * p + 1:2 * p + 2, :]
        kpos = r0 + lax.broadcasted_iota(jnp.int32, (sc, tq), 0)
        sc_scr[pl.ds(r0, sc), :] = jnp.where(kpos <= tpos, acc, -jnp.inf)
        return carry

    lax.fori_loop(0, nck, score_chunk, 0)

    def minmax_chunk(c, carry):
        mn, mx = carry
        x = sc_scr[pl.ds(pl.multiple_of(c * sc, sc), sc), :]
        mx = jnp.maximum(mx, jnp.max(x, axis=0, keepdims=True))
        mn = jnp.minimum(mn, jnp.min(jnp.where(x == -jnp.inf, jnp.inf, x), axis=0, keepdims=True))
        return mn, mx

    mn, mx = lax.fori_loop(0, nck, minmax_chunk,
                           (jnp.full((1, tq), jnp.inf, F32), jnp.full((1, tq), -jnp.inf, F32)))

    def count_ge(mid):
        def body(c, acc):
            x = sc_scr[pl.ds(pl.multiple_of(c * sc, sc), sc), :]
            hit = jnp.where(x >= mid, 1.0, 0.0).reshape(8, sc // 64, 8, tq)
            return acc + jnp.sum(jnp.sum(hit, axis=1), axis=0)
        return jnp.sum(lax.fori_loop(0, nck, body, jnp.zeros((8, tq), F32)), axis=0, keepdims=True)

    thr = _select_threshold(count_ge, mn, mx, (tpos + 1).astype(F32), topk)

    def count_gt(c, acc):
        x = sc_scr[pl.ds(pl.multiple_of(c * sc, sc), sc), :]
        hit = jnp.where(x > thr, 1.0, 0.0).reshape(8, sc // 64, 8, tq)
        return acc + jnp.sum(jnp.sum(hit, axis=1), axis=0)
    n_above = jnp.sum(lax.fori_loop(0, nck, count_gt, jnp.zeros((8, tq), F32)), axis=0, keepdims=True)
    crowded = count_ge(thr) > topk
    places = topk - n_above

    @pl.when(jnp.max(jnp.where(crowded, 1.0, 0.0)) > 0.5)
    def _():
        earlier = (lax.broadcasted_iota(jnp.int32, (sc, sc), 1)
                   < lax.broadcasted_iota(jnp.int32, (sc, sc), 0)).astype(BF16)

        def retire(c, seen):
            rows = pl.ds(pl.multiple_of(c * sc, sc), sc)
            x = sc_scr[rows, :]
            tie = jnp.logical_and(x == thr, crowded)
            tie_f = jnp.where(tie, 1.0, 0.0)
            rank = seen + _dot(earlier, tie_f.astype(BF16))
            sc_scr[rows, :] = jnp.where(jnp.logical_and(tie, rank >= places), -jnp.inf, x)
            return seen + jnp.sum(tie_f, axis=0, keepdims=True)

        lax.fori_loop(0, nck, retire, jnp.zeros((1, tq), F32))

    m_scr[...] = jnp.full(m_scr.shape, NEG_BIG, F32)
    l_scr[...] = jnp.zeros(l_scr.shape, F32)
    acc_scr[...] = jnp.zeros(acc_scr.shape, F32)
    q = qb_ref[...].astype(BF16)
    scale = HEAD_DIM_B ** -0.5

    def attn_chunk(c, carry):
        r0 = pl.multiple_of(c * sc, sc)
        sel = sc_scr[pl.ds(r0, sc), :] >= thr
        for n in range(N_KV_B):
            kc = kb_ref[pl.ds(r0, sc), n * HEAD_DIM_B:(n + 1) * HEAD_DIM_B].astype(BF16)
            vc = vb_ref[pl.ds(r0, sc), n * HEAD_DIM_B:(n + 1) * HEAD_DIM_B].astype(BF16)
            for g in range(GROUP_B):
                h = n * GROUP_B + g
                hs = slice(h * HEAD_DIM_B, (h + 1) * HEAD_DIM_B)
                s = jnp.where(sel, _dot_nt(kc, q[:, hs]) * scale, NEG_BIG)
                m_old = m_scr[h:h + 1, :]
                m_new = jnp.maximum(m_old, jnp.max(s, axis=0, keepdims=True))
                pexp = jnp.exp(s - m_new)
                alpha = jnp.exp(m_old - m_new)
                l_scr[h:h + 1, :] = alpha * l_scr[h:h + 1, :] + jnp.sum(pexp, axis=0, keepdims=True)
                acc_scr[hs, :] = alpha * acc_scr[hs, :] + _dot_tn(vc, pexp.astype(BF16))
                m_scr[h:h + 1, :] = m_new
        return carry

    lax.fori_loop(0, nck, attn_chunk, 0)

    for h in range(H_B):
        hs = slice(h * HEAD_DIM_B, (h + 1) * HEAD_DIM_B)
        ot = acc_scr[hs, :] / l_scr[h:h + 1, :]
        o_ref[:, hs] = ot.T.astype(o_ref.dtype)


def _dsa_prompt(p, k_new, v_new, ki_new, nseq, s):
    tq, sc = Q_BLOCK, min(KEY_CHUNK, s)
    topk = min(TOPK_MAX, s // 4)
    nqb = s // tq
    qi0 = PD_OFF["qi"] // (H_I * D_I)
    qb0 = PD_OFF["qb"] // WIDTH_B
    tl0 = TAIL_OFF // LANES
    return pl.pallas_call(
        functools.partial(_dsa_prompt_body, tq=tq, sc=sc, topk=topk),
        grid=(nseq, nqb),
        in_specs=[pl.BlockSpec((tq, H_I * D_I), lambda b, i: (b * nqb + i, qi0)),
                  pl.BlockSpec((tq, LANES), lambda b, i: (b * nqb + i, tl0)),
                  pl.BlockSpec((tq, WIDTH_B), lambda b, i: (b * nqb + i, qb0)),
                  pl.BlockSpec((s, KV_WIDTH_B), lambda b, i: (b, 0)),
                  pl.BlockSpec((s, KV_WIDTH_B), lambda b, i: (b, 0)),
                  pl.BlockSpec((s, D_I), lambda b, i: (b, 0))],
        out_specs=pl.BlockSpec((tq, WIDTH_B), lambda b, i: (b * nqb + i, 0)),
        out_shape=jax.ShapeDtypeStruct((nseq * s, WIDTH_B), BF16),
        scratch_shapes=[pltpu.VMEM((s, tq), F32), pltpu.VMEM((H_B, tq), F32),
                        pltpu.VMEM((H_B, tq), F32), pltpu.VMEM((WIDTH_B, tq), F32)],
        compiler_params=_cparams("parallel", "arbitrary"),
        name="dsa_prompt",
    )(p, p, p, k_new, v_new, ki_new)


PAGES_PER_STEP = 32
TIE_CHUNK = 512


def _dsa_sample_scores_body(pt_ref, q_ref, w_ref, *rest, nsteps, pps, t):
    page_refs, new_ref, o_ref, onew_ref = rest[:pps], rest[pps], rest[pps + 1], rest[pps + 2]
    j = pl.program_id(1)
    q = q_ref[...].astype(BF16)
    wv = w_ref[...] * ((D_I ** -0.5) * (H_I ** -0.5))

    def scores(keys_t):
        d = _dot(q, keys_t.astype(BF16))
        return jnp.sum((jnp.maximum(d, 0.0) * wv).reshape(t, H_I, PAGE_SIZE), axis=1)

    @pl.when(j < nsteps)
    def _():
        o_ref[...] = jnp.concatenate([scores(r[...]) for r in page_refs], axis=1)

    @pl.when(j == nsteps)
    def _():
        knew = lax.broadcasted_iota(jnp.int32, (t, PAGE_SIZE), 1)
        tnew = lax.broadcasted_iota(jnp.int32, (t, PAGE_SIZE), 0)
        onew_ref[...] = jnp.where(knew <= tnew, scores(new_ref[...]), -jnp.inf)


def _page_specs(block, layer, nsteps, pps):
    def spec(p):
        def index(b, j, pt):
            return (layer, pt[b, jnp.minimum(j, nsteps - 1) * pps + p]) + (0,) * (len(block) - 2)
        return pl.BlockSpec(block, index)
    return [spec(p) for p in range(pps)]


def _dsa_sample_scores(page_table, q, wv, cache_kidx, layer, ki_new):
    nseq, npages = page_table.shape
    t = q.shape[1] // H_I
    pps = math.gcd(PAGES_PER_STEP, npages)
    nsteps = npages // pps
    return pl.pallas_call(
        functools.partial(_dsa_sample_scores_body, nsteps=nsteps, pps=pps, t=t),
        grid_spec=pltpu.PrefetchScalarGridSpec(
            num_scalar_prefetch=1,
            grid=(nseq, nsteps + 1),
            in_specs=[pl.BlockSpec((None, t * H_I, D_I), lambda b, j, pt: (b, 0, 0)),
                      pl.BlockSpec((None, t * H_I, PAGE_SIZE), lambda b, j, pt: (b, 0, 0)),
                      *_page_specs((None, None, D_I, PAGE_SIZE), layer, nsteps, pps),
                      pl.BlockSpec((None, D_I, PAGE_SIZE), lambda b, j, pt: (b, 0, 0))],
            out_specs=[pl.BlockSpec((None, t, pps * PAGE_SIZE),
                                    lambda b, j, pt: (b, 0, jnp.minimum(j, nsteps - 1))),
                       pl.BlockSpec((None, t, PAGE_SIZE), lambda b, j, pt: (b, 0, 0))],
        ),
        out_shape=[jax.ShapeDtypeStruct((nseq, t, npages * PAGE_SIZE), F32),
                   jax.ShapeDtypeStruct((nseq, t, PAGE_SIZE), F32)],
        compiler_params=_cparams("parallel", "arbitrary"),
        name="dsa_sample_scores",
    )(page_table, q, wv, *([cache_kidx] * pps), ki_new)


def _dsa_sample_attn_body(pt_ref, sc_ref, scn_ref, q_ref, *rest, nsteps, pps, t, past, topk, tie_chunk):
    kpage_refs, vpage_refs = rest[:pps], rest[pps:2 * pps]
    knew_ref, vnew_ref, o_ref, scp_scr, scn_scr, thr_scr, m_scr, l_scr, acc_scr = rest[2 * pps:]
    j = pl.program_id(1)

    @pl.when(j == 0)
    def _():
        xp, xn = sc_ref[...], scn_ref[...]
        scp_scr[...] = xp
        scn_scr[...] = xn
        mx = jnp.maximum(jnp.max(xp, axis=1, keepdims=True), jnp.max(xn, axis=1, keepdims=True))
        mn = jnp.minimum(jnp.min(xp, axis=1, keepdims=True),
                         jnp.min(jnp.where(xn == -jnp.inf, jnp.inf, xn), axis=1, keepdims=True))
        cnt = lambda hit: jnp.sum(jnp.where(hit, 1.0, 0.0), axis=1, keepdims=True)

        def count_ge(mid):
            return cnt(sc_ref[...] >= mid) + cnt(scn_ref[...] >= mid)

        n_valid = (past + 1 + lax.broadcasted_iota(jnp.int32, (t, 1), 0)).astype(F32)
        thr = _select_threshold(count_ge, mn, mx, n_valid, topk)

        crowded = count_ge(thr) > topk
        places = topk - (cnt(xp > thr) + cnt(xn > thr))

        @pl.when(jnp.max(jnp.where(crowded, 1.0, 0.0)) > 0.5)
        def _():
            cw = tie_chunk
            earlier = (lax.broadcasted_iota(jnp.int32, (cw, cw), 0)
                       < lax.broadcasted_iota(jnp.int32, (cw, cw), 1)).astype(BF16)

            def retire(x, seen, width):
                tie = jnp.logical_and(x == thr, crowded)
                tie_f = jnp.where(tie, 1.0, 0.0)
                rank = seen + _dot(tie_f.astype(BF16), earlier[:width, :width])
                return (jnp.where(jnp.logical_and(tie, rank >= places), -jnp.inf, x),
                        seen + jnp.sum(tie_f, axis=1, keepdims=True))

            def past_chunk(c, seen):
                cols = pl.ds(pl.multiple_of(c * cw, cw), cw)
                x, seen = retire(scp_scr[:, cols], seen, cw)
                scp_scr[:, cols] = x
                return seen

            seen = lax.fori_loop(0, past // cw, past_chunk, jnp.zeros((t, 1), F32))
            scn_scr[...] = retire(scn_scr[...], seen, PAGE_SIZE)[0]

        thr_scr[...] = jnp.broadcast_to(thr, thr_scr.shape)
        m_scr[...] = jnp.full(m_scr.shape, NEG_BIG, F32)
        l_scr[...] = jnp.zeros(l_scr.shape, F32)
        acc_scr[...] = jnp.zeros(acc_scr.shape, F32)

    scale = HEAD_DIM_B ** -0.5

    def attend(x, keys, vals):
        sel_t = jnp.where(x >= thr_scr[:, 0:1], 1.0, 0.0)
        sel = jnp.concatenate([sel_t] * GROUP_B, axis=0) > 0.5
        for n in range(N_KV_B):
            s = jnp.where(sel, _dot_nt(q_ref[n].astype(BF16), keys(n).astype(BF16)) * scale, NEG_BIG)
            m_old = m_scr[n]
            m_new = jnp.maximum(m_old, jnp.max(s, axis=1, keepdims=True))
            pexp = jnp.exp(s - m_new)
            alpha = jnp.exp(m_old - m_new)
            l_scr[n] = alpha * l_scr[n] + jnp.sum(pexp, axis=1, keepdims=True)
            acc_scr[n] = alpha * acc_scr[n] + _dot(pexp.astype(BF16), vals(n).astype(BF16))
            m_scr[n] = m_new

    def page_head(r, n):
        return r[pl.ds(n, PAGE_SIZE, stride=N_KV_B), :]

    @pl.when(j < nsteps)
    def _():
        width = pps * PAGE_SIZE
        x = scp_scr[:, pl.ds(pl.multiple_of(j * width, width), width)]
        attend(x,
               lambda n: jnp.concatenate([page_head(r, n) for r in kpage_refs], axis=0),
               lambda n: jnp.concatenate([page_head(r, n) for r in vpage_refs], axis=0))

    @pl.when(j == nsteps)
    def _():
        hs = lambda n: slice(n * HEAD_DIM_B, (n + 1) * HEAD_DIM_B)
        attend(scn_scr[...], lambda n: knew_ref[:, hs(n)], lambda n: vnew_ref[:, hs(n)])
        for n in range(N_KV_B):
            o_ref[n] = (acc_scr[n] / l_scr[n]).astype(o_ref.dtype)


def _dsa_sample_attn(page_table, scores, scores_new, q, cache_k, cache_v, layer, k_new, v_new):
    nseq, npages = page_table.shape
    t = scores.shape[1]
    past = npages * PAGE_SIZE
    topk = min(TOPK_MAX, (past + t) // 4)
    rows = GROUP_B * t
    pps = math.gcd(PAGES_PER_STEP, npages)
    nsteps = npages // pps
    page_block = (None, None, PAGE_SIZE * N_KV_B, HEAD_DIM_B)
    const3 = lambda b, j, pt: (b, 0, 0)
    return pl.pallas_call(
        functools.partial(_dsa_sample_attn_body, nsteps=nsteps, pps=pps, t=t, past=past, topk=topk,
                          tie_chunk=math.gcd(TIE_CHUNK, past)),
        grid_spec=pltpu.PrefetchScalarGridSpec(
            num_scalar_prefetch=1,
            grid=(nseq, nsteps + 1),
            in_specs=[pl.BlockSpec((None, t, past), const3),
                      pl.BlockSpec((None, t, PAGE_SIZE), const3),
                      pl.BlockSpec((None, N_KV_B, rows, HEAD_DIM_B), lambda b, j, pt: (b, 0, 0, 0)),
                      *_page_specs(page_block, layer, nsteps, pps),
                      *_page_specs(page_block, layer, nsteps, pps),
                      pl.BlockSpec((None, PAGE_SIZE, KV_WIDTH_B), const3),
                      pl.BlockSpec((None, PAGE_SIZE, KV_WIDTH_B), const3)],
            out_specs=pl.BlockSpec((None, N_KV_B, rows, HEAD_DIM_B), lambda b, j, pt: (b, 0, 0, 0)),
            scratch_shapes=[pltpu.VMEM((t, past), F32),
                            pltpu.VMEM((t, PAGE_SIZE), F32),
                            pltpu.VMEM((t, PAGE_SIZE), F32),
                            pltpu.VMEM((N_KV_B, rows, 1), F32),
                            pltpu.VMEM((N_KV_B, rows, 1), F32),
                            pltpu.VMEM((N_KV_B, rows, HEAD_DIM_B), F32)],
        ),
        out_shape=jax.ShapeDtypeStruct((nseq, N_KV_B, rows, HEAD_DIM_B), BF16),
        compiler_params=_cparams("parallel", "arbitrary"),
        name="dsa_sample_attn",
    )(page_table, scores, scores_new, q, *([cache_k] * pps), *([cache_v] * pps), k_new, v_new)


def _dsa_sample(ps, k_s, v_s, ki_s, page_table, cache_k, cache_v, cache_kidx, layer):
    nseq = page_table.shape[0]
    t = ps.shape[0] // nseq
    seg = lambda name, width: ps[:, PD_OFF[name]:PD_OFF[name] + width]
    qi = seg("qi", H_I * D_I).reshape(nseq, t * H_I, D_I)
    wi = seg("wi", H_I).reshape(nseq, t * H_I, 1)
    wv = jnp.broadcast_to(wi, (nseq, t * H_I, PAGE_SIZE))
    pad_rows = lambda a: jnp.pad(a.reshape(nseq, t, -1), ((0, 0), (0, PAGE_SIZE - t), (0, 0)))
    kidx_t = cache_kidx.transpose(0, 1, 3, 2)
    scores, scores_new = _dsa_sample_scores(page_table, qi, wv, kidx_t, layer,
                                            pad_rows(ki_s).transpose(0, 2, 1))
    q = seg("qb", WIDTH_B).reshape(nseq, t, N_KV_B, GROUP_B, HEAD_DIM_B)
    q = q.transpose(0, 2, 3, 1, 4).reshape(nseq, N_KV_B, GROUP_B * t, HEAD_DIM_B)
    pool_rows = lambda c: c.reshape(c.shape[0], c.shape[1], PAGE_SIZE * N_KV_B, HEAD_DIM_B)
    o = _dsa_sample_attn(page_table, scores, scores_new, q, pool_rows(cache_k), pool_rows(cache_v),
                         layer, pad_rows(k_s), pad_rows(v_s))
    o = o.reshape(nseq, N_KV_B, GROUP_B, t, HEAD_DIM_B).transpose(0, 3, 1, 2, 4)
    return o.reshape(nseq * t, WIDTH_B)


def _prep_w_mid(w_in):
    sizes = dict(zip(IN_NAMES, IN_SIZES))
    z0 = min(SRC_OFF[n] for n in MID_ORDER + ("aa", "ba"))
    z1 = max(SRC_OFF[n] + sizes[n] for n in MID_ORDER)
    zone = lax.optimization_barrier(lax.slice_in_dim(w_in, z0, z1, axis=2))
    cols = [zone[:, :, SRC_OFF[n] - z0:SRC_OFF[n] - z0 + sizes[n]].astype(BF16) for n in MID_ORDER]
    cols.append(jnp.zeros(w_in.shape[:2] + (PD_WIDTH - sum(sizes[n] for n in MID_ORDER),), BF16))
    return jnp.concatenate(cols, axis=2)


def kernel(x_prompt, x_sample, cache_k, cache_v, cache_kidx, page_table, state_gdn, state_gdn_conv,
           state_sconv, final_norm, norm1, norm2, w_in, conv_a, a_log, dt_bias, gdn_norm, conv_c,
           w_branch_a, w_branch_b, w_branch_c, w_o, w_gate, w_up, w_down):
    bp, tp, d = x_prompt.shape
    bs, ts = x_sample.shape[:2]
    mp, ms = bp * tp, bs * ts
    x = jnp.concatenate([x_prompt.reshape(mp, d), x_sample.reshape(ms, d)], axis=0)
    new_p = [[] for _ in range(6)]
    new_s = [[] for _ in range(6)]
    w_mid = _prep_w_mid(w_in)
    w_down16 = w_down.astype(BF16)
    for l in range(DEPTH):
        xn = _rmsnorm(x, norm1[l], BF16)
        p = _in_proj_main(xn, w_in, l)
        pd, k_new, v_new, ki_new = _in_proj_mid(xn, w_mid, l)
        g, beta = pd, jnp.stack([a_log[l], dt_bias[l]]).astype(F32)
        conv_w = conv_a[l].astype(F32)

        oa_p, gdn_p = _gdn(p, 0, bp, tp, min(GDN_CHUNK, tp), conv_w, g, beta,
                           jnp.zeros((bp, CONV_A - 1, 3 * WIDTH_A), F32),
                           jnp.zeros((bp, H_A, DK_A, DV_A), F32), gdn_norm[l])
        oa_s, gdn_s = _gdn(p, mp, bs, ts, ts, conv_w, g, beta,
                           state_gdn_conv[l], state_gdn[l], gdn_norm[l])
        ob_p = _dsa_prompt(pd, k_new, v_new, ki_new, bp, tp)
        ob_s = _dsa_sample(pd[mp:], k_new[mp:], v_new[mp:], ki_new[mp:], page_table,
                           cache_k, cache_v, cache_kidx, l)
        cw = conv_c[l].astype(F32)
        oc_p, sconv_p = _sconv(p, 0, bp, tp, cw, jnp.zeros((bp, CONV_C - 1, WIDTH_C), F32))
        oc_s, sconv_s = _sconv(p, mp, bs, ts, cw, state_sconv[l])

        oa = jnp.concatenate([oa_p, oa_s], axis=0)
        ob = jnp.concatenate([ob_p, ob_s], axis=0)
        oc = jnp.concatenate([oc_p, oc_s], axis=0)
        merged = _merge(oa, ob, oc, w_branch_a, w_branch_b, w_branch_c, l, p)
        x = _out_proj(merged, w_o, l, x)

        hn = _rmsnorm(x, norm2[l], BF16)
        h = _swiglu(hn, w_gate, w_up, l)
        x = _out_proj(h, w_down16, l, x, name="ffn_down")

        nb = CONV_A - 1
        tail_rows = lambda r0, t: lax.slice(p, (r0 + max(t - nb, 0), 0), (r0 + t, 3 * WIDTH_A))
        gconv_p = jnp.stack([tail_rows(b * tp, tp) for b in range(bp)])
        gconv_s = jnp.stack([tail_rows(mp + b * ts, ts) for b in range(bs)])
        if tp < nb:
            gconv_p = jnp.concatenate([jnp.zeros((bp, nb - tp, 3 * WIDTH_A), F32), gconv_p], axis=1)
        if ts < nb:
            gconv_s = jnp.concatenate([state_gdn_conv[l].astype(F32)[:, ts:], gconv_s], axis=1)
        kv = lambda a, rows, b, t: a[rows].reshape(b, t, N_KV_B, HEAD_DIM_B)
        rp, rs = slice(0, mp), slice(mp, mp + ms)
        for lst, val in zip(new_p, (kv(k_new, rp, bp, tp), kv(v_new, rp, bp, tp),
                                    ki_new[rp].reshape(bp, tp, D_I), gdn_p, gconv_p, sconv_p)):
            lst.append(val)
        for lst, val in zip(new_s, (kv(k_new, rs, bs, ts), kv(v_new, rs, bs, ts),
                                    ki_new[rs].reshape(bs, ts, D_I), gdn_s, gconv_s, sconv_s)):
            lst.append(val)

    y_prompt = _rmsnorm(x, final_norm, F32, 0, mp).reshape(bp, tp, d)
    y_sample = _rmsnorm(x, final_norm, F32, mp, ms).reshape(bs, ts, d)
    outs_p = [jnp.stack(a) for a in new_p]
    outs_s = [jnp.stack(a) for a in new_s]
    return (y_prompt, y_sample, *outs_p, *outs_s)
```
